```python
import math
import jax, jax.numpy as jnp
from jax import lax
import numpy as np

D_MODEL = 1024
BATCH = 8
SEQ = 16384
DEPTH = 4

HEAD_DIM = 64
BLOCK = 128
EPS = 1e-6
NEG = -1e30
A_Q_HEADS = 8
A_KV_HEADS = 2
A_GROUP = A_Q_HEADS // A_KV_HEADS
A_WINDOW = 128
B_BRANCHES = ((128, 1), (512, 4), (2048, 16))
B_HEADS_PER_BRANCH = 4
B_HEADS = len(B_BRANCHES) * B_HEADS_PER_BRANCH
NUM_BUCKETS = 32
MAX_DISTANCE = 2048
N_BIAS_HEADS = A_Q_HEADS + B_HEADS
A_IN = (A_Q_HEADS + 2 * A_KV_HEADS) * HEAD_DIM
B_IN = 3 * B_HEADS * HEAD_DIM
AB_IN = A_IN + B_IN
AB_OUT = (A_Q_HEADS + B_HEADS_PER_BRANCH) * HEAD_DIM
C_HEADS = 8
C_NOPE = 64
C_ROPE = 32
C_V = 64
C_Q_RANK = 384
C_KV_RANK = 256
C_DOWN = C_Q_RANK + C_KV_RANK + C_ROPE
ROPE_THETA = 10000.0
D_FF = 4 * D_MODEL
N_EVEN = (DEPTH + 1) // 2
N_ODD = DEPTH // 2

kernel_name = 'hybrid_swa_dilated_mla_trunk'


def rmsnorm(x, g):
    xf = x.astype(jnp.float32)
    y = xf * lax.rsqrt(jnp.mean(xf * xf, axis=-1, keepdims=True) + EPS)
    return (y * g.astype(jnp.float32)).astype(x.dtype)


def t5_bucket(n):
    max_exact = NUM_BUCKETS // 2
    nf = jnp.maximum(n, 1).astype(jnp.float32)
    large = max_exact + (jnp.log(nf / max_exact) / math.log(MAX_DISTANCE / max_exact)
                         * (NUM_BUCKETS - max_exact)).astype(jnp.int32)
    return jnp.where(n < max_exact, n, jnp.minimum(large, NUM_BUCKETS - 1))


def band_bias(table, dilation, max_dist):
    i = jnp.arange(BLOCK)[:, None]
    j = jnp.arange(2 * BLOCK)[None, :]
    dist = i + BLOCK - j
    inband = (dist >= 0) & (dist <= max_dist)
    bias = table[t5_bucket(jnp.maximum(dist, 0) * dilation)].astype(jnp.float32)
    bias = jnp.where(inband[..., None], bias, NEG)
    return bias.transpose(2, 0, 1)


def banded_attention(q, k, v, bias, sinks=None, with_lse=False):
    Bn, L, KVH, G, dh = q.shape
    n = L // BLOCK
    qb = q.reshape(Bn, n, BLOCK, KVH, G, dh)

    def two_blocks(t):
        cur = t.reshape(Bn, n, BLOCK, KVH, dh)
        prev = jnp.pad(cur, ((0, 0), (1, 0), (0, 0), (0, 0), (0, 0)))[:, :-1]
        return jnp.concatenate([prev, cur], axis=2)

    kb, vb = two_blocks(k), two_blocks(v)
    logits = jnp.einsum('bnqhgd,bnkhd->bnhgqk', qb, kb).astype(jnp.float32) * (dh ** -0.5)
    logits = logits + bias.reshape(KVH, G, BLOCK, 2 * BLOCK)
    first = (jnp.arange(n)[:, None] == 0) & (jnp.arange(2 * BLOCK)[None, :] < BLOCK)
    logits = jnp.where(first[None, :, None, None, None, :], NEG, logits)
    m = logits.max(axis=-1)
    if sinks is not None:
        s = sinks.astype(jnp.float32).reshape(KVH, G)[None, None, :, :, None]
        m = jnp.maximum(m, s)
    p = jnp.exp(logits - m[..., None])
    l = p.sum(axis=-1)
    if sinks is not None:
        l = l + jnp.exp(s - m)
    acc = jnp.einsum('bnhgqk,bnkhd->bnqhgd', p.astype(v.dtype), vb)
    o = (acc / l.transpose(0, 1, 4, 2, 3)[..., None].astype(v.dtype)).reshape(Bn, L, KVH, G, dh)
    if with_lse:
        lse = (m + jnp.log(l)).transpose(0, 1, 4, 2, 3).reshape(Bn, L, KVH, G)
        return o, lse
    return o


def dilate(t, d):
    Bn, S = t.shape[:2]
    L = S // d
    t = jnp.moveaxis(t.reshape((Bn, L, d) + t.shape[2:]), 2, 1).reshape((Bn * d, L) + t.shape[2:])
    pad = (-L) % BLOCK
    return jnp.pad(t, [(0, 0), (0, pad)] + [(0, 0)] * (t.ndim - 2))


def undilate(t, Bn, S, d):
    L = S // d
    t = t[:, :L].reshape((Bn, d, L) + t.shape[2:])
    return jnp.moveaxis(t, 1, 2).reshape((Bn, S) + t.shape[3:])


def even_mixer(xn, w_in, sinks, w_out, bias_a, bias_b):
    Bn, S, _ = xn.shape
    proj = xn @ w_in
    pa, pb = proj[..., :A_IN], proj[..., A_IN:]
    qd, kd = A_Q_HEADS * HEAD_DIM, A_KV_HEADS * HEAD_DIM
    qa = pa[..., :qd].reshape(Bn, S, A_KV_HEADS, A_GROUP, HEAD_DIM)
    ka = pa[..., qd:qd + kd].reshape(Bn, S, A_KV_HEADS, HEAD_DIM)
    va = pa[..., qd + kd:].reshape(Bn, S, A_KV_HEADS, HEAD_DIM)
    out_a = banded_attention(qa, ka, va, bias_a, sinks=sinks).reshape(Bn, S, qd)

    qkv_b = pb.reshape(Bn, S, len(B_BRANCHES), 3, B_HEADS_PER_BRANCH, HEAD_DIM)
    outs, lses = [], []
    for g, (window, dil) in enumerate(B_BRANCHES):
        q, k, v = (dilate(qkv_b[:, :, g, i], dil) for i in range(3))
        o, lse = banded_attention(q[:, :, :, None, :], k, v, bias_b[g], with_lse=True)
        outs.append(undilate(o[:, :, :, 0], Bn, S, dil))
        lses.append(undilate(lse[..., 0], Bn, S, dil))
    wts = jax.nn.softmax(jnp.stack(lses), axis=0)
    out_b = jnp.einsum('gbsh,gbshd->bshd', wts.astype(xn.dtype), jnp.stack(outs)).reshape(Bn, S, -1)
    return jnp.concatenate([out_a, out_b], axis=-1) @ w_out


def rope(t):
    S, r = t.shape[1], t.shape[-1]
    inv = ROPE_THETA ** (-jnp.arange(0, r, 2, dtype=jnp.float32) / r)
    ang = jnp.arange(S, dtype=jnp.float32)[:, None] * inv[None, :]
    shape = (1, S) + (1,) * (t.ndim - 3) + (r // 2,)
    cos, sin = jnp.cos(ang).reshape(shape), jnp.sin(ang).reshape(shape)
    t1, t2 = t[..., :r // 2].astype(jnp.float32), t[..., r // 2:].astype(jnp.float32)
    return jnp.concatenate([t1 * cos - t2 * sin, t1 * sin + t2 * cos], axis=-1).astype(t.dtype)


def causal_mla_attention(q_nope, q_rope, k_nope, k_rope, v):
    Bn, S, H, _ = q_nope.shape
    n = S // BLOCK
    scale = (C_NOPE + C_ROPE) ** -0.5
    kpos = jnp.arange(S)

    def block(args):
        qn, qr, i = args
        logits = (jnp.einsum('bqhd,bkhd->bhqk', qn, k_nope)
                  + jnp.einsum('bqhr,bkr->bhqk', qr, k_rope)).astype(jnp.float32) * scale
        qpos = i * BLOCK + jnp.arange(BLOCK)
        logits = jnp.where(kpos[None, :] <= qpos[:, None], logits, NEG)
        p = jax.nn.softmax(logits, axis=-1).astype(v.dtype)
        return jnp.einsum('bhqk,bkhd->bqhd', p, v)

    qn_b = q_nope.reshape(Bn, n, BLOCK, H, C_NOPE).transpose(1, 0, 2, 3, 4)
    qr_b = q_rope.reshape(Bn, n, BLOCK, H, C_ROPE).transpose(1, 0, 2, 3, 4)
    out = lax.map(block, (qn_b, qr_b, jnp.arange(n)))
    return out.transpose(1, 0, 2, 3, 4).reshape(Bn, S, H * C_V)


def mla_mixer(xn, w_down, q_norm, w_uq, kv_norm, w_ukv, w_o):
    Bn, S, _ = xn.shape
    down = xn @ w_down
    c_q = down[..., :C_Q_RANK]
    c_kv = down[..., C_Q_RANK:C_Q_RANK + C_KV_RANK]
    k_rope = rope(down[..., C_Q_RANK + C_KV_RANK:])
    q = (rmsnorm(c_q, q_norm) @ w_uq).reshape(Bn, S, C_HEADS, C_NOPE + C_ROPE)
    kv = (rmsnorm(c_kv, kv_norm) @ w_ukv).reshape(Bn, S, C_HEADS, C_NOPE + C_V)
    out = causal_mla_attention(q[..., :C_NOPE], rope(q[..., C_NOPE:]),
                               kv[..., :C_NOPE], k_rope, kv[..., C_NOPE:])
    return out @ w_o


def squared_relu_mlp(xn, w_up, w_down):
    h = jax.nn.relu(xn @ w_up)
    return (h * h) @ w_down


def _fwd_setup_inputs(seed: int = 0) -> dict:
    key = jax.random.key(seed)
    ks = jax.random.split(key, 16)

    def w(k, shape, fan_in):
        return jax.random.normal(k, shape, jnp.float32) * fan_in ** -0.5

    def gain(k, shape):
        return 1.0 + 0.05 * jax.random.normal(k, shape, jnp.float32)

    return {
        'x': jax.random.normal(ks[0], (BATCH, SEQ, D_MODEL), jnp.float32),
        'rel_bias': 0.5 * jax.random.normal(ks[1], (NUM_BUCKETS, N_BIAS_HEADS), jnp.float32),
        'attn_norm': gain(ks[2], (DEPTH, D_MODEL)),
        'mlp_norm': gain(ks[3], (DEPTH, D_MODEL)),
        'final_norm': gain(ks[4], (D_MODEL,)),
        'w_in_ab': w(ks[5], (N_EVEN, D_MODEL, AB_IN), D_MODEL),
        'sinks': jax.random.normal(ks[6], (N_EVEN, A_Q_HEADS), jnp.float32),
        'w_out_ab': w(ks[7], (N_EVEN, AB_OUT, D_MODEL), AB_OUT),
        'w_down_c': w(ks[8], (N_ODD, D_MODEL, C_DOWN), D_MODEL),
        'q_norm_c': gain(ks[9], (N_ODD, C_Q_RANK)),
        'w_uq_c': w(ks[10], (N_ODD, C_Q_RANK, C_HEADS * (C_NOPE + C_ROPE)), C_Q_RANK),
        'kv_norm_c': gain(ks[11], (N_ODD, C_KV_RANK)),
        'w_ukv_c': w(ks[12], (N_ODD, C_KV_RANK, C_HEADS * (C_NOPE + C_V)), C_KV_RANK),
        'w_o_c': w(ks[13], (N_ODD, C_HEADS * C_V, D_MODEL), C_HEADS * C_V),
        'w_mlp_up': w(ks[14], (DEPTH, D_MODEL, D_FF), D_MODEL),
        'w_mlp_down': w(ks[15], (DEPTH, D_FF, D_MODEL), D_FF),
    }


def _fwd_reference(x, rel_bias, attn_norm, mlp_norm, final_norm, w_in_ab, sinks, w_out_ab,
              w_down_c, q_norm_c, w_uq_c, kv_norm_c, w_ukv_c, w_o_c, w_mlp_up, w_mlp_down):
    bias_a = band_bias(rel_bias[:, :A_Q_HEADS], 1, A_WINDOW - 1)
    bias_b = [band_bias(rel_bias[:, A_Q_HEADS + g * B_HEADS_PER_BRANCH:A_Q_HEADS + (g + 1) * B_HEADS_PER_BRANCH],
                        dil, window // dil)
              for g, (window, dil) in enumerate(B_BRANCHES)]
    h = x
    for layer in range(DEPTH):
        xn = rmsnorm(h, attn_norm[layer])
        if layer % 2 == 0:
            e = layer // 2
            mix = even_mixer(xn, w_in_ab[e], sinks[e], w_out_ab[e], bias_a, bias_b)
        else:
            o = layer // 2
            mix = mla_mixer(xn, w_down_c[o], q_norm_c[o], w_uq_c[o], kv_norm_c[o], w_ukv_c[o], w_o_c[o])
        h = h + mix
        h = h + squared_relu_mlp(rmsnorm(h, mlp_norm[layer]), w_mlp_up[layer], w_mlp_down[layer])
    return rmsnorm(h, final_norm)


import jax as _jax
import jax.numpy as _jnp

TWIN_FORMAT = 'train_step'
FWD_PARAMS = ['x', 'rel_bias', 'attn_norm', 'mlp_norm', 'final_norm', 'w_in_ab', 'sinks', 'w_out_ab', 'w_down_c', 'q_norm_c', 'w_uq_c', 'kv_norm_c', 'w_ukv_c', 'w_o_c', 'w_mlp_up', 'w_mlp_down']
TWIN_WEIGHTS = ['rel_bias', 'attn_norm', 'mlp_norm', 'final_norm', 'w_in_ab', 'sinks', 'w_out_ab', 'w_down_c', 'q_norm_c', 'w_uq_c', 'kv_norm_c', 'w_ukv_c', 'w_o_c', 'w_mlp_up', 'w_mlp_down']
TWIN_DIFF_INPUT = 'x'
TWIN_INPUTS = ['x', 'rel_bias', 'attn_norm', 'mlp_norm', 'final_norm', 'w_in_ab', 'sinks', 'w_out_ab', 'w_down_c', 'q_norm_c', 'w_uq_c', 'kv_norm_c', 'w_ukv_c', 'w_o_c', 'w_mlp_up', 'w_mlp_down', 'loss_target', 'm_rel_bias', 'm_attn_norm', 'm_mlp_norm', 'm_final_norm', 'm_w_in_ab', 'm_sinks', 'm_w_out_ab', 'm_w_down_c', 'm_q_norm_c', 'm_w_uq_c', 'm_kv_norm_c', 'm_w_ukv_c', 'm_w_o_c', 'm_w_mlp_up', 'm_w_mlp_down', 'v_rel_bias', 'v_attn_norm', 'v_mlp_norm', 'v_final_norm', 'v_w_in_ab', 'v_sinks', 'v_w_out_ab', 'v_w_down_c', 'v_q_norm_c', 'v_w_uq_c', 'v_kv_norm_c', 'v_w_ukv_c', 'v_w_o_c', 'v_w_mlp_up', 'v_w_mlp_down']
TWIN_OUTPUTS = ['loss', 'grad_x', 'grad_rel_bias', 'grad_attn_norm', 'grad_mlp_norm', 'grad_final_norm', 'grad_w_in_ab', 'grad_sinks', 'grad_w_out_ab', 'grad_w_down_c', 'grad_q_norm_c', 'grad_w_uq_c', 'grad_kv_norm_c', 'grad_w_ukv_c', 'grad_w_o_c', 'grad_w_mlp_up', 'grad_w_mlp_down', 'delta_rel_bias', 'delta_attn_norm', 'delta_mlp_norm', 'delta_final_norm', 'delta_w_in_ab', 'delta_sinks', 'delta_w_out_ab', 'delta_w_down_c', 'delta_q_norm_c', 'delta_w_uq_c', 'delta_kv_norm_c', 'delta_w_ukv_c', 'delta_w_o_c', 'delta_w_mlp_up', 'delta_w_mlp_down', 'new_m_rel_bias', 'new_m_attn_norm', 'new_m_mlp_norm', 'new_m_final_norm', 'new_m_w_in_ab', 'new_m_sinks', 'new_m_w_out_ab', 'new_m_w_down_c', 'new_m_q_norm_c', 'new_m_w_uq_c', 'new_m_kv_norm_c', 'new_m_w_ukv_c', 'new_m_w_o_c', 'new_m_w_mlp_up', 'new_m_w_mlp_down', 'new_v_rel_bias', 'new_v_attn_norm', 'new_v_mlp_norm', 'new_v_final_norm', 'new_v_w_in_ab', 'new_v_sinks', 'new_v_w_out_ab', 'new_v_w_down_c', 'new_v_q_norm_c', 'new_v_w_uq_c', 'new_v_kv_norm_c', 'new_v_w_ukv_c', 'new_v_w_o_c', 'new_v_w_mlp_up', 'new_v_w_mlp_down']
TWIN_LEAF_KINDS = {'loss': 'loss', 'grad_x': 'grad_x', 'grad_rel_bias': 'grad_w', 'grad_attn_norm': 'grad_w', 'grad_mlp_norm': 'grad_w', 'grad_final_norm': 'grad_w', 'grad_w_in_ab': 'grad_w', 'grad_sinks': 'grad_w', 'grad_w_out_ab': 'grad_w', 'grad_w_down_c': 'grad_w', 'grad_q_norm_c': 'grad_w', 'grad_w_uq_c': 'grad_w', 'grad_kv_norm_c': 'grad_w', 'grad_w_ukv_c': 'grad_w', 'grad_w_o_c': 'grad_w', 'grad_w_mlp_up': 'grad_w', 'grad_w_mlp_down': 'grad_w', 'delta_rel_bias': 'delta_w', 'delta_attn_norm': 'delta_w', 'delta_mlp_norm': 'delta_w', 'delta_final_norm': 'delta_w', 'delta_w_in_ab': 'delta_w', 'delta_sinks': 'delta_w', 'delta_w_out_ab': 'delta_w', 'delta_w_down_c': 'delta_w', 'delta_q_norm_c': 'delta_w', 'delta_w_uq_c': 'delta_w', 'delta_kv_norm_c': 'delta_w', 'delta_w_ukv_c': 'delta_w', 'delta_w_o_c': 'delta_w', 'delta_w_mlp_up': 'delta_w', 'delta_w_mlp_down': 'delta_w', 'new_m_rel_bias': 'new_m', 'new_m_attn_norm': 'new_m', 'new_m_mlp_norm': 'new_m', 'new_m_final_norm': 'new_m', 'new_m_w_in_ab': 'new_m', 'new_m_sinks': 'new_m', 'new_m_w_out_ab': 'new_m', 'new_m_w_down_c': 'new_m', 'new_m_q_norm_c': 'new_m', 'new_m_w_uq_c': 'new_m', 'new_m_kv_norm_c': 'new_m', 'new_m_w_ukv_c': 'new_m', 'new_m_w_o_c': 'new_m', 'new_m_w_mlp_up': 'new_m', 'new_m_w_mlp_down': 'new_m', 'new_v_rel_bias': 'new_v', 'new_v_attn_norm': 'new_v', 'new_v_mlp_norm': 'new_v', 'new_v_final_norm': 'new_v', 'new_v_w_in_ab': 'new_v', 'new_v_sinks': 'new_v', 'new_v_w_out_ab': 'new_v', 'new_v_w_down_c': 'new_v', 'new_v_q_norm_c': 'new_v', 'new_v_w_uq_c': 'new_v', 'new_v_kv_norm_c': 'new_v', 'new_v_w_ukv_c': 'new_v', 'new_v_w_o_c': 'new_v', 'new_v_w_mlp_up': 'new_v', 'new_v_w_mlp_down': 'new_v'}


def _forward(args):
    return _fwd_reference(*[args[k] for k in FWD_PARAMS])


def _output_shape():
    def fwd():
        inp = _fwd_setup_inputs(0)
        return _fwd_reference(*[inp[k] for k in FWD_PARAMS])
    out = _jax.eval_shape(fwd)
    return out.shape, out.dtype

N_MICROBATCH = 1
ADAM_LR = 0.001
ADAM_B1 = 0.9
ADAM_B2 = 0.999
ADAM_EPS = 1e-08
ADAM_WD = 0.01
ADAM_STEP = 10
PER_EXAMPLE_BATCH_AXIS = {'x': 0, 'loss_target': 0}
SHARED_INPUTS = []
_WEIGHT_DTYPES = {'rel_bias': _jnp.float32, 'attn_norm': _jnp.float32, 'mlp_norm': _jnp.float32, 'final_norm': _jnp.float32, 'w_in_ab': _jnp.float32, 'sinks': _jnp.float32, 'w_out_ab': _jnp.float32, 'w_down_c': _jnp.float32, 'q_norm_c': _jnp.float32, 'w_uq_c': _jnp.float32, 'kv_norm_c': _jnp.float32, 'w_ukv_c': _jnp.float32, 'w_o_c': _jnp.float32, 'w_mlp_up': _jnp.float32, 'w_mlp_down': _jnp.float32}
MOMENT_SCALE = {'rel_bias': 9.652086e-02, 'attn_norm': 6.292492e-01, 'mlp_norm': 5.577026e-01, 'final_norm': 1.353183e+02, 'w_in_ab': 2.120671e-01, 'sinks': 9.798491e-02, 'w_out_ab': 5.628743e-01, 'w_down_c': 1.215744e+00, 'q_norm_c': 5.327357e-02, 'w_uq_c': 3.686134e-02, 'kv_norm_c': 2.252321e+00, 'w_ukv_c': 1.043376e+00, 'w_o_c': 9.479698e-01, 'w_mlp_up': 2.693922e-01, 'w_mlp_down': 1.321106e+00}


def _to_microbatches(a, axis):
    t = _jnp.moveaxis(a, axis, 0)
    t = t.reshape((N_MICROBATCH, t.shape[0] // N_MICROBATCH) + t.shape[1:])
    return _jnp.moveaxis(t, 1, axis + 1)


def setup_inputs(seed: int = 0) -> dict:
    inp = _fwd_setup_inputs(seed)
    key = _jax.random.fold_in(_jax.random.key(seed), 7919)
    shape, _ = _output_shape()
    out = dict(inp)
    out["loss_target"] = _jax.random.normal(_jax.random.fold_in(key, 0), shape, _jnp.float32)
    for i, name in enumerate(TWIN_WEIGHTS):
        w = inp[name].astype(_jnp.float32)
        if MOMENT_SCALE is None:
            s = _jnp.sqrt(_jnp.mean(_jnp.square(w)) + 1e-30)
        else:
            s = MOMENT_SCALE[name]
        km, kv = _jax.random.split(_jax.random.fold_in(key, i + 1))
        out[name] = w
        out["m_" + name] = s * _jax.random.normal(km, w.shape, _jnp.float32)
        out["v_" + name] = (s * s) * _jax.random.uniform(kv, w.shape, _jnp.float32, 0.5, 1.5)
    if N_MICROBATCH > 1:
        for name, axis in PER_EXAMPLE_BATCH_AXIS.items():
            out[name] = _to_microbatches(out[name], axis)
    return {'x': out['x'], 'rel_bias': out['rel_bias'], 'attn_norm': out['attn_norm'], 'mlp_norm': out['mlp_norm'], 'final_norm': out['final_norm'], 'w_in_ab': out['w_in_ab'], 'sinks': out['sinks'], 'w_out_ab': out['w_out_ab'], 'w_down_c': out['w_down_c'], 'q_norm_c': out['q_norm_c'], 'w_uq_c': out['w_uq_c'], 'kv_norm_c': out['kv_norm_c'], 'w_ukv_c': out['w_ukv_c'], 'w_o_c': out['w_o_c'], 'w_mlp_up': out['w_mlp_up'], 'w_mlp_down': out['w_mlp_down'], 'loss_target': out['loss_target'], 'm_rel_bias': out['m_rel_bias'], 'm_attn_norm': out['m_attn_norm'], 'm_mlp_norm': out['m_mlp_norm'], 'm_final_norm': out['m_final_norm'], 'm_w_in_ab': out['m_w_in_ab'], 'm_sinks': out['m_sinks'], 'm_w_out_ab': out['m_w_out_ab'], 'm_w_down_c': out['m_w_down_c'], 'm_q_norm_c': out['m_q_norm_c'], 'm_w_uq_c': out['m_w_uq_c'], 'm_kv_norm_c': out['m_kv_norm_c'], 'm_w_ukv_c': out['m_w_ukv_c'], 'm_w_o_c': out['m_w_o_c'], 'm_w_mlp_up': out['m_w_mlp_up'], 'm_w_mlp_down': out['m_w_mlp_down'], 'v_rel_bias': out['v_rel_bias'], 'v_attn_norm': out['v_attn_norm'], 'v_mlp_norm': out['v_mlp_norm'], 'v_final_norm': out['v_final_norm'], 'v_w_in_ab': out['v_w_in_ab'], 'v_sinks': out['v_sinks'], 'v_w_out_ab': out['v_w_out_ab'], 'v_w_down_c': out['v_w_down_c'], 'v_q_norm_c': out['v_q_norm_c'], 'v_w_uq_c': out['v_w_uq_c'], 'v_kv_norm_c': out['v_kv_norm_c'], 'v_w_ukv_c': out['v_w_ukv_c'], 'v_w_o_c': out['v_w_o_c'], 'v_w_mlp_up': out['v_w_mlp_up'], 'v_w_mlp_down': out['v_w_mlp_down']}


def _loss(weights, diff, rest, loss_target):
    with _jax.named_scope("forward"):
        args = {**rest, TWIN_DIFF_INPUT: diff, **{k: w.astype(_WEIGHT_DTYPES[k]) for k, w in weights.items()}}
        y = _forward(args)
    with _jax.named_scope("loss_head"):
        err = _jnp.square(y.astype(_jnp.float32) - loss_target)
        return 0.5 * _jnp.sum(_jnp.mean(err, axis=-1)) if err.ndim else 0.5 * err


def _adamw(w, g, m, v):
    m = ADAM_B1 * m + (1.0 - ADAM_B1) * g
    v = ADAM_B2 * v + (1.0 - ADAM_B2) * _jnp.square(g)
    m_hat = m / (1.0 - ADAM_B1 ** ADAM_STEP)
    v_hat = v / (1.0 - ADAM_B2 ** ADAM_STEP)
    delta = -ADAM_LR * (m_hat / (_jnp.sqrt(v_hat) + ADAM_EPS) + ADAM_WD * w)
    return delta, m, v


def reference(x, rel_bias, attn_norm, mlp_norm, final_norm, w_in_ab, sinks, w_out_ab, w_down_c, q_norm_c, w_uq_c, kv_norm_c, w_ukv_c, w_o_c, w_mlp_up, w_mlp_down, loss_target, m_rel_bias, m_attn_norm, m_mlp_norm, m_final_norm, m_w_in_ab, m_sinks, m_w_out_ab, m_w_down_c, m_q_norm_c, m_w_uq_c, m_kv_norm_c, m_w_ukv_c, m_w_o_c, m_w_mlp_up, m_w_mlp_down, v_rel_bias, v_attn_norm, v_mlp_norm, v_final_norm, v_w_in_ab, v_sinks, v_w_out_ab, v_w_down_c, v_q_norm_c, v_w_uq_c, v_kv_norm_c, v_w_ukv_c, v_w_o_c, v_w_mlp_up, v_w_mlp_down):
    given = dict(x=x, rel_bias=rel_bias, attn_norm=attn_norm, mlp_norm=mlp_norm, final_norm=final_norm, w_in_ab=w_in_ab, sinks=sinks, w_out_ab=w_out_ab, w_down_c=w_down_c, q_norm_c=q_norm_c, w_uq_c=w_uq_c, kv_norm_c=kv_norm_c, w_ukv_c=w_ukv_c, w_o_c=w_o_c, w_mlp_up=w_mlp_up, w_mlp_down=w_mlp_down, loss_target=loss_target, m_rel_bias=m_rel_bias, m_attn_norm=m_attn_norm, m_mlp_norm=m_mlp_norm, m_final_norm=m_final_norm, m_w_in_ab=m_w_in_ab, m_sinks=m_sinks, m_w_out_ab=m_w_out_ab, m_w_down_c=m_w_down_c, m_q_norm_c=m_q_norm_c, m_w_uq_c=m_w_uq_c, m_kv_norm_c=m_kv_norm_c, m_w_ukv_c=m_w_ukv_c, m_w_o_c=m_w_o_c, m_w_mlp_up=m_w_mlp_up, m_w_mlp_down=m_w_mlp_down, v_rel_bias=v_rel_bias, v_attn_norm=v_attn_norm, v_mlp_norm=v_mlp_norm, v_final_norm=v_final_norm, v_w_in_ab=v_w_in_ab, v_sinks=v_sinks, v_w_out_ab=v_w_out_ab, v_w_down_c=v_w_down_c, v_q_norm_c=v_q_norm_c, v_w_uq_c=v_w_uq_c, v_kv_norm_c=v_kv_norm_c, v_w_ukv_c=v_w_ukv_c, v_w_o_c=v_w_o_c, v_w_mlp_up=v_w_mlp_up, v_w_mlp_down=v_w_mlp_down)
    weights = {n: given[n] for n in TWIN_WEIGHTS}
    shared = {n: given[n] for n in SHARED_INPUTS}
    per_example = {n: given[n] for n in ['x']}
    grad_fn = _jax.value_and_grad(_loss, argnums=(0, 1))

    def one_microbatch(ex, loss_target):
        ex = dict(ex)
        diff = ex.pop(TWIN_DIFF_INPUT)
        return grad_fn(weights, diff, {**shared, **ex}, loss_target)

    if N_MICROBATCH == 1:
        loss, (grad_w, grad_x) = one_microbatch(per_example, given["loss_target"])
    else:
        def body(carry, xs):
            loss_sum, grad_sum = carry
            l_k, (gw_k, gx_k) = one_microbatch(xs[0], xs[1])
            with _jax.named_scope("update"):
                return (loss_sum + l_k, _jax.tree.map(_jnp.add, grad_sum, gw_k)), gx_k

        init = (_jnp.zeros((), _jnp.float32), _jax.tree.map(_jnp.zeros_like, weights))
        (loss, grad_w), grad_x = _jax.lax.scan(body, init, (per_example, given["loss_target"]))
    with _jax.named_scope("update"):
        delta_w, new_m, new_v = {}, {}, {}
        for n in TWIN_WEIGHTS:
            delta_w[n], new_m[n], new_v[n] = _adamw(weights[n], grad_w[n], given["m_" + n], given["v_" + n])
    return (loss, grad_x, *[grad_w[n] for n in TWIN_WEIGHTS], *[delta_w[n] for n in TWIN_WEIGHTS],
            *[new_m[n] for n in TWIN_WEIGHTS], *[new_v[n] for n in TWIN_WEIGHTS])
```

```python
import math

import jax
import jax.numpy as jnp
from jax import lax
from jax.experimental import pallas as pl
from jax.experimental.pallas import tpu as pltpu

F32 = jnp.float32
BF16 = jnp.bfloat16
MESH = pl.DeviceIdType.MESH
ALL_AXES = ("x", "y", "c")

EPS = 1e-6
NEG = -1e30
BLOCK = 128
HEAD_DIM = 64
A_Q_HEADS, A_KV_HEADS = 8, 2
A_WINDOW = 128
B_BRANCHES = ((128, 1), (512, 4), (2048, 16))
B_HEADS_PER_BRANCH = 4
NUM_BUCKETS, MAX_DISTANCE = 32, 2048
A_IN = (A_Q_HEADS + 2 * A_KV_HEADS) * HEAD_DIM
C_HEADS, C_NOPE, C_ROPE, C_V = 8, 64, 32, 64
C_Q_RANK, C_KV_RANK = 384, 256
ROPE_THETA = 10000.0
ADAM_LR, ADAM_B1, ADAM_B2, ADAM_EPS, ADAM_WD, ADAM_STEP = 0.001, 0.9, 0.999, 1e-08, 0.01, 10

V7X_VMEM_LIMIT_BYTES = 56 * 1024 * 1024
LANES = 128
COMM_COLS = 1024
COMM_ROW_ALIGN = 1024

NT = (((1,), (1,)), ((), ()))
NN = (((1,), (0,)), ((), ()))
TN = (((0,), (0,)), ((), ()))
DIMS = {"nn": NN, "nt": NT, "tn": TN}


def _params(sem):
    return pltpu.CompilerParams(dimension_semantics=sem, vmem_limit_bytes=V7X_VMEM_LIMIT_BYTES)


def _divisor_tile(n, limit, align):
    if n <= limit:
        return n
    t = (limit // align) * align
    while t >= align:
        if n % t == 0:
            return t
        t -= align
    return n


def _ew(fn, ins, outs, name, acc_outs=(), target_bytes=6 << 20):
    rows = max(a.shape[0] for a in ins)

    def vmem_row_bytes(cols, dtype):
        return -(-cols // LANES) * LANES * jnp.dtype(dtype).itemsize

    per_row = sum(vmem_row_bytes(a.shape[1], a.dtype) for a in ins if a.shape[0] == rows)
    per_row += sum(vmem_row_bytes(c, d) for c, d in outs)
    tr = _divisor_tile(rows, max(16, target_bytes // max(per_row, 1)), 16)
    n_in, n_out = len(ins), len(outs)

    def body(*refs):
        res = fn(*[r[...] for r in refs[:n_in]])
        for r, v in zip(refs[n_in:n_in + n_out], res[:n_out]):
            r[...] = v.astype(r.dtype)
        if acc_outs:
            acc_refs = refs[n_in + n_out:]

            @pl.when(pl.program_id(0) == 0)
            def _():
                for r in acc_refs:
                    r[...] = jnp.zeros_like(r)

            for r, v in zip(acc_refs, res[n_out:]):
                r[...] += v

    def spec(a):
        if a.shape[0] == rows:
            return pl.BlockSpec((tr, a.shape[1]), lambda i: (i, 0))
        return pl.BlockSpec((1, a.shape[1]), lambda i: (0, 0))

    out_shape = [jax.ShapeDtypeStruct((rows, c), d) for c, d in outs]
    out_shape += [jax.ShapeDtypeStruct((1, c), F32) for c in acc_outs]
    out_specs = [pl.BlockSpec((tr, c), lambda i: (i, 0)) for c, _ in outs]
    out_specs += [pl.BlockSpec((1, c), lambda i: (0, 0)) for c in acc_outs]
    return pl.pallas_call(
        body, name=name, grid=(rows // tr,), in_specs=[spec(a) for a in ins], out_specs=out_specs,
        out_shape=out_shape, compiler_params=_params(("arbitrary",)),
    )(*ins)


def _rms_fwd(x, g, name):
    def fn(xv, gv):
        return ((xv * lax.rsqrt(jnp.mean(xv * xv, axis=-1, keepdims=True) + EPS)) * gv,)

    return _ew(fn, [x, g.reshape(1, -1)], [(x.shape[1], BF16)], name)[0]


def _rms_bwd(x, g, dy, add, name):
    def fn(xv, gv, dyv, *rest):
        rstd = lax.rsqrt(jnp.mean(xv * xv, axis=-1, keepdims=True) + EPS)
        xh = xv * rstd
        dyg = dyv.astype(F32) * gv
        dx = rstd * (dyg - xh * jnp.mean(dyg * xh, axis=-1, keepdims=True))
        if rest:
            dx = dx + rest[0]
        return dx, jnp.sum(dyv.astype(F32) * xh, axis=0, keepdims=True)

    ins = [x, g.reshape(1, -1), dy] + ([] if add is None else [add])
    dx, dg = _ew(fn, ins, [(x.shape[1], F32)], name, acc_outs=(x.shape[1],))
    return dx, dg[0]


def _final_loss(h, g, target, name):
    d = h.shape[1]

    def fn(xv, gv, tv):
        rstd = lax.rsqrt(jnp.mean(xv * xv, axis=-1, keepdims=True) + EPS)
        xh = xv * rstd
        err = xh * gv - tv
        part = 0.5 * jnp.sum(jnp.mean(err * err, axis=-1, keepdims=True), axis=0, keepdims=True)
        dy = err * (1.0 / d)
        dyg = dy * gv
        dx = rstd * (dyg - xh * jnp.mean(dyg * xh, axis=-1, keepdims=True))
        return dx, jnp.broadcast_to(part, (1, LANES)), jnp.sum(dy * xh, axis=0, keepdims=True)

    dx, loss, dg = _ew(fn, [h, g.reshape(1, -1), target], [(d, F32)], name, acc_outs=(LANES, d))
    return loss[0, 0], dx, dg[0]


def _adamw(w, g, m, v, name):
    def fn(wv, gv, mv, vv):
        mn = ADAM_B1 * mv + (1.0 - ADAM_B1) * gv
        vn = ADAM_B2 * vv + (1.0 - ADAM_B2) * jnp.square(gv)
        m_hat = mn / (1.0 - ADAM_B1 ** ADAM_STEP)
        v_hat = vn / (1.0 - ADAM_B2 ** ADAM_STEP)
        return -ADAM_LR * (m_hat / (jnp.sqrt(v_hat) + ADAM_EPS) + ADAM_WD * wv), mn, vn

    shape = w.shape
    cols = shape[-1] if w.ndim > 1 else w.size
    view = [t.reshape(-1, cols) for t in (w, g, m, v)]
    return [t.reshape(shape) for t in _ew(fn, view, [(cols, F32)] * 3, name)]


def _mm(a, b, dims, outs, name, epi=None, extras=(), tm=512, tn=1024, tk=1024):
    if dims == "nn":
        (m, k), n = a.shape, b.shape[1]
    elif dims == "nt":
        (m, k), n = a.shape, b.shape[0]
    else:
        (k, m), n = a.shape, b.shape[1]
    tm, tn, tk = _divisor_tile(m, tm, LANES), _divisor_tile(n, tn, LANES), _divisor_tile(k, tk, LANES)
    nk = k // tk
    n_ex, n_out = len(extras), len(outs)

    def body(a_ref, b_ref, *rest):
        ex_refs, out_refs = rest[:n_ex], rest[n_ex:n_ex + n_out]

        def finish(acc):
            res = epi(acc, *[r[...] for r in ex_refs]) if epi else (acc,)
            for r, v in zip(out_refs, res):
                r[...] = v.astype(r.dtype)

        part = lax.dot_general(a_ref[...].astype(BF16), b_ref[...].astype(BF16), DIMS[dims],
                               preferred_element_type=F32)
        if nk == 1:
            finish(part)
        else:
            acc_ref = rest[-1]
            kk = pl.program_id(2)

            @pl.when(kk == 0)
            def _():
                acc_ref[...] = part

            @pl.when(kk > 0)
            def _():
                acc_ref[...] += part

            @pl.when(kk == nk - 1)
            def _():
                finish(acc_ref[...])

    if dims == "nn":
        a_spec = pl.BlockSpec((tm, tk), lambda i, j, kk: (i, kk))
        b_spec = pl.BlockSpec((tk, tn), lambda i, j, kk: (kk, j))
    elif dims == "nt":
        a_spec = pl.BlockSpec((tm, tk), lambda i, j, kk: (i, kk))
        b_spec = pl.BlockSpec((tn, tk), lambda i, j, kk: (j, kk))
    else:
        a_spec = pl.BlockSpec((tk, tm), lambda i, j, kk: (kk, i))
        b_spec = pl.BlockSpec((tk, tn), lambda i, j, kk: (kk, j))
    tile = pl.BlockSpec((tm, tn), lambda i, j, kk: (i, j))
    return pl.pallas_call(
        body, name=name, grid=(m // tm, n // tn, nk),
        in_specs=[a_spec, b_spec] + [tile] * n_ex, out_specs=[tile] * n_out,
        out_shape=[jax.ShapeDtypeStruct((m, n), d) for d in outs],
        scratch_shapes=[pltpu.VMEM((tm, tn), F32)] if nk > 1 else [],
        compiler_params=_params(("parallel", "parallel", "arbitrary")),
    )(a, b, *extras)


def _add_epi(acc, res):
    return (acc + res,)


def _relu2_epi(acc):
    r = jnp.maximum(acc, 0.0)
    return r * r, r


def _relu2_bwd_epi(acc, r):
    return (acc * (2.0 * r.astype(F32)),)


def _band_geometry(t, blocks_per_seq):
    rows = min(1024, blocks_per_seq * BLOCK)
    nb = rows // BLOCK
    assert blocks_per_seq % nb == 0 and t % rows == 0
    return rows, nb, t // rows


def _band_logits(qj, kk, bias, first):
    s = lax.dot_general(qj, kk, NT, preferred_element_type=F32) * (HEAD_DIM ** -0.5) + bias
    if first is not None:
        col = lax.broadcasted_iota(jnp.int32, s.shape, 1)
        s = jnp.where(col < jnp.where(first, BLOCK, 0), NEG, s)
    return s


def _band_fwd(q, k, v, bias, sinks, blocks_per_seq, name):
    hq, t, dh = q.shape
    group = hq // k.shape[0]
    rows, nb, nchunks = _band_geometry(t, blocks_per_seq)
    has_sink = sinks is not None

    def body(q_ref, kc_ref, kp_ref, vc_ref, vp_ref, bias_ref, *rest):
        o_ref, lse_ref = rest[-2:]
        i = pl.program_id(1)
        bias_v = bias_ref[0]
        sink = rest[0][0][:, :1] if has_sink else None
        for j in range(nb):
            cur = slice(j * BLOCK, (j + 1) * BLOCK)
            prev = slice((j - 1) * BLOCK, j * BLOCK)
            kk = jnp.concatenate([kp_ref[0] if j == 0 else kc_ref[0, prev, :], kc_ref[0, cur, :]], axis=0)
            vv = jnp.concatenate([vp_ref[0] if j == 0 else vc_ref[0, prev, :], vc_ref[0, cur, :]], axis=0)
            first = lax.rem(i * nb, blocks_per_seq) == 0 if j == 0 else None
            s = _band_logits(q_ref[0, cur, :], kk, bias_v, first)
            m = jnp.max(s, axis=1, keepdims=True)
            if has_sink:
                m = jnp.maximum(m, sink)
            p = jnp.exp(s - m)
            l = jnp.sum(p, axis=1, keepdims=True)
            if has_sink:
                l = l + jnp.exp(sink - m)
            acc = jnp.dot(p.astype(BF16), vv, preferred_element_type=F32)
            o_ref[0, cur, :] = acc / l
            lse_ref[0, cur, :] = jnp.broadcast_to(m + jnp.log(l), (BLOCK, LANES))

    cur_q = pl.BlockSpec((1, rows, dh), lambda h, i: (h, i, 0))
    cur_kv = pl.BlockSpec((1, rows, dh), lambda h, i: (h // group, i, 0))
    prev_kv = pl.BlockSpec((1, BLOCK, dh), lambda h, i: (h // group, jnp.maximum(i * nb - 1, 0), 0))
    in_specs = [cur_q, cur_kv, prev_kv, cur_kv, prev_kv, pl.BlockSpec((1, BLOCK, 2 * BLOCK), lambda h, i: (h, 0, 0))]
    ins = [q, k, k, v, v, bias]
    if has_sink:
        in_specs.append(pl.BlockSpec((1, 1, LANES), lambda h, i: (h, 0, 0)))
        ins.append(sinks)
    return pl.pallas_call(
        body, name=name, grid=(hq, nchunks), in_specs=in_specs,
        out_specs=[cur_q, pl.BlockSpec((1, rows, LANES), lambda h, i: (h, i, 0))],
        out_shape=[jax.ShapeDtypeStruct((hq, t, dh), F32), jax.ShapeDtypeStruct((hq, t, LANES), F32)],
        compiler_params=_params(("parallel", "arbitrary")),
    )(*ins)


def _band_bwd(q, k, v, o, lse, do, bias, sinks, blocks_per_seq, name):
    hq, t, dh = q.shape
    group = hq // k.shape[0]
    rows, nb, nchunks = _band_geometry(t, blocks_per_seq)
    has_sink = sinks is not None
    scale = HEAD_DIM ** -0.5

    def body(q_ref, kc_ref, kp_ref, vc_ref, vp_ref, o_ref, lse_ref, do_ref, bias_ref, *rest):
        dq_ref, dk_ref, dv_ref, dbias_ref, dsink_ref, dk_carry, dv_carry = rest[-7:]
        step = pl.program_id(1)
        chunk = nchunks - 1 - step
        bias_v = bias_ref[0]
        sink = rest[0][0][:, :1] if has_sink else None

        @pl.when(step == 0)
        def _():
            dk_carry[...] = jnp.zeros_like(dk_carry)
            dv_carry[...] = jnp.zeros_like(dv_carry)
            dbias_ref[...] = jnp.zeros_like(dbias_ref)
            dsink_ref[...] = jnp.zeros_like(dsink_ref)

        dks = [jnp.zeros((BLOCK, dh), F32) for _ in range(nb + 1)]
        dvs = [jnp.zeros((BLOCK, dh), F32) for _ in range(nb + 1)]
        dks[nb] = dk_carry[...]
        dvs[nb] = dv_carry[...]
        for j in range(nb - 1, -1, -1):
            cur = slice(j * BLOCK, (j + 1) * BLOCK)
            prev = slice((j - 1) * BLOCK, j * BLOCK)
            kk = jnp.concatenate([kp_ref[0] if j == 0 else kc_ref[0, prev, :], kc_ref[0, cur, :]], axis=0)
            vv = jnp.concatenate([vp_ref[0] if j == 0 else vc_ref[0, prev, :], vc_ref[0, cur, :]], axis=0)
            first = lax.rem(chunk * nb, blocks_per_seq) == 0 if j == 0 else None
            qj, doj = q_ref[0, cur, :], do_ref[0, cur, :]
            lse_j = lse_ref[0, cur, :][:, :1]
            p = jnp.exp(_band_logits(qj, kk, bias_v, first) - lse_j)
            dp = lax.dot_general(doj, vv, NT, preferred_element_type=F32)
            delta = jnp.sum(doj.astype(F32) * o_ref[0, cur, :], axis=1, keepdims=True)
            ds = p * (dp - delta)
            dbias_ref[0] += ds
            if has_sink:
                dsink = -jnp.sum(jnp.exp(sink - lse_j) * delta, axis=0, keepdims=True)
                dsink_ref[0] += jnp.broadcast_to(dsink, (1, LANES))
            dsb = (ds * scale).astype(BF16)
            dq_ref[0, cur, :] = jnp.dot(dsb, kk, preferred_element_type=F32)
            dkk = lax.dot_general(dsb, qj, TN, preferred_element_type=F32)
            dvv = lax.dot_general(p.astype(BF16), doj, TN, preferred_element_type=F32)
            dks[j] += dkk[:BLOCK]
            dks[j + 1] += dkk[BLOCK:]
            dvs[j] += dvv[:BLOCK]
            dvs[j + 1] += dvv[BLOCK:]
        for j in range(nb):
            cur = slice(j * BLOCK, (j + 1) * BLOCK)
            dk_ref[0, cur, :] = dks[j + 1]
            dv_ref[0, cur, :] = dvs[j + 1]
        dk_carry[...] = dks[0]
        dv_carry[...] = dvs[0]

    def rev(i):
        return nchunks - 1 - i

    cur_q = pl.BlockSpec((1, rows, dh), lambda h, i: (h, rev(i), 0))
    cur_kv = pl.BlockSpec((1, rows, dh), lambda h, i: (h // group, rev(i), 0))
    prev_kv = pl.BlockSpec((1, BLOCK, dh), lambda h, i: (h // group, jnp.maximum(rev(i) * nb - 1, 0), 0))
    cur_lse = pl.BlockSpec((1, rows, LANES), lambda h, i: (h, rev(i), 0))
    per_head_bias = pl.BlockSpec((1, BLOCK, 2 * BLOCK), lambda h, i: (h, 0, 0))
    per_head_row = pl.BlockSpec((1, 1, LANES), lambda h, i: (h, 0, 0))
    in_specs = [cur_q, cur_kv, prev_kv, cur_kv, prev_kv, cur_q, cur_lse, cur_q, per_head_bias]
    ins = [q, k, k, v, v, o, lse, do, bias]
    if has_sink:
        in_specs.append(per_head_row)
        ins.append(sinks)
    full = jax.ShapeDtypeStruct((hq, t, dh), F32)
    return pl.pallas_call(
        body, name=name, grid=(hq, nchunks), in_specs=in_specs,
        out_specs=[cur_q, cur_q, cur_q, per_head_bias, per_head_row],
        out_shape=[full, full, full, jax.ShapeDtypeStruct((hq, BLOCK, 2 * BLOCK), F32),
                   jax.ShapeDtypeStruct((hq, 1, LANES), F32)],
        scratch_shapes=[pltpu.VMEM((BLOCK, dh), F32), pltpu.VMEM((BLOCK, dh), F32)],
        compiler_params=_params(("parallel", "arbitrary")),
    )(*ins)


C_QK = C_NOPE + C_ROPE
C_SCALE = C_QK ** -0.5


def _causal_tile(t):
    return min(1024, t)


def _mla_fwd(q, k, v, name):
    h, t, _ = q.shape
    tq = _causal_tile(t)
    n = t // tq

    def body(q_ref, k_ref, v_ref, o_ref, lse_ref, m_scr, l_scr, acc_scr):
        qi, ki = pl.program_id(1), pl.program_id(2)

        @pl.when(ki == 0)
        def _():
            m_scr[...] = jnp.full_like(m_scr, NEG)
            l_scr[...] = jnp.zeros_like(l_scr)
            acc_scr[...] = jnp.zeros_like(acc_scr)

        @pl.when(ki <= qi)
        def _():
            s = lax.dot_general(q_ref[0], k_ref[0], NT, preferred_element_type=F32) * C_SCALE
            qpos = qi * tq + lax.broadcasted_iota(jnp.int32, s.shape, 0)
            kpos = ki * tq + lax.broadcasted_iota(jnp.int32, s.shape, 1)
            s = jnp.where(kpos <= qpos, s, NEG)
            m_prev = m_scr[:, :1]
            m_new = jnp.maximum(m_prev, jnp.max(s, axis=1, keepdims=True))
            alpha = jnp.exp(m_prev - m_new)
            p = jnp.exp(s - m_new)
            l_scr[...] = jnp.broadcast_to(alpha * l_scr[:, :1] + jnp.sum(p, axis=1, keepdims=True), l_scr.shape)
            acc_scr[...] = alpha * acc_scr[...] + jnp.dot(p.astype(BF16), v_ref[0], preferred_element_type=F32)
            m_scr[...] = jnp.broadcast_to(m_new, m_scr.shape)

        @pl.when(ki == qi)
        def _():
            l = l_scr[:, :1]
            o_ref[0] = acc_scr[...] / l
            lse_ref[0] = jnp.broadcast_to(m_scr[:, :1] + jnp.log(l), lse_ref.shape[1:])

    def kv_spec(d):
        return pl.BlockSpec((1, tq, d), lambda hh, qi, ki: (hh, jnp.minimum(ki, qi), 0))

    return pl.pallas_call(
        body, name=name, grid=(h, n, n),
        in_specs=[pl.BlockSpec((1, tq, C_QK), lambda hh, qi, ki: (hh, qi, 0)), kv_spec(C_QK), kv_spec(C_V)],
        out_specs=[pl.BlockSpec((1, tq, C_V), lambda hh, qi, ki: (hh, qi, 0)),
                   pl.BlockSpec((1, tq, LANES), lambda hh, qi, ki: (hh, qi, 0))],
        out_shape=[jax.ShapeDtypeStruct((h, t, C_V), F32), jax.ShapeDtypeStruct((h, t, LANES), F32)],
        scratch_shapes=[pltpu.VMEM((tq, LANES), F32), pltpu.VMEM((tq, LANES), F32), pltpu.VMEM((tq, C_V), F32)],
        compiler_params=_params(("parallel", "arbitrary", "arbitrary")),
    )(q, k, v)


def _mla_bwd(q, k, kt, v, o, lse, do, name):
    h, t, _ = q.shape
    tq = _causal_tile(t)
    n = t // tq

    def body(q_ref, k_ref, kt_ref, v_ref, o_ref, lse_ref, do_ref, dqt_ref, dk_ref, dv_ref, dk_acc, dv_acc):
        ki, qi = pl.program_id(1), pl.program_id(2)

        @pl.when(qi == 0)
        def _():
            dk_acc[...] = jnp.zeros_like(dk_acc)
            dv_acc[...] = jnp.zeros_like(dv_acc)

        @pl.when(jnp.logical_and(ki == 0, qi == 0))
        def _():
            dqt_ref[...] = jnp.zeros_like(dqt_ref)

        @pl.when(qi >= ki)
        def _():
            qv, dov = q_ref[0], do_ref[0]
            st = lax.dot_general(k_ref[0], qv, NT, preferred_element_type=F32) * C_SCALE
            kpos = ki * tq + lax.broadcasted_iota(jnp.int32, st.shape, 0)
            qpos = qi * tq + lax.broadcasted_iota(jnp.int32, st.shape, 1)
            lse_row = jnp.transpose(lse_ref[0])[:1]
            pt = jnp.where(kpos <= qpos, jnp.exp(st - lse_row), 0.0)
            dpt = lax.dot_general(v_ref[0], dov, NT, preferred_element_type=F32)
            delta = jnp.sum(dov.astype(F32) * o_ref[0], axis=1, keepdims=True)
            delta_row = jnp.transpose(jnp.broadcast_to(delta, (tq, LANES)))[:1]
            dst = (pt * (dpt - delta_row) * C_SCALE).astype(BF16)
            dv_acc[...] += jnp.dot(pt.astype(BF16), dov, preferred_element_type=F32)
            dk_acc[...] += jnp.dot(dst, qv, preferred_element_type=F32)
            cols = pl.ds(pl.multiple_of(qi * tq, tq), tq)
            dqt_ref[0, :, cols] += jnp.dot(kt_ref[0], dst, preferred_element_type=F32)

        @pl.when(qi == n - 1)
        def _():
            dk_ref[0] = dk_acc[...]
            dv_ref[0] = dv_acc[...]

    def q_spec(d):
        return pl.BlockSpec((1, tq, d), lambda hh, ki, qi: (hh, jnp.maximum(qi, ki), 0))

    def k_spec(d):
        return pl.BlockSpec((1, tq, d), lambda hh, ki, qi: (hh, ki, 0))

    return pl.pallas_call(
        body, name=name, grid=(h, n, n),
        in_specs=[q_spec(C_QK), k_spec(C_QK), pl.BlockSpec((1, C_QK, tq), lambda hh, ki, qi: (hh, 0, ki)),
                  k_spec(C_V), q_spec(C_V), q_spec(LANES), q_spec(C_V)],
        out_specs=[pl.BlockSpec((1, C_QK, t), lambda hh, ki, qi: (hh, 0, 0)), k_spec(C_QK), k_spec(C_V)],
        out_shape=[jax.ShapeDtypeStruct((h, C_QK, t), F32), jax.ShapeDtypeStruct((h, t, C_QK), F32),
                   jax.ShapeDtypeStruct((h, t, C_V), F32)],
        scratch_shapes=[pltpu.VMEM((tq, C_QK), F32), pltpu.VMEM((tq, C_V), F32)],
        compiler_params=_params(("parallel", "arbitrary", "arbitrary")),
    )(q, k, kt, v, o, lse, do)


def _exchange(src, flips, src_idx, name):
    n = len(flips)
    _, r, c = src.shape

    def body(src_ref, dst_ref, send_sems, recv_sems):
        me = [lax.axis_index(a) for a in ALL_AXES]
        copies = []
        for kk, flip in enumerate(flips):
            peer = tuple(1 - p if f else p for p, f in zip(me, flip))
            copies.append(pltpu.make_async_remote_copy(
                src_ref=src_ref.at[src_idx[kk]], dst_ref=dst_ref.at[kk], send_sem=send_sems.at[kk],
                recv_sem=recv_sems.at[kk], device_id=peer, device_id_type=MESH))
        for cp in copies:
            cp.start()
        for cp in copies:
            cp.wait_recv()
        for cp in copies:
            cp.wait_send()

    return pl.pallas_call(
        body, name=name, in_specs=[pl.BlockSpec(memory_space=pl.ANY)], out_specs=pl.BlockSpec(memory_space=pl.ANY),
        out_shape=jax.ShapeDtypeStruct((n, r, c), src.dtype),
        scratch_shapes=[pltpu.SemaphoreType.DMA((n,)), pltpu.SemaphoreType.DMA((n,))],
    )(src)


FLIP_C = (0, 0, 1)
CHIP_FLIPS = ((0, 1, 0), (1, 0, 0), (1, 1, 0))
ALL_FLIPS = tuple((a >> 2 & 1, a >> 1 & 1, a & 1) for a in range(1, 8))


def _pick(stacked, idx):
    return lax.dynamic_index_in_dim(stacked, idx, axis=0, keepdims=False)


def _to_comm_rows(parts, dtype):
    flat = jnp.concatenate([p.reshape(-1) for p in parts]).astype(dtype)
    rows = -(-flat.size // (COMM_COLS * COMM_ROW_ALIGN)) * COMM_ROW_ALIGN
    return jnp.pad(flat, (0, rows * COMM_COLS - flat.size)).reshape(rows, COMM_COLS)


def _from_comm_rows(buf, shapes):
    flat, out, off = buf.reshape(-1), [], 0
    for s in shapes:
        size = math.prod(s)
        out.append(flat[off:off + size].reshape(s))
        off += size
    return out


def _gather_shards(buf, chip, core):
    half = buf.shape[0] // 2
    mine = lax.dynamic_slice_in_dim(buf, core * half, half, axis=0)
    over_ici = _exchange(mine[None], CHIP_FLIPS, (0, 0, 0), "gather_ici")
    over_d2d = _exchange(over_ici.reshape(1, 3 * half, -1), (FLIP_C,), (0,), "gather_d2d").reshape(over_ici.shape)
    both = jnp.stack([over_ici, over_d2d])
    remote = jnp.concatenate([_pick(both, core), _pick(both, 1 - core)], axis=1)
    by_flip = jnp.concatenate([buf[None], remote])
    return jnp.stack([_pick(by_flip, jnp.bitwise_xor(chip, a)) for a in range(4)])


def _reduce_scatter(portions, chip, core):
    half = portions.shape[1] // 2
    keep = lax.dynamic_slice_in_dim(portions, core * half, half, axis=1)
    give = lax.dynamic_slice_in_dim(portions, (1 - core) * half, half, axis=1)
    got = _exchange(give.reshape(1, 4 * half, -1), (FLIP_C,), (0,), "reduce_d2d").reshape(keep.shape)
    pair = _ew(lambda p, q: (p + q,), [keep.reshape(4 * half, -1), got.reshape(4 * half, -1)],
               [(COMM_COLS, F32)], "reduce_pair_sum")[0].reshape(keep.shape)
    out = jnp.stack([_pick(pair, jnp.bitwise_xor(chip, f)) for f in (1, 2, 3)])
    others = _exchange(out, CHIP_FLIPS, (0, 1, 2), "reduce_ici")
    total = _ew(lambda p, q, r, s: (p + q + r + s,), [_pick(pair, chip), others[0], others[1], others[2]],
                [(COMM_COLS, F32)], "reduce_chip_sum")[0]
    other_half = _exchange(total[None], (FLIP_C,), (0,), "reduce_share")[0]
    halves = jnp.stack([total, other_half])
    return jnp.concatenate([_pick(halves, core), _pick(halves, 1 - core)], axis=0)


def _all_reduce_small(buf, dev):
    got = _exchange(buf[None], ALL_FLIPS, (0,) * 7, "small_gather")
    by_flip = jnp.concatenate([buf[None], got])
    ordered = [_pick(by_flip, jnp.bitwise_xor(dev, a)) for a in range(8)]

    def fn(*t):
        s = t[0]
        for u in t[1:]:
            s = s + u
        return (s,)

    return _ew(fn, ordered, [(buf.shape[1], F32)], "small_sum")[0]


def _band_bucket_onehot(dilation, max_dist):
    i = jnp.arange(BLOCK)[:, None]
    j = jnp.arange(2 * BLOCK)[None, :]
    dist = i + BLOCK - j
    inband = (dist >= 0) & (dist <= max_dist)
    n = jnp.maximum(dist, 0) * dilation
    max_exact = NUM_BUCKETS // 2
    nf = jnp.maximum(n, 1).astype(F32)
    large = max_exact + (jnp.log(nf / max_exact) / math.log(MAX_DISTANCE / max_exact)
                         * (NUM_BUCKETS - max_exact)).astype(jnp.int32)
    bucket = jnp.where(n < max_exact, n, jnp.minimum(large, NUM_BUCKETS - 1))
    onehot = (bucket[..., None] == jnp.arange(NUM_BUCKETS)) & inband[..., None]
    return onehot.reshape(-1, NUM_BUCKETS).astype(F32), inband.reshape(-1)


def _band_bias(table, onehot, inband):
    vals = jnp.einsum("pb,bh->hp", onehot, table, precision=lax.Precision.HIGHEST)
    return jnp.where(inband[None, :], vals, NEG).reshape(-1, BLOCK, 2 * BLOCK)


BAND_VARIANTS = ((1, A_WINDOW - 1),) + tuple((dil, window // dil) for window, dil in B_BRANCHES)


def _dilate_heads(t, d):
    s, h, dh = t.shape
    return t.reshape(s // d, d, h, dh).transpose(2, 1, 0, 3).reshape(h, s, dh)


def _undilate_heads(t, d):
    h, s, dh = t.shape
    return t.reshape(h, d, s // d, dh).transpose(2, 1, 0, 3).reshape(s, h, dh)


def _rope_tables(s, reps):
    inv = ROPE_THETA ** (-jnp.arange(0, C_ROPE, 2, dtype=F32) / C_ROPE)
    ang = jnp.arange(s, dtype=F32)[:, None] * inv[None, :]
    return jnp.tile(jnp.cos(ang), (1, reps)), jnp.tile(jnp.sin(ang), (1, reps))


def _rope(t1, t2, cos, sin, name):
    return _ew(lambda a, b, c, s: (a * c - b * s, a * s + b * c), [t1, t2, cos, sin],
               [(t1.shape[1], F32)] * 2, name)


def _rope_bwd(d1, d2, cos, sin, name):
    n = len(d1)

    def fn(*t):
        a, b = sum(t[1:n], t[0]), sum(t[n + 1:2 * n], t[n])
        c, s = t[2 * n], t[2 * n + 1]
        return a * c + b * s, b * c - a * s

    return _ew(fn, list(d1) + list(d2) + [cos, sin], [(d1[0].shape[1], F32)] * 2, name)


def _even_fwd(xn, h, w_in, w_out, sinks_row, biases, tag):
    s = xn.shape[0]
    proj = _mm(xn, w_in, "nn", [BF16], f"in_proj{tag}")[0]
    pa = proj[:, :A_IN]
    qd, kd = A_Q_HEADS * HEAD_DIM, A_KV_HEADS * HEAD_DIM
    qa = pa[:, :qd].reshape(s, A_Q_HEADS, HEAD_DIM).transpose(1, 0, 2)
    ka = pa[:, qd:qd + kd].reshape(s, A_KV_HEADS, HEAD_DIM).transpose(1, 0, 2)
    va = pa[:, qd + kd:].reshape(s, A_KV_HEADS, HEAD_DIM).transpose(1, 0, 2)
    oa, lse_a = _band_fwd(qa, ka, va, biases[0], sinks_row, s // BLOCK, f"swa_fwd{tag}")
    pb = proj[:, A_IN:].reshape(s, len(B_BRANCHES), 3, B_HEADS_PER_BRANCH, HEAD_DIM)
    qkv_b, outs, lses = [], [], []
    for g, (_, dil) in enumerate(B_BRANCHES):
        qg, kg, vg = (_dilate_heads(pb[:, g, i], dil) for i in range(3))
        og, lg = _band_fwd(qg, kg, vg, biases[1 + g], None, s // dil // BLOCK, f"dil{g}_fwd{tag}")
        qkv_b.append((qg, kg, vg))
        outs.append(_undilate_heads(og, dil).reshape(s, -1))
        lses.append(jnp.repeat(_undilate_heads(lg[:, :, :1], dil).reshape(s, -1), HEAD_DIM, axis=1))

    def merge(o0, o1, o2, l0, l1, l2):
        m = jnp.maximum(jnp.maximum(l0, l1), l2)
        e0, e1, e2 = jnp.exp(l0 - m), jnp.exp(l1 - m), jnp.exp(l2 - m)
        den = e0 + e1 + e2
        return (e0 * o0 + e1 * o1 + e2 * o2) / den, m + jnp.log(den)

    width = B_HEADS_PER_BRANCH * HEAD_DIM
    out_b, lse_b = _ew(merge, outs + lses, [(width, F32)] * 2, f"dil_merge{tag}")
    cat = jnp.concatenate([oa.transpose(1, 0, 2).reshape(s, qd), out_b], axis=1).astype(BF16)
    h_mid = _mm(cat, w_out, "nn", [F32], f"out_proj{tag}", epi=_add_epi, extras=(h,))[0]
    return h_mid, (qa, ka, va, oa, lse_a, qkv_b, out_b, lse_b, cat)


def _even_bwd(dh, xn, saved, w_in, w_out, sinks_row, biases, onehots, tag):
    qa, ka, va, oa, lse_a, qkv_b, out_b, lse_b, cat = saved
    s = xn.shape[0]
    qd = A_Q_HEADS * HEAD_DIM
    g_w_out = _mm(cat, dh, "tn", [F32], f"out_proj_dw{tag}")[0]
    dcat = _mm(dh, w_out, "nt", [BF16], f"out_proj_dx{tag}")[0]
    do_a = dcat[:, :qd].reshape(s, A_Q_HEADS, HEAD_DIM).transpose(1, 0, 2)
    dqa, dka8, dva8, dbias_a, dsink = _band_bwd(qa, ka, va, oa, lse_a, do_a, biases[0], sinks_row, s // BLOCK,
                                                f"swa_bwd{tag}")
    group = A_Q_HEADS // A_KV_HEADS

    def group_sum(t):
        parts = [t.reshape(A_KV_HEADS, group, s * HEAD_DIM)[:, i] for i in range(group)]
        total = _ew(lambda p, q, r, u: (p + q + r + u,), [p.reshape(-1, COMM_COLS) for p in parts],
                    [(COMM_COLS, F32)], f"gqa_sum{tag}")[0]
        return total.reshape(A_KV_HEADS, s, HEAD_DIM).transpose(1, 0, 2).reshape(s, -1)

    dpa = [dqa.transpose(1, 0, 2).reshape(s, qd), group_sum(dka8), group_sum(dva8)]
    do_b = dcat[:, qd:].reshape(s, B_HEADS_PER_BRANCH, HEAD_DIM)
    out_b3 = out_b.reshape(s, B_HEADS_PER_BRANCH, HEAD_DIM)
    lse_b3 = lse_b.reshape(s, B_HEADS_PER_BRANCH, HEAD_DIM)[:, :, :1]
    dpb, dbias_b = [], []
    for g, (_, dil) in enumerate(B_BRANCHES):
        qg, kg, vg = qkv_b[g]
        lse_g = jnp.broadcast_to(_dilate_heads(lse_b3, dil), (B_HEADS_PER_BRANCH, s, LANES))
        dqg, dkg, dvg, dbg, _ = _band_bwd(qg, kg, vg, _dilate_heads(out_b3, dil), lse_g, _dilate_heads(do_b, dil),
                                          biases[1 + g], None, s // dil // BLOCK, f"dil{g}_bwd{tag}")
        dpb.append(jnp.stack([_undilate_heads(t, dil) for t in (dqg, dkg, dvg)], axis=1))
        dbias_b.append(dbg)
    dproj = jnp.concatenate(dpa + [jnp.stack(dpb, axis=1).reshape(s, -1)], axis=1).astype(BF16)
    g_w_in = _mm(xn, dproj, "tn", [F32], f"in_proj_dw{tag}")[0]
    dxn = _mm(dproj, w_in, "nt", [F32], f"in_proj_dx{tag}")[0]
    cols = [_mm(db.reshape(db.shape[0], -1), onehots[v][0], "nn", [F32], f"bias_buckets{v}{tag}")[0].T
            for v, db in enumerate([dbias_a] + dbias_b)]
    return dxn, g_w_in, g_w_out, jnp.concatenate(cols, axis=1), dsink[:, 0, 0]


def _mla_layer_fwd(xn, h, w_dn, q_norm, w_uq, kv_norm, w_ukv, w_o, ropes, tag):
    s = xn.shape[0]
    (cos_q, sin_q), (cos_k, sin_k) = ropes
    down = _mm(xn, w_dn, "nn", [F32], f"mla_down{tag}")[0]
    c_q, c_kv = down[:, :C_Q_RANK], down[:, C_Q_RANK:C_Q_RANK + C_KV_RANK]
    k_r = down[:, C_Q_RANK + C_KV_RANK:]
    half = C_ROPE // 2
    cqn = _rms_fwd(c_q, q_norm, f"mla_qnorm{tag}")
    ckvn = _rms_fwd(c_kv, kv_norm, f"mla_kvnorm{tag}")
    q = _mm(cqn, w_uq, "nn", [F32], f"mla_uq{tag}")[0].reshape(s, C_HEADS, C_QK)
    kv = _mm(ckvn, w_ukv, "nn", [BF16], f"mla_ukv{tag}")[0].reshape(s, C_HEADS, C_NOPE + C_V)
    q1, q2 = _rope(q[:, :, C_NOPE:C_NOPE + half].reshape(s, -1), q[:, :, C_NOPE + half:].reshape(s, -1),
                   cos_q, sin_q, f"mla_rope_q{tag}")
    k1, k2 = _rope(k_r[:, :half], k_r[:, half:], cos_k, sin_k, f"mla_rope_k{tag}")
    qh = jnp.concatenate([q[:, :, :C_NOPE], q1.reshape(s, C_HEADS, half), q2.reshape(s, C_HEADS, half)],
                         axis=2).astype(BF16).transpose(1, 0, 2)
    k_rope = jnp.broadcast_to(jnp.concatenate([k1, k2], axis=1).astype(BF16)[:, None, :], (s, C_HEADS, C_ROPE))
    kh = jnp.concatenate([kv[:, :, :C_NOPE], k_rope], axis=2).transpose(1, 0, 2)
    vh = kv[:, :, C_NOPE:].transpose(1, 0, 2)
    o, lse = _mla_fwd(qh, kh, vh, f"mla_attn_fwd{tag}")
    o2d = o.transpose(1, 0, 2).reshape(s, -1).astype(BF16)
    h_mid = _mm(o2d, w_o, "nn", [F32], f"mla_o{tag}", epi=_add_epi, extras=(h,))[0]
    return h_mid, (c_q, c_kv, cqn, ckvn, qh, kh, vh, o, lse, o2d)


def _mla_layer_bwd(dh, xn, saved, w_dn, q_norm, w_uq, kv_norm, w_ukv, w_o, ropes, tag):
    c_q, c_kv, cqn, ckvn, qh, kh, vh, o, lse, o2d = saved
    s = xn.shape[0]
    (cos_q, sin_q), (cos_k, sin_k) = ropes
    half = C_ROPE // 2
    g_w_o = _mm(o2d, dh, "tn", [F32], f"mla_o_dw{tag}")[0]
    do = _mm(dh, w_o, "nt", [BF16], f"mla_o_dx{tag}")[0].reshape(s, C_HEADS, C_V).transpose(1, 0, 2)
    dqt, dk, dv = _mla_bwd(qh, kh, kh.transpose(0, 2, 1), vh, o, lse, do, f"mla_attn_bwd{tag}")
    dq = dqt.transpose(2, 0, 1)
    dq1, dq2 = _rope_bwd([dq[:, :, C_NOPE:C_NOPE + half].reshape(s, -1)], [dq[:, :, C_NOPE + half:].reshape(s, -1)],
                         cos_q, sin_q, f"mla_rope_q_bwd{tag}")
    dq = jnp.concatenate([dq[:, :, :C_NOPE], dq1.reshape(s, C_HEADS, half), dq2.reshape(s, C_HEADS, half)],
                         axis=2).reshape(s, -1).astype(BF16)
    dk1, dk2 = _rope_bwd([dk[hh, :, C_NOPE:C_NOPE + half] for hh in range(C_HEADS)],
                         [dk[hh, :, C_NOPE + half:] for hh in range(C_HEADS)], cos_k, sin_k, f"mla_rope_k_bwd{tag}")
    dkv = jnp.concatenate([dk[:, :, :C_NOPE], dv], axis=2).transpose(1, 0, 2).reshape(s, -1).astype(BF16)
    g_w_uq = _mm(cqn, dq, "tn", [F32], f"mla_uq_dw{tag}")[0]
    dcqn = _mm(dq, w_uq, "nt", [F32], f"mla_uq_dx{tag}")[0]
    g_w_ukv = _mm(ckvn, dkv, "tn", [F32], f"mla_ukv_dw{tag}")[0]
    dckvn = _mm(dkv, w_ukv, "nt", [F32], f"mla_ukv_dx{tag}")[0]
    dc_q, g_q_norm = _rms_bwd(c_q, q_norm, dcqn, None, f"mla_qnorm_bwd{tag}")
    dc_kv, g_kv_norm = _rms_bwd(c_kv, kv_norm, dckvn, None, f"mla_kvnorm_bwd{tag}")
    ddown = jnp.concatenate([dc_q, dc_kv, dk1, dk2], axis=1).astype(BF16)
    g_w_dn = _mm(xn, ddown, "tn", [F32], f"mla_down_dw{tag}")[0]
    dxn = _mm(ddown, w_dn, "nt", [F32], f"mla_down_dx{tag}")[0]
    return dxn, g_w_dn, g_q_norm, g_w_uq, g_kv_norm, g_w_ukv, g_w_o


SHARDED = (("w_in_ab", 2), ("w_out_ab", 2), ("w_down_c", 1), ("w_uq_c", 2), ("w_ukv_c", 2), ("w_o_c", 2),
           ("w_mlp_up", 2), ("w_mlp_down", 1))
SHARDED_NORMS = ("q_norm_c", "kv_norm_c")


def kernel(x, rel_bias, attn_norm, mlp_norm, final_norm, w_in_ab, sinks, w_out_ab, w_down_c, q_norm_c, w_uq_c, kv_norm_c, w_ukv_c, w_o_c, w_mlp_up, w_mlp_down, loss_target, m_rel_bias, m_attn_norm, m_mlp_norm, m_final_norm, m_w_in_ab, m_sinks, m_w_out_ab, m_w_down_c, m_q_norm_c, m_w_uq_c, m_kv_norm_c, m_w_ukv_c, m_w_o_c, m_w_mlp_up, m_w_mlp_down, v_rel_bias, v_attn_norm, v_mlp_norm, v_final_norm, v_w_in_ab, v_sinks, v_w_out_ab, v_w_down_c, v_q_norm_c, v_w_uq_c, v_kv_norm_c, v_w_ukv_c, v_w_o_c, v_w_mlp_up, v_w_mlp_down):
    given = dict(locals())
    chip = lax.axis_index("x") * 2 + lax.axis_index("y")
    core = lax.axis_index("c")
    dev = chip * 2 + core
    depth = attn_norm.shape[0]
    s = x.shape[1]

    shards = [given[n] for n, _ in SHARDED]
    norm_shards = [given[n] for n in SHARDED_NORMS]
    packed = _to_comm_rows([t.astype(BF16) for t in shards]
                           + [lax.bitcast_convert_type(t, BF16) for t in norm_shards], BF16)
    by_chip = _gather_shards(packed, chip, core)
    shapes = [t.shape for t in shards] + [t.shape + (2,) for t in norm_shards]
    pieces = [_from_comm_rows(by_chip[a], shapes) for a in range(4)]
    full = {n: jnp.concatenate([pieces[a][i] for a in range(4)], axis=ax) for i, (n, ax) in enumerate(SHARDED)}
    for i, n in enumerate(SHARDED_NORMS):
        full[n] = jnp.concatenate([lax.bitcast_convert_type(pieces[a][len(SHARDED) + i], F32) for a in range(4)],
                                  axis=-1)

    onehots = [_band_bucket_onehot(dil, md) for dil, md in BAND_VARIANTS]
    head_cols = [(0, A_Q_HEADS)] + [(A_Q_HEADS + g * B_HEADS_PER_BRANCH, A_Q_HEADS + (g + 1) * B_HEADS_PER_BRANCH)
                                    for g in range(len(B_BRANCHES))]
    biases = [_band_bias(rel_bias[:, lo:hi], oh, inb) for (lo, hi), (oh, inb) in zip(head_cols, onehots)]
    ropes = (_rope_tables(s, C_HEADS), _rope_tables(s, 1))
    sink_rows = [jnp.broadcast_to(sinks[e][:, None, None], (A_Q_HEADS, 1, LANES)) for e in range(sinks.shape[0])]

    def odd_weights(o):
        return [full[n][o] for n in ("w_down_c", "q_norm_c", "w_uq_c", "kv_norm_c", "w_ukv_c", "w_o_c")]

    h = x[0]
    saved = []
    for l in range(depth):
        xn = _rms_fwd(h, attn_norm[l], f"attn_norm{l}")
        if l % 2 == 0:
            e = l // 2
            h_mid, mix = _even_fwd(xn, h, full["w_in_ab"][e], full["w_out_ab"][e], sink_rows[e], biases, f"_{l}")
        else:
            h_mid, mix = _mla_layer_fwd(xn, h, *odd_weights(l // 2), ropes, f"_{l}")
        xn2 = _rms_fwd(h_mid, mlp_norm[l], f"mlp_norm{l}")
        act, relu = _mm(xn2, full["w_mlp_up"][l], "nn", [BF16, BF16], f"mlp_up{l}", epi=_relu2_epi)
        h_out = _mm(act, full["w_mlp_down"][l], "nn", [F32], f"mlp_down{l}", epi=_add_epi, extras=(h_mid,))[0]
        saved.append((h, xn, mix, h_mid, xn2, act, relu))
        h = h_out
    loss_part, dh, g_final = _final_loss(h, final_norm, loss_target[0], "final_loss")
    loss = lax.psum(loss_part, ALL_AXES)

    grads = {n: [None] * given[n].shape[0] for n, _ in SHARDED}
    g_q_norm, g_kv_norm = [None] * q_norm_c.shape[0], [None] * kv_norm_c.shape[0]
    g_attn_norm, g_mlp_norm = [None] * depth, [None] * depth
    g_sinks = [None] * sinks.shape[0]
    g_rel_bias = None
    for l in range(depth - 1, -1, -1):
        h_in, xn, mix, h_mid, xn2, act, relu = saved[l]
        grads["w_mlp_down"][l] = _mm(act, dh, "tn", [F32], f"mlp_down_dw{l}")[0]
        du = _mm(dh, full["w_mlp_down"][l], "nt", [BF16], f"mlp_down_dx{l}", epi=_relu2_bwd_epi, extras=(relu,))[0]
        grads["w_mlp_up"][l] = _mm(xn2, du, "tn", [F32], f"mlp_up_dw{l}")[0]
        dxn2 = _mm(du, full["w_mlp_up"][l], "nt", [F32], f"mlp_up_dx{l}")[0]
        dh, g_mlp_norm[l] = _rms_bwd(h_mid, mlp_norm[l], dxn2, dh, f"mlp_norm_bwd{l}")
        if l % 2 == 0:
            e = l // 2
            dxn, grads["w_in_ab"][e], grads["w_out_ab"][e], g_table, g_sinks[e] = _even_bwd(
                dh, xn, mix, full["w_in_ab"][e], full["w_out_ab"][e], sink_rows[e], biases, onehots, f"_{l}")
            g_rel_bias = g_table if g_rel_bias is None else g_rel_bias + g_table
        else:
            o = l // 2
            (dxn, grads["w_down_c"][o], g_q_norm[o], grads["w_uq_c"][o], g_kv_norm[o], grads["w_ukv_c"][o],
             grads["w_o_c"][o]) = _mla_layer_bwd(dh, xn, mix, *odd_weights(o), ropes, f"_{l}")
        dh, g_attn_norm[l] = _rms_bwd(h_in, attn_norm[l], dxn, dh, f"attn_norm_bwd{l}")
    grad_x = dh[None]

    portions = []
    for a in range(4):
        parts = []
        for n, ax in SHARDED:
            g = jnp.stack(grads[n])
            size = g.shape[ax] // 4
            parts.append(lax.slice_in_dim(g, a * size, (a + 1) * size, axis=ax))
        portions.append(_to_comm_rows(parts, F32))
    reduced = _from_comm_rows(_reduce_scatter(jnp.stack(portions), chip, core), [t.shape for t in shards])
    g_shard = {n: reduced[i] for i, (n, _) in enumerate(SHARDED)}

    small = [jnp.stack(g_attn_norm), jnp.stack(g_mlp_norm), g_final, g_rel_bias, jnp.stack(g_sinks),
             jnp.stack(g_q_norm), jnp.stack(g_kv_norm)]
    small_shapes = [t.shape for t in small]
    summed = _from_comm_rows(_all_reduce_small(_to_comm_rows(small, F32)[:16], dev), small_shapes)
    g_small = dict(zip(("attn_norm", "mlp_norm", "final_norm", "rel_bias", "sinks"), summed[:5]))
    for n, g in zip(SHARDED_NORMS, summed[5:]):
        size = g.shape[1] // 4
        g_shard[n] = lax.dynamic_slice_in_dim(g, chip * size, size, axis=1)

    order = ["rel_bias", "attn_norm", "mlp_norm", "final_norm", "w_in_ab", "sinks", "w_out_ab", "w_down_c",
             "q_norm_c", "w_uq_c", "kv_norm_c", "w_ukv_c", "w_o_c", "w_mlp_up", "w_mlp_down"]
    g_all = {**g_small, **g_shard}
    deltas, new_m, new_v = [], [], []
    for n in order:
        d, mn, vn = _adamw(given[n], g_all[n], given["m_" + n], given["v_" + n], f"adamw_{n}")
        deltas.append(d)
        new_m.append(mn)
        new_v.append(vn)
    return (loss, grad_x, *[g_all[n] for n in order], *deltas, *new_m, *new_v)
```

```python
import math

import jax
import jax.numpy as jnp
from jax import lax
from jax.experimental import pallas as pl
from jax.experimental.pallas import tpu as pltpu

F32 = jnp.float32
BF16 = jnp.bfloat16
MESH = pl.DeviceIdType.MESH
ALL_AXES = ("x", "y", "c")

EPS = 1e-6
NEG = -1e30
BLOCK = 128
HEAD_DIM = 64
A_Q_HEADS, A_KV_HEADS = 8, 2
A_WINDOW = 128
B_BRANCHES = ((128, 1), (512, 4), (2048, 16))
B_HEADS_PER_BRANCH = 4
NUM_BUCKETS, MAX_DISTANCE = 32, 2048
A_IN = (A_Q_HEADS + 2 * A_KV_HEADS) * HEAD_DIM
C_HEADS, C_NOPE, C_ROPE, C_V = 8, 64, 32, 64
C_Q_RANK, C_KV_RANK = 384, 256
ROPE_THETA = 10000.0
ADAM_LR, ADAM_B1, ADAM_B2, ADAM_EPS, ADAM_WD, ADAM_STEP = 0.001, 0.9, 0.999, 1e-08, 0.01, 10

V7X_VMEM_LIMIT_BYTES = 56 * 1024 * 1024
LANES = 128
COMM_COLS = 1024
COMM_ROW_ALIGN = 1024

NT = (((1,), (1,)), ((), ()))
NN = (((1,), (0,)), ((), ()))
TN = (((0,), (0,)), ((), ()))
DIMS = {"nn": NN, "nt": NT, "tn": TN}


def _params(sem):
    return pltpu.CompilerParams(dimension_semantics=sem, vmem_limit_bytes=V7X_VMEM_LIMIT_BYTES)


def _divisor_tile(n, limit, align):
    if n <= limit:
        return n
    t = (limit // align) * align
    while t >= align:
        if n % t == 0:
            return t
        t -= align
    return n


def _ew(fn, ins, outs, name, acc_outs=(), target_bytes=6 << 20):
    rows = max(a.shape[0] for a in ins)

    def vmem_row_bytes(cols, dtype):
        return -(-cols // LANES) * LANES * jnp.dtype(dtype).itemsize

    per_row = sum(vmem_row_bytes(a.shape[1], a.dtype) for a in ins if a.shape[0] == rows)
    per_row += sum(vmem_row_bytes(c, d) for c, d in outs)
    tr = _divisor_tile(rows, max(16, target_bytes // max(per_row, 1)), 16)
    n_in, n_out = len(ins), len(outs)

    def body(*refs):
        res = fn(*[r[...] for r in refs[:n_in]])
        for r, v in zip(refs[n_in:n_in + n_out], res[:n_out]):
            r[...] = v.astype(r.dtype)
        if acc_outs:
            acc_refs = refs[n_in + n_out:]

            @pl.when(pl.program_id(0) == 0)
            def _():
                for r in acc_refs:
                    r[...] = jnp.zeros_like(r)

            for r, v in zip(acc_refs, res[n_out:]):
                r[...] += v

    def spec(a):
        if a.shape[0] == rows:
            return pl.BlockSpec((tr, a.shape[1]), lambda i: (i, 0))
        return pl.BlockSpec((1, a.shape[1]), lambda i: (0, 0))

    out_shape = [jax.ShapeDtypeStruct((rows, c), d) for c, d in outs]
    out_shape += [jax.ShapeDtypeStruct((1, c), F32) for c in acc_outs]
    out_specs = [pl.BlockSpec((tr, c), lambda i: (i, 0)) for c, _ in outs]
    out_specs += [pl.BlockSpec((1, c), lambda i: (0, 0)) for c in acc_outs]
    return pl.pallas_call(
        body, name=name, grid=(rows // tr,), in_specs=[spec(a) for a in ins], out_specs=out_specs,
        out_shape=out_shape, compiler_params=_params(("arbitrary",)),
    )(*ins)


def _rms_fwd(x, g, name):
    def fn(xv, gv):
        return ((xv * lax.rsqrt(jnp.mean(xv * xv, axis=-1, keepdims=True) + EPS)) * gv,)

    return _ew(fn, [x, g.reshape(1, -1)], [(x.shape[1], BF16)], name)[0]


def _rms_bwd(x, g, dy, add, name, also_bf16=False):
    def fn(xv, gv, dyv, *rest):
        rstd = lax.rsqrt(jnp.mean(xv * xv, axis=-1, keepdims=True) + EPS)
        xh = xv * rstd
        dyg = dyv.astype(F32) * gv
        dx = rstd * (dyg - xh * jnp.mean(dyg * xh, axis=-1, keepdims=True))
        if rest:
            dx = dx + rest[0]
        return (dx,) * (2 if also_bf16 else 1) + (jnp.sum(dyv.astype(F32) * xh, axis=0, keepdims=True),)

    ins = [x, g.reshape(1, -1), dy] + ([] if add is None else [add])
    outs = [(x.shape[1], F32)] + ([(x.shape[1], BF16)] if also_bf16 else [])
    *dx, dg = _ew(fn, ins, outs, name, acc_outs=(x.shape[1],))
    return (*dx, dg[0])


def _final_loss(h, g, target, name):
    d = h.shape[1]

    def fn(xv, gv, tv):
        rstd = lax.rsqrt(jnp.mean(xv * xv, axis=-1, keepdims=True) + EPS)
        xh = xv * rstd
        err = xh * gv - tv
        part = 0.5 * jnp.sum(jnp.mean(err * err, axis=-1, keepdims=True), axis=0, keepdims=True)
        dy = err * (1.0 / d)
        dyg = dy * gv
        dx = rstd * (dyg - xh * jnp.mean(dyg * xh, axis=-1, keepdims=True))
        return dx, dx, jnp.broadcast_to(part, (1, LANES)), jnp.sum(dy * xh, axis=0, keepdims=True)

    dx, dx16, loss, dg = _ew(fn, [h, g.reshape(1, -1), target], [(d, F32), (d, BF16)], name, acc_outs=(LANES, d))
    return loss[0, 0], dx, dx16, dg[0]


def _adamw(w, g, m, v, name):
    def fn(wv, gv, mv, vv):
        mn = ADAM_B1 * mv + (1.0 - ADAM_B1) * gv
        vn = ADAM_B2 * vv + (1.0 - ADAM_B2) * jnp.square(gv)
        m_hat = mn / (1.0 - ADAM_B1 ** ADAM_STEP)
        v_hat = vn / (1.0 - ADAM_B2 ** ADAM_STEP)
        return -ADAM_LR * (m_hat / (jnp.sqrt(v_hat) + ADAM_EPS) + ADAM_WD * wv), mn, vn

    shape = w.shape
    cols = shape[-1] if w.ndim > 1 else w.size
    view = [t.reshape(-1, cols) for t in (w, g, m, v)]
    return [t.reshape(shape) for t in _ew(fn, view, [(cols, F32)] * 3, name)]


def _mm(a, b, dims, outs, name, epi=None, extras=(), tm=1024, tn=1024, tk=1024):
    if dims == "nn":
        (m, k), n = a.shape, b.shape[1]
    elif dims == "nt":
        (m, k), n = a.shape, b.shape[0]
    else:
        (k, m), n = a.shape, b.shape[1]
    tm, tn, tk = _divisor_tile(m, tm, LANES), _divisor_tile(n, tn, LANES), _divisor_tile(k, tk, LANES)
    nk = k // tk
    n_ex, n_out = len(extras), len(outs)

    def body(a_ref, b_ref, *rest):
        ex_refs, out_refs = rest[:n_ex], rest[n_ex:n_ex + n_out]

        def finish(acc):
            res = epi(acc, *[r[...] for r in ex_refs]) if epi else (acc,)
            for r, v in zip(out_refs, res):
                r[...] = v.astype(r.dtype)

        part = lax.dot_general(a_ref[...].astype(BF16), b_ref[...].astype(BF16), DIMS[dims],
                               preferred_element_type=F32)
        if nk == 1:
            finish(part)
        else:
            acc_ref = rest[-1]
            kk = pl.program_id(2)

            @pl.when(kk == 0)
            def _():
                acc_ref[...] = part

            @pl.when(kk > 0)
            def _():
                acc_ref[...] += part

            @pl.when(kk == nk - 1)
            def _():
                finish(acc_ref[...])

    if dims == "nn":
        a_spec = pl.BlockSpec((tm, tk), lambda i, j, kk: (i, kk))
        b_spec = pl.BlockSpec((tk, tn), lambda i, j, kk: (kk, j))
    elif dims == "nt":
        a_spec = pl.BlockSpec((tm, tk), lambda i, j, kk: (i, kk))
        b_spec = pl.BlockSpec((tn, tk), lambda i, j, kk: (j, kk))
    else:
        a_spec = pl.BlockSpec((tk, tm), lambda i, j, kk: (kk, i))
        b_spec = pl.BlockSpec((tk, tn), lambda i, j, kk: (kk, j))
    tile = pl.BlockSpec((tm, tn), lambda i, j, kk: (i, j))
    return pl.pallas_call(
        body, name=name, grid=(m // tm, n // tn, nk),
        in_specs=[a_spec, b_spec] + [tile] * n_ex, out_specs=[tile] * n_out,
        out_shape=[jax.ShapeDtypeStruct((m, n), d) for d in outs],
        scratch_shapes=[pltpu.VMEM((tm, tn), F32)] if nk > 1 else [],
        compiler_params=_params(("parallel", "parallel", "arbitrary")),
    )(a, b, *extras)


def _add_epi(acc, res):
    return (acc + res,)


def _relu2_epi(acc):
    r = jnp.maximum(acc, 0.0)
    return r * r, r


def _relu2_bwd_epi(acc, r):
    return (acc * (2.0 * r.astype(F32)),)


def _band_geometry(t, blocks_per_seq):
    rows = min(1024, blocks_per_seq * BLOCK)
    nb = rows // BLOCK
    assert blocks_per_seq % nb == 0 and t % rows == 0
    return rows, nb, t // rows


def _band_logits(qj, kk, bias, first):
    s = lax.dot_general(qj, kk, NT, preferred_element_type=F32) * (HEAD_DIM ** -0.5) + bias
    if first is not None:
        col = lax.broadcasted_iota(jnp.int32, s.shape, 1)
        s = jnp.where(col < jnp.where(first, BLOCK, 0), NEG, s)
    return s


def _band_fwd(q, k, v, bias, sinks, blocks_per_seq, name):
    hq, t, dh = q.shape
    group = hq // k.shape[0]
    rows, nb, nchunks = _band_geometry(t, blocks_per_seq)
    has_sink = sinks is not None

    def body(q_ref, kc_ref, kp_ref, vc_ref, vp_ref, bias_ref, *rest):
        o_ref, lse_ref = rest[-2:]
        i = pl.program_id(1)
        bias_v = bias_ref[0]
        sink = rest[0][0][:, :1] if has_sink else None
        for j in range(nb):
            cur = slice(j * BLOCK, (j + 1) * BLOCK)
            prev = slice((j - 1) * BLOCK, j * BLOCK)
            kk = jnp.concatenate([kp_ref[0] if j == 0 else kc_ref[0, prev, :], kc_ref[0, cur, :]], axis=0)
            vv = jnp.concatenate([vp_ref[0] if j == 0 else vc_ref[0, prev, :], vc_ref[0, cur, :]], axis=0)
            first = lax.rem(i * nb, blocks_per_seq) == 0 if j == 0 else None
            s = _band_logits(q_ref[0, cur, :], kk, bias_v, first)
            m = jnp.max(s, axis=1, keepdims=True)
            if has_sink:
                m = jnp.maximum(m, sink)
            p = jnp.exp(s - m)
            l = jnp.sum(p, axis=1, keepdims=True)
            if has_sink:
                l = l + jnp.exp(sink - m)
            acc = jnp.dot(p.astype(BF16), vv, preferred_element_type=F32)
            o_ref[0, cur, :] = acc / l
            lse_ref[0, cur, :] = jnp.broadcast_to(m + jnp.log(l), (BLOCK, dh))

    cur_q = pl.BlockSpec((1, rows, dh), lambda h, i: (h, i, 0))
    cur_kv = pl.BlockSpec((1, rows, dh), lambda h, i: (h // group, i, 0))
    prev_kv = pl.BlockSpec((1, BLOCK, dh), lambda h, i: (h // group, jnp.maximum(i * nb - 1, 0), 0))
    in_specs = [cur_q, cur_kv, prev_kv, cur_kv, prev_kv, pl.BlockSpec((1, BLOCK, 2 * BLOCK), lambda h, i: (h, 0, 0))]
    ins = [q, k, k, v, v, bias]
    if has_sink:
        in_specs.append(pl.BlockSpec((1, 1, LANES), lambda h, i: (h, 0, 0)))
        ins.append(sinks)
    return pl.pallas_call(
        body, name=name, grid=(hq, nchunks), in_specs=in_specs,
        out_specs=[cur_q, cur_q],
        out_shape=[jax.ShapeDtypeStruct((hq, t, dh), F32), jax.ShapeDtypeStruct((hq, t, dh), F32)],
        compiler_params=_params(("parallel", "arbitrary")),
    )(*ins)


def _band_bwd(q, k, v, o, lse, do, bias, sinks, blocks_per_seq, name):
    hq, t, dh = q.shape
    group = hq // k.shape[0]
    rows, nb, nchunks = _band_geometry(t, blocks_per_seq)
    has_sink = sinks is not None
    scale = HEAD_DIM ** -0.5

    def body(q_ref, kc_ref, kp_ref, vc_ref, vp_ref, o_ref, lse_ref, do_ref, bias_ref, *rest):
        dq_ref, dk_ref, dv_ref, dbias_ref, dsink_ref, dk_carry, dv_carry = rest[-7:]
        step = pl.program_id(1)
        chunk = nchunks - 1 - step
        bias_v = bias_ref[0]
        sink = rest[0][0][:, :1] if has_sink else None

        @pl.when(step == 0)
        def _():
            dk_carry[...] = jnp.zeros_like(dk_carry)
            dv_carry[...] = jnp.zeros_like(dv_carry)
            dbias_ref[...] = jnp.zeros_like(dbias_ref)
            dsink_ref[...] = jnp.zeros_like(dsink_ref)

        dks = [jnp.zeros((BLOCK, dh), F32) for _ in range(nb + 1)]
        dvs = [jnp.zeros((BLOCK, dh), F32) for _ in range(nb + 1)]
        dks[nb] = dk_carry[...]
        dvs[nb] = dv_carry[...]
        for j in range(nb - 1, -1, -1):
            cur = slice(j * BLOCK, (j + 1) * BLOCK)
            prev = slice((j - 1) * BLOCK, j * BLOCK)
            kk = jnp.concatenate([kp_ref[0] if j == 0 else kc_ref[0, prev, :], kc_ref[0, cur, :]], axis=0)
            vv = jnp.concatenate([vp_ref[0] if j == 0 else vc_ref[0, prev, :], vc_ref[0, cur, :]], axis=0)
            first = lax.rem(chunk * nb, blocks_per_seq) == 0 if j == 0 else None
            qj, doj = q_ref[0, cur, :], do_ref[0, cur, :]
            lse_j = lse_ref[0, cur, :][:, :1]
            p = jnp.exp(_band_logits(qj, kk, bias_v, first) - lse_j)
            dp = lax.dot_general(doj, vv, NT, preferred_element_type=F32)
            delta = jnp.sum(doj.astype(F32) * o_ref[0, cur, :], axis=1, keepdims=True)
            ds = p * (dp - delta)
            dbias_ref[0] += ds
            if has_sink:
                dsink = -jnp.sum(jnp.exp(sink - lse_j) * delta, axis=0, keepdims=True)
                dsink_ref[0] += jnp.broadcast_to(dsink, (1, LANES))
            dsb = (ds * scale).astype(BF16)
            dq_ref[0, cur, :] = jnp.dot(dsb, kk, preferred_element_type=F32)
            dkk = lax.dot_general(dsb, qj, TN, preferred_element_type=F32)
            dvv = lax.dot_general(p.astype(BF16), doj, TN, preferred_element_type=F32)
            dks[j] += dkk[:BLOCK]
            dks[j + 1] += dkk[BLOCK:]
            dvs[j] += dvv[:BLOCK]
            dvs[j + 1] += dvv[BLOCK:]
        for j in range(nb):
            cur = slice(j * BLOCK, (j + 1) * BLOCK)
            dk_ref[0, cur, :] = dks[j + 1]
            dv_ref[0, cur, :] = dvs[j + 1]
        dk_carry[...] = dks[0]
        dv_carry[...] = dvs[0]

    def rev(i):
        return nchunks - 1 - i

    cur_q = pl.BlockSpec((1, rows, dh), lambda h, i: (h, rev(i), 0))
    cur_kv = pl.BlockSpec((1, rows, dh), lambda h, i: (h // group, rev(i), 0))
    prev_kv = pl.BlockSpec((1, BLOCK, dh), lambda h, i: (h // group, jnp.maximum(rev(i) * nb - 1, 0), 0))
    cur_lse = cur_q
    per_head_bias = pl.BlockSpec((1, BLOCK, 2 * BLOCK), lambda h, i: (h, 0, 0))
    per_head_row = pl.BlockSpec((1, 1, LANES), lambda h, i: (h, 0, 0))
    in_specs = [cur_q, cur_kv, prev_kv, cur_kv, prev_kv, cur_q, cur_lse, cur_q, per_head_bias]
    ins = [q, k, k, v, v, o, lse, do, bias]
    if has_sink:
        in_specs.append(per_head_row)
        ins.append(sinks)
    full = jax.ShapeDtypeStruct((hq, t, dh), F32)
    return pl.pallas_call(
        body, name=name, grid=(hq, nchunks), in_specs=in_specs,
        out_specs=[cur_q, cur_q, cur_q, per_head_bias, per_head_row],
        out_shape=[full, full, full, jax.ShapeDtypeStruct((hq, BLOCK, 2 * BLOCK), F32),
                   jax.ShapeDtypeStruct((hq, 1, LANES), F32)],
        scratch_shapes=[pltpu.VMEM((BLOCK, dh), F32), pltpu.VMEM((BLOCK, dh), F32)],
        compiler_params=_params(("parallel", "arbitrary")),
    )(*ins)


C_QK = C_NOPE + C_ROPE
C_SCALE = C_QK ** -0.5
LOG2E = math.log2(math.e)
C_EXP2 = C_SCALE * LOG2E
CAUSAL_SUB = 256


def _causal_tile(t):
    return min(1024, t)


def _mla_fwd(q, k, v1, name):
    h, t, _ = q.shape
    tq = _causal_tile(t)
    n = t // tq
    sub = min(CAUSAL_SUB, tq)

    def body(q_ref, k_ref, v_ref, o_ref, lse_ref, m_scr, acc_scr):
        qi, ki = pl.program_id(1), pl.program_id(2)

        @pl.when(ki == 0)
        def _():
            m_scr[...] = jnp.full_like(m_scr, NEG)
            acc_scr[...] = jnp.zeros_like(acc_scr)

        def tile(diagonal):
            for r in range(tq // sub):
                rows = slice(r * sub, (r + 1) * sub)
                nk = (r + 1) * sub if diagonal else tq
                s = lax.dot_general(q_ref[0, rows, :], k_ref[0, :nk, :], NT, preferred_element_type=F32)
                if diagonal:
                    qpos = r * sub + lax.broadcasted_iota(jnp.int32, s.shape, 0)
                    kpos = lax.broadcasted_iota(jnp.int32, s.shape, 1)
                    s = jnp.where(kpos <= qpos, s, NEG)
                m_prev = m_scr[rows, :1]
                m_new = jnp.maximum(m_prev, jnp.max(s, axis=1, keepdims=True))
                alpha = jnp.exp2((m_prev - m_new) * C_EXP2)
                p = jnp.exp2((s - m_new) * C_EXP2)
                pv = jnp.dot(p.astype(BF16), v_ref[0, :nk, :], preferred_element_type=F32)
                acc_scr[rows, :] = alpha * acc_scr[rows, :] + pv
                m_scr[rows, :] = jnp.broadcast_to(m_new, (sub, LANES))

        @pl.when(ki < qi)
        def _():
            tile(False)

        @pl.when(ki == qi)
        def _():
            tile(True)
            l = acc_scr[:, C_V:C_V + 1]
            o_ref[0] = acc_scr[:, :C_V] / l
            lse_ref[0] = jnp.broadcast_to(m_scr[:, :1] * C_SCALE + jnp.log(l), lse_ref.shape[1:])

    def kv_spec(d):
        return pl.BlockSpec((1, tq, d), lambda hh, qi, ki: (hh, jnp.minimum(ki, qi), 0))

    return pl.pallas_call(
        body, name=name, grid=(h, n, n),
        in_specs=[pl.BlockSpec((1, tq, C_QK), lambda hh, qi, ki: (hh, qi, 0)), kv_spec(C_QK), kv_spec(LANES)],
        out_specs=[pl.BlockSpec((1, tq, C_V), lambda hh, qi, ki: (hh, qi, 0)),
                   pl.BlockSpec((1, tq, LANES), lambda hh, qi, ki: (hh, qi, 0))],
        out_shape=[jax.ShapeDtypeStruct((h, t, C_V), F32), jax.ShapeDtypeStruct((h, t, LANES), F32)],
        scratch_shapes=[pltpu.VMEM((tq, LANES), F32), pltpu.VMEM((tq, LANES), F32)],
        compiler_params=_params(("parallel", "arbitrary", "arbitrary")),
    )(q, k, v1)


def _mla_bwd(q, k, kt, v1, o, lse, do, name):
    h, t, _ = q.shape
    tq = _causal_tile(t)
    n = t // tq
    sub = min(CAUSAL_SUB, tq)

    def body(q_ref, k_ref, kt_ref, v_ref, o_ref, lse_ref, do_ref, dqt_ref, dk_ref, dv_ref, dk_acc, dv_acc):
        ki, qi = pl.program_id(1), pl.program_id(2)

        @pl.when(qi == 0)
        def _():
            dk_acc[...] = jnp.zeros_like(dk_acc)
            dv_acc[...] = jnp.zeros_like(dv_acc)

        @pl.when(jnp.logical_and(ki == 0, qi == 0))
        def _():
            dqt_ref[...] = jnp.zeros_like(dqt_ref)

        def tile(diagonal):
            for c in range(tq // sub):
                cols = slice(c * sub, (c + 1) * sub)
                nk = (c + 1) * sub if diagonal else tq
                qc, doc = q_ref[0, cols, :], do_ref[0, cols, :]
                st = lax.dot_general(k_ref[0, :nk, :], qc, NT, preferred_element_type=F32)
                lse2 = jnp.transpose(lse_ref[0, cols, :])[:1] * LOG2E
                pt = jnp.exp2(st * C_EXP2 - lse2)
                if diagonal:
                    kpos = lax.broadcasted_iota(jnp.int32, st.shape, 0)
                    qpos = c * sub + lax.broadcasted_iota(jnp.int32, st.shape, 1)
                    pt = jnp.where(kpos <= qpos, pt, 0.0)
                dpt = lax.dot_general(v_ref[0, :nk, :C_V], doc, NT, preferred_element_type=F32)
                delta = jnp.sum(doc.astype(F32) * o_ref[0, cols, :], axis=1, keepdims=True)
                delta_row = jnp.transpose(jnp.broadcast_to(delta, (sub, LANES)))[:1]
                dst = (pt * (dpt - delta_row)).astype(BF16)
                dv_acc[:nk, :] += jnp.dot(pt.astype(BF16), doc, preferred_element_type=F32)
                dk_acc[:nk, :] += jnp.dot(dst, qc, preferred_element_type=F32)
                out_cols = pl.ds(pl.multiple_of(qi * tq + c * sub, sub), sub)
                dqt_ref[0, :, out_cols] += jnp.dot(kt_ref[0, :, :nk], dst, preferred_element_type=F32) * C_SCALE

        @pl.when(qi > ki)
        def _():
            tile(False)

        @pl.when(qi == ki)
        def _():
            tile(True)

        @pl.when(qi == n - 1)
        def _():
            dk_ref[0] = dk_acc[...] * C_SCALE
            dv_ref[0] = dv_acc[...]

    def q_spec(d):
        return pl.BlockSpec((1, tq, d), lambda hh, ki, qi: (hh, jnp.maximum(qi, ki), 0))

    def k_spec(d):
        return pl.BlockSpec((1, tq, d), lambda hh, ki, qi: (hh, ki, 0))

    return pl.pallas_call(
        body, name=name, grid=(h, n, n),
        in_specs=[q_spec(C_QK), k_spec(C_QK), pl.BlockSpec((1, C_QK, tq), lambda hh, ki, qi: (hh, 0, ki)),
                  k_spec(LANES), q_spec(C_V), q_spec(LANES), q_spec(C_V)],
        out_specs=[pl.BlockSpec((1, C_QK, t), lambda hh, ki, qi: (hh, 0, 0)), k_spec(C_QK), k_spec(C_V)],
        out_shape=[jax.ShapeDtypeStruct((h, C_QK, t), F32), jax.ShapeDtypeStruct((h, t, C_QK), F32),
                   jax.ShapeDtypeStruct((h, t, C_V), F32)],
        scratch_shapes=[pltpu.VMEM((tq, C_QK), F32), pltpu.VMEM((tq, C_V), F32)],
        compiler_params=_params(("parallel", "arbitrary", "arbitrary")),
    )(q, k, kt, v1, o, lse, do)


def _exchange(src, flips, src_idx, name):
    n = len(flips)
    _, r, c = src.shape

    def body(src_ref, dst_ref, send_sems, recv_sems):
        me = [lax.axis_index(a) for a in ALL_AXES]
        copies = []
        for kk, flip in enumerate(flips):
            peer = tuple(1 - p if f else p for p, f in zip(me, flip))
            copies.append(pltpu.make_async_remote_copy(
                src_ref=src_ref.at[src_idx[kk]], dst_ref=dst_ref.at[kk], send_sem=send_sems.at[kk],
                recv_sem=recv_sems.at[kk], device_id=peer, device_id_type=MESH))
        for cp in copies:
            cp.start()
        for cp in copies:
            cp.wait_recv()
        for cp in copies:
            cp.wait_send()

    return pl.pallas_call(
        body, name=name, in_specs=[pl.BlockSpec(memory_space=pl.ANY)], out_specs=pl.BlockSpec(memory_space=pl.ANY),
        out_shape=jax.ShapeDtypeStruct((n, r, c), src.dtype),
        scratch_shapes=[pltpu.SemaphoreType.DMA((n,)), pltpu.SemaphoreType.DMA((n,))],
    )(src)


FLIP_C = (0, 0, 1)
CHIP_FLIPS = ((0, 1, 0), (1, 0, 0), (1, 1, 0))
ALL_FLIPS = tuple((a >> 2 & 1, a >> 1 & 1, a & 1) for a in range(1, 8))


def _pick(stacked, idx):
    return lax.dynamic_index_in_dim(stacked, idx, axis=0, keepdims=False)


def _to_comm_rows(parts, dtype):
    flat = jnp.concatenate([p.reshape(-1) for p in parts]).astype(dtype)
    rows = -(-flat.size // (COMM_COLS * COMM_ROW_ALIGN)) * COMM_ROW_ALIGN
    return jnp.pad(flat, (0, rows * COMM_COLS - flat.size)).reshape(rows, COMM_COLS)


def _from_comm_rows(buf, shapes):
    flat, out, off = buf.reshape(-1), [], 0
    for s in shapes:
        size = math.prod(s)
        out.append(flat[off:off + size].reshape(s))
        off += size
    return out


def _gather_shards(buf, chip, core):
    half = buf.shape[0] // 2
    mine = lax.dynamic_slice_in_dim(buf, core * half, half, axis=0)
    over_ici = _exchange(mine[None], CHIP_FLIPS, (0, 0, 0), "gather_ici")
    over_d2d = _exchange(over_ici.reshape(1, 3 * half, -1), (FLIP_C,), (0,), "gather_d2d").reshape(over_ici.shape)
    both = jnp.stack([over_ici, over_d2d])
    remote = jnp.concatenate([_pick(both, core), _pick(both, 1 - core)], axis=1)
    by_flip = jnp.concatenate([buf[None], remote])
    return jnp.stack([_pick(by_flip, jnp.bitwise_xor(chip, a)) for a in range(4)])


def _reduce_scatter(portions, chip, core):
    half = portions.shape[1] // 2
    keep = lax.dynamic_slice_in_dim(portions, core * half, half, axis=1)
    give = lax.dynamic_slice_in_dim(portions, (1 - core) * half, half, axis=1)
    got = _exchange(give.reshape(1, 4 * half, -1), (FLIP_C,), (0,), "reduce_d2d").reshape(keep.shape)
    pair = _ew(lambda p, q: (p + q,), [keep.reshape(4 * half, -1), got.reshape(4 * half, -1)],
               [(COMM_COLS, F32)], "reduce_pair_sum")[0].reshape(keep.shape)
    out = jnp.stack([_pick(pair, jnp.bitwise_xor(chip, f)) for f in (1, 2, 3)]).astype(BF16)
    others = _exchange(out, CHIP_FLIPS, (0, 1, 2), "reduce_ici")
    total = _ew(lambda p, q, r, s: (p + q.astype(F32) + r.astype(F32) + s.astype(F32),),
                [_pick(pair, chip), others[0], others[1], others[2]],
                [(COMM_COLS, F32)], "reduce_chip_sum")[0]
    other_half = _exchange(total[None], (FLIP_C,), (0,), "reduce_share")[0]
    halves = jnp.stack([total, other_half])
    return jnp.concatenate([_pick(halves, core), _pick(halves, 1 - core)], axis=0)


def _all_reduce_small(buf, dev):
    got = _exchange(buf[None], ALL_FLIPS, (0,) * 7, "small_gather")
    by_flip = jnp.concatenate([buf[None], got])
    ordered = [_pick(by_flip, jnp.bitwise_xor(dev, a)) for a in range(8)]

    def fn(*t):
        s = t[0]
        for u in t[1:]:
            s = s + u
        return (s,)

    return _ew(fn, ordered, [(buf.shape[1], F32)], "small_sum")[0]


def _band_bucket_onehot(dilation, max_dist):
    i = jnp.arange(BLOCK)[:, None]
    j = jnp.arange(2 * BLOCK)[None, :]
    dist = i + BLOCK - j
    inband = (dist >= 0) & (dist <= max_dist)
    n = jnp.maximum(dist, 0) * dilation
    max_exact = NUM_BUCKETS // 2
    nf = jnp.maximum(n, 1).astype(F32)
    large = max_exact + (jnp.log(nf / max_exact) / math.log(MAX_DISTANCE / max_exact)
                         * (NUM_BUCKETS - max_exact)).astype(jnp.int32)
    bucket = jnp.where(n < max_exact, n, jnp.minimum(large, NUM_BUCKETS - 1))
    onehot = (bucket[..., None] == jnp.arange(NUM_BUCKETS)) & inband[..., None]
    return onehot.reshape(-1, NUM_BUCKETS).astype(F32), inband.reshape(-1)


def _band_bias(table, onehot, inband):
    vals = jnp.einsum("pb,bh->hp", onehot, table, precision=lax.Precision.HIGHEST)
    return jnp.where(inband[None, :], vals, NEG).reshape(-1, BLOCK, 2 * BLOCK)


BAND_VARIANTS = ((1, A_WINDOW - 1),) + tuple((dil, window // dil) for window, dil in B_BRANCHES)


LAYOUT_TILE_BYTES = 4 << 20


def _layout_rows(length, row_bytes):
    return _divisor_tile(length, max(16, LAYOUT_TILE_BYTES // row_bytes), 16)


def _split_heads(items, w, dil, name):
    s = items[0][0].shape[0]
    length = s // dil
    row_bytes = sum(-(-n * w // LANES) * LANES * (x.dtype.itemsize + jnp.dtype(d).itemsize) for x, _, n, d in items)
    tr = _layout_rows(length, row_bytes)
    nt = length // tr

    def body(*refs):
        for (x_ref, o_ref), (_, _, n, _) in zip(zip(refs[:len(items)], refs[len(items):]), items):
            for j in range(n):
                o_ref[j] = x_ref[:, j * w:(j + 1) * w].astype(o_ref.dtype)

    in_specs, out_specs, out_shape, views = [], [], [], []
    for x, first, n, d in items:
        bw, width = n * w, x.shape[1]
        assert bw % LANES == 0 and first % bw == 0 and (dil == 1 or width % bw == 0)
        in_specs.append(pl.BlockSpec((tr, bw), lambda r, i, c0=first // bw, wb=width // bw: (i, r * wb + c0)))
        out_specs.append(pl.BlockSpec((n, tr, w), lambda r, i: (0, r * nt + i, 0)))
        out_shape.append(jax.ShapeDtypeStruct((n, s, w), d))
        views.append(x.reshape(length, dil * width))
    return pl.pallas_call(
        body, name=name, grid=(dil, nt), in_specs=in_specs, out_specs=out_specs, out_shape=out_shape,
        compiler_params=_params(("parallel", "parallel")),
    )(*views)


def _merge_heads(items, dil, name, group_sum=1):
    s, w = items[0][0].shape[1:]
    length = s // dil
    row_bytes = sum(t.shape[0] * LANES * t.dtype.itemsize + t.shape[0] * w * jnp.dtype(d).itemsize for t, d in items)
    tr = _layout_rows(length, row_bytes)
    nt = length // tr

    def body(*refs):
        for t_ref, o_ref in zip(refs[:len(items)], refs[len(items):]):
            for j in range(t_ref.shape[0] // group_sum):
                v = t_ref[j * group_sum]
                for g in range(1, group_sum):
                    v = v + t_ref[j * group_sum + g]
                o_ref[:, j * w:(j + 1) * w] = v.astype(o_ref.dtype)

    in_specs, out_specs, out_shape = [], [], []
    for t, d in items:
        n = t.shape[0]
        bw = n // group_sum * w
        assert bw % LANES == 0
        in_specs.append(pl.BlockSpec((n, tr, w), lambda r, i: (0, r * nt + i, 0)))
        out_specs.append(pl.BlockSpec((tr, bw), lambda r, i: (i, r)))
        out_shape.append(jax.ShapeDtypeStruct((length, dil * bw), d))
    outs = pl.pallas_call(
        body, name=name, grid=(dil, nt), in_specs=in_specs, out_specs=out_specs, out_shape=out_shape,
        compiler_params=_params(("parallel", "parallel")),
    )(*[t for t, _ in items])
    return [o.reshape(s, -1) for o in outs]


ROPE_HALF = C_ROPE // 2
ROPE_PERIOD = 3 * LANES


def _rope_tables(s):
    inv = ROPE_THETA ** (-jnp.arange(0, C_ROPE, 2, dtype=F32) / C_ROPE)
    ang = jnp.arange(s, dtype=F32)[:, None] * inv[None, :]
    cos, sin = jnp.cos(ang), jnp.sin(ang)
    one, zero = jnp.ones((s, C_NOPE), F32), jnp.zeros((s, C_NOPE), F32)
    z16 = jnp.zeros((s, ROPE_HALF), F32)
    reps = ROPE_PERIOD // C_QK
    keep = jnp.tile(jnp.concatenate([one, cos, cos], axis=1), (1, reps))
    from_above = jnp.tile(jnp.concatenate([zero, -sin, z16], axis=1), (1, reps))
    from_below = jnp.tile(jnp.concatenate([zero, z16, sin], axis=1), (1, reps))
    return (cos, sin), (keep, from_above, from_below)


def _rope_rows(x, tables, inverse, name, dtype):
    width = x.shape[1]
    reps = width // ROPE_PERIOD
    sign = -1.0 if inverse else 1.0

    def fn(xv, keep, above, below):
        keep, above, below = (jnp.tile(t, (1, reps)) for t in (keep, above, below))
        up = pltpu.roll(xv, width - ROPE_HALF, 1)
        down = pltpu.roll(xv, ROPE_HALF, 1)
        return (xv * keep + sign * (up * above + down * below),)

    return _ew(fn, [x, *tables], [(width, dtype)], name)[0]


def _rotate_half_pairs(a, b, cos, sin, inverse):
    if inverse:
        return a * cos + b * sin, b * cos - a * sin
    return a * cos - b * sin, a * sin + b * cos


def _split_kv(kv, down, cos, sin, name):
    s = kv.shape[0]
    h = kv.shape[1] // (C_NOPE + C_V)
    tr = _layout_rows(s, 8 * kv.shape[1])
    r0 = C_Q_RANK + C_KV_RANK

    def body(kv_ref, down_ref, cos_ref, sin_ref, k_ref, v_ref):
        k1, k2 = _rotate_half_pairs(down_ref[:, r0:r0 + ROPE_HALF], down_ref[:, r0 + ROPE_HALF:r0 + C_ROPE],
                                    cos_ref[...], sin_ref[...], False)
        lane = lax.broadcasted_iota(jnp.int32, (tr, LANES - C_V), 1)
        tail = jnp.where(lane == 0, 1.0, 0.0).astype(BF16)
        for j in range(h):
            base = j * (C_NOPE + C_V)
            k_ref[j, :, :C_NOPE] = kv_ref[:, base:base + C_NOPE]
            k_ref[j, :, C_NOPE:C_NOPE + ROPE_HALF] = k1.astype(BF16)
            k_ref[j, :, C_NOPE + ROPE_HALF:] = k2.astype(BF16)
            v_ref[j, :, :C_V] = kv_ref[:, base + C_NOPE:base + C_NOPE + C_V]
            v_ref[j, :, C_V:] = tail

    def rows(width):
        return pl.BlockSpec((tr, width), lambda i: (i, 0))

    return pl.pallas_call(
        body, name=name, grid=(s // tr,),
        in_specs=[rows(kv.shape[1]), rows(down.shape[1]), rows(ROPE_HALF), rows(ROPE_HALF)],
        out_specs=[pl.BlockSpec((h, tr, C_QK), lambda i: (0, i, 0)), pl.BlockSpec((h, tr, LANES), lambda i: (0, i, 0))],
        out_shape=[jax.ShapeDtypeStruct((h, s, C_QK), BF16), jax.ShapeDtypeStruct((h, s, LANES), BF16)],
        compiler_params=_params(("parallel",)),
    )(kv, down, cos, sin)


def _merge_kv_bwd(dk, dv, cos, sin, name):
    h, s, _ = dk.shape
    tr = _layout_rows(s, 8 * h * LANES)

    def body(dk_ref, dv_ref, cos_ref, sin_ref, dkv_ref, dkr_ref):
        rot = dk_ref[0, :, C_NOPE:]
        for j in range(h):
            base = j * (C_NOPE + C_V)
            dkv_ref[:, base:base + C_NOPE] = dk_ref[j, :, :C_NOPE].astype(BF16)
            dkv_ref[:, base + C_NOPE:base + C_NOPE + C_V] = dv_ref[j].astype(BF16)
            if j:
                rot = rot + dk_ref[j, :, C_NOPE:]
        d1, d2 = _rotate_half_pairs(rot[:, :ROPE_HALF], rot[:, ROPE_HALF:], cos_ref[...], sin_ref[...], True)
        dkr_ref[:, :ROPE_HALF] = d1
        dkr_ref[:, ROPE_HALF:] = d2

    def rows(width):
        return pl.BlockSpec((tr, width), lambda i: (i, 0))

    return pl.pallas_call(
        body, name=name, grid=(s // tr,),
        in_specs=[pl.BlockSpec((h, tr, C_QK), lambda i: (0, i, 0)), pl.BlockSpec((h, tr, C_V), lambda i: (0, i, 0)),
                  rows(ROPE_HALF), rows(ROPE_HALF)],
        out_specs=[rows(h * (C_NOPE + C_V)), rows(C_ROPE)],
        out_shape=[jax.ShapeDtypeStruct((s, h * (C_NOPE + C_V)), BF16), jax.ShapeDtypeStruct((s, C_ROPE), F32)],
        compiler_params=_params(("parallel",)),
    )(dk, dv, cos, sin)


def _even_fwd(xn, h, w_in, w_out, sinks_row, biases, tag):
    s = xn.shape[0]
    proj = _mm(xn, w_in, "nn", [BF16], f"in_proj{tag}")[0]
    qd, kd = A_Q_HEADS * HEAD_DIM, A_KV_HEADS * HEAD_DIM
    qa, ka, va = _split_heads([(proj, 0, A_Q_HEADS, BF16), (proj, qd, A_KV_HEADS, BF16),
                               (proj, qd + kd, A_KV_HEADS, BF16)], HEAD_DIM, 1, f"swa_split{tag}")
    oa, lse_a = _band_fwd(qa, ka, va, biases[0], sinks_row, s // BLOCK, f"swa_fwd{tag}")
    out_a = _merge_heads([(oa, BF16)], 1, f"swa_merge{tag}")[0]
    width = B_HEADS_PER_BRANCH * HEAD_DIM
    qkv_b, outs, lses = [], [], []
    for g, (_, dil) in enumerate(B_BRANCHES):
        base = A_IN + g * 3 * width
        qkv = _split_heads([(proj, base + i * width, B_HEADS_PER_BRANCH, BF16) for i in range(3)], HEAD_DIM, dil,
                           f"dil{g}_split{tag}")
        og, lg = _band_fwd(*qkv, biases[1 + g], None, s // dil // BLOCK, f"dil{g}_fwd{tag}")
        qkv_b.append(qkv)
        merged = _merge_heads([(og, F32), (lg, F32)], dil, f"dil{g}_merge{tag}")
        outs.append(merged[0])
        lses.append(merged[1])

    def merge(o0, o1, o2, l0, l1, l2):
        m = jnp.maximum(jnp.maximum(l0, l1), l2)
        e0, e1, e2 = jnp.exp(l0 - m), jnp.exp(l1 - m), jnp.exp(l2 - m)
        den = e0 + e1 + e2
        out = (e0 * o0 + e1 * o1 + e2 * o2) / den
        return out, m + jnp.log(den), out

    out_b, lse_b, out_b16 = _ew(merge, outs + lses, [(width, F32), (width, F32), (width, BF16)], f"dil_merge{tag}")
    cat = jnp.concatenate([out_a, out_b16], axis=1)
    h_mid = _mm(cat, w_out, "nn", [F32], f"out_proj{tag}", epi=_add_epi, extras=(h,))[0]
    return h_mid, (qa, ka, va, oa, lse_a, qkv_b, out_b, lse_b, cat)


def _even_bwd(dh, xn, saved, w_in, w_out, sinks_row, biases, onehots, tag):
    qa, ka, va, oa, lse_a, qkv_b, out_b, lse_b, cat = saved
    s = xn.shape[0]
    qd = A_Q_HEADS * HEAD_DIM
    g_w_out = _mm(cat, dh, "tn", [F32], f"out_proj_dw{tag}")[0]
    dcat = _mm(dh, w_out, "nt", [BF16], f"out_proj_dx{tag}")[0]
    do_a = _split_heads([(dcat, 0, A_Q_HEADS, BF16)], HEAD_DIM, 1, f"swa_do_split{tag}")[0]
    dqa, dka8, dva8, dbias_a, dsink = _band_bwd(qa, ka, va, oa, lse_a, do_a, biases[0], sinks_row, s // BLOCK,
                                                f"swa_bwd{tag}")
    pieces = _merge_heads([(dqa, BF16)], 1, f"swa_dq_merge{tag}")
    pieces += _merge_heads([(dka8, BF16), (dva8, BF16)], 1, f"swa_dkv_merge{tag}", group_sum=A_Q_HEADS // A_KV_HEADS)
    dbias_b = []
    for g, (_, dil) in enumerate(B_BRANCHES):
        do_g, out_g, lse_g = _split_heads([(dcat, qd, B_HEADS_PER_BRANCH, BF16), (out_b, 0, B_HEADS_PER_BRANCH, F32),
                                           (lse_b, 0, B_HEADS_PER_BRANCH, F32)], HEAD_DIM, dil, f"dil{g}_do_split{tag}")
        dqg, dkg, dvg, dbg, _ = _band_bwd(*qkv_b[g], out_g, lse_g, do_g, biases[1 + g], None, s // dil // BLOCK,
                                          f"dil{g}_bwd{tag}")
        pieces += _merge_heads([(dqg, BF16), (dkg, BF16), (dvg, BF16)], dil, f"dil{g}_dqkv_merge{tag}")
        dbias_b.append(dbg)
    dproj = jnp.concatenate(pieces, axis=1)
    g_w_in = _mm(xn, dproj, "tn", [F32], f"in_proj_dw{tag}")[0]
    dxn = _mm(dproj, w_in, "nt", [F32], f"in_proj_dx{tag}")[0]
    cols = [_mm(db.reshape(db.shape[0], -1), onehots[v][0], "nn", [F32], f"bias_buckets{v}{tag}")[0].T
            for v, db in enumerate([dbias_a] + dbias_b)]
    return dxn, g_w_in, g_w_out, jnp.concatenate(cols, axis=1), dsink[:, 0, 0]


def _mla_layer_fwd(xn, h, w_dn, q_norm, w_uq, kv_norm, w_ukv, w_o, ropes, tag):
    (cos, sin), q_tables = ropes
    down = _mm(xn, w_dn, "nn", [F32], f"mla_down{tag}")[0]
    c_q, c_kv = down[:, :C_Q_RANK], down[:, C_Q_RANK:C_Q_RANK + C_KV_RANK]
    cqn = _rms_fwd(c_q, q_norm, f"mla_qnorm{tag}")
    ckvn = _rms_fwd(c_kv, kv_norm, f"mla_kvnorm{tag}")
    q = _mm(cqn, w_uq, "nn", [F32], f"mla_uq{tag}")[0]
    kv = _mm(ckvn, w_ukv, "nn", [BF16], f"mla_ukv{tag}")[0]
    qh = _split_heads([(_rope_rows(q, q_tables, False, f"mla_rope_q{tag}", BF16), 0, C_HEADS, BF16)], C_QK, 1,
                      f"mla_q_split{tag}")[0]
    kh, v1h = _split_kv(kv, down, cos, sin, f"mla_kv_split{tag}")
    o, lse = _mla_fwd(qh, kh, v1h, f"mla_attn_fwd{tag}")
    o2d = _merge_heads([(o, BF16)], 1, f"mla_o_merge{tag}")[0]
    h_mid = _mm(o2d, w_o, "nn", [F32], f"mla_o{tag}", epi=_add_epi, extras=(h,))[0]
    return h_mid, (c_q, c_kv, cqn, ckvn, qh, kh, v1h, o, lse, o2d)


def _mla_layer_bwd(dh, xn, saved, w_dn, q_norm, w_uq, kv_norm, w_ukv, w_o, ropes, tag):
    c_q, c_kv, cqn, ckvn, qh, kh, v1h, o, lse, o2d = saved
    (cos, sin), q_tables = ropes
    g_w_o = _mm(o2d, dh, "tn", [F32], f"mla_o_dw{tag}")[0]
    do2d = _mm(dh, w_o, "nt", [BF16], f"mla_o_dx{tag}")[0]
    do = _split_heads([(do2d, 0, C_HEADS, BF16)], C_V, 1, f"mla_do_split{tag}")[0]
    dqt, dk, dv = _mla_bwd(qh, kh, kh.transpose(0, 2, 1), v1h, o, lse, do, f"mla_attn_bwd{tag}")
    dq_roped = _merge_heads([(dqt.transpose(0, 2, 1), F32)], 1, f"mla_dq_merge{tag}")[0]
    dq = _rope_rows(dq_roped, q_tables, True, f"mla_rope_q_bwd{tag}", BF16)
    dkv, dk_rope = _merge_kv_bwd(dk, dv, cos, sin, f"mla_dkv_merge{tag}")
    g_w_uq = _mm(cqn, dq, "tn", [F32], f"mla_uq_dw{tag}")[0]
    dcqn = _mm(dq, w_uq, "nt", [F32], f"mla_uq_dx{tag}")[0]
    g_w_ukv = _mm(ckvn, dkv, "tn", [F32], f"mla_ukv_dw{tag}")[0]
    dckvn = _mm(dkv, w_ukv, "nt", [F32], f"mla_ukv_dx{tag}")[0]
    dc_q, g_q_norm = _rms_bwd(c_q, q_norm, dcqn, None, f"mla_qnorm_bwd{tag}")
    dc_kv, g_kv_norm = _rms_bwd(c_kv, kv_norm, dckvn, None, f"mla_kvnorm_bwd{tag}")
    ddown = jnp.concatenate([dc_q, dc_kv, dk_rope], axis=1).astype(BF16)
    g_w_dn = _mm(xn, ddown, "tn", [F32], f"mla_down_dw{tag}")[0]
    dxn = _mm(ddown, w_dn, "nt", [F32], f"mla_down_dx{tag}")[0]
    return dxn, g_w_dn, g_q_norm, g_w_uq, g_kv_norm, g_w_ukv, g_w_o


SHARDED = (("w_in_ab", 2), ("w_out_ab", 2), ("w_down_c", 1), ("w_uq_c", 2), ("w_ukv_c", 2), ("w_o_c", 2),
           ("w_mlp_up", 2), ("w_mlp_down", 1))
SHARDED_NORMS = ("q_norm_c", "kv_norm_c")


def kernel(x, rel_bias, attn_norm, mlp_norm, final_norm, w_in_ab, sinks, w_out_ab, w_down_c, q_norm_c, w_uq_c, kv_norm_c, w_ukv_c, w_o_c, w_mlp_up, w_mlp_down, loss_target, m_rel_bias, m_attn_norm, m_mlp_norm, m_final_norm, m_w_in_ab, m_sinks, m_w_out_ab, m_w_down_c, m_q_norm_c, m_w_uq_c, m_kv_norm_c, m_w_ukv_c, m_w_o_c, m_w_mlp_up, m_w_mlp_down, v_rel_bias, v_attn_norm, v_mlp_norm, v_final_norm, v_w_in_ab, v_sinks, v_w_out_ab, v_w_down_c, v_q_norm_c, v_w_uq_c, v_kv_norm_c, v_w_ukv_c, v_w_o_c, v_w_mlp_up, v_w_mlp_down):
    given = dict(locals())
    chip = lax.axis_index("x") * 2 + lax.axis_index("y")
    core = lax.axis_index("c")
    dev = chip * 2 + core
    depth = attn_norm.shape[0]
    s = x.shape[1]

    shards = [given[n] for n, _ in SHARDED]
    norm_shards = [given[n] for n in SHARDED_NORMS]
    packed = _to_comm_rows([t.astype(BF16) for t in shards]
                           + [lax.bitcast_convert_type(t, BF16) for t in norm_shards], BF16)
    by_chip = _gather_shards(packed, chip, core)
    shapes = [t.shape for t in shards] + [t.shape + (2,) for t in norm_shards]
    pieces = [_from_comm_rows(by_chip[a], shapes) for a in range(4)]
    full = {n: jnp.concatenate([pieces[a][i] for a in range(4)], axis=ax) for i, (n, ax) in enumerate(SHARDED)}
    for i, n in enumerate(SHARDED_NORMS):
        full[n] = jnp.concatenate([lax.bitcast_convert_type(pieces[a][len(SHARDED) + i], F32) for a in range(4)],
                                  axis=-1)

    onehots = [_band_bucket_onehot(dil, md) for dil, md in BAND_VARIANTS]
    head_cols = [(0, A_Q_HEADS)] + [(A_Q_HEADS + g * B_HEADS_PER_BRANCH, A_Q_HEADS + (g + 1) * B_HEADS_PER_BRANCH)
                                    for g in range(len(B_BRANCHES))]
    biases = [_band_bias(rel_bias[:, lo:hi], oh, inb) for (lo, hi), (oh, inb) in zip(head_cols, onehots)]
    ropes = _rope_tables(s)
    sink_rows = [jnp.broadcast_to(sinks[e][:, None, None], (A_Q_HEADS, 1, LANES)) for e in range(sinks.shape[0])]

    def odd_weights(o):
        return [full[n][o] for n in ("w_down_c", "q_norm_c", "w_uq_c", "kv_norm_c", "w_ukv_c", "w_o_c")]

    h = x[0]
    saved = []
    for l in range(depth):
        xn = _rms_fwd(h, attn_norm[l], f"attn_norm{l}")
        if l % 2 == 0:
            e = l // 2
            h_mid, mix = _even_fwd(xn, h, full["w_in_ab"][e], full["w_out_ab"][e], sink_rows[e], biases, f"_{l}")
        else:
            h_mid, mix = _mla_layer_fwd(xn, h, *odd_weights(l // 2), ropes, f"_{l}")
        xn2 = _rms_fwd(h_mid, mlp_norm[l], f"mlp_norm{l}")
        act, relu = _mm(xn2, full["w_mlp_up"][l], "nn", [BF16, BF16], f"mlp_up{l}", epi=_relu2_epi)
        h_out = _mm(act, full["w_mlp_down"][l], "nn", [F32], f"mlp_down{l}", epi=_add_epi, extras=(h_mid,))[0]
        saved.append((h, xn, mix, h_mid, xn2, act, relu))
        h = h_out
    loss_part, dh, dh16, g_final = _final_loss(h, final_norm, loss_target[0], "final_loss")
    loss = lax.psum(loss_part, ALL_AXES)

    grads = {n: [None] * given[n].shape[0] for n, _ in SHARDED}
    g_q_norm, g_kv_norm = [None] * q_norm_c.shape[0], [None] * kv_norm_c.shape[0]
    g_attn_norm, g_mlp_norm = [None] * depth, [None] * depth
    g_sinks = [None] * sinks.shape[0]
    g_rel_bias = None
    for l in range(depth - 1, -1, -1):
        h_in, xn, mix, h_mid, xn2, act, relu = saved[l]
        grads["w_mlp_down"][l] = _mm(act, dh16, "tn", [F32], f"mlp_down_dw{l}")[0]
        du = _mm(dh16, full["w_mlp_down"][l], "nt", [BF16], f"mlp_down_dx{l}", epi=_relu2_bwd_epi, extras=(relu,))[0]
        grads["w_mlp_up"][l] = _mm(xn2, du, "tn", [F32], f"mlp_up_dw{l}")[0]
        dxn2 = _mm(du, full["w_mlp_up"][l], "nt", [F32], f"mlp_up_dx{l}")[0]
        dh, dh16, g_mlp_norm[l] = _rms_bwd(h_mid, mlp_norm[l], dxn2, dh, f"mlp_norm_bwd{l}", also_bf16=True)
        if l % 2 == 0:
            e = l // 2
            dxn, grads["w_in_ab"][e], grads["w_out_ab"][e], g_table, g_sinks[e] = _even_bwd(
                dh16, xn, mix, full["w_in_ab"][e], full["w_out_ab"][e], sink_rows[e], biases, onehots, f"_{l}")
            g_rel_bias = g_table if g_rel_bias is None else g_rel_bias + g_table
        else:
            o = l // 2
            (dxn, grads["w_down_c"][o], g_q_norm[o], grads["w_uq_c"][o], g_kv_norm[o], grads["w_ukv_c"][o],
             grads["w_o_c"][o]) = _mla_layer_bwd(dh16, xn, mix, *odd_weights(o), ropes, f"_{l}")
        if l:
            dh, dh16, g_attn_norm[l] = _rms_bwd(h_in, attn_norm[l], dxn, dh, f"attn_norm_bwd{l}", also_bf16=True)
        else:
            dh, g_attn_norm[l] = _rms_bwd(h_in, attn_norm[l], dxn, dh, f"attn_norm_bwd{l}")
    grad_x = dh[None]

    portions = []
    for a in range(4):
        parts = []
        for n, ax in SHARDED:
            g = jnp.stack(grads[n])
            size = g.shape[ax] // 4
            parts.append(lax.slice_in_dim(g, a * size, (a + 1) * size, axis=ax))
        portions.append(_to_comm_rows(parts, F32))
    reduced = _from_comm_rows(_reduce_scatter(jnp.stack(portions), chip, core), [t.shape for t in shards])
    g_shard = {n: reduced[i] for i, (n, _) in enumerate(SHARDED)}

    small = [jnp.stack(g_attn_norm), jnp.stack(g_mlp_norm), g_final, g_rel_bias, jnp.stack(g_sinks),
             jnp.stack(g_q_norm), jnp.stack(g_kv_norm)]
    small_shapes = [t.shape for t in small]
    summed = _from_comm_rows(_all_reduce_small(_to_comm_rows(small, F32)[:16], dev), small_shapes)
    g_small = dict(zip(("attn_norm", "mlp_norm", "final_norm", "rel_bias", "sinks"), summed[:5]))
    for n, g in zip(SHARDED_NORMS, summed[5:]):
        size = g.shape[1] // 4
        g_shard[n] = lax.dynamic_slice_in_dim(g, chip * size, size, axis=1)

    order = ["rel_bias", "attn_norm", "mlp_norm", "final_norm", "w_in_ab", "sinks", "w_out_ab", "w_down_c",
             "q_norm_c", "w_uq_c", "kv_norm_c", "w_ukv_c", "w_o_c", "w_mlp_up", "w_mlp_down"]
    g_all = {**g_small, **g_shard}
    deltas, new_m, new_v = [], [], []
    for n in order:
        d, mn, vn = _adamw(given[n], g_all[n], given["m_" + n], given["v_" + n], f"adamw_{n}")
        deltas.append(d)
        new_m.append(mn)
        new_v.append(vn)
    return (loss, grad_x, *[g_all[n] for n in order], *deltas, *new_m, *new_v)
```

```python
import math

import jax
import jax.numpy as jnp
from jax import lax
from jax.experimental import pallas as pl
from jax.experimental.pallas import tpu as pltpu

F32 = jnp.float32
BF16 = jnp.bfloat16
MESH = pl.DeviceIdType.MESH
ALL_AXES = ("x", "y", "c")

EPS = 1e-6
NEG = -1e30
BLOCK = 128
HEAD_DIM = 64
A_Q_HEADS, A_KV_HEADS = 8, 2
A_WINDOW = 128
B_BRANCHES = ((128, 1), (512, 4), (2048, 16))
B_HEADS_PER_BRANCH = 4
NUM_BUCKETS, MAX_DISTANCE = 32, 2048
A_IN = (A_Q_HEADS + 2 * A_KV_HEADS) * HEAD_DIM
C_HEADS, C_NOPE, C_ROPE, C_V = 8, 64, 32, 64
C_Q_RANK, C_KV_RANK = 384, 256
ROPE_THETA = 10000.0
ADAM_LR, ADAM_B1, ADAM_B2, ADAM_EPS, ADAM_WD, ADAM_STEP = 0.001, 0.9, 0.999, 1e-08, 0.01, 10

V7X_VMEM_LIMIT_BYTES = 56 * 1024 * 1024
LANES = 128
COMM_COLS = 1024
COMM_ROW_ALIGN = 1024

NT = (((1,), (1,)), ((), ()))
NN = (((1,), (0,)), ((), ()))
TN = (((0,), (0,)), ((), ()))
DIMS = {"nn": NN, "nt": NT, "tn": TN}


def _params(sem):
    return pltpu.CompilerParams(dimension_semantics=sem, vmem_limit_bytes=V7X_VMEM_LIMIT_BYTES)


def _divisor_tile(n, limit, align):
    if n <= limit:
        return n
    t = (limit // align) * align
    while t >= align:
        if n % t == 0:
            return t
        t -= align
    return n


def _ew(fn, ins, outs, name, acc_outs=(), target_bytes=6 << 20):
    rows = max(a.shape[0] for a in ins)

    def vmem_row_bytes(cols, dtype):
        return -(-cols // LANES) * LANES * jnp.dtype(dtype).itemsize

    per_row = sum(vmem_row_bytes(a.shape[1], a.dtype) for a in ins if a.shape[0] == rows)
    per_row += sum(vmem_row_bytes(c, d) for c, d in outs)
    tr = _divisor_tile(rows, max(16, target_bytes // max(per_row, 1)), 16)
    n_in, n_out = len(ins), len(outs)

    def body(*refs):
        res = fn(*[r[...] for r in refs[:n_in]])
        for r, v in zip(refs[n_in:n_in + n_out], res[:n_out]):
            r[...] = v.astype(r.dtype)
        if acc_outs:
            acc_refs = refs[n_in + n_out:]

            @pl.when(pl.program_id(0) == 0)
            def _():
                for r in acc_refs:
                    r[...] = jnp.zeros_like(r)

            for r, v in zip(acc_refs, res[n_out:]):
                r[...] += v

    def spec(a):
        if a.shape[0] == rows:
            return pl.BlockSpec((tr, a.shape[1]), lambda i: (i, 0))
        return pl.BlockSpec((1, a.shape[1]), lambda i: (0, 0))

    out_shape = [jax.ShapeDtypeStruct((rows, c), d) for c, d in outs]
    out_shape += [jax.ShapeDtypeStruct((1, c), F32) for c in acc_outs]
    out_specs = [pl.BlockSpec((tr, c), lambda i: (i, 0)) for c, _ in outs]
    out_specs += [pl.BlockSpec((1, c), lambda i: (0, 0)) for c in acc_outs]
    return pl.pallas_call(
        body, name=name, grid=(rows // tr,), in_specs=[spec(a) for a in ins], out_specs=out_specs,
        out_shape=out_shape, compiler_params=_params(("arbitrary",)),
    )(*ins)


def _rms_fwd(x, g, name):
    def fn(xv, gv):
        return ((xv * lax.rsqrt(jnp.mean(xv * xv, axis=-1, keepdims=True) + EPS)) * gv,)

    return _ew(fn, [x, g.reshape(1, -1)], [(x.shape[1], BF16)], name)[0]


def _rms_bwd(x, g, dy, add, name, also_bf16=False):
    def fn(xv, gv, dyv, *rest):
        rstd = lax.rsqrt(jnp.mean(xv * xv, axis=-1, keepdims=True) + EPS)
        xh = xv * rstd
        dyg = dyv.astype(F32) * gv
        dx = rstd * (dyg - xh * jnp.mean(dyg * xh, axis=-1, keepdims=True))
        if rest:
            dx = dx + rest[0]
        return (dx,) * (2 if also_bf16 else 1) + (jnp.sum(dyv.astype(F32) * xh, axis=0, keepdims=True),)

    ins = [x, g.reshape(1, -1), dy] + ([] if add is None else [add])
    outs = [(x.shape[1], F32)] + ([(x.shape[1], BF16)] if also_bf16 else [])
    *dx, dg = _ew(fn, ins, outs, name, acc_outs=(x.shape[1],))
    return (*dx, dg[0])


def _final_loss(h, g, target, name):
    d = h.shape[1]

    def fn(xv, gv, tv):
        rstd = lax.rsqrt(jnp.mean(xv * xv, axis=-1, keepdims=True) + EPS)
        xh = xv * rstd
        err = xh * gv - tv
        part = 0.5 * jnp.sum(jnp.mean(err * err, axis=-1, keepdims=True), axis=0, keepdims=True)
        dy = err * (1.0 / d)
        dyg = dy * gv
        dx = rstd * (dyg - xh * jnp.mean(dyg * xh, axis=-1, keepdims=True))
        return dx, dx, jnp.broadcast_to(part, (1, LANES)), jnp.sum(dy * xh, axis=0, keepdims=True)

    dx, dx16, loss, dg = _ew(fn, [h, g.reshape(1, -1), target], [(d, F32), (d, BF16)], name, acc_outs=(LANES, d))
    return loss[0, 0], dx, dx16, dg[0]


def _adamw(w, g, m, v, name):
    def fn(wv, gv, mv, vv):
        mn = ADAM_B1 * mv + (1.0 - ADAM_B1) * gv
        vn = ADAM_B2 * vv + (1.0 - ADAM_B2) * jnp.square(gv)
        m_hat = mn / (1.0 - ADAM_B1 ** ADAM_STEP)
        v_hat = vn / (1.0 - ADAM_B2 ** ADAM_STEP)
        return -ADAM_LR * (m_hat / (jnp.sqrt(v_hat) + ADAM_EPS) + ADAM_WD * wv), mn, vn

    shape = w.shape
    cols = shape[-1] if w.ndim > 1 else w.size
    view = [t.reshape(-1, cols) for t in (w, g, m, v)]
    return [t.reshape(shape) for t in _ew(fn, view, [(cols, F32)] * 3, name)]


MM_VMEM_BUDGET_BYTES = 40 << 20


def _mm(a, b, dims, outs, name, epi=None, extras=(), tm=2048, tn=1024, tk=1024):
    if dims == "nn":
        (m, k), n = a.shape, b.shape[1]
    elif dims == "nt":
        (m, k), n = a.shape, b.shape[0]
    else:
        (k, m), n = a.shape, b.shape[1]
    tn, tk = _divisor_tile(n, tn, LANES), _divisor_tile(k, tk, LANES)

    def tile_bytes(rows):
        per_out = sum(jnp.dtype(d).itemsize for d in outs) + sum(e.dtype.itemsize for e in extras)
        return 2 * (rows * tk * a.dtype.itemsize + tk * tn * b.dtype.itemsize + rows * tn * per_out) + 4 * rows * tn

    tm = _divisor_tile(m, tm, LANES)
    while tile_bytes(tm) > MM_VMEM_BUDGET_BYTES and tm % (2 * LANES) == 0:
        tm //= 2
    nk = k // tk
    n_ex, n_out = len(extras), len(outs)

    def body(a_ref, b_ref, *rest):
        ex_refs, out_refs = rest[:n_ex], rest[n_ex:n_ex + n_out]

        def finish(acc):
            res = epi(acc, *[r[...] for r in ex_refs]) if epi else (acc,)
            for r, v in zip(out_refs, res):
                r[...] = v.astype(r.dtype)

        part = lax.dot_general(a_ref[...].astype(BF16), b_ref[...].astype(BF16), DIMS[dims],
                               preferred_element_type=F32)
        if nk == 1:
            finish(part)
        else:
            acc_ref = rest[-1]
            kk = pl.program_id(2)

            @pl.when(kk == 0)
            def _():
                acc_ref[...] = part

            @pl.when(kk > 0)
            def _():
                acc_ref[...] += part

            @pl.when(kk == nk - 1)
            def _():
                finish(acc_ref[...])

    if dims == "nn":
        a_spec = pl.BlockSpec((tm, tk), lambda i, j, kk: (i, kk))
        b_spec = pl.BlockSpec((tk, tn), lambda i, j, kk: (kk, j))
    elif dims == "nt":
        a_spec = pl.BlockSpec((tm, tk), lambda i, j, kk: (i, kk))
        b_spec = pl.BlockSpec((tn, tk), lambda i, j, kk: (j, kk))
    else:
        a_spec = pl.BlockSpec((tk, tm), lambda i, j, kk: (kk, i))
        b_spec = pl.BlockSpec((tk, tn), lambda i, j, kk: (kk, j))
    tile = pl.BlockSpec((tm, tn), lambda i, j, kk: (i, j))
    return pl.pallas_call(
        body, name=name, grid=(m // tm, n // tn, nk),
        in_specs=[a_spec, b_spec] + [tile] * n_ex, out_specs=[tile] * n_out,
        out_shape=[jax.ShapeDtypeStruct((m, n), d) for d in outs],
        scratch_shapes=[pltpu.VMEM((tm, tn), F32)] if nk > 1 else [],
        compiler_params=_params(("parallel", "parallel", "arbitrary")),
    )(a, b, *extras)


def _add_epi(acc, res):
    return (acc + res,)


def _relu2_epi(acc):
    r = jnp.maximum(acc, 0.0)
    return r * r, r


def _relu2_bwd_epi(acc, r):
    return (acc * (2.0 * r.astype(F32)),)


def _band_geometry(t, blocks_per_seq):
    rows = min(1024, blocks_per_seq * BLOCK)
    nb = rows // BLOCK
    assert blocks_per_seq % nb == 0 and t % rows == 0
    return rows, nb, t // rows


def _band_logits(qj, kk, bias, first):
    s = lax.dot_general(qj, kk, NT, preferred_element_type=F32) * (HEAD_DIM ** -0.5) + bias
    if first is not None:
        col = lax.broadcasted_iota(jnp.int32, s.shape, 1)
        s = jnp.where(col < jnp.where(first, BLOCK, 0), NEG, s)
    return s


def _band_fwd(q, k, v, bias, sinks, blocks_per_seq, name):
    hq, t, dh = q.shape
    group = hq // k.shape[0]
    rows, nb, nchunks = _band_geometry(t, blocks_per_seq)
    has_sink = sinks is not None

    def body(q_ref, kc_ref, kp_ref, vc_ref, vp_ref, bias_ref, *rest):
        o_ref, lse_ref = rest[-2:]
        i = pl.program_id(1)
        bias_v = bias_ref[0]
        sink = rest[0][0][:, :1] if has_sink else None
        for j in range(nb):
            cur = slice(j * BLOCK, (j + 1) * BLOCK)
            prev = slice((j - 1) * BLOCK, j * BLOCK)
            kk = jnp.concatenate([kp_ref[0] if j == 0 else kc_ref[0, prev, :], kc_ref[0, cur, :]], axis=0)
            vv = jnp.concatenate([vp_ref[0] if j == 0 else vc_ref[0, prev, :], vc_ref[0, cur, :]], axis=0)
            first = lax.rem(i * nb, blocks_per_seq) == 0 if j == 0 else None
            s = _band_logits(q_ref[0, cur, :], kk, bias_v, first)
            m = jnp.max(s, axis=1, keepdims=True)
            if has_sink:
                m = jnp.maximum(m, sink)
            p = jnp.exp(s - m)
            l = jnp.sum(p, axis=1, keepdims=True)
            if has_sink:
                l = l + jnp.exp(sink - m)
            acc = jnp.dot(p.astype(BF16), vv, preferred_element_type=F32)
            o_ref[0, cur, :] = acc / l
            lse_ref[0, cur, :] = jnp.broadcast_to(m + jnp.log(l), (BLOCK, dh))

    cur_q = pl.BlockSpec((1, rows, dh), lambda h, i: (h, i, 0))
    cur_kv = pl.BlockSpec((1, rows, dh), lambda h, i: (h // group, i, 0))
    prev_kv = pl.BlockSpec((1, BLOCK, dh), lambda h, i: (h // group, jnp.maximum(i * nb - 1, 0), 0))
    in_specs = [cur_q, cur_kv, prev_kv, cur_kv, prev_kv, pl.BlockSpec((1, BLOCK, 2 * BLOCK), lambda h, i: (h, 0, 0))]
    ins = [q, k, k, v, v, bias]
    if has_sink:
        in_specs.append(pl.BlockSpec((1, 1, LANES), lambda h, i: (h, 0, 0)))
        ins.append(sinks)
    return pl.pallas_call(
        body, name=name, grid=(hq, nchunks), in_specs=in_specs,
        out_specs=[cur_q, cur_q],
        out_shape=[jax.ShapeDtypeStruct((hq, t, dh), F32), jax.ShapeDtypeStruct((hq, t, dh), F32)],
        compiler_params=_params(("parallel", "arbitrary")),
    )(*ins)


def _band_bwd(q, k, v, o, lse, do, bias, sinks, blocks_per_seq, name):
    hq, t, dh = q.shape
    group = hq // k.shape[0]
    rows, nb, nchunks = _band_geometry(t, blocks_per_seq)
    has_sink = sinks is not None
    scale = HEAD_DIM ** -0.5

    def body(q_ref, kc_ref, kp_ref, vc_ref, vp_ref, o_ref, lse_ref, do_ref, bias_ref, *rest):
        dq_ref, dk_ref, dv_ref, dbias_ref, dsink_ref, dk_carry, dv_carry = rest[-7:]
        step = pl.program_id(1)
        chunk = nchunks - 1 - step
        bias_v = bias_ref[0]
        sink = rest[0][0][:, :1] if has_sink else None

        @pl.when(step == 0)
        def _():
            dk_carry[...] = jnp.zeros_like(dk_carry)
            dv_carry[...] = jnp.zeros_like(dv_carry)
            dbias_ref[...] = jnp.zeros_like(dbias_ref)
            dsink_ref[...] = jnp.zeros_like(dsink_ref)

        dks = [jnp.zeros((BLOCK, dh), F32) for _ in range(nb + 1)]
        dvs = [jnp.zeros((BLOCK, dh), F32) for _ in range(nb + 1)]
        dks[nb] = dk_carry[...]
        dvs[nb] = dv_carry[...]
        for j in range(nb - 1, -1, -1):
            cur = slice(j * BLOCK, (j + 1) * BLOCK)
            prev = slice((j - 1) * BLOCK, j * BLOCK)
            kk = jnp.concatenate([kp_ref[0] if j == 0 else kc_ref[0, prev, :], kc_ref[0, cur, :]], axis=0)
            vv = jnp.concatenate([vp_ref[0] if j == 0 else vc_ref[0, prev, :], vc_ref[0, cur, :]], axis=0)
            first = lax.rem(chunk * nb, blocks_per_seq) == 0 if j == 0 else None
            qj, doj = q_ref[0, cur, :], do_ref[0, cur, :]
            lse_j = lse_ref[0, cur, :][:, :1]
            p = jnp.exp(_band_logits(qj, kk, bias_v, first) - lse_j)
            dp = lax.dot_general(doj, vv, NT, preferred_element_type=F32)
            delta = jnp.sum(doj.astype(F32) * o_ref[0, cur, :], axis=1, keepdims=True)
            ds = p * (dp - delta)
            dbias_ref[0] += ds
            if has_sink:
                dsink = -jnp.sum(jnp.exp(sink - lse_j) * delta, axis=0, keepdims=True)
                dsink_ref[0] += jnp.broadcast_to(dsink, (1, LANES))
            dsb = (ds * scale).astype(BF16)
            dq_ref[0, cur, :] = jnp.dot(dsb, kk, preferred_element_type=F32)
            dkk = lax.dot_general(dsb, qj, TN, preferred_element_type=F32)
            dvv = lax.dot_general(p.astype(BF16), doj, TN, preferred_element_type=F32)
            dks[j] += dkk[:BLOCK]
            dks[j + 1] += dkk[BLOCK:]
            dvs[j] += dvv[:BLOCK]
            dvs[j + 1] += dvv[BLOCK:]
        for j in range(nb):
            cur = slice(j * BLOCK, (j + 1) * BLOCK)
            dk_ref[0, cur, :] = dks[j + 1]
            dv_ref[0, cur, :] = dvs[j + 1]
        dk_carry[...] = dks[0]
        dv_carry[...] = dvs[0]

    def rev(i):
        return nchunks - 1 - i

    cur_q = pl.BlockSpec((1, rows, dh), lambda h, i: (h, rev(i), 0))
    cur_kv = pl.BlockSpec((1, rows, dh), lambda h, i: (h // group, rev(i), 0))
    prev_kv = pl.BlockSpec((1, BLOCK, dh), lambda h, i: (h // group, jnp.maximum(rev(i) * nb - 1, 0), 0))
    cur_lse = cur_q
    per_head_bias = pl.BlockSpec((1, BLOCK, 2 * BLOCK), lambda h, i: (h, 0, 0))
    per_head_row = pl.BlockSpec((1, 1, LANES), lambda h, i: (h, 0, 0))
    in_specs = [cur_q, cur_kv, prev_kv, cur_kv, prev_kv, cur_q, cur_lse, cur_q, per_head_bias]
    ins = [q, k, k, v, v, o, lse, do, bias]
    if has_sink:
        in_specs.append(per_head_row)
        ins.append(sinks)
    full = jax.ShapeDtypeStruct((hq, t, dh), F32)
    return pl.pallas_call(
        body, name=name, grid=(hq, nchunks), in_specs=in_specs,
        out_specs=[cur_q, cur_q, cur_q, per_head_bias, per_head_row],
        out_shape=[full, full, full, jax.ShapeDtypeStruct((hq, BLOCK, 2 * BLOCK), F32),
                   jax.ShapeDtypeStruct((hq, 1, LANES), F32)],
        scratch_shapes=[pltpu.VMEM((BLOCK, dh), F32), pltpu.VMEM((BLOCK, dh), F32)],
        compiler_params=_params(("parallel", "arbitrary")),
    )(*ins)


C_QK = C_NOPE + C_ROPE
C_SCALE = C_QK ** -0.5
LOG2E = math.log2(math.e)
C_EXP2 = C_SCALE * LOG2E
CAUSAL_SUB = 256


def _causal_tile(t):
    return min(1024, t)


CAUSAL_Q_CHAIN = 128
CAUSAL_K_CHAIN = 256
STAT_ROWS = 8


def _mla_fwd(q, k, v1t, name):
    h, t, _ = q.shape
    tq = _causal_tile(t)
    n = t // tq
    qs, ks = min(CAUSAL_Q_CHAIN, tq), min(CAUSAL_K_CHAIN, tq)

    def body(q_ref, k_ref, v_ref, ot_ref, lse_ref, m_scr, acc_scr):
        qi, ki = pl.program_id(1), pl.program_id(2)

        @pl.when(ki == 0)
        def _():
            m_scr[...] = jnp.full_like(m_scr, NEG)
            acc_scr[...] = jnp.zeros_like(acc_scr)

        def tile(diagonal):
            for r in range(tq // qs):
                cols = slice(r * qs, (r + 1) * qs)
                q_sub = q_ref[0, cols, :]
                m, acc = m_scr[:1, cols], acc_scr[:, cols]
                for kc in range(tq // ks):
                    k0 = kc * ks
                    if diagonal and k0 > r * qs + qs - 1:
                        continue
                    st = lax.dot_general(k_ref[0, k0:k0 + ks, :], q_sub, NT, preferred_element_type=F32)
                    if diagonal and k0 + ks - 1 > r * qs:
                        kpos = k0 + lax.broadcasted_iota(jnp.int32, st.shape, 0)
                        qpos = r * qs + lax.broadcasted_iota(jnp.int32, st.shape, 1)
                        st = jnp.where(kpos <= qpos, st, NEG)
                    m_new = jnp.maximum(m, jnp.max(st, axis=0, keepdims=True))
                    alpha = jnp.exp2((m - m_new) * C_EXP2)
                    pt = jnp.exp2((st - m_new) * C_EXP2).astype(BF16)
                    acc = acc * alpha + jnp.dot(v_ref[0, :, k0:k0 + ks], pt, preferred_element_type=F32)
                    m = m_new
                m_scr[:, cols] = jnp.broadcast_to(m, (STAT_ROWS, qs))
                acc_scr[:, cols] = acc

        @pl.when(ki < qi)
        def _():
            tile(False)

        @pl.when(ki == qi)
        def _():
            tile(True)
            l = acc_scr[C_V:C_V + 1, :]
            ot_ref[0] = acc_scr[:C_V, :] / l
            lse_ref[0] = jnp.broadcast_to(m_scr[:1, :] * C_SCALE + jnp.log(l), (STAT_ROWS, tq))

    return pl.pallas_call(
        body, name=name, grid=(h, n, n),
        in_specs=[pl.BlockSpec((1, tq, C_QK), lambda hh, qi, ki: (hh, qi, 0)),
                  pl.BlockSpec((1, tq, C_QK), lambda hh, qi, ki: (hh, jnp.minimum(ki, qi), 0)),
                  pl.BlockSpec((1, LANES, tq), lambda hh, qi, ki: (hh, 0, jnp.minimum(ki, qi)))],
        out_specs=[pl.BlockSpec((1, C_V, tq), lambda hh, qi, ki: (hh, 0, qi)),
                   pl.BlockSpec((1, STAT_ROWS, tq), lambda hh, qi, ki: (hh, 0, qi))],
        out_shape=[jax.ShapeDtypeStruct((h, C_V, t), F32), jax.ShapeDtypeStruct((h, STAT_ROWS, t), F32)],
        scratch_shapes=[pltpu.VMEM((STAT_ROWS, tq), F32), pltpu.VMEM((LANES, tq), F32)],
        compiler_params=_params(("parallel", "arbitrary", "arbitrary")),
    )(q, k, v1t)


def _mla_bwd(q, k, kt, v1, o, lse, do, name):
    h, t, _ = q.shape
    tq = _causal_tile(t)
    n = t // tq
    sub = min(CAUSAL_SUB, tq)

    def body(q_ref, k_ref, kt_ref, v_ref, o_ref, lse_ref, do_ref, dqt_ref, dk_ref, dv_ref, dk_acc, dv_acc):
        ki, qi = pl.program_id(1), pl.program_id(2)

        @pl.when(qi == 0)
        def _():
            dk_acc[...] = jnp.zeros_like(dk_acc)
            dv_acc[...] = jnp.zeros_like(dv_acc)

        @pl.when(jnp.logical_and(ki == 0, qi == 0))
        def _():
            dqt_ref[...] = jnp.zeros_like(dqt_ref)

        def tile(diagonal):
            for c in range(tq // sub):
                cols = slice(c * sub, (c + 1) * sub)
                nk = (c + 1) * sub if diagonal else tq
                qc, doc = q_ref[0, cols, :], do_ref[0, cols, :]
                st = lax.dot_general(k_ref[0, :nk, :], qc, NT, preferred_element_type=F32)
                lse2 = lse_ref[0, :1, cols] * LOG2E
                pt = jnp.exp2(st * C_EXP2 - lse2)
                if diagonal:
                    kpos = lax.broadcasted_iota(jnp.int32, st.shape, 0)
                    qpos = c * sub + lax.broadcasted_iota(jnp.int32, st.shape, 1)
                    pt = jnp.where(kpos <= qpos, pt, 0.0)
                dpt = lax.dot_general(v_ref[0, :nk, :C_V], doc, NT, preferred_element_type=F32)
                delta = jnp.sum(doc.astype(F32) * o_ref[0, cols, :], axis=1, keepdims=True)
                delta_row = jnp.transpose(jnp.broadcast_to(delta, (sub, LANES)))[:1]
                dst = (pt * (dpt - delta_row)).astype(BF16)
                dv_acc[:nk, :] += jnp.dot(pt.astype(BF16), doc, preferred_element_type=F32)
                dk_acc[:nk, :] += jnp.dot(dst, qc, preferred_element_type=F32)
                out_cols = pl.ds(pl.multiple_of(qi * tq + c * sub, sub), sub)
                dqt_ref[0, :, out_cols] += jnp.dot(kt_ref[0, :, :nk], dst, preferred_element_type=F32) * C_SCALE

        @pl.when(qi > ki)
        def _():
            tile(False)

        @pl.when(qi == ki)
        def _():
            tile(True)

        @pl.when(qi == n - 1)
        def _():
            dk_ref[0] = dk_acc[...] * C_SCALE
            dv_ref[0] = dv_acc[...]

    def q_spec(d):
        return pl.BlockSpec((1, tq, d), lambda hh, ki, qi: (hh, jnp.maximum(qi, ki), 0))

    def k_spec(d):
        return pl.BlockSpec((1, tq, d), lambda hh, ki, qi: (hh, ki, 0))

    return pl.pallas_call(
        body, name=name, grid=(h, n, n),
        in_specs=[q_spec(C_QK), k_spec(C_QK), pl.BlockSpec((1, C_QK, tq), lambda hh, ki, qi: (hh, 0, ki)),
                  k_spec(LANES), q_spec(C_V),
                  pl.BlockSpec((1, STAT_ROWS, tq), lambda hh, ki, qi: (hh, 0, jnp.maximum(qi, ki))), q_spec(C_V)],
        out_specs=[pl.BlockSpec((1, C_QK, t), lambda hh, ki, qi: (hh, 0, 0)), k_spec(C_QK), k_spec(C_V)],
        out_shape=[jax.ShapeDtypeStruct((h, C_QK, t), F32), jax.ShapeDtypeStruct((h, t, C_QK), F32),
                   jax.ShapeDtypeStruct((h, t, C_V), F32)],
        scratch_shapes=[pltpu.VMEM((tq, C_QK), F32), pltpu.VMEM((tq, C_V), F32)],
        compiler_params=_params(("parallel", "arbitrary", "arbitrary")),
    )(q, k, kt, v1, o, lse, do)


def _exchange(src, flips, src_idx, name):
    n = len(flips)
    _, r, c = src.shape

    def body(src_ref, dst_ref, send_sems, recv_sems):
        me = [lax.axis_index(a) for a in ALL_AXES]
        copies = []
        for kk, flip in enumerate(flips):
            peer = tuple(1 - p if f else p for p, f in zip(me, flip))
            copies.append(pltpu.make_async_remote_copy(
                src_ref=src_ref.at[src_idx[kk]], dst_ref=dst_ref.at[kk], send_sem=send_sems.at[kk],
                recv_sem=recv_sems.at[kk], device_id=peer, device_id_type=MESH))
        for cp in copies:
            cp.start()
        for cp in copies:
            cp.wait_recv()
        for cp in copies:
            cp.wait_send()

    return pl.pallas_call(
        body, name=name, in_specs=[pl.BlockSpec(memory_space=pl.ANY)], out_specs=pl.BlockSpec(memory_space=pl.ANY),
        out_shape=jax.ShapeDtypeStruct((n, r, c), src.dtype),
        scratch_shapes=[pltpu.SemaphoreType.DMA((n,)), pltpu.SemaphoreType.DMA((n,))],
    )(src)


FLIP_C = (0, 0, 1)
CHIP_FLIPS = ((0, 1, 0), (1, 0, 0), (1, 1, 0))
ALL_FLIPS = tuple((a >> 2 & 1, a >> 1 & 1, a & 1) for a in range(1, 8))


def _pick(stacked, idx):
    return lax.dynamic_index_in_dim(stacked, idx, axis=0, keepdims=False)


def _to_comm_rows(parts, dtype):
    flat = jnp.concatenate([p.reshape(-1) for p in parts]).astype(dtype)
    rows = -(-flat.size // (COMM_COLS * COMM_ROW_ALIGN)) * COMM_ROW_ALIGN
    return jnp.pad(flat, (0, rows * COMM_COLS - flat.size)).reshape(rows, COMM_COLS)


def _from_comm_rows(buf, shapes):
    flat, out, off = buf.reshape(-1), [], 0
    for s in shapes:
        size = math.prod(s)
        out.append(flat[off:off + size].reshape(s))
        off += size
    return out


def _gather_shards(buf, chip, core):
    half = buf.shape[0] // 2
    mine = lax.dynamic_slice_in_dim(buf, core * half, half, axis=0)
    over_ici = _exchange(mine[None], CHIP_FLIPS, (0, 0, 0), "gather_ici")
    over_d2d = _exchange(over_ici.reshape(1, 3 * half, -1), (FLIP_C,), (0,), "gather_d2d").reshape(over_ici.shape)
    both = jnp.stack([over_ici, over_d2d])
    remote = jnp.concatenate([_pick(both, core), _pick(both, 1 - core)], axis=1)
    by_flip = jnp.concatenate([buf[None], remote])
    return jnp.stack([_pick(by_flip, jnp.bitwise_xor(chip, a)) for a in range(4)])


def _reduce_scatter(portions, chip, core):
    half = portions.shape[1] // 2
    keep = lax.dynamic_slice_in_dim(portions, core * half, half, axis=1)
    give = lax.dynamic_slice_in_dim(portions, (1 - core) * half, half, axis=1)
    got = _exchange(give.reshape(1, 4 * half, -1), (FLIP_C,), (0,), "reduce_d2d").reshape(keep.shape)
    pair = _ew(lambda p, q: (p + q,), [keep.reshape(4 * half, -1), got.reshape(4 * half, -1)],
               [(COMM_COLS, F32)], "reduce_pair_sum")[0].reshape(keep.shape)
    out = jnp.stack([_pick(pair, jnp.bitwise_xor(chip, f)) for f in (1, 2, 3)]).astype(BF16)
    others = _exchange(out, CHIP_FLIPS, (0, 1, 2), "reduce_ici")
    total = _ew(lambda p, q, r, s: (p + q.astype(F32) + r.astype(F32) + s.astype(F32),),
                [_pick(pair, chip), others[0], others[1], others[2]],
                [(COMM_COLS, F32)], "reduce_chip_sum")[0]
    other_half = _exchange(total[None], (FLIP_C,), (0,), "reduce_share")[0]
    halves = jnp.stack([total, other_half])
    return jnp.concatenate([_pick(halves, core), _pick(halves, 1 - core)], axis=0)


def _all_reduce_small(buf, dev):
    got = _exchange(buf[None], ALL_FLIPS, (0,) * 7, "small_gather")
    by_flip = jnp.concatenate([buf[None], got])
    ordered = [_pick(by_flip, jnp.bitwise_xor(dev, a)) for a in range(8)]

    def fn(*t):
        s = t[0]
        for u in t[1:]:
            s = s + u
        return (s,)

    return _ew(fn, ordered, [(buf.shape[1], F32)], "small_sum")[0]


def _band_bucket_onehot(dilation, max_dist):
    i = jnp.arange(BLOCK)[:, None]
    j = jnp.arange(2 * BLOCK)[None, :]
    dist = i + BLOCK - j
    inband = (dist >= 0) & (dist <= max_dist)
    n = jnp.maximum(dist, 0) * dilation
    max_exact = NUM_BUCKETS // 2
    nf = jnp.maximum(n, 1).astype(F32)
    large = max_exact + (jnp.log(nf / max_exact) / math.log(MAX_DISTANCE / max_exact)
                         * (NUM_BUCKETS - max_exact)).astype(jnp.int32)
    bucket = jnp.where(n < max_exact, n, jnp.minimum(large, NUM_BUCKETS - 1))
    onehot = (bucket[..., None] == jnp.arange(NUM_BUCKETS)) & inband[..., None]
    return onehot.reshape(-1, NUM_BUCKETS).astype(F32), inband.reshape(-1)


def _band_bias(table, onehot, inband):
    vals = jnp.einsum("pb,bh->hp", onehot, table, precision=lax.Precision.HIGHEST)
    return jnp.where(inband[None, :], vals, NEG).reshape(-1, BLOCK, 2 * BLOCK)


BAND_VARIANTS = ((1, A_WINDOW - 1),) + tuple((dil, window // dil) for window, dil in B_BRANCHES)


LAYOUT_TILE_BYTES = 4 << 20


def _layout_rows(length, row_bytes):
    return _divisor_tile(length, max(16, LAYOUT_TILE_BYTES // row_bytes), 16)


def _split_heads(items, w, dil, name):
    s = items[0][0].shape[0]
    length = s // dil
    row_bytes = sum(-(-n * w // LANES) * LANES * (x.dtype.itemsize + jnp.dtype(d).itemsize) for x, _, n, d in items)
    tr = _layout_rows(length, row_bytes)
    nt = length // tr

    def body(*refs):
        for (x_ref, o_ref), (_, _, n, _) in zip(zip(refs[:len(items)], refs[len(items):]), items):
            for j in range(n):
                o_ref[j] = x_ref[:, j * w:(j + 1) * w].astype(o_ref.dtype)

    in_specs, out_specs, out_shape, views = [], [], [], []
    for x, first, n, d in items:
        bw, width = n * w, x.shape[1]
        assert bw % LANES == 0 and first % bw == 0 and (dil == 1 or width % bw == 0)
        in_specs.append(pl.BlockSpec((tr, bw), lambda r, i, c0=first // bw, wb=width // bw: (i, r * wb + c0)))
        out_specs.append(pl.BlockSpec((n, tr, w), lambda r, i: (0, r * nt + i, 0)))
        out_shape.append(jax.ShapeDtypeStruct((n, s, w), d))
        views.append(x.reshape(length, dil * width))
    return pl.pallas_call(
        body, name=name, grid=(dil, nt), in_specs=in_specs, out_specs=out_specs, out_shape=out_shape,
        compiler_params=_params(("parallel", "parallel")),
    )(*views)


def _merge_heads(items, dil, name, group_sum=1):
    s, w = items[0][0].shape[1:]
    length = s // dil
    row_bytes = sum(t.shape[0] * LANES * t.dtype.itemsize + t.shape[0] * w * jnp.dtype(d).itemsize for t, d in items)
    tr = _layout_rows(length, row_bytes)
    nt = length // tr

    def body(*refs):
        for t_ref, o_ref in zip(refs[:len(items)], refs[len(items):]):
            for j in range(t_ref.shape[0] // group_sum):
                v = t_ref[j * group_sum]
                for g in range(1, group_sum):
                    v = v + t_ref[j * group_sum + g]
                o_ref[:, j * w:(j + 1) * w] = v.astype(o_ref.dtype)

    in_specs, out_specs, out_shape = [], [], []
    for t, d in items:
        n = t.shape[0]
        bw = n // group_sum * w
        assert bw % LANES == 0
        in_specs.append(pl.BlockSpec((n, tr, w), lambda r, i: (0, r * nt + i, 0)))
        out_specs.append(pl.BlockSpec((tr, bw), lambda r, i: (i, r)))
        out_shape.append(jax.ShapeDtypeStruct((length, dil * bw), d))
    outs = pl.pallas_call(
        body, name=name, grid=(dil, nt), in_specs=in_specs, out_specs=out_specs, out_shape=out_shape,
        compiler_params=_params(("parallel", "parallel")),
    )(*[t for t, _ in items])
    return [o.reshape(s, -1) for o in outs]


ROPE_HALF = C_ROPE // 2
ROPE_PERIOD = 3 * LANES


def _rope_tables(s):
    inv = ROPE_THETA ** (-jnp.arange(0, C_ROPE, 2, dtype=F32) / C_ROPE)
    ang = jnp.arange(s, dtype=F32)[:, None] * inv[None, :]
    cos, sin = jnp.cos(ang), jnp.sin(ang)
    one, zero = jnp.ones((s, C_NOPE), F32), jnp.zeros((s, C_NOPE), F32)
    z16 = jnp.zeros((s, ROPE_HALF), F32)
    reps = ROPE_PERIOD // C_QK
    keep = jnp.tile(jnp.concatenate([one, cos, cos], axis=1), (1, reps))
    from_above = jnp.tile(jnp.concatenate([zero, -sin, z16], axis=1), (1, reps))
    from_below = jnp.tile(jnp.concatenate([zero, z16, sin], axis=1), (1, reps))
    return (cos, sin), (keep, from_above, from_below)


def _rope_rows(x, tables, inverse, name, dtype):
    width = x.shape[1]
    reps = width // ROPE_PERIOD
    sign = -1.0 if inverse else 1.0

    def fn(xv, keep, above, below):
        keep, above, below = (jnp.tile(t, (1, reps)) for t in (keep, above, below))
        up = pltpu.roll(xv, width - ROPE_HALF, 1)
        down = pltpu.roll(xv, ROPE_HALF, 1)
        return (xv * keep + sign * (up * above + down * below),)

    return _ew(fn, [x, *tables], [(width, dtype)], name)[0]


def _rotate_half_pairs(a, b, cos, sin, inverse):
    if inverse:
        return a * cos + b * sin, b * cos - a * sin
    return a * cos - b * sin, a * sin + b * cos


def _split_kv(kv, down, cos, sin, name):
    s = kv.shape[0]
    h = kv.shape[1] // (C_NOPE + C_V)
    tr = _layout_rows(s, 8 * kv.shape[1])
    r0 = C_Q_RANK + C_KV_RANK

    def body(kv_ref, down_ref, cos_ref, sin_ref, k_ref, v_ref):
        k1, k2 = _rotate_half_pairs(down_ref[:, r0:r0 + ROPE_HALF], down_ref[:, r0 + ROPE_HALF:r0 + C_ROPE],
                                    cos_ref[...], sin_ref[...], False)
        lane = lax.broadcasted_iota(jnp.int32, (tr, LANES - C_V), 1)
        tail = jnp.where(lane == 0, 1.0, 0.0).astype(BF16)
        for j in range(h):
            base = j * (C_NOPE + C_V)
            k_ref[j, :, :C_NOPE] = kv_ref[:, base:base + C_NOPE]
            k_ref[j, :, C_NOPE:C_NOPE + ROPE_HALF] = k1.astype(BF16)
            k_ref[j, :, C_NOPE + ROPE_HALF:] = k2.astype(BF16)
            v_ref[j, :, :C_V] = kv_ref[:, base + C_NOPE:base + C_NOPE + C_V]
            v_ref[j, :, C_V:] = tail

    def rows(width):
        return pl.BlockSpec((tr, width), lambda i: (i, 0))

    return pl.pallas_call(
        body, name=name, grid=(s // tr,),
        in_specs=[rows(kv.shape[1]), rows(down.shape[1]), rows(ROPE_HALF), rows(ROPE_HALF)],
        out_specs=[pl.BlockSpec((h, tr, C_QK), lambda i: (0, i, 0)), pl.BlockSpec((h, tr, LANES), lambda i: (0, i, 0))],
        out_shape=[jax.ShapeDtypeStruct((h, s, C_QK), BF16), jax.ShapeDtypeStruct((h, s, LANES), BF16)],
        compiler_params=_params(("parallel",)),
    )(kv, down, cos, sin)


def _merge_kv_bwd(dk, dv, cos, sin, name):
    h, s, _ = dk.shape
    tr = _layout_rows(s, 8 * h * LANES)

    def body(dk_ref, dv_ref, cos_ref, sin_ref, dkv_ref, dkr_ref):
        rot = dk_ref[0, :, C_NOPE:]
        for j in range(h):
            base = j * (C_NOPE + C_V)
            dkv_ref[:, base:base + C_NOPE] = dk_ref[j, :, :C_NOPE].astype(BF16)
            dkv_ref[:, base + C_NOPE:base + C_NOPE + C_V] = dv_ref[j].astype(BF16)
            if j:
                rot = rot + dk_ref[j, :, C_NOPE:]
        d1, d2 = _rotate_half_pairs(rot[:, :ROPE_HALF], rot[:, ROPE_HALF:], cos_ref[...], sin_ref[...], True)
        dkr_ref[:, :ROPE_HALF] = d1
        dkr_ref[:, ROPE_HALF:] = d2

    def rows(width):
        return pl.BlockSpec((tr, width), lambda i: (i, 0))

    return pl.pallas_call(
        body, name=name, grid=(s // tr,),
        in_specs=[pl.BlockSpec((h, tr, C_QK), lambda i: (0, i, 0)), pl.BlockSpec((h, tr, C_V), lambda i: (0, i, 0)),
                  rows(ROPE_HALF), rows(ROPE_HALF)],
        out_specs=[rows(h * (C_NOPE + C_V)), rows(C_ROPE)],
        out_shape=[jax.ShapeDtypeStruct((s, h * (C_NOPE + C_V)), BF16), jax.ShapeDtypeStruct((s, C_ROPE), F32)],
        compiler_params=_params(("parallel",)),
    )(dk, dv, cos, sin)


def _even_fwd(xn, h, w_in, w_out, sinks_row, biases, tag):
    s = xn.shape[0]
    proj = _mm(xn, w_in, "nn", [BF16], f"in_proj{tag}")[0]
    qd, kd = A_Q_HEADS * HEAD_DIM, A_KV_HEADS * HEAD_DIM
    qa, ka, va = _split_heads([(proj, 0, A_Q_HEADS, BF16), (proj, qd, A_KV_HEADS, BF16),
                               (proj, qd + kd, A_KV_HEADS, BF16)], HEAD_DIM, 1, f"swa_split{tag}")
    oa, lse_a = _band_fwd(qa, ka, va, biases[0], sinks_row, s // BLOCK, f"swa_fwd{tag}")
    out_a = _merge_heads([(oa, BF16)], 1, f"swa_merge{tag}")[0]
    width = B_HEADS_PER_BRANCH * HEAD_DIM
    qkv_b, outs, lses = [], [], []
    for g, (_, dil) in enumerate(B_BRANCHES):
        base = A_IN + g * 3 * width
        src, base = (proj, base) if dil == 1 else (proj[:, base:base + 3 * width], 0)
        qkv = _split_heads([(src, base + i * width, B_HEADS_PER_BRANCH, BF16) for i in range(3)], HEAD_DIM, dil,
                           f"dil{g}_split{tag}")
        og, lg = _band_fwd(*qkv, biases[1 + g], None, s // dil // BLOCK, f"dil{g}_fwd{tag}")
        qkv_b.append(qkv)
        merged = _merge_heads([(og, F32), (lg, F32)], dil, f"dil{g}_merge{tag}")
        outs.append(merged[0])
        lses.append(merged[1])

    def merge(o0, o1, o2, l0, l1, l2):
        m = jnp.maximum(jnp.maximum(l0, l1), l2)
        e0, e1, e2 = jnp.exp(l0 - m), jnp.exp(l1 - m), jnp.exp(l2 - m)
        den = e0 + e1 + e2
        out = (e0 * o0 + e1 * o1 + e2 * o2) / den
        return out, m + jnp.log(den), out

    out_b, lse_b, out_b16 = _ew(merge, outs + lses, [(width, F32), (width, F32), (width, BF16)], f"dil_merge{tag}")
    cat = jnp.concatenate([out_a, out_b16], axis=1)
    h_mid = _mm(cat, w_out, "nn", [F32], f"out_proj{tag}", epi=_add_epi, extras=(h,))[0]
    return h_mid, (qa, ka, va, oa, lse_a, qkv_b, out_b, lse_b, cat)


def _even_bwd(dh, xn, saved, w_in, w_out, sinks_row, biases, onehots, tag):
    qa, ka, va, oa, lse_a, qkv_b, out_b, lse_b, cat = saved
    s = xn.shape[0]
    qd = A_Q_HEADS * HEAD_DIM
    g_w_out = _mm(cat, dh, "tn", [F32], f"out_proj_dw{tag}")[0]
    dcat = _mm(dh, w_out, "nt", [BF16], f"out_proj_dx{tag}")[0]
    do_a = _split_heads([(dcat, 0, A_Q_HEADS, BF16)], HEAD_DIM, 1, f"swa_do_split{tag}")[0]
    dqa, dka8, dva8, dbias_a, dsink = _band_bwd(qa, ka, va, oa, lse_a, do_a, biases[0], sinks_row, s // BLOCK,
                                                f"swa_bwd{tag}")
    pieces = _merge_heads([(dqa, BF16)], 1, f"swa_dq_merge{tag}")
    pieces += _merge_heads([(dka8, BF16), (dva8, BF16)], 1, f"swa_dkv_merge{tag}", group_sum=A_Q_HEADS // A_KV_HEADS)
    dbias_b = []
    dcat_b = dcat[:, qd:]
    for g, (_, dil) in enumerate(B_BRANCHES):
        do_g, out_g, lse_g = _split_heads([(dcat_b, 0, B_HEADS_PER_BRANCH, BF16), (out_b, 0, B_HEADS_PER_BRANCH, F32),
                                           (lse_b, 0, B_HEADS_PER_BRANCH, F32)], HEAD_DIM, dil, f"dil{g}_do_split{tag}")
        dqg, dkg, dvg, dbg, _ = _band_bwd(*qkv_b[g], out_g, lse_g, do_g, biases[1 + g], None, s // dil // BLOCK,
                                          f"dil{g}_bwd{tag}")
        pieces += _merge_heads([(dqg, BF16), (dkg, BF16), (dvg, BF16)], dil, f"dil{g}_dqkv_merge{tag}")
        dbias_b.append(dbg)
    dproj = jnp.concatenate(pieces, axis=1)
    g_w_in = _mm(xn, dproj, "tn", [F32], f"in_proj_dw{tag}")[0]
    dxn = _mm(dproj, w_in, "nt", [F32], f"in_proj_dx{tag}")[0]
    cols = [_mm(db.reshape(db.shape[0], -1), onehots[v][0], "nn", [F32], f"bias_buckets{v}{tag}")[0].T
            for v, db in enumerate([dbias_a] + dbias_b)]
    return dxn, g_w_in, g_w_out, jnp.concatenate(cols, axis=1), dsink[:, 0, 0]


def _mla_layer_fwd(xn, h, w_dn, q_norm, w_uq, kv_norm, w_ukv, w_o, ropes, tag):
    (cos, sin), q_tables = ropes
    down = _mm(xn, w_dn, "nn", [F32], f"mla_down{tag}")[0]
    c_q, c_kv = down[:, :C_Q_RANK], down[:, C_Q_RANK:C_Q_RANK + C_KV_RANK]
    cqn = _rms_fwd(c_q, q_norm, f"mla_qnorm{tag}")
    ckvn = _rms_fwd(c_kv, kv_norm, f"mla_kvnorm{tag}")
    q = _mm(cqn, w_uq, "nn", [F32], f"mla_uq{tag}")[0]
    kv = _mm(ckvn, w_ukv, "nn", [BF16], f"mla_ukv{tag}")[0]
    qh = _split_heads([(_rope_rows(q, q_tables, False, f"mla_rope_q{tag}", BF16), 0, C_HEADS, BF16)], C_QK, 1,
                      f"mla_q_split{tag}")[0]
    kh, v1h = _split_kv(kv, down, cos, sin, f"mla_kv_split{tag}")
    ot, lse = _mla_fwd(qh, kh, v1h.transpose(0, 2, 1), f"mla_attn_fwd{tag}")
    o = ot.transpose(0, 2, 1)
    o2d = _merge_heads([(o, BF16)], 1, f"mla_o_merge{tag}")[0]
    h_mid = _mm(o2d, w_o, "nn", [F32], f"mla_o{tag}", epi=_add_epi, extras=(h,))[0]
    return h_mid, (c_q, c_kv, cqn, ckvn, qh, kh, v1h, o, lse, o2d)


def _mla_layer_bwd(dh, xn, saved, w_dn, q_norm, w_uq, kv_norm, w_ukv, w_o, ropes, tag):
    c_q, c_kv, cqn, ckvn, qh, kh, v1h, o, lse, o2d = saved
    (cos, sin), q_tables = ropes
    g_w_o = _mm(o2d, dh, "tn", [F32], f"mla_o_dw{tag}")[0]
    do2d = _mm(dh, w_o, "nt", [BF16], f"mla_o_dx{tag}")[0]
    do = _split_heads([(do2d, 0, C_HEADS, BF16)], C_V, 1, f"mla_do_split{tag}")[0]
    dqt, dk, dv = _mla_bwd(qh, kh, kh.transpose(0, 2, 1), v1h, o, lse, do, f"mla_attn_bwd{tag}")
    dq_roped = _merge_heads([(dqt.transpose(0, 2, 1), F32)], 1, f"mla_dq_merge{tag}")[0]
    dq = _rope_rows(dq_roped, q_tables, True, f"mla_rope_q_bwd{tag}", BF16)
    dkv, dk_rope = _merge_kv_bwd(dk, dv, cos, sin, f"mla_dkv_merge{tag}")
    g_w_uq = _mm(cqn, dq, "tn", [F32], f"mla_uq_dw{tag}")[0]
    dcqn = _mm(dq, w_uq, "nt", [F32], f"mla_uq_dx{tag}")[0]
    g_w_ukv = _mm(ckvn, dkv, "tn", [F32], f"mla_ukv_dw{tag}")[0]
    dckvn = _mm(dkv, w_ukv, "nt", [F32], f"mla_ukv_dx{tag}")[0]
    dc_q, g_q_norm = _rms_bwd(c_q, q_norm, dcqn, None, f"mla_qnorm_bwd{tag}")
    dc_kv, g_kv_norm = _rms_bwd(c_kv, kv_norm, dckvn, None, f"mla_kvnorm_bwd{tag}")
    ddown = jnp.concatenate([dc_q, dc_kv, dk_rope], axis=1).astype(BF16)
    g_w_dn = _mm(xn, ddown, "tn", [F32], f"mla_down_dw{tag}")[0]
    dxn = _mm(ddown, w_dn, "nt", [F32], f"mla_down_dx{tag}")[0]
    return dxn, g_w_dn, g_q_norm, g_w_uq, g_kv_norm, g_w_ukv, g_w_o


SHARDED = (("w_in_ab", 2), ("w_out_ab", 2), ("w_down_c", 1), ("w_uq_c", 2), ("w_ukv_c", 2), ("w_o_c", 2),
           ("w_mlp_up", 2), ("w_mlp_down", 1))
SHARDED_NORMS = ("q_norm_c", "kv_norm_c")


def kernel(x, rel_bias, attn_norm, mlp_norm, final_norm, w_in_ab, sinks, w_out_ab, w_down_c, q_norm_c, w_uq_c, kv_norm_c, w_ukv_c, w_o_c, w_mlp_up, w_mlp_down, loss_target, m_rel_bias, m_attn_norm, m_mlp_norm, m_final_norm, m_w_in_ab, m_sinks, m_w_out_ab, m_w_down_c, m_q_norm_c, m_w_uq_c, m_kv_norm_c, m_w_ukv_c, m_w_o_c, m_w_mlp_up, m_w_mlp_down, v_rel_bias, v_attn_norm, v_mlp_norm, v_final_norm, v_w_in_ab, v_sinks, v_w_out_ab, v_w_down_c, v_q_norm_c, v_w_uq_c, v_kv_norm_c, v_w_ukv_c, v_w_o_c, v_w_mlp_up, v_w_mlp_down):
    given = dict(locals())
    chip = lax.axis_index("x") * 2 + lax.axis_index("y")
    core = lax.axis_index("c")
    dev = chip * 2 + core
    depth = attn_norm.shape[0]
    s = x.shape[1]

    shards = [given[n] for n, _ in SHARDED]
    norm_shards = [given[n] for n in SHARDED_NORMS]
    packed = _to_comm_rows([t.astype(BF16) for t in shards]
                           + [lax.bitcast_convert_type(t, BF16) for t in norm_shards], BF16)
    by_chip = _gather_shards(packed, chip, core)
    shapes = [t.shape for t in shards] + [t.shape + (2,) for t in norm_shards]
    pieces = [_from_comm_rows(by_chip[a], shapes) for a in range(4)]
    full = {n: jnp.concatenate([pieces[a][i] for a in range(4)], axis=ax) for i, (n, ax) in enumerate(SHARDED)}
    for i, n in enumerate(SHARDED_NORMS):
        full[n] = jnp.concatenate([lax.bitcast_convert_type(pieces[a][len(SHARDED) + i], F32) for a in range(4)],
                                  axis=-1)

    onehots = [_band_bucket_onehot(dil, md) for dil, md in BAND_VARIANTS]
    head_cols = [(0, A_Q_HEADS)] + [(A_Q_HEADS + g * B_HEADS_PER_BRANCH, A_Q_HEADS + (g + 1) * B_HEADS_PER_BRANCH)
                                    for g in range(len(B_BRANCHES))]
    biases = [_band_bias(rel_bias[:, lo:hi], oh, inb) for (lo, hi), (oh, inb) in zip(head_cols, onehots)]
    ropes = _rope_tables(s)
    sink_rows = [jnp.broadcast_to(sinks[e][:, None, None], (A_Q_HEADS, 1, LANES)) for e in range(sinks.shape[0])]

    def odd_weights(o):
        return [full[n][o] for n in ("w_down_c", "q_norm_c", "w_uq_c", "kv_norm_c", "w_ukv_c", "w_o_c")]

    h = x[0]
    saved = []
    for l in range(depth):
        xn = _rms_fwd(h, attn_norm[l], f"attn_norm{l}")
        if l % 2 == 0:
            e = l // 2
            h_mid, mix = _even_fwd(xn, h, full["w_in_ab"][e], full["w_out_ab"][e], sink_rows[e], biases, f"_{l}")
        else:
            h_mid, mix = _mla_layer_fwd(xn, h, *odd_weights(l // 2), ropes, f"_{l}")
        xn2 = _rms_fwd(h_mid, mlp_norm[l], f"mlp_norm{l}")
        act, relu = _mm(xn2, full["w_mlp_up"][l], "nn", [BF16, BF16], f"mlp_up{l}", epi=_relu2_epi)
        h_out = _mm(act, full["w_mlp_down"][l], "nn", [F32], f"mlp_down{l}", epi=_add_epi, extras=(h_mid,))[0]
        saved.append((h, xn, mix, h_mid, xn2, act, relu))
        h = h_out
    loss_part, dh, dh16, g_final = _final_loss(h, final_norm, loss_target[0], "final_loss")

    grads = {n: [None] * given[n].shape[0] for n, _ in SHARDED}
    g_q_norm, g_kv_norm = [None] * q_norm_c.shape[0], [None] * kv_norm_c.shape[0]
    g_attn_norm, g_mlp_norm = [None] * depth, [None] * depth
    g_sinks = [None] * sinks.shape[0]
    g_rel_bias = None
    for l in range(depth - 1, -1, -1):
        h_in, xn, mix, h_mid, xn2, act, relu = saved[l]
        grads["w_mlp_down"][l] = _mm(act, dh16, "tn", [F32], f"mlp_down_dw{l}")[0]
        du = _mm(dh16, full["w_mlp_down"][l], "nt", [BF16], f"mlp_down_dx{l}", epi=_relu2_bwd_epi, extras=(relu,))[0]
        grads["w_mlp_up"][l] = _mm(xn2, du, "tn", [F32], f"mlp_up_dw{l}")[0]
        dxn2 = _mm(du, full["w_mlp_up"][l], "nt", [F32], f"mlp_up_dx{l}")[0]
        dh, dh16, g_mlp_norm[l] = _rms_bwd(h_mid, mlp_norm[l], dxn2, dh, f"mlp_norm_bwd{l}", also_bf16=True)
        if l % 2 == 0:
            e = l // 2
            dxn, grads["w_in_ab"][e], grads["w_out_ab"][e], g_table, g_sinks[e] = _even_bwd(
                dh16, xn, mix, full["w_in_ab"][e], full["w_out_ab"][e], sink_rows[e], biases, onehots, f"_{l}")
            g_rel_bias = g_table if g_rel_bias is None else g_rel_bias + g_table
        else:
            o = l // 2
            (dxn, grads["w_down_c"][o], g_q_norm[o], grads["w_uq_c"][o], g_kv_norm[o], grads["w_ukv_c"][o],
             grads["w_o_c"][o]) = _mla_layer_bwd(dh16, xn, mix, *odd_weights(o), ropes, f"_{l}")
        if l:
            dh, dh16, g_attn_norm[l] = _rms_bwd(h_in, attn_norm[l], dxn, dh, f"attn_norm_bwd{l}", also_bf16=True)
        else:
            dh, g_attn_norm[l] = _rms_bwd(h_in, attn_norm[l], dxn, dh, f"attn_norm_bwd{l}")
    grad_x = dh[None]

    portions = []
    for a in range(4):
        parts = []
        for n, ax in SHARDED:
            g = jnp.stack(grads[n])
            size = g.shape[ax] // 4
            parts.append(lax.slice_in_dim(g, a * size, (a + 1) * size, axis=ax))
        portions.append(_to_comm_rows(parts, F32))
    reduced = _from_comm_rows(_reduce_scatter(jnp.stack(portions), chip, core), [t.shape for t in shards])
    g_shard = {n: reduced[i] for i, (n, _) in enumerate(SHARDED)}

    small = [jnp.stack(g_attn_norm), jnp.stack(g_mlp_norm), g_final, g_rel_bias, jnp.stack(g_sinks),
             jnp.stack(g_q_norm), jnp.stack(g_kv_norm), loss_part.reshape(1)]
    small_shapes = [t.shape for t in small]
    summed = _from_comm_rows(_all_reduce_small(_to_comm_rows(small, F32)[:16], dev), small_shapes)
    loss = summed[7][0]
    g_small = dict(zip(("attn_norm", "mlp_norm", "final_norm", "rel_bias", "sinks"), summed[:5]))
    for n, g in zip(SHARDED_NORMS, summed[5:7]):
        size = g.shape[1] // 4
        g_shard[n] = lax.dynamic_slice_in_dim(g, chip * size, size, axis=1)

    order = ["rel_bias", "attn_norm", "mlp_norm", "final_norm", "w_in_ab", "sinks", "w_out_ab", "w_down_c",
             "q_norm_c", "w_uq_c", "kv_norm_c", "w_ukv_c", "w_o_c", "w_mlp_up", "w_mlp_down"]
    g_all = {**g_small, **g_shard}
    deltas, new_m, new_v = [], [], []
    for n in order:
        d, mn, vn = _adamw(given[n], g_all[n], given["m_" + n], given["v_" + n], f"adamw_{n}")
        deltas.append(d)
        new_m.append(mn)
        new_v.append(vn)
    return (loss, grad_x, *[g_all[n] for n in order], *deltas, *new_m, *new_v)
```

```python
import math

import jax
import jax.numpy as jnp
from jax import lax
from jax.experimental import pallas as pl
from jax.experimental.pallas import tpu as pltpu

F32 = jnp.float32
BF16 = jnp.bfloat16
MESH = pl.DeviceIdType.MESH
ALL_AXES = ("x", "y", "c")

EPS = 1e-6
NEG = -1e30
BLOCK = 128
HEAD_DIM = 64
A_Q_HEADS, A_KV_HEADS = 8, 2
A_WINDOW = 128
B_BRANCHES = ((128, 1), (512, 4), (2048, 16))
B_HEADS_PER_BRANCH = 4
NUM_BUCKETS, MAX_DISTANCE = 32, 2048
A_IN = (A_Q_HEADS + 2 * A_KV_HEADS) * HEAD_DIM
C_HEADS, C_NOPE, C_ROPE, C_V = 8, 64, 32, 64
C_Q_RANK, C_KV_RANK = 384, 256
ROPE_THETA = 10000.0
ADAM_LR, ADAM_B1, ADAM_B2, ADAM_EPS, ADAM_WD, ADAM_STEP = 0.001, 0.9, 0.999, 1e-08, 0.01, 10

V7X_VMEM_LIMIT_BYTES = 56 * 1024 * 1024
LANES = 128
COMM_COLS = 1024
COMM_ROW_ALIGN = 1024

NT = (((1,), (1,)), ((), ()))
NN = (((1,), (0,)), ((), ()))
TN = (((0,), (0,)), ((), ()))
DIMS = {"nn": NN, "nt": NT, "tn": TN}


def _params(sem):
    return pltpu.CompilerParams(dimension_semantics=sem, vmem_limit_bytes=V7X_VMEM_LIMIT_BYTES)


def _divisor_tile(n, limit, align):
    if n <= limit:
        return n
    t = (limit // align) * align
    while t >= align:
        if n % t == 0:
            return t
        t -= align
    return n


def _ew(fn, ins, outs, name, acc_outs=(), target_bytes=6 << 20):
    rows = max(a.shape[0] for a in ins)

    def vmem_row_bytes(cols, dtype):
        return -(-cols // LANES) * LANES * jnp.dtype(dtype).itemsize

    per_row = sum(vmem_row_bytes(a.shape[1], a.dtype) for a in ins if a.shape[0] == rows)
    per_row += sum(vmem_row_bytes(c, d) for c, d in outs)
    tr = _divisor_tile(rows, max(16, target_bytes // max(per_row, 1)), 16)
    n_in, n_out = len(ins), len(outs)

    def body(*refs):
        res = fn(*[r[...] for r in refs[:n_in]])
        for r, v in zip(refs[n_in:n_in + n_out], res[:n_out]):
            r[...] = v.astype(r.dtype)
        if acc_outs:
            acc_refs = refs[n_in + n_out:]

            @pl.when(pl.program_id(0) == 0)
            def _():
                for r in acc_refs:
                    r[...] = jnp.zeros_like(r)

            for r, v in zip(acc_refs, res[n_out:]):
                r[...] += v

    def spec(a):
        if a.shape[0] == rows:
            return pl.BlockSpec((tr, a.shape[1]), lambda i: (i, 0))
        return pl.BlockSpec((1, a.shape[1]), lambda i: (0, 0))

    out_shape = [jax.ShapeDtypeStruct((rows, c), d) for c, d in outs]
    out_shape += [jax.ShapeDtypeStruct((1, c), F32) for c in acc_outs]
    out_specs = [pl.BlockSpec((tr, c), lambda i: (i, 0)) for c, _ in outs]
    out_specs += [pl.BlockSpec((1, c), lambda i: (0, 0)) for c in acc_outs]
    return pl.pallas_call(
        body, name=name, grid=(rows // tr,), in_specs=[spec(a) for a in ins], out_specs=out_specs,
        out_shape=out_shape, compiler_params=_params(("arbitrary",)),
    )(*ins)


def _rms_fwd(x, g, name):
    def fn(xv, gv):
        return ((xv * lax.rsqrt(jnp.mean(xv * xv, axis=-1, keepdims=True) + EPS)) * gv,)

    return _ew(fn, [x, g.reshape(1, -1)], [(x.shape[1], BF16)], name)[0]


def _rms_bwd(x, g, dy, add, name, also_bf16=False):
    def fn(xv, gv, dyv, *rest):
        rstd = lax.rsqrt(jnp.mean(xv * xv, axis=-1, keepdims=True) + EPS)
        xh = xv * rstd
        dyg = dyv.astype(F32) * gv
        dx = rstd * (dyg - xh * jnp.mean(dyg * xh, axis=-1, keepdims=True))
        if rest:
            dx = dx + rest[0]
        return (dx,) * (2 if also_bf16 else 1) + (jnp.sum(dyv.astype(F32) * xh, axis=0, keepdims=True),)

    ins = [x, g.reshape(1, -1), dy] + ([] if add is None else [add])
    outs = [(x.shape[1], F32)] + ([(x.shape[1], BF16)] if also_bf16 else [])
    *dx, dg = _ew(fn, ins, outs, name, acc_outs=(x.shape[1],))
    return (*dx, dg[0])


def _final_loss(h, g, target, name):
    d = h.shape[1]

    def fn(xv, gv, tv):
        rstd = lax.rsqrt(jnp.mean(xv * xv, axis=-1, keepdims=True) + EPS)
        xh = xv * rstd
        err = xh * gv - tv
        part = 0.5 * jnp.sum(jnp.mean(err * err, axis=-1, keepdims=True), axis=0, keepdims=True)
        dy = err * (1.0 / d)
        dyg = dy * gv
        dx = rstd * (dyg - xh * jnp.mean(dyg * xh, axis=-1, keepdims=True))
        return dx, dx, jnp.broadcast_to(part, (1, LANES)), jnp.sum(dy * xh, axis=0, keepdims=True)

    dx, dx16, loss, dg = _ew(fn, [h, g.reshape(1, -1), target], [(d, F32), (d, BF16)], name, acc_outs=(LANES, d))
    return loss[0, 0], dx, dx16, dg[0]


def _adamw(w, g, m, v, name):
    def fn(wv, gv, mv, vv):
        mn = ADAM_B1 * mv + (1.0 - ADAM_B1) * gv
        vn = ADAM_B2 * vv + (1.0 - ADAM_B2) * jnp.square(gv)
        m_hat = mn / (1.0 - ADAM_B1 ** ADAM_STEP)
        v_hat = vn / (1.0 - ADAM_B2 ** ADAM_STEP)
        return -ADAM_LR * (m_hat / (jnp.sqrt(v_hat) + ADAM_EPS) + ADAM_WD * wv), mn, vn

    shape = w.shape
    cols = shape[-1] if w.ndim > 1 else w.size
    view = [t.reshape(-1, cols) for t in (w, g, m, v)]
    return [t.reshape(shape) for t in _ew(fn, view, [(cols, F32)] * 3, name)]


MM_VMEM_BUDGET_BYTES = 40 << 20


def _mm(a, b, dims, outs, name, epi=None, extras=(), tm=2048, tn=1024, tk=1024):
    if dims == "nn":
        (m, k), n = a.shape, b.shape[1]
    elif dims == "nt":
        (m, k), n = a.shape, b.shape[0]
    else:
        (k, m), n = a.shape, b.shape[1]
    tn, tk = _divisor_tile(n, tn, LANES), _divisor_tile(k, tk, LANES)

    def tile_bytes(rows):
        per_out = sum(jnp.dtype(d).itemsize for d in outs) + sum(e.dtype.itemsize for e in extras)
        return 2 * (rows * tk * a.dtype.itemsize + tk * tn * b.dtype.itemsize + rows * tn * per_out) + 4 * rows * tn

    tm = _divisor_tile(m, tm, LANES)
    while tile_bytes(tm) > MM_VMEM_BUDGET_BYTES and tm % (2 * LANES) == 0:
        tm //= 2
    nk = k // tk
    n_ex, n_out = len(extras), len(outs)

    def body(a_ref, b_ref, *rest):
        ex_refs, out_refs = rest[:n_ex], rest[n_ex:n_ex + n_out]

        def finish(acc):
            res = epi(acc, *[r[...] for r in ex_refs]) if epi else (acc,)
            for r, v in zip(out_refs, res):
                r[...] = v.astype(r.dtype)

        part = lax.dot_general(a_ref[...].astype(BF16), b_ref[...].astype(BF16), DIMS[dims],
                               preferred_element_type=F32)
        if nk == 1:
            finish(part)
        else:
            acc_ref = rest[-1]
            kk = pl.program_id(2)

            @pl.when(kk == 0)
            def _():
                acc_ref[...] = part

            @pl.when(kk > 0)
            def _():
                acc_ref[...] += part

            @pl.when(kk == nk - 1)
            def _():
                finish(acc_ref[...])

    if dims == "nn":
        a_spec = pl.BlockSpec((tm, tk), lambda i, j, kk: (i, kk))
        b_spec = pl.BlockSpec((tk, tn), lambda i, j, kk: (kk, j))
    elif dims == "nt":
        a_spec = pl.BlockSpec((tm, tk), lambda i, j, kk: (i, kk))
        b_spec = pl.BlockSpec((tn, tk), lambda i, j, kk: (j, kk))
    else:
        a_spec = pl.BlockSpec((tk, tm), lambda i, j, kk: (kk, i))
        b_spec = pl.BlockSpec((tk, tn), lambda i, j, kk: (kk, j))
    tile = pl.BlockSpec((tm, tn), lambda i, j, kk: (i, j))
    return pl.pallas_call(
        body, name=name, grid=(m // tm, n // tn, nk),
        in_specs=[a_spec, b_spec] + [tile] * n_ex, out_specs=[tile] * n_out,
        out_shape=[jax.ShapeDtypeStruct((m, n), d) for d in outs],
        scratch_shapes=[pltpu.VMEM((tm, tn), F32)] if nk > 1 else [],
        compiler_params=_params(("parallel", "parallel", "arbitrary")),
    )(a, b, *extras)


def _add_epi(acc, res):
    return (acc + res,)


def _relu2_epi(acc):
    r = jnp.maximum(acc, 0.0)
    return r * r, r


def _relu2_bwd_epi(acc, r):
    return (acc * (2.0 * r.astype(F32)),)


BAND_SPAN = 256


def _band_geometry(t, blocks_per_seq):
    rows = min(1024, blocks_per_seq * BLOCK)
    nb = rows // BLOCK
    assert blocks_per_seq % nb == 0 and t % rows == 0
    return rows, nb, t // rows, min(BAND_SPAN, rows)


def _span_bias(bias, span):
    n = span // BLOCK
    neg = jnp.full(bias.shape[:2] + (BLOCK,), NEG, F32)
    rows = [jnp.concatenate([neg] * a + [bias[:, :, :BLOCK], bias[:, :, BLOCK:]] + [neg] * (n - 1 - a), axis=2)
            for a in range(n)]
    return jnp.concatenate(rows, axis=1)


def _fold_span_bias_grad(dbias, span):
    n = span // BLOCK
    parts = [dbias[:, a * BLOCK:(a + 1) * BLOCK, a * BLOCK:(a + 2) * BLOCK] for a in range(n)]
    return sum(parts[1:], parts[0])


def _band_logits(qj, kk, bias, first):
    s = lax.dot_general(qj, kk, NT, preferred_element_type=F32) * (HEAD_DIM ** -0.5) + bias
    if first is not None:
        col = lax.broadcasted_iota(jnp.int32, s.shape, 1)
        s = jnp.where(col < jnp.where(first, BLOCK, 0), NEG, s)
    return s


def _band_fwd(q, k, v, bias, sinks, blocks_per_seq, name):
    hq, t, dh = q.shape
    group = hq // k.shape[0]
    rows, nb, nchunks, span = _band_geometry(t, blocks_per_seq)
    has_sink = sinks is not None

    def body(q_ref, kc_ref, kp_ref, vc_ref, vp_ref, bias_ref, *rest):
        o_ref, lse_ref = rest[-2:]
        i = pl.program_id(1)
        bias_v = bias_ref[0]
        sink = rest[0][0][:, :1] if has_sink else None
        for j in range(rows // span):
            cur = slice(j * span, (j + 1) * span)
            prev = slice(j * span - BLOCK, j * span)
            kk = jnp.concatenate([kp_ref[0] if j == 0 else kc_ref[0, prev, :], kc_ref[0, cur, :]], axis=0)
            vv = jnp.concatenate([vp_ref[0] if j == 0 else vc_ref[0, prev, :], vc_ref[0, cur, :]], axis=0)
            first = lax.rem(i * nb, blocks_per_seq) == 0 if j == 0 else None
            s = _band_logits(q_ref[0, cur, :], kk, bias_v, first)
            m = jnp.max(s, axis=1, keepdims=True)
            if has_sink:
                m = jnp.maximum(m, sink)
            p = jnp.exp(s - m)
            l = jnp.sum(p, axis=1, keepdims=True)
            if has_sink:
                l = l + jnp.exp(sink - m)
            acc = jnp.dot(p.astype(BF16), vv, preferred_element_type=F32)
            o_ref[0, cur, :] = acc / l
            lse_ref[0, cur, :] = jnp.broadcast_to(m + jnp.log(l), (span, dh))

    cur_q = pl.BlockSpec((1, rows, dh), lambda h, i: (h, i, 0))
    cur_kv = pl.BlockSpec((1, rows, dh), lambda h, i: (h // group, i, 0))
    prev_kv = pl.BlockSpec((1, BLOCK, dh), lambda h, i: (h // group, jnp.maximum(i * nb - 1, 0), 0))
    in_specs = [cur_q, cur_kv, prev_kv, cur_kv, prev_kv, pl.BlockSpec((1, span, span + BLOCK), lambda h, i: (h, 0, 0))]
    ins = [q, k, k, v, v, _span_bias(bias, span)]
    if has_sink:
        in_specs.append(pl.BlockSpec((1, 1, LANES), lambda h, i: (h, 0, 0)))
        ins.append(sinks)
    return pl.pallas_call(
        body, name=name, grid=(hq, nchunks), in_specs=in_specs,
        out_specs=[cur_q, cur_q],
        out_shape=[jax.ShapeDtypeStruct((hq, t, dh), F32), jax.ShapeDtypeStruct((hq, t, dh), F32)],
        compiler_params=_params(("parallel", "arbitrary")),
    )(*ins)


def _band_bwd(q, k, v, o, lse, do, bias, sinks, blocks_per_seq, name):
    hq, t, dh = q.shape
    group = hq // k.shape[0]
    rows, nb, nchunks, span = _band_geometry(t, blocks_per_seq)
    per_span = span // BLOCK
    has_sink = sinks is not None
    scale = HEAD_DIM ** -0.5

    def body(q_ref, kc_ref, kp_ref, vc_ref, vp_ref, o_ref, lse_ref, do_ref, bias_ref, *rest):
        dq_ref, dk_ref, dv_ref, dbias_ref, dsink_ref, dk_carry, dv_carry = rest[-7:]
        step = pl.program_id(1)
        chunk = nchunks - 1 - step
        bias_v = bias_ref[0]
        sink = rest[0][0][:, :1] if has_sink else None

        @pl.when(step == 0)
        def _():
            dk_carry[...] = jnp.zeros_like(dk_carry)
            dv_carry[...] = jnp.zeros_like(dv_carry)
            dbias_ref[...] = jnp.zeros_like(dbias_ref)
            dsink_ref[...] = jnp.zeros_like(dsink_ref)

        dks = [jnp.zeros((BLOCK, dh), F32) for _ in range(nb + 1)]
        dvs = [jnp.zeros((BLOCK, dh), F32) for _ in range(nb + 1)]
        dks[nb] = dk_carry[...]
        dvs[nb] = dv_carry[...]
        for j in range(rows // span - 1, -1, -1):
            cur = slice(j * span, (j + 1) * span)
            prev = slice(j * span - BLOCK, j * span)
            kk = jnp.concatenate([kp_ref[0] if j == 0 else kc_ref[0, prev, :], kc_ref[0, cur, :]], axis=0)
            vv = jnp.concatenate([vp_ref[0] if j == 0 else vc_ref[0, prev, :], vc_ref[0, cur, :]], axis=0)
            first = lax.rem(chunk * nb, blocks_per_seq) == 0 if j == 0 else None
            qj, doj = q_ref[0, cur, :], do_ref[0, cur, :]
            lse_j = lse_ref[0, cur, :][:, :1]
            p = jnp.exp(_band_logits(qj, kk, bias_v, first) - lse_j)
            dp = lax.dot_general(doj, vv, NT, preferred_element_type=F32)
            delta = jnp.sum(doj.astype(F32) * o_ref[0, cur, :], axis=1, keepdims=True)
            ds = p * (dp - delta)
            dbias_ref[0] += ds
            if has_sink:
                dsink = -jnp.sum(jnp.exp(sink - lse_j) * delta, axis=0, keepdims=True)
                dsink_ref[0] += jnp.broadcast_to(dsink, (1, LANES))
            dsb = (ds * scale).astype(BF16)
            dq_ref[0, cur, :] = jnp.dot(dsb, kk, preferred_element_type=F32)
            dkk = lax.dot_general(dsb, qj, TN, preferred_element_type=F32)
            dvv = lax.dot_general(p.astype(BF16), doj, TN, preferred_element_type=F32)
            for b in range(per_span + 1):
                piece = slice(b * BLOCK, (b + 1) * BLOCK)
                dks[j * per_span + b] += dkk[piece]
                dvs[j * per_span + b] += dvv[piece]
        for j in range(nb):
            cur = slice(j * BLOCK, (j + 1) * BLOCK)
            dk_ref[0, cur, :] = dks[j + 1]
            dv_ref[0, cur, :] = dvs[j + 1]
        dk_carry[...] = dks[0]
        dv_carry[...] = dvs[0]

    def rev(i):
        return nchunks - 1 - i

    cur_q = pl.BlockSpec((1, rows, dh), lambda h, i: (h, rev(i), 0))
    cur_kv = pl.BlockSpec((1, rows, dh), lambda h, i: (h // group, rev(i), 0))
    prev_kv = pl.BlockSpec((1, BLOCK, dh), lambda h, i: (h // group, jnp.maximum(rev(i) * nb - 1, 0), 0))
    cur_lse = cur_q
    per_head_bias = pl.BlockSpec((1, span, span + BLOCK), lambda h, i: (h, 0, 0))
    per_head_row = pl.BlockSpec((1, 1, LANES), lambda h, i: (h, 0, 0))
    in_specs = [cur_q, cur_kv, prev_kv, cur_kv, prev_kv, cur_q, cur_lse, cur_q, per_head_bias]
    ins = [q, k, k, v, v, o, lse, do, _span_bias(bias, span)]
    if has_sink:
        in_specs.append(per_head_row)
        ins.append(sinks)
    full = jax.ShapeDtypeStruct((hq, t, dh), F32)
    dq, dk, dv, dbias, dsink = pl.pallas_call(
        body, name=name, grid=(hq, nchunks), in_specs=in_specs,
        out_specs=[cur_q, cur_q, cur_q, per_head_bias, per_head_row],
        out_shape=[full, full, full, jax.ShapeDtypeStruct((hq, span, span + BLOCK), F32),
                   jax.ShapeDtypeStruct((hq, 1, LANES), F32)],
        scratch_shapes=[pltpu.VMEM((BLOCK, dh), F32), pltpu.VMEM((BLOCK, dh), F32)],
        compiler_params=_params(("parallel", "arbitrary")),
    )(*ins)
    return dq, dk, dv, _fold_span_bias_grad(dbias, span), dsink


C_QK = C_NOPE + C_ROPE
C_SCALE = C_QK ** -0.5
LOG2E = math.log2(math.e)
C_EXP2 = C_SCALE * LOG2E
CAUSAL_SUB = 256


def _causal_tile(t):
    return min(1024, t)


CAUSAL_Q_CHAIN = 128
CAUSAL_K_CHAIN = 256
STAT_ROWS = 8


def _mla_fwd(q, k, v1t, name):
    h, t, _ = q.shape
    tq = _causal_tile(t)
    n = t // tq
    qs, ks = min(CAUSAL_Q_CHAIN, tq), min(CAUSAL_K_CHAIN, tq)

    def body(q_ref, k_ref, v_ref, ot_ref, lse_ref, m_scr, acc_scr):
        qi, ki = pl.program_id(1), pl.program_id(2)

        @pl.when(ki == 0)
        def _():
            m_scr[...] = jnp.full_like(m_scr, NEG)
            acc_scr[...] = jnp.zeros_like(acc_scr)

        def tile(diagonal):
            for r in range(tq // qs):
                cols = slice(r * qs, (r + 1) * qs)
                q_sub = q_ref[0, cols, :]
                m, acc = m_scr[:1, cols], acc_scr[:, cols]
                for kc in range(tq // ks):
                    k0 = kc * ks
                    if diagonal and k0 > r * qs + qs - 1:
                        continue
                    st = lax.dot_general(k_ref[0, k0:k0 + ks, :], q_sub, NT, preferred_element_type=F32)
                    if diagonal and k0 + ks - 1 > r * qs:
                        kpos = k0 + lax.broadcasted_iota(jnp.int32, st.shape, 0)
                        qpos = r * qs + lax.broadcasted_iota(jnp.int32, st.shape, 1)
                        st = jnp.where(kpos <= qpos, st, NEG)
                    m_new = jnp.maximum(m, jnp.max(st, axis=0, keepdims=True))
                    alpha = jnp.exp2((m - m_new) * C_EXP2)
                    pt = jnp.exp2((st - m_new) * C_EXP2).astype(BF16)
                    acc = acc * alpha + jnp.dot(v_ref[0, :, k0:k0 + ks], pt, preferred_element_type=F32)
                    m = m_new
                m_scr[:, cols] = jnp.broadcast_to(m, (STAT_ROWS, qs))
                acc_scr[:, cols] = acc

        @pl.when(ki < qi)
        def _():
            tile(False)

        @pl.when(ki == qi)
        def _():
            tile(True)
            l = acc_scr[C_V:C_V + 1, :]
            ot_ref[0] = acc_scr[:C_V, :] / l
            lse_ref[0] = jnp.broadcast_to(m_scr[:1, :] * C_SCALE + jnp.log(l), (STAT_ROWS, tq))

    return pl.pallas_call(
        body, name=name, grid=(h, n, n),
        in_specs=[pl.BlockSpec((1, tq, C_QK), lambda hh, qi, ki: (hh, qi, 0)),
                  pl.BlockSpec((1, tq, C_QK), lambda hh, qi, ki: (hh, jnp.minimum(ki, qi), 0)),
                  pl.BlockSpec((1, LANES, tq), lambda hh, qi, ki: (hh, 0, jnp.minimum(ki, qi)))],
        out_specs=[pl.BlockSpec((1, C_V, tq), lambda hh, qi, ki: (hh, 0, qi)),
                   pl.BlockSpec((1, STAT_ROWS, tq), lambda hh, qi, ki: (hh, 0, qi))],
        out_shape=[jax.ShapeDtypeStruct((h, C_V, t), F32), jax.ShapeDtypeStruct((h, STAT_ROWS, t), F32)],
        scratch_shapes=[pltpu.VMEM((STAT_ROWS, tq), F32), pltpu.VMEM((LANES, tq), F32)],
        compiler_params=_params(("parallel", "arbitrary", "arbitrary")),
    )(q, k, v1t)


def _mla_bwd(q, k, v1, o, lse, do, name):
    h, t, _ = q.shape
    tq = _causal_tile(t)
    n = t // tq
    sub = min(CAUSAL_SUB, tq)

    def body(q_ref, k_ref, v_ref, o_ref, lse_ref, do_ref, dqt_ref, dk_ref, dv_ref, dk_acc, dv_acc):
        ki, qi = pl.program_id(1), pl.program_id(2)

        @pl.when(qi == 0)
        def _():
            dk_acc[...] = jnp.zeros_like(dk_acc)
            dv_acc[...] = jnp.zeros_like(dv_acc)

        @pl.when(jnp.logical_and(ki == 0, qi == 0))
        def _():
            dqt_ref[...] = jnp.zeros_like(dqt_ref)

        def tile(diagonal):
            for c in range(tq // sub):
                cols = slice(c * sub, (c + 1) * sub)
                nk = (c + 1) * sub if diagonal else tq
                qc, doc = q_ref[0, cols, :], do_ref[0, cols, :]
                st = lax.dot_general(k_ref[0, :nk, :], qc, NT, preferred_element_type=F32)
                lse2 = lse_ref[0, :1, cols] * LOG2E
                pt = jnp.exp2(st * C_EXP2 - lse2)
                if diagonal:
                    kpos = lax.broadcasted_iota(jnp.int32, st.shape, 0)
                    qpos = c * sub + lax.broadcasted_iota(jnp.int32, st.shape, 1)
                    pt = jnp.where(kpos <= qpos, pt, 0.0)
                dpt = lax.dot_general(v_ref[0, :nk, :C_V], doc, NT, preferred_element_type=F32)
                delta = jnp.sum(doc.astype(F32) * o_ref[0, cols, :], axis=1, keepdims=True)
                delta_row = jnp.transpose(jnp.broadcast_to(delta, (sub, LANES)))[:1]
                dst = (pt * (dpt - delta_row)).astype(BF16)
                dv_acc[:nk, :] += jnp.dot(pt.astype(BF16), doc, preferred_element_type=F32)
                dk_acc[:nk, :] += jnp.dot(dst, qc, preferred_element_type=F32)
                out_cols = pl.ds(pl.multiple_of(qi * tq + c * sub, sub), sub)
                dqt_ref[0, :, out_cols] += lax.dot_general(k_ref[0, :nk, :], dst, TN,
                                                           preferred_element_type=F32) * C_SCALE

        @pl.when(qi > ki)
        def _():
            tile(False)

        @pl.when(qi == ki)
        def _():
            tile(True)

        @pl.when(qi == n - 1)
        def _():
            dk_ref[0] = dk_acc[...] * C_SCALE
            dv_ref[0] = dv_acc[...]

    def q_spec(d):
        return pl.BlockSpec((1, tq, d), lambda hh, ki, qi: (hh, jnp.maximum(qi, ki), 0))

    def k_spec(d):
        return pl.BlockSpec((1, tq, d), lambda hh, ki, qi: (hh, ki, 0))

    return pl.pallas_call(
        body, name=name, grid=(h, n, n),
        in_specs=[q_spec(C_QK), k_spec(C_QK), k_spec(LANES), q_spec(C_V),
                  pl.BlockSpec((1, STAT_ROWS, tq), lambda hh, ki, qi: (hh, 0, jnp.maximum(qi, ki))), q_spec(C_V)],
        out_specs=[pl.BlockSpec((1, C_QK, t), lambda hh, ki, qi: (hh, 0, 0)), k_spec(C_QK), k_spec(C_V)],
        out_shape=[jax.ShapeDtypeStruct((h, C_QK, t), F32), jax.ShapeDtypeStruct((h, t, C_QK), F32),
                   jax.ShapeDtypeStruct((h, t, C_V), F32)],
        scratch_shapes=[pltpu.VMEM((tq, C_QK), F32), pltpu.VMEM((tq, C_V), F32)],
        compiler_params=_params(("parallel", "arbitrary", "arbitrary")),
    )(q, k, v1, o, lse, do)


def _exchange(src, flips, src_idx, name):
    n = len(flips)
    _, r, c = src.shape

    def body(src_ref, dst_ref, send_sems, recv_sems):
        me = [lax.axis_index(a) for a in ALL_AXES]
        copies = []
        for kk, flip in enumerate(flips):
            peer = tuple(1 - p if f else p for p, f in zip(me, flip))
            copies.append(pltpu.make_async_remote_copy(
                src_ref=src_ref.at[src_idx[kk]], dst_ref=dst_ref.at[kk], send_sem=send_sems.at[kk],
                recv_sem=recv_sems.at[kk], device_id=peer, device_id_type=MESH))
        for cp in copies:
            cp.start()
        for cp in copies:
            cp.wait_recv()
        for cp in copies:
            cp.wait_send()

    return pl.pallas_call(
        body, name=name, in_specs=[pl.BlockSpec(memory_space=pl.ANY)], out_specs=pl.BlockSpec(memory_space=pl.ANY),
        out_shape=jax.ShapeDtypeStruct((n, r, c), src.dtype),
        scratch_shapes=[pltpu.SemaphoreType.DMA((n,)), pltpu.SemaphoreType.DMA((n,))],
    )(src)


FLIP_C = (0, 0, 1)
CHIP_FLIPS = ((0, 1, 0), (1, 0, 0), (1, 1, 0))
ALL_FLIPS = tuple((a >> 2 & 1, a >> 1 & 1, a & 1) for a in range(1, 8))


def _pick(stacked, idx):
    return lax.dynamic_index_in_dim(stacked, idx, axis=0, keepdims=False)


def _to_comm_rows(parts, dtype):
    flat = jnp.concatenate([p.reshape(-1) for p in parts]).astype(dtype)
    rows = -(-flat.size // (COMM_COLS * COMM_ROW_ALIGN)) * COMM_ROW_ALIGN
    return jnp.pad(flat, (0, rows * COMM_COLS - flat.size)).reshape(rows, COMM_COLS)


def _from_comm_rows(buf, shapes):
    flat, out, off = buf.reshape(-1), [], 0
    for s in shapes:
        size = math.prod(s)
        out.append(flat[off:off + size].reshape(s))
        off += size
    return out


def _my_place():
    x, y, c = (lax.axis_index(a) for a in ALL_AXES)
    return (x, y, c), 2 * x + y


def _flipped(me, flip):
    return tuple(1 - p if f else p for p, f in zip(me, flip))


def _remote(src, dst, sems, k, peer):
    send_sems, recv_sems = sems
    return pltpu.make_async_remote_copy(src_ref=src, dst_ref=dst, send_sem=send_sems.at[k], recv_sem=recv_sems.at[k],
                                        device_id=peer, device_id_type=MESH)


def _gather_shards(buf):
    rows, cols = buf.shape
    half = rows // 2
    n = len(CHIP_FLIPS)

    def body(src_ref, out_ref, send_sems, recv_sems, local_sem):
        me, chip = _my_place()
        sems = (send_sems, recv_sems)
        sibling = _flipped(me, FLIP_C)
        mine = pl.ds(pl.multiple_of(me[2] * half, half), half)
        theirs = pl.ds(pl.multiple_of((1 - me[2]) * half, half), half)
        own = pltpu.make_async_copy(src_ref, out_ref.at[chip], local_sem)
        own.start()
        peers = [_flipped(me, f) for f in CHIP_FLIPS]
        from_chip = [2 * p[0] + p[1] for p in peers]
        over_ici = [_remote(src_ref.at[mine], out_ref.at[chip, mine], sems, k, peers[k]) for k in range(n)]
        for cp in over_ici:
            cp.start()
        passed = [_remote(out_ref.at[from_chip[k], mine], out_ref.at[from_chip[k], mine], sems, n + k, sibling)
                  for k in range(n)]
        for k in range(n):
            _remote(src_ref.at[mine], out_ref.at[from_chip[k], mine], sems, k, peers[k]).wait_recv()
            passed[k].start()
        for k in range(n):
            _remote(out_ref.at[from_chip[k], theirs], out_ref.at[from_chip[k], theirs], sems, n + k, sibling).wait_recv()
        for cp in over_ici + passed:
            cp.wait_send()
        own.wait()

    return pl.pallas_call(
        body, name="gather_shards", in_specs=[pl.BlockSpec(memory_space=pl.ANY)],
        out_specs=pl.BlockSpec(memory_space=pl.ANY), out_shape=jax.ShapeDtypeStruct((4, rows, cols), buf.dtype),
        scratch_shapes=[pltpu.SemaphoreType.DMA((2 * n,)), pltpu.SemaphoreType.DMA((2 * n,)), pltpu.SemaphoreType.DMA],
    )(buf)


def _swap_other_halves(portions):
    _, rows, cols = portions.shape
    half = rows // 2

    def body(src_ref, dst_ref, send_sems, recv_sems):
        me, _ = _my_place()
        theirs = pl.ds(pl.multiple_of((1 - me[2]) * half, half), half)
        cp = _remote(src_ref.at[:, theirs], dst_ref, (send_sems, recv_sems), 0, _flipped(me, FLIP_C))
        cp.start()
        cp.wait_recv()
        cp.wait_send()

    return pl.pallas_call(
        body, name="reduce_d2d", in_specs=[pl.BlockSpec(memory_space=pl.ANY)],
        out_specs=pl.BlockSpec(memory_space=pl.ANY), out_shape=jax.ShapeDtypeStruct((4, half, cols), portions.dtype),
        scratch_shapes=[pltpu.SemaphoreType.DMA((1,)), pltpu.SemaphoreType.DMA((1,))],
    )(portions)


def _join_halves(total):
    half, cols = total.shape

    def body(src_ref, out_ref, send_sems, recv_sems, local_sem):
        me, _ = _my_place()
        mine = pl.ds(pl.multiple_of(me[2] * half, half), half)
        theirs = pl.ds(pl.multiple_of((1 - me[2]) * half, half), half)
        own = pltpu.make_async_copy(src_ref, out_ref.at[mine], local_sem)
        own.start()
        sems = (send_sems, recv_sems)
        cp = _remote(src_ref, out_ref.at[mine], sems, 0, _flipped(me, FLIP_C))
        cp.start()
        _remote(src_ref, out_ref.at[theirs], sems, 0, _flipped(me, FLIP_C)).wait_recv()
        cp.wait_send()
        own.wait()

    return pl.pallas_call(
        body, name="reduce_share", in_specs=[pl.BlockSpec(memory_space=pl.ANY)],
        out_specs=pl.BlockSpec(memory_space=pl.ANY), out_shape=jax.ShapeDtypeStruct((2 * half, cols), total.dtype),
        scratch_shapes=[pltpu.SemaphoreType.DMA((1,)), pltpu.SemaphoreType.DMA((1,)), pltpu.SemaphoreType.DMA],
    )(total)


def _reduce_scatter(portions, chip, core):
    half = portions.shape[1] // 2
    keep = lax.dynamic_slice_in_dim(portions, core * half, half, axis=1)
    got = _swap_other_halves(portions)
    pair = _ew(lambda p, q: (p + q,), [keep.reshape(4 * half, -1), got.reshape(4 * half, -1)],
               [(COMM_COLS, F32)], "reduce_pair_sum")[0].reshape(keep.shape)
    out = jnp.stack([_pick(pair, jnp.bitwise_xor(chip, f)) for f in (1, 2, 3)]).astype(BF16)
    others = _exchange(out, CHIP_FLIPS, (0, 1, 2), "reduce_ici")
    total = _ew(lambda p, q, r, s: (p + q.astype(F32) + r.astype(F32) + s.astype(F32),),
                [_pick(pair, chip), others[0], others[1], others[2]],
                [(COMM_COLS, F32)], "reduce_chip_sum")[0]
    return _join_halves(total)


def _all_reduce_small(buf, dev):
    got = _exchange(buf[None], ALL_FLIPS, (0,) * 7, "small_gather")
    by_flip = jnp.concatenate([buf[None], got])
    ordered = [_pick(by_flip, jnp.bitwise_xor(dev, a)) for a in range(8)]

    def fn(*t):
        s = t[0]
        for u in t[1:]:
            s = s + u
        return (s,)

    return _ew(fn, ordered, [(buf.shape[1], F32)], "small_sum")[0]


def _band_bucket_onehot(dilation, max_dist):
    i = jnp.arange(BLOCK)[:, None]
    j = jnp.arange(2 * BLOCK)[None, :]
    dist = i + BLOCK - j
    inband = (dist >= 0) & (dist <= max_dist)
    n = jnp.maximum(dist, 0) * dilation
    max_exact = NUM_BUCKETS // 2
    nf = jnp.maximum(n, 1).astype(F32)
    large = max_exact + (jnp.log(nf / max_exact) / math.log(MAX_DISTANCE / max_exact)
                         * (NUM_BUCKETS - max_exact)).astype(jnp.int32)
    bucket = jnp.where(n < max_exact, n, jnp.minimum(large, NUM_BUCKETS - 1))
    onehot = (bucket[..., None] == jnp.arange(NUM_BUCKETS)) & inband[..., None]
    return onehot.reshape(-1, NUM_BUCKETS).astype(F32), inband.reshape(-1)


def _band_bias(table, onehot, inband):
    vals = jnp.einsum("pb,bh->hp", onehot, table, precision=lax.Precision.HIGHEST)
    return jnp.where(inband[None, :], vals, NEG).reshape(-1, BLOCK, 2 * BLOCK)


BAND_VARIANTS = ((1, A_WINDOW - 1),) + tuple((dil, window // dil) for window, dil in B_BRANCHES)


LAYOUT_TILE_BYTES = 4 << 20


def _layout_rows(length, row_bytes):
    return _divisor_tile(length, max(16, LAYOUT_TILE_BYTES // row_bytes), 16)


def _split_heads(items, w, dil, name):
    s = items[0][0].shape[0]
    length = s // dil
    row_bytes = sum(-(-n * w // LANES) * LANES * (x.dtype.itemsize + jnp.dtype(d).itemsize) for x, _, n, d in items)
    tr = _layout_rows(length, row_bytes)
    nt = length // tr

    def body(*refs):
        for (x_ref, o_ref), (_, _, n, _) in zip(zip(refs[:len(items)], refs[len(items):]), items):
            for j in range(n):
                o_ref[j] = x_ref[:, j * w:(j + 1) * w].astype(o_ref.dtype)

    in_specs, out_specs, out_shape, views = [], [], [], []
    for x, first, n, d in items:
        bw, width = n * w, x.shape[1]
        assert bw % LANES == 0 and first % bw == 0 and (dil == 1 or width % bw == 0)
        in_specs.append(pl.BlockSpec((tr, bw), lambda r, i, c0=first // bw, wb=width // bw: (i, r * wb + c0)))
        out_specs.append(pl.BlockSpec((n, tr, w), lambda r, i: (0, r * nt + i, 0)))
        out_shape.append(jax.ShapeDtypeStruct((n, s, w), d))
        views.append(x.reshape(length, dil * width))
    return pl.pallas_call(
        body, name=name, grid=(dil, nt), in_specs=in_specs, out_specs=out_specs, out_shape=out_shape,
        compiler_params=_params(("parallel", "parallel")),
    )(*views)


def _merge_heads(items, dil, name, group_sum=1):
    s, w = items[0][0].shape[1:]
    length = s // dil
    row_bytes = sum(t.shape[0] * LANES * t.dtype.itemsize + t.shape[0] * w * jnp.dtype(d).itemsize for t, d in items)
    tr = _layout_rows(length, row_bytes)
    nt = length // tr

    def body(*refs):
        for t_ref, o_ref in zip(refs[:len(items)], refs[len(items):]):
            for j in range(t_ref.shape[0] // group_sum):
                v = t_ref[j * group_sum]
                for g in range(1, group_sum):
                    v = v + t_ref[j * group_sum + g]
                o_ref[:, j * w:(j + 1) * w] = v.astype(o_ref.dtype)

    in_specs, out_specs, out_shape = [], [], []
    for t, d in items:
        n = t.shape[0]
        bw = n // group_sum * w
        assert bw % LANES == 0
        in_specs.append(pl.BlockSpec((n, tr, w), lambda r, i: (0, r * nt + i, 0)))
        out_specs.append(pl.BlockSpec((tr, bw), lambda r, i: (i, r)))
        out_shape.append(jax.ShapeDtypeStruct((length, dil * bw), d))
    outs = pl.pallas_call(
        body, name=name, grid=(dil, nt), in_specs=in_specs, out_specs=out_specs, out_shape=out_shape,
        compiler_params=_params(("parallel", "parallel")),
    )(*[t for t, _ in items])
    return [o.reshape(s, -1) for o in outs]


def _merge_heads_transposed(t, dtype, name):
    n, w, s = t.shape
    tr = _layout_rows(s, 2 * n * w * t.dtype.itemsize)

    def body(t_ref, o_ref):
        o_ref[...] = jnp.transpose(t_ref[...].reshape(n * w, tr)).astype(o_ref.dtype)

    return pl.pallas_call(
        body, name=name, grid=(s // tr,), in_specs=[pl.BlockSpec((n, w, tr), lambda i: (0, 0, i))],
        out_specs=pl.BlockSpec((tr, n * w), lambda i: (i, 0)), out_shape=jax.ShapeDtypeStruct((s, n * w), dtype),
        compiler_params=_params(("parallel",)),
    )(t)


ROPE_HALF = C_ROPE // 2
ROPE_PERIOD = 3 * LANES


def _rope_tables(s):
    inv = ROPE_THETA ** (-jnp.arange(0, C_ROPE, 2, dtype=F32) / C_ROPE)
    ang = jnp.arange(s, dtype=F32)[:, None] * inv[None, :]
    cos, sin = jnp.cos(ang), jnp.sin(ang)
    one, zero = jnp.ones((s, C_NOPE), F32), jnp.zeros((s, C_NOPE), F32)
    z16 = jnp.zeros((s, ROPE_HALF), F32)
    reps = ROPE_PERIOD // C_QK
    keep = jnp.tile(jnp.concatenate([one, cos, cos], axis=1), (1, reps))
    from_above = jnp.tile(jnp.concatenate([zero, -sin, z16], axis=1), (1, reps))
    from_below = jnp.tile(jnp.concatenate([zero, z16, sin], axis=1), (1, reps))
    return (cos, sin), (keep, from_above, from_below)


def _rope_rows(x, tables, inverse, name, dtype):
    width = x.shape[1]
    reps = width // ROPE_PERIOD
    sign = -1.0 if inverse else 1.0

    def fn(xv, keep, above, below):
        keep, above, below = (jnp.tile(t, (1, reps)) for t in (keep, above, below))
        up = pltpu.roll(xv, width - ROPE_HALF, 1)
        down = pltpu.roll(xv, ROPE_HALF, 1)
        return (xv * keep + sign * (up * above + down * below),)

    return _ew(fn, [x, *tables], [(width, dtype)], name)[0]


def _rotate_half_pairs(a, b, cos, sin, inverse):
    if inverse:
        return a * cos + b * sin, b * cos - a * sin
    return a * cos - b * sin, a * sin + b * cos


def _split_kv(kv, down, cos, sin, name):
    s = kv.shape[0]
    h = kv.shape[1] // (C_NOPE + C_V)
    tr = _layout_rows(s, 8 * kv.shape[1])
    r0 = C_Q_RANK + C_KV_RANK

    def body(kv_ref, down_ref, cos_ref, sin_ref, k_ref, v_ref):
        k1, k2 = _rotate_half_pairs(down_ref[:, r0:r0 + ROPE_HALF], down_ref[:, r0 + ROPE_HALF:r0 + C_ROPE],
                                    cos_ref[...], sin_ref[...], False)
        lane = lax.broadcasted_iota(jnp.int32, (tr, LANES - C_V), 1)
        tail = jnp.where(lane == 0, 1.0, 0.0).astype(BF16)
        for j in range(h):
            base = j * (C_NOPE + C_V)
            k_ref[j, :, :C_NOPE] = kv_ref[:, base:base + C_NOPE]
            k_ref[j, :, C_NOPE:C_NOPE + ROPE_HALF] = k1.astype(BF16)
            k_ref[j, :, C_NOPE + ROPE_HALF:] = k2.astype(BF16)
            v_ref[j, :, :C_V] = kv_ref[:, base + C_NOPE:base + C_NOPE + C_V]
            v_ref[j, :, C_V:] = tail

    def rows(width):
        return pl.BlockSpec((tr, width), lambda i: (i, 0))

    return pl.pallas_call(
        body, name=name, grid=(s // tr,),
        in_specs=[rows(kv.shape[1]), rows(down.shape[1]), rows(ROPE_HALF), rows(ROPE_HALF)],
        out_specs=[pl.BlockSpec((h, tr, C_QK), lambda i: (0, i, 0)), pl.BlockSpec((h, tr, LANES), lambda i: (0, i, 0))],
        out_shape=[jax.ShapeDtypeStruct((h, s, C_QK), BF16), jax.ShapeDtypeStruct((h, s, LANES), BF16)],
        compiler_params=_params(("parallel",)),
    )(kv, down, cos, sin)


def _merge_kv_bwd(dk, dv, cos, sin, name):
    h, s, _ = dk.shape
    tr = _layout_rows(s, 8 * h * LANES)

    def body(dk_ref, dv_ref, cos_ref, sin_ref, dkv_ref, dkr_ref):
        rot = dk_ref[0, :, C_NOPE:]
        for j in range(h):
            base = j * (C_NOPE + C_V)
            dkv_ref[:, base:base + C_NOPE] = dk_ref[j, :, :C_NOPE].astype(BF16)
            dkv_ref[:, base + C_NOPE:base + C_NOPE + C_V] = dv_ref[j].astype(BF16)
            if j:
                rot = rot + dk_ref[j, :, C_NOPE:]
        d1, d2 = _rotate_half_pairs(rot[:, :ROPE_HALF], rot[:, ROPE_HALF:], cos_ref[...], sin_ref[...], True)
        dkr_ref[:, :ROPE_HALF] = d1
        dkr_ref[:, ROPE_HALF:] = d2

    def rows(width):
        return pl.BlockSpec((tr, width), lambda i: (i, 0))

    return pl.pallas_call(
        body, name=name, grid=(s // tr,),
        in_specs=[pl.BlockSpec((h, tr, C_QK), lambda i: (0, i, 0)), pl.BlockSpec((h, tr, C_V), lambda i: (0, i, 0)),
                  rows(ROPE_HALF), rows(ROPE_HALF)],
        out_specs=[rows(h * (C_NOPE + C_V)), rows(C_ROPE)],
        out_shape=[jax.ShapeDtypeStruct((s, h * (C_NOPE + C_V)), BF16), jax.ShapeDtypeStruct((s, C_ROPE), F32)],
        compiler_params=_params(("parallel",)),
    )(dk, dv, cos, sin)


def _even_fwd(xn, h, w_in, w_out, sinks_row, biases, tag):
    s = xn.shape[0]
    proj = _mm(xn, w_in, "nn", [BF16], f"in_proj{tag}")[0]
    qd, kd = A_Q_HEADS * HEAD_DIM, A_KV_HEADS * HEAD_DIM
    qa, ka, va = _split_heads([(proj, 0, A_Q_HEADS, BF16), (proj, qd, A_KV_HEADS, BF16),
                               (proj, qd + kd, A_KV_HEADS, BF16)], HEAD_DIM, 1, f"swa_split{tag}")
    oa, lse_a = _band_fwd(qa, ka, va, biases[0], sinks_row, s // BLOCK, f"swa_fwd{tag}")
    out_a = _merge_heads([(oa, BF16)], 1, f"swa_merge{tag}")[0]
    width = B_HEADS_PER_BRANCH * HEAD_DIM
    qkv_b, outs, lses = [], [], []
    for g, (_, dil) in enumerate(B_BRANCHES):
        base = A_IN + g * 3 * width
        src, base = (proj, base) if dil == 1 else (proj[:, base:base + 3 * width], 0)
        qkv = _split_heads([(src, base + i * width, B_HEADS_PER_BRANCH, BF16) for i in range(3)], HEAD_DIM, dil,
                           f"dil{g}_split{tag}")
        og, lg = _band_fwd(*qkv, biases[1 + g], None, s // dil // BLOCK, f"dil{g}_fwd{tag}")
        qkv_b.append(qkv)
        merged = _merge_heads([(og, F32), (lg, F32)], dil, f"dil{g}_merge{tag}")
        outs.append(merged[0])
        lses.append(merged[1])

    def merge(o0, o1, o2, l0, l1, l2):
        m = jnp.maximum(jnp.maximum(l0, l1), l2)
        e0, e1, e2 = jnp.exp(l0 - m), jnp.exp(l1 - m), jnp.exp(l2 - m)
        den = e0 + e1 + e2
        out = (e0 * o0 + e1 * o1 + e2 * o2) / den
        return out, m + jnp.log(den), out

    out_b, lse_b, out_b16 = _ew(merge, outs + lses, [(width, F32), (width, F32), (width, BF16)], f"dil_merge{tag}")
    cat = jnp.concatenate([out_a, out_b16], axis=1)
    h_mid = _mm(cat, w_out, "nn", [F32], f"out_proj{tag}", epi=_add_epi, extras=(h,))[0]
    return h_mid, (qa, ka, va, oa, lse_a, qkv_b, out_b, lse_b, cat)


def _even_bwd(dh, xn, saved, w_in, w_out, sinks_row, biases, onehots, tag):
    qa, ka, va, oa, lse_a, qkv_b, out_b, lse_b, cat = saved
    s = xn.shape[0]
    qd = A_Q_HEADS * HEAD_DIM
    g_w_out = _mm(cat, dh, "tn", [F32], f"out_proj_dw{tag}")[0]
    dcat = _mm(dh, w_out, "nt", [BF16], f"out_proj_dx{tag}")[0]
    do_a = _split_heads([(dcat, 0, A_Q_HEADS, BF16)], HEAD_DIM, 1, f"swa_do_split{tag}")[0]
    dqa, dka8, dva8, dbias_a, dsink = _band_bwd(qa, ka, va, oa, lse_a, do_a, biases[0], sinks_row, s // BLOCK,
                                                f"swa_bwd{tag}")
    pieces = _merge_heads([(dqa, BF16)], 1, f"swa_dq_merge{tag}")
    pieces += _merge_heads([(dka8, BF16), (dva8, BF16)], 1, f"swa_dkv_merge{tag}", group_sum=A_Q_HEADS // A_KV_HEADS)
    dbias_b = []
    dcat_b = dcat[:, qd:]
    for g, (_, dil) in enumerate(B_BRANCHES):
        do_g, out_g, lse_g = _split_heads([(dcat_b, 0, B_HEADS_PER_BRANCH, BF16), (out_b, 0, B_HEADS_PER_BRANCH, F32),
                                           (lse_b, 0, B_HEADS_PER_BRANCH, F32)], HEAD_DIM, dil, f"dil{g}_do_split{tag}")
        dqg, dkg, dvg, dbg, _ = _band_bwd(*qkv_b[g], out_g, lse_g, do_g, biases[1 + g], None, s // dil // BLOCK,
                                          f"dil{g}_bwd{tag}")
        pieces += _merge_heads([(dqg, BF16), (dkg, BF16), (dvg, BF16)], dil, f"dil{g}_dqkv_merge{tag}")
        dbias_b.append(dbg)
    dproj = jnp.concatenate(pieces, axis=1)
    g_w_in = _mm(xn, dproj, "tn", [F32], f"in_proj_dw{tag}")[0]
    dxn = _mm(dproj, w_in, "nt", [F32], f"in_proj_dx{tag}")[0]
    cols = [_mm(db.reshape(db.shape[0], -1), onehots[v][0], "nn", [F32], f"bias_buckets{v}{tag}")[0].T
            for v, db in enumerate([dbias_a] + dbias_b)]
    return dxn, g_w_in, g_w_out, jnp.concatenate(cols, axis=1), dsink[:, 0, 0]


def _mla_layer_fwd(xn, h, w_dn, q_norm, w_uq, kv_norm, w_ukv, w_o, ropes, tag):
    (cos, sin), q_tables = ropes
    down = _mm(xn, w_dn, "nn", [F32], f"mla_down{tag}")[0]
    c_q, c_kv = down[:, :C_Q_RANK], down[:, C_Q_RANK:C_Q_RANK + C_KV_RANK]
    cqn = _rms_fwd(c_q, q_norm, f"mla_qnorm{tag}")
    ckvn = _rms_fwd(c_kv, kv_norm, f"mla_kvnorm{tag}")
    q = _mm(cqn, w_uq, "nn", [F32], f"mla_uq{tag}")[0]
    kv = _mm(ckvn, w_ukv, "nn", [BF16], f"mla_ukv{tag}")[0]
    qh = _split_heads([(_rope_rows(q, q_tables, False, f"mla_rope_q{tag}", BF16), 0, C_HEADS, BF16)], C_QK, 1,
                      f"mla_q_split{tag}")[0]
    kh, v1h = _split_kv(kv, down, cos, sin, f"mla_kv_split{tag}")
    ot, lse = _mla_fwd(qh, kh, v1h.transpose(0, 2, 1), f"mla_attn_fwd{tag}")
    o = ot.transpose(0, 2, 1)
    o2d = _merge_heads_transposed(ot, BF16, f"mla_o_merge{tag}")
    h_mid = _mm(o2d, w_o, "nn", [F32], f"mla_o{tag}", epi=_add_epi, extras=(h,))[0]
    return h_mid, (c_q, c_kv, cqn, ckvn, qh, kh, v1h, o, lse, o2d)


def _mla_layer_bwd(dh, xn, saved, w_dn, q_norm, w_uq, kv_norm, w_ukv, w_o, ropes, tag):
    c_q, c_kv, cqn, ckvn, qh, kh, v1h, o, lse, o2d = saved
    (cos, sin), q_tables = ropes
    g_w_o = _mm(o2d, dh, "tn", [F32], f"mla_o_dw{tag}")[0]
    do2d = _mm(dh, w_o, "nt", [BF16], f"mla_o_dx{tag}")[0]
    do = _split_heads([(do2d, 0, C_HEADS, BF16)], C_V, 1, f"mla_do_split{tag}")[0]
    dqt, dk, dv = _mla_bwd(qh, kh, v1h, o, lse, do, f"mla_attn_bwd{tag}")
    dq_roped = _merge_heads_transposed(dqt, F32, f"mla_dq_merge{tag}")
    dq = _rope_rows(dq_roped, q_tables, True, f"mla_rope_q_bwd{tag}", BF16)
    dkv, dk_rope = _merge_kv_bwd(dk, dv, cos, sin, f"mla_dkv_merge{tag}")
    g_w_uq = _mm(cqn, dq, "tn", [F32], f"mla_uq_dw{tag}")[0]
    dcqn = _mm(dq, w_uq, "nt", [F32], f"mla_uq_dx{tag}")[0]
    g_w_ukv = _mm(ckvn, dkv, "tn", [F32], f"mla_ukv_dw{tag}")[0]
    dckvn = _mm(dkv, w_ukv, "nt", [F32], f"mla_ukv_dx{tag}")[0]
    dc_q, g_q_norm = _rms_bwd(c_q, q_norm, dcqn, None, f"mla_qnorm_bwd{tag}")
    dc_kv, g_kv_norm = _rms_bwd(c_kv, kv_norm, dckvn, None, f"mla_kvnorm_bwd{tag}")
    ddown = jnp.concatenate([dc_q, dc_kv, dk_rope], axis=1).astype(BF16)
    g_w_dn = _mm(xn, ddown, "tn", [F32], f"mla_down_dw{tag}")[0]
    dxn = _mm(ddown, w_dn, "nt", [F32], f"mla_down_dx{tag}")[0]
    return dxn, g_w_dn, g_q_norm, g_w_uq, g_kv_norm, g_w_ukv, g_w_o


SHARDED = (("w_in_ab", 2), ("w_out_ab", 2), ("w_down_c", 1), ("w_uq_c", 2), ("w_ukv_c", 2), ("w_o_c", 2),
           ("w_mlp_up", 2), ("w_mlp_down", 1))
SHARDED_NORMS = ("q_norm_c", "kv_norm_c")


def kernel(x, rel_bias, attn_norm, mlp_norm, final_norm, w_in_ab, sinks, w_out_ab, w_down_c, q_norm_c, w_uq_c, kv_norm_c, w_ukv_c, w_o_c, w_mlp_up, w_mlp_down, loss_target, m_rel_bias, m_attn_norm, m_mlp_norm, m_final_norm, m_w_in_ab, m_sinks, m_w_out_ab, m_w_down_c, m_q_norm_c, m_w_uq_c, m_kv_norm_c, m_w_ukv_c, m_w_o_c, m_w_mlp_up, m_w_mlp_down, v_rel_bias, v_attn_norm, v_mlp_norm, v_final_norm, v_w_in_ab, v_sinks, v_w_out_ab, v_w_down_c, v_q_norm_c, v_w_uq_c, v_kv_norm_c, v_w_ukv_c, v_w_o_c, v_w_mlp_up, v_w_mlp_down):
    given = dict(locals())
    chip = lax.axis_index("x") * 2 + lax.axis_index("y")
    core = lax.axis_index("c")
    dev = chip * 2 + core
    depth = attn_norm.shape[0]
    s = x.shape[1]

    shards = [given[n] for n, _ in SHARDED]
    norm_shards = [given[n] for n in SHARDED_NORMS]
    packed = _to_comm_rows([t.astype(BF16) for t in shards]
                           + [lax.bitcast_convert_type(t, BF16) for t in norm_shards], BF16)
    by_chip = _gather_shards(packed)
    shapes = [t.shape for t in shards] + [t.shape + (2,) for t in norm_shards]
    pieces = [_from_comm_rows(by_chip[a], shapes) for a in range(4)]
    full = {n: jnp.concatenate([pieces[a][i] for a in range(4)], axis=ax) for i, (n, ax) in enumerate(SHARDED)}
    for i, n in enumerate(SHARDED_NORMS):
        full[n] = jnp.concatenate([lax.bitcast_convert_type(pieces[a][len(SHARDED) + i], F32) for a in range(4)],
                                  axis=-1)

    onehots = [_band_bucket_onehot(dil, md) for dil, md in BAND_VARIANTS]
    head_cols = [(0, A_Q_HEADS)] + [(A_Q_HEADS + g * B_HEADS_PER_BRANCH, A_Q_HEADS + (g + 1) * B_HEADS_PER_BRANCH)
                                    for g in range(len(B_BRANCHES))]
    biases = [_band_bias(rel_bias[:, lo:hi], oh, inb) for (lo, hi), (oh, inb) in zip(head_cols, onehots)]
    ropes = _rope_tables(s)
    sink_rows = [jnp.broadcast_to(sinks[e][:, None, None], (A_Q_HEADS, 1, LANES)) for e in range(sinks.shape[0])]

    def odd_weights(o):
        return [full[n][o] for n in ("w_down_c", "q_norm_c", "w_uq_c", "kv_norm_c", "w_ukv_c", "w_o_c")]

    h = x[0]
    saved = []
    for l in range(depth):
        xn = _rms_fwd(h, attn_norm[l], f"attn_norm{l}")
        if l % 2 == 0:
            e = l // 2
            h_mid, mix = _even_fwd(xn, h, full["w_in_ab"][e], full["w_out_ab"][e], sink_rows[e], biases, f"_{l}")
        else:
            h_mid, mix = _mla_layer_fwd(xn, h, *odd_weights(l // 2), ropes, f"_{l}")
        xn2 = _rms_fwd(h_mid, mlp_norm[l], f"mlp_norm{l}")
        act, relu = _mm(xn2, full["w_mlp_up"][l], "nn", [BF16, BF16], f"mlp_up{l}", epi=_relu2_epi)
        h_out = _mm(act, full["w_mlp_down"][l], "nn", [F32], f"mlp_down{l}", epi=_add_epi, extras=(h_mid,))[0]
        saved.append((h, xn, mix, h_mid, xn2, act, relu))
        h = h_out
    loss_part, dh, dh16, g_final = _final_loss(h, final_norm, loss_target[0], "final_loss")

    grads = {n: [None] * given[n].shape[0] for n, _ in SHARDED}
    g_q_norm, g_kv_norm = [None] * q_norm_c.shape[0], [None] * kv_norm_c.shape[0]
    g_attn_norm, g_mlp_norm = [None] * depth, [None] * depth
    g_sinks = [None] * sinks.shape[0]
    g_rel_bias = None
    for l in range(depth - 1, -1, -1):
        h_in, xn, mix, h_mid, xn2, act, relu = saved[l]
        grads["w_mlp_down"][l] = _mm(act, dh16, "tn", [F32], f"mlp_down_dw{l}")[0]
        du = _mm(dh16, full["w_mlp_down"][l], "nt", [BF16], f"mlp_down_dx{l}", epi=_relu2_bwd_epi, extras=(relu,))[0]
        grads["w_mlp_up"][l] = _mm(xn2, du, "tn", [F32], f"mlp_up_dw{l}")[0]
        dxn2 = _mm(du, full["w_mlp_up"][l], "nt", [F32], f"mlp_up_dx{l}")[0]
        dh, dh16, g_mlp_norm[l] = _rms_bwd(h_mid, mlp_norm[l], dxn2, dh, f"mlp_norm_bwd{l}", also_bf16=True)
        if l % 2 == 0:
            e = l // 2
            dxn, grads["w_in_ab"][e], grads["w_out_ab"][e], g_table, g_sinks[e] = _even_bwd(
                dh16, xn, mix, full["w_in_ab"][e], full["w_out_ab"][e], sink_rows[e], biases, onehots, f"_{l}")
            g_rel_bias = g_table if g_rel_bias is None else g_rel_bias + g_table
        else:
            o = l // 2
            (dxn, grads["w_down_c"][o], g_q_norm[o], grads["w_uq_c"][o], g_kv_norm[o], grads["w_ukv_c"][o],
             grads["w_o_c"][o]) = _mla_layer_bwd(dh16, xn, mix, *odd_weights(o), ropes, f"_{l}")
        if l:
            dh, dh16, g_attn_norm[l] = _rms_bwd(h_in, attn_norm[l], dxn, dh, f"attn_norm_bwd{l}", also_bf16=True)
        else:
            dh, g_attn_norm[l] = _rms_bwd(h_in, attn_norm[l], dxn, dh, f"attn_norm_bwd{l}")
    grad_x = dh[None]

    portions = []
    for a in range(4):
        parts = []
        for n, ax in SHARDED:
            g = jnp.stack(grads[n])
            size = g.shape[ax] // 4
            parts.append(lax.slice_in_dim(g, a * size, (a + 1) * size, axis=ax))
        portions.append(_to_comm_rows(parts, F32))
    reduced = _from_comm_rows(_reduce_scatter(jnp.stack(portions), chip, core), [t.shape for t in shards])
    g_shard = {n: reduced[i] for i, (n, _) in enumerate(SHARDED)}

    small = [jnp.stack(g_attn_norm), jnp.stack(g_mlp_norm), g_final, g_rel_bias, jnp.stack(g_sinks),
             jnp.stack(g_q_norm), jnp.stack(g_kv_norm), loss_part.reshape(1)]
    small_shapes = [t.shape for t in small]
    summed = _from_comm_rows(_all_reduce_small(_to_comm_rows(small, F32)[:16], dev), small_shapes)
    loss = summed[7][0]
    g_small = dict(zip(("attn_norm", "mlp_norm", "final_norm", "rel_bias", "sinks"), summed[:5]))
    for n, g in zip(SHARDED_NORMS, summed[5:7]):
        size = g.shape[1] // 4
        g_shard[n] = lax.dynamic_slice_in_dim(g, chip * size, size, axis=1)

    order = ["rel_bias", "attn_norm", "mlp_norm", "final_norm", "w_in_ab", "sinks", "w_out_ab", "w_down_c",
             "q_norm_c", "w_uq_c", "kv_norm_c", "w_ukv_c", "w_o_c", "w_mlp_up", "w_mlp_down"]
    g_all = {**g_small, **g_shard}
    deltas, new_m, new_v = [], [], []
    for n in order:
        d, mn, vn = _adamw(given[n], g_all[n], given["m_" + n], given["v_" + n], f"adamw_{n}")
        deltas.append(d)
        new_m.append(mn)
        new_v.append(vn)
    return (loss, grad_x, *[g_all[n] for n in order], *deltas, *new_m, *new_v)
```

```python
import math

import jax
import jax.numpy as jnp
from jax import lax
from jax.experimental import pallas as pl
from jax.experimental.pallas import tpu as pltpu

F32 = jnp.float32
BF16 = jnp.bfloat16
MESH = pl.DeviceIdType.MESH
ALL_AXES = ("x", "y", "c")

EPS = 1e-6
NEG = -1e30
BLOCK = 128
HEAD_DIM = 64
A_Q_HEADS, A_KV_HEADS = 8, 2
A_WINDOW = 128
B_BRANCHES = ((128, 1), (512, 4), (2048, 16))
B_HEADS_PER_BRANCH = 4
NUM_BUCKETS, MAX_DISTANCE = 32, 2048
A_IN = (A_Q_HEADS + 2 * A_KV_HEADS) * HEAD_DIM
C_HEADS, C_NOPE, C_ROPE, C_V = 8, 64, 32, 64
C_Q_RANK, C_KV_RANK = 384, 256
ROPE_THETA = 10000.0
ADAM_LR, ADAM_B1, ADAM_B2, ADAM_EPS, ADAM_WD, ADAM_STEP = 0.001, 0.9, 0.999, 1e-08, 0.01, 10

V7X_VMEM_LIMIT_BYTES = 56 * 1024 * 1024
LANES = 128
COMM_COLS = 1024
COMM_ROW_ALIGN = 1024

NT = (((1,), (1,)), ((), ()))
NN = (((1,), (0,)), ((), ()))
TN = (((0,), (0,)), ((), ()))
DIMS = {"nn": NN, "nt": NT, "tn": TN}


def _params(sem):
    return pltpu.CompilerParams(dimension_semantics=sem, vmem_limit_bytes=V7X_VMEM_LIMIT_BYTES)


def _divisor_tile(n, limit, align):
    if n <= limit:
        return n
    t = (limit // align) * align
    while t >= align:
        if n % t == 0:
            return t
        t -= align
    return n


def _ew(fn, ins, outs, name, acc_outs=(), target_bytes=6 << 20):
    rows = max(a.shape[0] for a in ins)

    def vmem_row_bytes(cols, dtype):
        return -(-cols // LANES) * LANES * jnp.dtype(dtype).itemsize

    per_row = sum(vmem_row_bytes(a.shape[1], a.dtype) for a in ins if a.shape[0] == rows)
    per_row += sum(vmem_row_bytes(c, d) for c, d in outs)
    tr = _divisor_tile(rows, max(16, target_bytes // max(per_row, 1)), 16)
    n_in, n_out = len(ins), len(outs)

    def body(*refs):
        res = fn(*[r[...] for r in refs[:n_in]])
        for r, v in zip(refs[n_in:n_in + n_out], res[:n_out]):
            r[...] = v.astype(r.dtype)
        if acc_outs:
            acc_refs = refs[n_in + n_out:]

            @pl.when(pl.program_id(0) == 0)
            def _():
                for r in acc_refs:
                    r[...] = jnp.zeros_like(r)

            for r, v in zip(acc_refs, res[n_out:]):
                r[...] += v

    def spec(a):
        if a.shape[0] == rows:
            return pl.BlockSpec((tr, a.shape[1]), lambda i: (i, 0))
        return pl.BlockSpec((1, a.shape[1]), lambda i: (0, 0))

    out_shape = [jax.ShapeDtypeStruct((rows, c), d) for c, d in outs]
    out_shape += [jax.ShapeDtypeStruct((1, c), F32) for c in acc_outs]
    out_specs = [pl.BlockSpec((tr, c), lambda i: (i, 0)) for c, _ in outs]
    out_specs += [pl.BlockSpec((1, c), lambda i: (0, 0)) for c in acc_outs]
    return pl.pallas_call(
        body, name=name, grid=(rows // tr,), in_specs=[spec(a) for a in ins], out_specs=out_specs,
        out_shape=out_shape, compiler_params=_params(("arbitrary",)),
    )(*ins)


def _rms_fwd(x, g, name):
    def fn(xv, gv):
        return ((xv * lax.rsqrt(jnp.mean(xv * xv, axis=-1, keepdims=True) + EPS)) * gv,)

    return _ew(fn, [x, g.reshape(1, -1)], [(x.shape[1], BF16)], name)[0]


def _rms_bwd(x, g, dy, add, name, also_bf16=False):
    def fn(xv, gv, dyv, *rest):
        rstd = lax.rsqrt(jnp.mean(xv * xv, axis=-1, keepdims=True) + EPS)
        xh = xv * rstd
        dyg = dyv.astype(F32) * gv
        dx = rstd * (dyg - xh * jnp.mean(dyg * xh, axis=-1, keepdims=True))
        if rest:
            dx = dx + rest[0]
        return (dx,) * (2 if also_bf16 else 1) + (jnp.sum(dyv.astype(F32) * xh, axis=0, keepdims=True),)

    ins = [x, g.reshape(1, -1), dy] + ([] if add is None else [add])
    outs = [(x.shape[1], F32)] + ([(x.shape[1], BF16)] if also_bf16 else [])
    *dx, dg = _ew(fn, ins, outs, name, acc_outs=(x.shape[1],))
    return (*dx, dg[0])


def _final_loss(h, g, target, name):
    d = h.shape[1]

    def fn(xv, gv, tv):
        rstd = lax.rsqrt(jnp.mean(xv * xv, axis=-1, keepdims=True) + EPS)
        xh = xv * rstd
        err = xh * gv - tv
        part = 0.5 * jnp.sum(jnp.mean(err * err, axis=-1, keepdims=True), axis=0, keepdims=True)
        dy = err * (1.0 / d)
        dyg = dy * gv
        dx = rstd * (dyg - xh * jnp.mean(dyg * xh, axis=-1, keepdims=True))
        return dx, dx, jnp.broadcast_to(part, (1, LANES)), jnp.sum(dy * xh, axis=0, keepdims=True)

    dx, dx16, loss, dg = _ew(fn, [h, g.reshape(1, -1), target], [(d, F32), (d, BF16)], name, acc_outs=(LANES, d))
    return loss[0, 0], dx, dx16, dg[0]


def _adamw(w, g, m, v, name):
    def fn(wv, gv, mv, vv):
        mn = ADAM_B1 * mv + (1.0 - ADAM_B1) * gv
        vn = ADAM_B2 * vv + (1.0 - ADAM_B2) * jnp.square(gv)
        m_hat = mn / (1.0 - ADAM_B1 ** ADAM_STEP)
        v_hat = vn / (1.0 - ADAM_B2 ** ADAM_STEP)
        return -ADAM_LR * (m_hat / (jnp.sqrt(v_hat) + ADAM_EPS) + ADAM_WD * wv), mn, vn

    shape = w.shape
    cols = shape[-1] if w.ndim > 1 else w.size
    view = [t.reshape(-1, cols) for t in (w, g, m, v)]
    return [t.reshape(shape) for t in _ew(fn, view, [(cols, F32)] * 3, name)]


MM_VMEM_BUDGET_BYTES = 40 << 20


def _mm(a, b, dims, outs, name, epi=None, extras=(), tm=2048, tn=1024, tk=1024):
    if dims == "nn":
        (m, k), n = a.shape, b.shape[1]
    elif dims == "nt":
        (m, k), n = a.shape, b.shape[0]
    else:
        (k, m), n = a.shape, b.shape[1]
    tn, tk = _divisor_tile(n, tn, LANES), _divisor_tile(k, tk, LANES)

    def tile_bytes(rows):
        per_out = sum(jnp.dtype(d).itemsize for d in outs) + sum(e.dtype.itemsize for e in extras)
        return 2 * (rows * tk * a.dtype.itemsize + tk * tn * b.dtype.itemsize + rows * tn * per_out) + 4 * rows * tn

    tm = _divisor_tile(m, tm, LANES)
    while tile_bytes(tm) > MM_VMEM_BUDGET_BYTES and tm % (2 * LANES) == 0:
        tm //= 2
    nk = k // tk
    n_ex, n_out = len(extras), len(outs)

    def body(a_ref, b_ref, *rest):
        ex_refs, out_refs = rest[:n_ex], rest[n_ex:n_ex + n_out]

        def finish(acc):
            res = epi(acc, *[r[...] for r in ex_refs]) if epi else (acc,)
            for r, v in zip(out_refs, res):
                r[...] = v.astype(r.dtype)

        part = lax.dot_general(a_ref[...].astype(BF16), b_ref[...].astype(BF16), DIMS[dims],
                               preferred_element_type=F32)
        if nk == 1:
            finish(part)
        else:
            acc_ref = rest[-1]
            kk = pl.program_id(2)

            @pl.when(kk == 0)
            def _():
                acc_ref[...] = part

            @pl.when(kk > 0)
            def _():
                acc_ref[...] += part

            @pl.when(kk == nk - 1)
            def _():
                finish(acc_ref[...])

    if dims == "nn":
        a_spec = pl.BlockSpec((tm, tk), lambda i, j, kk: (i, kk))
        b_spec = pl.BlockSpec((tk, tn), lambda i, j, kk: (kk, j))
    elif dims == "nt":
        a_spec = pl.BlockSpec((tm, tk), lambda i, j, kk: (i, kk))
        b_spec = pl.BlockSpec((tn, tk), lambda i, j, kk: (j, kk))
    else:
        a_spec = pl.BlockSpec((tk, tm), lambda i, j, kk: (kk, i))
        b_spec = pl.BlockSpec((tk, tn), lambda i, j, kk: (kk, j))
    tile = pl.BlockSpec((tm, tn), lambda i, j, kk: (i, j))
    return pl.pallas_call(
        body, name=name, grid=(m // tm, n // tn, nk),
        in_specs=[a_spec, b_spec] + [tile] * n_ex, out_specs=[tile] * n_out,
        out_shape=[jax.ShapeDtypeStruct((m, n), d) for d in outs],
        scratch_shapes=[pltpu.VMEM((tm, tn), F32)] if nk > 1 else [],
        compiler_params=_params(("parallel", "parallel", "arbitrary")),
    )(a, b, *extras)


def _add_epi(acc, res):
    return (acc + res,)


def _relu2_epi(acc):
    r = jnp.maximum(acc, 0.0)
    return r * r, r


def _relu2_bwd_epi(acc, r):
    return (acc * (2.0 * r.astype(F32)),)


BAND_SPAN_MAX = 256


def _band_geometry(t, blocks_per_seq, span):
    rows = min(1024, blocks_per_seq * BLOCK)
    nb = rows // BLOCK
    assert blocks_per_seq % nb == 0 and t % rows == 0 and span <= BAND_SPAN_MAX
    return rows, nb, t // rows, min(span, rows)


def _span_bias(bias, span):
    n = span // BLOCK
    neg = jnp.full(bias.shape[:2] + (BLOCK,), NEG, F32)
    rows = [jnp.concatenate([neg] * a + [bias[:, :, :BLOCK], bias[:, :, BLOCK:]] + [neg] * (n - 1 - a), axis=2)
            for a in range(n)]
    return jnp.concatenate(rows, axis=1)


def _fold_span_bias_grad(dbias, span):
    n = span // BLOCK
    parts = [dbias[:, a * BLOCK:(a + 1) * BLOCK, a * BLOCK:(a + 2) * BLOCK] for a in range(n)]
    return sum(parts[1:], parts[0])


def _band_logits(qj, kk, bias, first):
    s = lax.dot_general(qj, kk, NT, preferred_element_type=F32) * (HEAD_DIM ** -0.5) + bias
    if first is not None:
        col = lax.broadcasted_iota(jnp.int32, s.shape, 1)
        s = jnp.where(col < jnp.where(first, BLOCK, 0), NEG, s)
    return s


def _band_fwd(q, k, v, bias, sinks, blocks_per_seq, span, name):
    hq, t, dh = q.shape
    group = hq // k.shape[0]
    rows, nb, nchunks, span = _band_geometry(t, blocks_per_seq, span)
    has_sink = sinks is not None

    def body(q_ref, kc_ref, kp_ref, vc_ref, vp_ref, bias_ref, *rest):
        o_ref, lse_ref = rest[-2:]
        i = pl.program_id(1)
        bias_v = bias_ref[0]
        sink = rest[0][0, :span, :1] if has_sink else None
        for j in range(rows // span):
            cur = slice(j * span, (j + 1) * span)
            prev = slice(j * span - BLOCK, j * span)
            kk = jnp.concatenate([kp_ref[0] if j == 0 else kc_ref[0, prev, :], kc_ref[0, cur, :]], axis=0)
            vv = jnp.concatenate([vp_ref[0] if j == 0 else vc_ref[0, prev, :], vc_ref[0, cur, :]], axis=0)
            first = lax.rem(i * nb, blocks_per_seq) == 0 if j == 0 else None
            s = _band_logits(q_ref[0, cur, :], kk, bias_v, first)
            m = jnp.max(s, axis=1, keepdims=True)
            if has_sink:
                m = jnp.maximum(m, sink)
            p = jnp.exp(s - m)
            l = jnp.sum(p, axis=1, keepdims=True)
            if has_sink:
                l = l + jnp.exp(sink - m)
            acc = jnp.dot(p.astype(BF16), vv, preferred_element_type=F32)
            o_ref[0, cur, :] = acc / l
            lse_ref[0, cur, :] = jnp.broadcast_to(m + jnp.log(l), (span, dh))

    cur_q = pl.BlockSpec((1, rows, dh), lambda h, i: (h, i, 0))
    cur_kv = pl.BlockSpec((1, rows, dh), lambda h, i: (h // group, i, 0))
    prev_kv = pl.BlockSpec((1, BLOCK, dh), lambda h, i: (h // group, jnp.maximum(i * nb - 1, 0), 0))
    in_specs = [cur_q, cur_kv, prev_kv, cur_kv, prev_kv, pl.BlockSpec((1, span, span + BLOCK), lambda h, i: (h, 0, 0))]
    ins = [q, k, k, v, v, _span_bias(bias, span)]
    if has_sink:
        in_specs.append(pl.BlockSpec((1, BAND_SPAN_MAX, LANES), lambda h, i: (h, 0, 0)))
        ins.append(sinks)
    return pl.pallas_call(
        body, name=name, grid=(hq, nchunks), in_specs=in_specs,
        out_specs=[cur_q, cur_q],
        out_shape=[jax.ShapeDtypeStruct((hq, t, dh), F32), jax.ShapeDtypeStruct((hq, t, dh), F32)],
        compiler_params=_params(("parallel", "arbitrary")),
    )(*ins)


def _band_bwd(q, k, v, o, lse, do, bias, sinks, blocks_per_seq, span, name):
    hq, t, dh = q.shape
    group = hq // k.shape[0]
    rows, nb, nchunks, span = _band_geometry(t, blocks_per_seq, span)
    per_span = span // BLOCK
    has_sink = sinks is not None
    scale = HEAD_DIM ** -0.5

    def body(q_ref, kc_ref, kp_ref, vc_ref, vp_ref, o_ref, lse_ref, do_ref, bias_ref, *rest):
        dq_ref, dk_ref, dv_ref, dbias_ref, dsink_ref, dk_carry, dv_carry = rest[-7:]
        step = pl.program_id(1)
        chunk = nchunks - 1 - step
        bias_v = bias_ref[0]
        sink = rest[0][0, :span, :1] if has_sink else None

        @pl.when(step == 0)
        def _():
            dk_carry[...] = jnp.zeros_like(dk_carry)
            dv_carry[...] = jnp.zeros_like(dv_carry)
            dbias_ref[...] = jnp.zeros_like(dbias_ref)
            dsink_ref[...] = jnp.zeros_like(dsink_ref)

        dks = [jnp.zeros((BLOCK, dh), F32) for _ in range(nb + 1)]
        dvs = [jnp.zeros((BLOCK, dh), F32) for _ in range(nb + 1)]
        dks[nb] = dk_carry[...]
        dvs[nb] = dv_carry[...]
        for j in range(rows // span - 1, -1, -1):
            cur = slice(j * span, (j + 1) * span)
            prev = slice(j * span - BLOCK, j * span)
            kk = jnp.concatenate([kp_ref[0] if j == 0 else kc_ref[0, prev, :], kc_ref[0, cur, :]], axis=0)
            vv = jnp.concatenate([vp_ref[0] if j == 0 else vc_ref[0, prev, :], vc_ref[0, cur, :]], axis=0)
            first = lax.rem(chunk * nb, blocks_per_seq) == 0 if j == 0 else None
            qj, doj = q_ref[0, cur, :], do_ref[0, cur, :]
            lse_j = lse_ref[0, cur, :][:, :1]
            p = jnp.exp(_band_logits(qj, kk, bias_v, first) - lse_j)
            dp = lax.dot_general(doj, vv, NT, preferred_element_type=F32)
            delta = jnp.sum(doj.astype(F32) * o_ref[0, cur, :], axis=1, keepdims=True)
            ds = p * (dp - delta)
            dbias_ref[0] += ds
            if has_sink:
                dsink = -jnp.sum(jnp.exp(sink - lse_j) * delta, axis=0, keepdims=True)
                dsink_ref[0] += jnp.broadcast_to(dsink, (1, LANES))
            dsb = (ds * scale).astype(BF16)
            dq_ref[0, cur, :] = jnp.dot(dsb, kk, preferred_element_type=F32)
            dkk = lax.dot_general(dsb, qj, TN, preferred_element_type=F32)
            dvv = lax.dot_general(p.astype(BF16), doj, TN, preferred_element_type=F32)
            for b in range(per_span + 1):
                piece = slice(b * BLOCK, (b + 1) * BLOCK)
                dks[j * per_span + b] += dkk[piece]
                dvs[j * per_span + b] += dvv[piece]
        for j in range(nb):
            cur = slice(j * BLOCK, (j + 1) * BLOCK)
            dk_ref[0, cur, :] = dks[j + 1]
            dv_ref[0, cur, :] = dvs[j + 1]
        dk_carry[...] = dks[0]
        dv_carry[...] = dvs[0]

    def rev(i):
        return nchunks - 1 - i

    cur_q = pl.BlockSpec((1, rows, dh), lambda h, i: (h, rev(i), 0))
    cur_kv = pl.BlockSpec((1, rows, dh), lambda h, i: (h // group, rev(i), 0))
    prev_kv = pl.BlockSpec((1, BLOCK, dh), lambda h, i: (h // group, jnp.maximum(rev(i) * nb - 1, 0), 0))
    cur_lse = cur_q
    per_head_bias = pl.BlockSpec((1, span, span + BLOCK), lambda h, i: (h, 0, 0))
    per_head_row = pl.BlockSpec((1, 1, LANES), lambda h, i: (h, 0, 0))
    in_specs = [cur_q, cur_kv, prev_kv, cur_kv, prev_kv, cur_q, cur_lse, cur_q, per_head_bias]
    ins = [q, k, k, v, v, o, lse, do, _span_bias(bias, span)]
    if has_sink:
        in_specs.append(pl.BlockSpec((1, BAND_SPAN_MAX, LANES), lambda h, i: (h, 0, 0)))
        ins.append(sinks)
    full = jax.ShapeDtypeStruct((hq, t, dh), F32)
    dq, dk, dv, dbias, dsink = pl.pallas_call(
        body, name=name, grid=(hq, nchunks), in_specs=in_specs,
        out_specs=[cur_q, cur_q, cur_q, per_head_bias, per_head_row],
        out_shape=[full, full, full, jax.ShapeDtypeStruct((hq, span, span + BLOCK), F32),
                   jax.ShapeDtypeStruct((hq, 1, LANES), F32)],
        scratch_shapes=[pltpu.VMEM((BLOCK, dh), F32), pltpu.VMEM((BLOCK, dh), F32)],
        compiler_params=_params(("parallel", "arbitrary")),
    )(*ins)
    return dq, dk, dv, _fold_span_bias_grad(dbias, span), dsink


C_QK = C_NOPE + C_ROPE
C_SCALE = C_QK ** -0.5
LOG2E = math.log2(math.e)
C_EXP2 = C_SCALE * LOG2E
CAUSAL_SUB = 256


def _causal_tile(t):
    return min(1024, t)


CAUSAL_Q_CHAIN = 128
CAUSAL_K_CHAIN = 256
STAT_ROWS = 8


def _mla_fwd(q, k, v1t, name):
    h, t, _ = q.shape
    tq = _causal_tile(t)
    n = t // tq
    qs, ks = min(CAUSAL_Q_CHAIN, tq), min(CAUSAL_K_CHAIN, tq)

    def body(q_ref, k_ref, v_ref, ot_ref, lse_ref, m_scr, acc_scr):
        qi, ki = pl.program_id(1), pl.program_id(2)

        @pl.when(ki == 0)
        def _():
            m_scr[...] = jnp.full_like(m_scr, NEG)
            acc_scr[...] = jnp.zeros_like(acc_scr)

        def tile(diagonal):
            for r in range(tq // qs):
                cols = slice(r * qs, (r + 1) * qs)
                q_sub = q_ref[0, cols, :]
                m, acc = m_scr[:1, cols], acc_scr[:, cols]
                for kc in range(tq // ks):
                    k0 = kc * ks
                    if diagonal and k0 > r * qs + qs - 1:
                        continue
                    st = lax.dot_general(k_ref[0, k0:k0 + ks, :], q_sub, NT, preferred_element_type=F32)
                    if diagonal and k0 + ks - 1 > r * qs:
                        kpos = k0 + lax.broadcasted_iota(jnp.int32, st.shape, 0)
                        qpos = r * qs + lax.broadcasted_iota(jnp.int32, st.shape, 1)
                        st = jnp.where(kpos <= qpos, st, NEG)
                    m_new = jnp.maximum(m, jnp.max(st, axis=0, keepdims=True))
                    alpha = jnp.exp2((m - m_new) * C_EXP2)
                    pt = jnp.exp2((st - m_new) * C_EXP2).astype(BF16)
                    acc = acc * alpha + jnp.dot(v_ref[0, :, k0:k0 + ks], pt, preferred_element_type=F32)
                    m = m_new
                m_scr[:, cols] = jnp.broadcast_to(m, (STAT_ROWS, qs))
                acc_scr[:, cols] = acc

        @pl.when(ki < qi)
        def _():
            tile(False)

        @pl.when(ki == qi)
        def _():
            tile(True)
            l = acc_scr[C_V:C_V + 1, :]
            ot_ref[0] = acc_scr[:C_V, :] / l
            lse_ref[0] = jnp.broadcast_to(m_scr[:1, :] * C_SCALE + jnp.log(l), (STAT_ROWS, tq))

    return pl.pallas_call(
        body, name=name, grid=(h, n, n),
        in_specs=[pl.BlockSpec((1, tq, C_QK), lambda hh, qi, ki: (hh, qi, 0)),
                  pl.BlockSpec((1, tq, C_QK), lambda hh, qi, ki: (hh, jnp.minimum(ki, qi), 0)),
                  pl.BlockSpec((1, LANES, tq), lambda hh, qi, ki: (hh, 0, jnp.minimum(ki, qi)))],
        out_specs=[pl.BlockSpec((1, C_V, tq), lambda hh, qi, ki: (hh, 0, qi)),
                   pl.BlockSpec((1, STAT_ROWS, tq), lambda hh, qi, ki: (hh, 0, qi))],
        out_shape=[jax.ShapeDtypeStruct((h, C_V, t), F32), jax.ShapeDtypeStruct((h, STAT_ROWS, t), F32)],
        scratch_shapes=[pltpu.VMEM((STAT_ROWS, tq), F32), pltpu.VMEM((LANES, tq), F32)],
        compiler_params=_params(("parallel", "arbitrary", "arbitrary")),
    )(q, k, v1t)


def _mla_bwd(q, k, v1, o, lse, do, name):
    h, t, _ = q.shape
    tq = _causal_tile(t)
    n = t // tq
    sub = min(CAUSAL_SUB, tq)

    def body(q_ref, k_ref, v_ref, o_ref, lse_ref, do_ref, dqt_ref, dk_ref, dv_ref, dk_acc, dv_acc):
        ki, qi = pl.program_id(1), pl.program_id(2)

        @pl.when(qi == 0)
        def _():
            dk_acc[...] = jnp.zeros_like(dk_acc)
            dv_acc[...] = jnp.zeros_like(dv_acc)

        @pl.when(jnp.logical_and(ki == 0, qi == 0))
        def _():
            dqt_ref[...] = jnp.zeros_like(dqt_ref)

        def tile(diagonal):
            for c in range(tq // sub):
                cols = slice(c * sub, (c + 1) * sub)
                nk = (c + 1) * sub if diagonal else tq
                qc, doc = q_ref[0, cols, :], do_ref[0, cols, :]
                st = lax.dot_general(k_ref[0, :nk, :], qc, NT, preferred_element_type=F32)
                lse2 = lse_ref[0, :1, cols] * LOG2E
                pt = jnp.exp2(st * C_EXP2 - lse2)
                if diagonal:
                    kpos = lax.broadcasted_iota(jnp.int32, st.shape, 0)
                    qpos = c * sub + lax.broadcasted_iota(jnp.int32, st.shape, 1)
                    pt = jnp.where(kpos <= qpos, pt, 0.0)
                dpt = lax.dot_general(v_ref[0, :nk, :C_V], doc, NT, preferred_element_type=F32)
                delta = jnp.sum(doc.astype(F32) * o_ref[0, cols, :], axis=1, keepdims=True)
                delta_row = jnp.transpose(jnp.broadcast_to(delta, (sub, LANES)))[:1]
                dst = (pt * (dpt - delta_row)).astype(BF16)
                dv_acc[:nk, :] += jnp.dot(pt.astype(BF16), doc, preferred_element_type=F32)
                dk_acc[:nk, :] += jnp.dot(dst, qc, preferred_element_type=F32)
                out_cols = pl.ds(pl.multiple_of(qi * tq + c * sub, sub), sub)
                dqt_ref[0, :, out_cols] += lax.dot_general(k_ref[0, :nk, :], dst, TN,
                                                           preferred_element_type=F32) * C_SCALE

        @pl.when(qi > ki)
        def _():
            tile(False)

        @pl.when(qi == ki)
        def _():
            tile(True)

        @pl.when(qi == n - 1)
        def _():
            dk_ref[0] = dk_acc[...] * C_SCALE
            dv_ref[0] = dv_acc[...]

    def q_spec(d):
        return pl.BlockSpec((1, tq, d), lambda hh, ki, qi: (hh, jnp.maximum(qi, ki), 0))

    def k_spec(d):
        return pl.BlockSpec((1, tq, d), lambda hh, ki, qi: (hh, ki, 0))

    return pl.pallas_call(
        body, name=name, grid=(h, n, n),
        in_specs=[q_spec(C_QK), k_spec(C_QK), k_spec(LANES), q_spec(C_V),
                  pl.BlockSpec((1, STAT_ROWS, tq), lambda hh, ki, qi: (hh, 0, jnp.maximum(qi, ki))), q_spec(C_V)],
        out_specs=[pl.BlockSpec((1, C_QK, t), lambda hh, ki, qi: (hh, 0, 0)), k_spec(C_QK), k_spec(C_V)],
        out_shape=[jax.ShapeDtypeStruct((h, C_QK, t), F32), jax.ShapeDtypeStruct((h, t, C_QK), F32),
                   jax.ShapeDtypeStruct((h, t, C_V), F32)],
        scratch_shapes=[pltpu.VMEM((tq, C_QK), F32), pltpu.VMEM((tq, C_V), F32)],
        compiler_params=_params(("parallel", "arbitrary", "arbitrary")),
    )(q, k, v1, o, lse, do)


def _exchange(src, flips, src_idx, name):
    n = len(flips)
    _, r, c = src.shape

    def body(src_ref, dst_ref, send_sems, recv_sems):
        me = [lax.axis_index(a) for a in ALL_AXES]
        copies = []
        for kk, flip in enumerate(flips):
            peer = tuple(1 - p if f else p for p, f in zip(me, flip))
            copies.append(pltpu.make_async_remote_copy(
                src_ref=src_ref.at[src_idx[kk]], dst_ref=dst_ref.at[kk], send_sem=send_sems.at[kk],
                recv_sem=recv_sems.at[kk], device_id=peer, device_id_type=MESH))
        for cp in copies:
            cp.start()
        for cp in copies:
            cp.wait_recv()
        for cp in copies:
            cp.wait_send()

    return pl.pallas_call(
        body, name=name, in_specs=[pl.BlockSpec(memory_space=pl.ANY)], out_specs=pl.BlockSpec(memory_space=pl.ANY),
        out_shape=jax.ShapeDtypeStruct((n, r, c), src.dtype),
        scratch_shapes=[pltpu.SemaphoreType.DMA((n,)), pltpu.SemaphoreType.DMA((n,))],
    )(src)


FLIP_C = (0, 0, 1)
CHIP_FLIPS = ((0, 1, 0), (1, 0, 0), (1, 1, 0))
ALL_FLIPS = tuple((a >> 2 & 1, a >> 1 & 1, a & 1) for a in range(1, 8))


def _pick(stacked, idx):
    return lax.dynamic_index_in_dim(stacked, idx, axis=0, keepdims=False)


def _to_comm_rows(parts, dtype):
    flat = jnp.concatenate([p.reshape(-1) for p in parts]).astype(dtype)
    rows = -(-flat.size // (COMM_COLS * COMM_ROW_ALIGN)) * COMM_ROW_ALIGN
    return jnp.pad(flat, (0, rows * COMM_COLS - flat.size)).reshape(rows, COMM_COLS)


def _from_comm_rows(buf, shapes):
    flat, out, off = buf.reshape(-1), [], 0
    for s in shapes:
        size = math.prod(s)
        out.append(flat[off:off + size].reshape(s))
        off += size
    return out


def _my_place():
    x, y, c = (lax.axis_index(a) for a in ALL_AXES)
    return (x, y, c), 2 * x + y


def _flipped(me, flip):
    return tuple(1 - p if f else p for p, f in zip(me, flip))


def _remote(src, dst, sems, k, peer):
    send_sems, recv_sems = sems
    return pltpu.make_async_remote_copy(src_ref=src, dst_ref=dst, send_sem=send_sems.at[k], recv_sem=recv_sems.at[k],
                                        device_id=peer, device_id_type=MESH)


def _gather_shards(buf, chip):
    rows, cols = buf.shape
    half = rows // 2
    n = len(CHIP_FLIPS)

    def body(src_ref, out_ref, send_sems, recv_sems):
        me, chip = _my_place()
        sems = (send_sems, recv_sems)
        sibling = _flipped(me, FLIP_C)
        mine = pl.ds(pl.multiple_of(me[2] * half, half), half)
        theirs = pl.ds(pl.multiple_of((1 - me[2]) * half, half), half)
        peers = [_flipped(me, f) for f in CHIP_FLIPS]
        from_chip = [2 * p[0] + p[1] for p in peers]
        over_ici = [_remote(src_ref.at[mine], out_ref.at[chip, mine], sems, k, peers[k]) for k in range(n)]
        for cp in over_ici:
            cp.start()
        passed = [_remote(out_ref.at[from_chip[k], mine], out_ref.at[from_chip[k], mine], sems, n + k, sibling)
                  for k in range(n)]
        for k in range(n):
            _remote(src_ref.at[mine], out_ref.at[from_chip[k], mine], sems, k, peers[k]).wait_recv()
            passed[k].start()
        for k in range(n):
            _remote(out_ref.at[from_chip[k], theirs], out_ref.at[from_chip[k], theirs], sems, n + k, sibling).wait_recv()
        for cp in over_ici + passed:
            cp.wait_send()

    others = pl.pallas_call(
        body, name="gather_shards", in_specs=[pl.BlockSpec(memory_space=pl.ANY)],
        out_specs=pl.BlockSpec(memory_space=pl.ANY), out_shape=jax.ShapeDtypeStruct((4, rows, cols), buf.dtype),
        scratch_shapes=[pltpu.SemaphoreType.DMA((2 * n,)), pltpu.SemaphoreType.DMA((2 * n,))],
    )(buf)
    return lax.dynamic_update_slice(others, buf[None], (chip, 0, 0))


def _swap_other_halves(portions):
    _, rows, cols = portions.shape
    half = rows // 2

    def body(src_ref, dst_ref, send_sems, recv_sems):
        me, _ = _my_place()
        theirs = pl.ds(pl.multiple_of((1 - me[2]) * half, half), half)
        cp = _remote(src_ref.at[:, theirs], dst_ref, (send_sems, recv_sems), 0, _flipped(me, FLIP_C))
        cp.start()
        cp.wait_recv()
        cp.wait_send()

    return pl.pallas_call(
        body, name="reduce_d2d", in_specs=[pl.BlockSpec(memory_space=pl.ANY)],
        out_specs=pl.BlockSpec(memory_space=pl.ANY), out_shape=jax.ShapeDtypeStruct((4, half, cols), portions.dtype),
        scratch_shapes=[pltpu.SemaphoreType.DMA((1,)), pltpu.SemaphoreType.DMA((1,))],
    )(portions)


def _join_halves(total, core):
    half, cols = total.shape

    def body(src_ref, out_ref, send_sems, recv_sems):
        me, _ = _my_place()
        mine = pl.ds(pl.multiple_of(me[2] * half, half), half)
        theirs = pl.ds(pl.multiple_of((1 - me[2]) * half, half), half)
        sems = (send_sems, recv_sems)
        cp = _remote(src_ref, out_ref.at[mine], sems, 0, _flipped(me, FLIP_C))
        cp.start()
        _remote(src_ref, out_ref.at[theirs], sems, 0, _flipped(me, FLIP_C)).wait_recv()
        cp.wait_send()

    from_sibling = pl.pallas_call(
        body, name="reduce_share", in_specs=[pl.BlockSpec(memory_space=pl.ANY)],
        out_specs=pl.BlockSpec(memory_space=pl.ANY), out_shape=jax.ShapeDtypeStruct((2 * half, cols), total.dtype),
        scratch_shapes=[pltpu.SemaphoreType.DMA((1,)), pltpu.SemaphoreType.DMA((1,))],
    )(total)
    return lax.dynamic_update_slice(from_sibling, total, (core * half, 0))


def _reduce_scatter(portions, chip, core):
    half = portions.shape[1] // 2
    keep = lax.dynamic_slice_in_dim(portions, core * half, half, axis=1)
    got = _swap_other_halves(portions)
    pair = _ew(lambda p, q: (p + q,), [keep.reshape(4 * half, -1), got.reshape(4 * half, -1)],
               [(COMM_COLS, F32)], "reduce_pair_sum")[0].reshape(keep.shape)
    out = jnp.stack([_pick(pair, jnp.bitwise_xor(chip, f)) for f in (1, 2, 3)]).astype(BF16)
    others = _exchange(out, CHIP_FLIPS, (0, 1, 2), "reduce_ici")
    total = _ew(lambda p, q, r, s: (p + q.astype(F32) + r.astype(F32) + s.astype(F32),),
                [_pick(pair, chip), others[0], others[1], others[2]],
                [(COMM_COLS, F32)], "reduce_chip_sum")[0]
    return _join_halves(total, core)


def _all_reduce_small(buf, dev):
    got = _exchange(buf[None], ALL_FLIPS, (0,) * 7, "small_gather")
    by_flip = jnp.concatenate([buf[None], got])
    ordered = [_pick(by_flip, jnp.bitwise_xor(dev, a)) for a in range(8)]

    def fn(*t):
        s = t[0]
        for u in t[1:]:
            s = s + u
        return (s,)

    return _ew(fn, ordered, [(buf.shape[1], F32)], "small_sum")[0]


def _band_bucket_onehot(dilation, max_dist):
    i = jnp.arange(BLOCK)[:, None]
    j = jnp.arange(2 * BLOCK)[None, :]
    dist = i + BLOCK - j
    inband = (dist >= 0) & (dist <= max_dist)
    n = jnp.maximum(dist, 0) * dilation
    max_exact = NUM_BUCKETS // 2
    nf = jnp.maximum(n, 1).astype(F32)
    large = max_exact + (jnp.log(nf / max_exact) / math.log(MAX_DISTANCE / max_exact)
                         * (NUM_BUCKETS - max_exact)).astype(jnp.int32)
    bucket = jnp.where(n < max_exact, n, jnp.minimum(large, NUM_BUCKETS - 1))
    onehot = (bucket[..., None] == jnp.arange(NUM_BUCKETS)) & inband[..., None]
    return onehot.reshape(-1, NUM_BUCKETS).astype(F32), inband.reshape(-1)


def _band_bias(table, onehot, inband):
    vals = jnp.einsum("pb,bh->hp", onehot, table, precision=lax.Precision.HIGHEST)
    return jnp.where(inband[None, :], vals, NEG).reshape(-1, BLOCK, 2 * BLOCK)


BAND_VARIANTS = ((1, A_WINDOW - 1),) + tuple((dil, window // dil) for window, dil in B_BRANCHES)


LAYOUT_TILE_BYTES = 4 << 20


def _layout_rows(length, row_bytes):
    return _divisor_tile(length, max(16, LAYOUT_TILE_BYTES // row_bytes), 16)


def _split_heads(items, w, dil, name):
    s = items[0][0].shape[0]
    length = s // dil
    row_bytes = sum(-(-n * w // LANES) * LANES * (x.dtype.itemsize + jnp.dtype(d).itemsize) for x, _, n, d in items)
    tr = _layout_rows(length, row_bytes)
    nt = length // tr

    def body(*refs):
        for (x_ref, o_ref), (_, _, n, _) in zip(zip(refs[:len(items)], refs[len(items):]), items):
            for j in range(n):
                o_ref[j] = x_ref[:, j * w:(j + 1) * w].astype(o_ref.dtype)

    in_specs, out_specs, out_shape, views = [], [], [], []
    for x, first, n, d in items:
        bw, width = n * w, x.shape[1]
        assert bw % LANES == 0 and first % bw == 0 and (dil == 1 or width % bw == 0)
        in_specs.append(pl.BlockSpec((tr, bw), lambda r, i, c0=first // bw, wb=width // bw: (i, r * wb + c0)))
        out_specs.append(pl.BlockSpec((n, tr, w), lambda r, i: (0, r * nt + i, 0)))
        out_shape.append(jax.ShapeDtypeStruct((n, s, w), d))
        views.append(x.reshape(length, dil * width))
    return pl.pallas_call(
        body, name=name, grid=(dil, nt), in_specs=in_specs, out_specs=out_specs, out_shape=out_shape,
        compiler_params=_params(("parallel", "parallel")),
    )(*views)


def _merge_heads(items, dil, name, group_sum=1):
    s, w = items[0][0].shape[1:]
    length = s // dil
    row_bytes = sum(t.shape[0] * LANES * t.dtype.itemsize + t.shape[0] * w * jnp.dtype(d).itemsize for t, d in items)
    tr = _layout_rows(length, row_bytes)
    nt = length // tr

    def body(*refs):
        for t_ref, o_ref in zip(refs[:len(items)], refs[len(items):]):
            for j in range(t_ref.shape[0] // group_sum):
                v = t_ref[j * group_sum]
                for g in range(1, group_sum):
                    v = v + t_ref[j * group_sum + g]
                o_ref[:, j * w:(j + 1) * w] = v.astype(o_ref.dtype)

    in_specs, out_specs, out_shape = [], [], []
    for t, d in items:
        n = t.shape[0]
        bw = n // group_sum * w
        assert bw % LANES == 0
        in_specs.append(pl.BlockSpec((n, tr, w), lambda r, i: (0, r * nt + i, 0)))
        out_specs.append(pl.BlockSpec((tr, bw), lambda r, i: (i, r)))
        out_shape.append(jax.ShapeDtypeStruct((length, dil * bw), d))
    outs = pl.pallas_call(
        body, name=name, grid=(dil, nt), in_specs=in_specs, out_specs=out_specs, out_shape=out_shape,
        compiler_params=_params(("parallel", "parallel")),
    )(*[t for t, _ in items])
    return [o.reshape(s, -1) for o in outs]


def _merge_heads_transposed(t, dtype, name):
    n, w, s = t.shape
    tr = _layout_rows(s, 2 * n * w * t.dtype.itemsize)

    def body(t_ref, o_ref):
        o_ref[...] = jnp.transpose(t_ref[...].reshape(n * w, tr)).astype(o_ref.dtype)

    return pl.pallas_call(
        body, name=name, grid=(s // tr,), in_specs=[pl.BlockSpec((n, w, tr), lambda i: (0, 0, i))],
        out_specs=pl.BlockSpec((tr, n * w), lambda i: (i, 0)), out_shape=jax.ShapeDtypeStruct((s, n * w), dtype),
        compiler_params=_params(("parallel",)),
    )(t)


ROPE_HALF = C_ROPE // 2
ROPE_PERIOD = 3 * LANES


def _rope_tables(s):
    inv = ROPE_THETA ** (-jnp.arange(0, C_ROPE, 2, dtype=F32) / C_ROPE)
    ang = jnp.arange(s, dtype=F32)[:, None] * inv[None, :]
    cos, sin = jnp.cos(ang), jnp.sin(ang)
    one, zero = jnp.ones((s, C_NOPE), F32), jnp.zeros((s, C_NOPE), F32)
    z16 = jnp.zeros((s, ROPE_HALF), F32)
    reps = ROPE_PERIOD // C_QK
    keep = jnp.tile(jnp.concatenate([one, cos, cos], axis=1), (1, reps))
    from_above = jnp.tile(jnp.concatenate([zero, -sin, z16], axis=1), (1, reps))
    from_below = jnp.tile(jnp.concatenate([zero, z16, sin], axis=1), (1, reps))
    return (cos, sin), (keep, from_above, from_below)


def _rope_rows(x, tables, inverse, name, dtype):
    width = x.shape[1]
    reps = width // ROPE_PERIOD
    sign = -1.0 if inverse else 1.0

    def fn(xv, keep, above, below):
        keep, above, below = (jnp.tile(t, (1, reps)) for t in (keep, above, below))
        up = pltpu.roll(xv, width - ROPE_HALF, 1)
        down = pltpu.roll(xv, ROPE_HALF, 1)
        return (xv * keep + sign * (up * above + down * below),)

    return _ew(fn, [x, *tables], [(width, dtype)], name)[0]


def _rotate_half_pairs(a, b, cos, sin, inverse):
    if inverse:
        return a * cos + b * sin, b * cos - a * sin
    return a * cos - b * sin, a * sin + b * cos


def _split_kv(kv, down, cos, sin, name):
    s = kv.shape[0]
    h = kv.shape[1] // (C_NOPE + C_V)
    tr = _layout_rows(s, 8 * kv.shape[1])
    r0 = C_Q_RANK + C_KV_RANK

    def body(kv_ref, down_ref, cos_ref, sin_ref, k_ref, v_ref):
        k1, k2 = _rotate_half_pairs(down_ref[:, r0:r0 + ROPE_HALF], down_ref[:, r0 + ROPE_HALF:r0 + C_ROPE],
                                    cos_ref[...], sin_ref[...], False)
        lane = lax.broadcasted_iota(jnp.int32, (tr, LANES - C_V), 1)
        tail = jnp.where(lane == 0, 1.0, 0.0).astype(BF16)
        for j in range(h):
            base = j * (C_NOPE + C_V)
            k_ref[j, :, :C_NOPE] = kv_ref[:, base:base + C_NOPE]
            k_ref[j, :, C_NOPE:C_NOPE + ROPE_HALF] = k1.astype(BF16)
            k_ref[j, :, C_NOPE + ROPE_HALF:] = k2.astype(BF16)
            v_ref[j, :, :C_V] = kv_ref[:, base + C_NOPE:base + C_NOPE + C_V]
            v_ref[j, :, C_V:] = tail

    def rows(width):
        return pl.BlockSpec((tr, width), lambda i: (i, 0))

    return pl.pallas_call(
        body, name=name, grid=(s // tr,),
        in_specs=[rows(kv.shape[1]), rows(down.shape[1]), rows(ROPE_HALF), rows(ROPE_HALF)],
        out_specs=[pl.BlockSpec((h, tr, C_QK), lambda i: (0, i, 0)), pl.BlockSpec((h, tr, LANES), lambda i: (0, i, 0))],
        out_shape=[jax.ShapeDtypeStruct((h, s, C_QK), BF16), jax.ShapeDtypeStruct((h, s, LANES), BF16)],
        compiler_params=_params(("parallel",)),
    )(kv, down, cos, sin)


def _merge_kv_bwd(dk, dv, cos, sin, name):
    h, s, _ = dk.shape
    tr = _layout_rows(s, 8 * h * LANES)

    def body(dk_ref, dv_ref, cos_ref, sin_ref, dkv_ref, dkr_ref):
        rot = dk_ref[0, :, C_NOPE:]
        for j in range(h):
            base = j * (C_NOPE + C_V)
            dkv_ref[:, base:base + C_NOPE] = dk_ref[j, :, :C_NOPE].astype(BF16)
            dkv_ref[:, base + C_NOPE:base + C_NOPE + C_V] = dv_ref[j].astype(BF16)
            if j:
                rot = rot + dk_ref[j, :, C_NOPE:]
        d1, d2 = _rotate_half_pairs(rot[:, :ROPE_HALF], rot[:, ROPE_HALF:], cos_ref[...], sin_ref[...], True)
        dkr_ref[:, :ROPE_HALF] = d1
        dkr_ref[:, ROPE_HALF:] = d2

    def rows(width):
        return pl.BlockSpec((tr, width), lambda i: (i, 0))

    return pl.pallas_call(
        body, name=name, grid=(s // tr,),
        in_specs=[pl.BlockSpec((h, tr, C_QK), lambda i: (0, i, 0)), pl.BlockSpec((h, tr, C_V), lambda i: (0, i, 0)),
                  rows(ROPE_HALF), rows(ROPE_HALF)],
        out_specs=[rows(h * (C_NOPE + C_V)), rows(C_ROPE)],
        out_shape=[jax.ShapeDtypeStruct((s, h * (C_NOPE + C_V)), BF16), jax.ShapeDtypeStruct((s, C_ROPE), F32)],
        compiler_params=_params(("parallel",)),
    )(dk, dv, cos, sin)


SPAN_FWD, SPAN_FWD_SINK, SPAN_BWD = 128, 256, 256


def _even_fwd(xn, h, w_in, w_out, sinks_row, biases, tag):
    s = xn.shape[0]
    proj = _mm(xn, w_in, "nn", [BF16], f"in_proj{tag}")[0]
    qd, kd = A_Q_HEADS * HEAD_DIM, A_KV_HEADS * HEAD_DIM
    qa, ka, va = _split_heads([(proj, 0, A_Q_HEADS, BF16), (proj, qd, A_KV_HEADS, BF16),
                               (proj, qd + kd, A_KV_HEADS, BF16)], HEAD_DIM, 1, f"swa_split{tag}")
    oa, lse_a = _band_fwd(qa, ka, va, biases[0], sinks_row, s // BLOCK, SPAN_FWD_SINK, f"swa_fwd{tag}")
    out_a = _merge_heads([(oa, BF16)], 1, f"swa_merge{tag}")[0]
    width = B_HEADS_PER_BRANCH * HEAD_DIM
    qkv_b, outs, lses = [], [], []
    for g, (_, dil) in enumerate(B_BRANCHES):
        base = A_IN + g * 3 * width
        src, base = (proj, base) if dil == 1 else (proj[:, base:base + 3 * width], 0)
        qkv = _split_heads([(src, base + i * width, B_HEADS_PER_BRANCH, BF16) for i in range(3)], HEAD_DIM, dil,
                           f"dil{g}_split{tag}")
        og, lg = _band_fwd(*qkv, biases[1 + g], None, s // dil // BLOCK, SPAN_FWD, f"dil{g}_fwd{tag}")
        qkv_b.append(qkv)
        merged = _merge_heads([(og, F32), (lg, F32)], dil, f"dil{g}_merge{tag}")
        outs.append(merged[0])
        lses.append(merged[1])

    def merge(o0, o1, o2, l0, l1, l2):
        m = jnp.maximum(jnp.maximum(l0, l1), l2)
        e0, e1, e2 = jnp.exp(l0 - m), jnp.exp(l1 - m), jnp.exp(l2 - m)
        den = e0 + e1 + e2
        out = (e0 * o0 + e1 * o1 + e2 * o2) / den
        return out, m + jnp.log(den), out

    out_b, lse_b, out_b16 = _ew(merge, outs + lses, [(width, F32), (width, F32), (width, BF16)], f"dil_merge{tag}")
    cat = jnp.concatenate([out_a, out_b16], axis=1)
    h_mid = _mm(cat, w_out, "nn", [F32], f"out_proj{tag}", epi=_add_epi, extras=(h,))[0]
    return h_mid, (qa, ka, va, oa, lse_a, qkv_b, out_b, lse_b, cat)


def _even_bwd(dh, xn, saved, w_in, w_out, sinks_row, biases, onehots, tag):
    qa, ka, va, oa, lse_a, qkv_b, out_b, lse_b, cat = saved
    s = xn.shape[0]
    qd = A_Q_HEADS * HEAD_DIM
    g_w_out = _mm(cat, dh, "tn", [F32], f"out_proj_dw{tag}")[0]
    dcat = _mm(dh, w_out, "nt", [BF16], f"out_proj_dx{tag}")[0]
    do_a = _split_heads([(dcat, 0, A_Q_HEADS, BF16)], HEAD_DIM, 1, f"swa_do_split{tag}")[0]
    dqa, dka8, dva8, dbias_a, dsink = _band_bwd(qa, ka, va, oa, lse_a, do_a, biases[0], sinks_row, s // BLOCK,
                                                SPAN_BWD, f"swa_bwd{tag}")
    pieces = _merge_heads([(dqa, BF16)], 1, f"swa_dq_merge{tag}")
    pieces += _merge_heads([(dka8, BF16), (dva8, BF16)], 1, f"swa_dkv_merge{tag}", group_sum=A_Q_HEADS // A_KV_HEADS)
    dbias_b = []
    dcat_b = dcat[:, qd:]
    for g, (_, dil) in enumerate(B_BRANCHES):
        do_g, out_g, lse_g = _split_heads([(dcat_b, 0, B_HEADS_PER_BRANCH, BF16), (out_b, 0, B_HEADS_PER_BRANCH, F32),
                                           (lse_b, 0, B_HEADS_PER_BRANCH, F32)], HEAD_DIM, dil, f"dil{g}_do_split{tag}")
        dqg, dkg, dvg, dbg, _ = _band_bwd(*qkv_b[g], out_g, lse_g, do_g, biases[1 + g], None, s // dil // BLOCK,
                                          SPAN_BWD, f"dil{g}_bwd{tag}")
        pieces += _merge_heads([(dqg, BF16), (dkg, BF16), (dvg, BF16)], dil, f"dil{g}_dqkv_merge{tag}")
        dbias_b.append(dbg)
    dproj = jnp.concatenate(pieces, axis=1)
    g_w_in = _mm(xn, dproj, "tn", [F32], f"in_proj_dw{tag}")[0]
    dxn = _mm(dproj, w_in, "nt", [F32], f"in_proj_dx{tag}")[0]
    cols = [_mm(db.reshape(db.shape[0], -1), onehots[v][0], "nn", [F32], f"bias_buckets{v}{tag}")[0].T
            for v, db in enumerate([dbias_a] + dbias_b)]
    return dxn, g_w_in, g_w_out, jnp.concatenate(cols, axis=1), dsink[:, 0, 0]


def _mla_layer_fwd(xn, h, w_dn, q_norm, w_uq, kv_norm, w_ukv, w_o, ropes, tag):
    (cos, sin), q_tables = ropes
    down = _mm(xn, w_dn, "nn", [F32], f"mla_down{tag}")[0]
    c_q, c_kv = down[:, :C_Q_RANK], down[:, C_Q_RANK:C_Q_RANK + C_KV_RANK]
    cqn = _rms_fwd(c_q, q_norm, f"mla_qnorm{tag}")
    ckvn = _rms_fwd(c_kv, kv_norm, f"mla_kvnorm{tag}")
    q = _mm(cqn, w_uq, "nn", [F32], f"mla_uq{tag}")[0]
    kv = _mm(ckvn, w_ukv, "nn", [BF16], f"mla_ukv{tag}")[0]
    qh = _split_heads([(_rope_rows(q, q_tables, False, f"mla_rope_q{tag}", BF16), 0, C_HEADS, BF16)], C_QK, 1,
                      f"mla_q_split{tag}")[0]
    kh, v1h = _split_kv(kv, down, cos, sin, f"mla_kv_split{tag}")
    ot, lse = _mla_fwd(qh, kh, v1h.transpose(0, 2, 1), f"mla_attn_fwd{tag}")
    o = ot.transpose(0, 2, 1)
    o2d = _merge_heads_transposed(ot, BF16, f"mla_o_merge{tag}")
    h_mid = _mm(o2d, w_o, "nn", [F32], f"mla_o{tag}", epi=_add_epi, extras=(h,))[0]
    return h_mid, (c_q, c_kv, cqn, ckvn, qh, kh, v1h, o, lse, o2d)


def _mla_layer_bwd(dh, xn, saved, w_dn, q_norm, w_uq, kv_norm, w_ukv, w_o, ropes, tag):
    c_q, c_kv, cqn, ckvn, qh, kh, v1h, o, lse, o2d = saved
    (cos, sin), q_tables = ropes
    g_w_o = _mm(o2d, dh, "tn", [F32], f"mla_o_dw{tag}")[0]
    do2d = _mm(dh, w_o, "nt", [BF16], f"mla_o_dx{tag}")[0]
    do = _split_heads([(do2d, 0, C_HEADS, BF16)], C_V, 1, f"mla_do_split{tag}")[0]
    dqt, dk, dv = _mla_bwd(qh, kh, v1h, o, lse, do, f"mla_attn_bwd{tag}")
    dq_roped = _merge_heads_transposed(dqt, F32, f"mla_dq_merge{tag}")
    dq = _rope_rows(dq_roped, q_tables, True, f"mla_rope_q_bwd{tag}", BF16)
    dkv, dk_rope = _merge_kv_bwd(dk, dv, cos, sin, f"mla_dkv_merge{tag}")
    g_w_uq = _mm(cqn, dq, "tn", [F32], f"mla_uq_dw{tag}")[0]
    dcqn = _mm(dq, w_uq, "nt", [F32], f"mla_uq_dx{tag}")[0]
    g_w_ukv = _mm(ckvn, dkv, "tn", [F32], f"mla_ukv_dw{tag}")[0]
    dckvn = _mm(dkv, w_ukv, "nt", [F32], f"mla_ukv_dx{tag}")[0]
    dc_q, g_q_norm = _rms_bwd(c_q, q_norm, dcqn, None, f"mla_qnorm_bwd{tag}")
    dc_kv, g_kv_norm = _rms_bwd(c_kv, kv_norm, dckvn, None, f"mla_kvnorm_bwd{tag}")
    ddown = jnp.concatenate([dc_q, dc_kv, dk_rope], axis=1).astype(BF16)
    g_w_dn = _mm(xn, ddown, "tn", [F32], f"mla_down_dw{tag}")[0]
    dxn = _mm(ddown, w_dn, "nt", [F32], f"mla_down_dx{tag}")[0]
    return dxn, g_w_dn, g_q_norm, g_w_uq, g_kv_norm, g_w_ukv, g_w_o


SHARDED = (("w_in_ab", 2), ("w_out_ab", 2), ("w_down_c", 1), ("w_uq_c", 2), ("w_ukv_c", 2), ("w_o_c", 2),
           ("w_mlp_up", 2), ("w_mlp_down", 1))
SHARDED_NORMS = ("q_norm_c", "kv_norm_c")


def kernel(x, rel_bias, attn_norm, mlp_norm, final_norm, w_in_ab, sinks, w_out_ab, w_down_c, q_norm_c, w_uq_c, kv_norm_c, w_ukv_c, w_o_c, w_mlp_up, w_mlp_down, loss_target, m_rel_bias, m_attn_norm, m_mlp_norm, m_final_norm, m_w_in_ab, m_sinks, m_w_out_ab, m_w_down_c, m_q_norm_c, m_w_uq_c, m_kv_norm_c, m_w_ukv_c, m_w_o_c, m_w_mlp_up, m_w_mlp_down, v_rel_bias, v_attn_norm, v_mlp_norm, v_final_norm, v_w_in_ab, v_sinks, v_w_out_ab, v_w_down_c, v_q_norm_c, v_w_uq_c, v_kv_norm_c, v_w_ukv_c, v_w_o_c, v_w_mlp_up, v_w_mlp_down):
    given = dict(locals())
    chip = lax.axis_index("x") * 2 + lax.axis_index("y")
    core = lax.axis_index("c")
    dev = chip * 2 + core
    depth = attn_norm.shape[0]
    s = x.shape[1]

    shards = [given[n] for n, _ in SHARDED]
    norm_shards = [given[n] for n in SHARDED_NORMS]
    packed = _to_comm_rows([t.astype(BF16) for t in shards]
                           + [lax.bitcast_convert_type(t, BF16) for t in norm_shards], BF16)
    by_chip = _gather_shards(packed, chip)
    shapes = [t.shape for t in shards] + [t.shape + (2,) for t in norm_shards]
    pieces = [_from_comm_rows(by_chip[a], shapes) for a in range(4)]
    full = {n: jnp.concatenate([pieces[a][i] for a in range(4)], axis=ax) for i, (n, ax) in enumerate(SHARDED)}
    for i, n in enumerate(SHARDED_NORMS):
        full[n] = jnp.concatenate([lax.bitcast_convert_type(pieces[a][len(SHARDED) + i], F32) for a in range(4)],
                                  axis=-1)

    onehots = [_band_bucket_onehot(dil, md) for dil, md in BAND_VARIANTS]
    head_cols = [(0, A_Q_HEADS)] + [(A_Q_HEADS + g * B_HEADS_PER_BRANCH, A_Q_HEADS + (g + 1) * B_HEADS_PER_BRANCH)
                                    for g in range(len(B_BRANCHES))]
    biases = [_band_bias(rel_bias[:, lo:hi], oh, inb) for (lo, hi), (oh, inb) in zip(head_cols, onehots)]
    ropes = _rope_tables(s)
    sink_rows = [jnp.broadcast_to(sinks[e][:, None, None], (A_Q_HEADS, BAND_SPAN_MAX, LANES))
                 for e in range(sinks.shape[0])]

    def odd_weights(o):
        return [full[n][o] for n in ("w_down_c", "q_norm_c", "w_uq_c", "kv_norm_c", "w_ukv_c", "w_o_c")]

    h = x[0]
    saved = []
    for l in range(depth):
        xn = _rms_fwd(h, attn_norm[l], f"attn_norm{l}")
        if l % 2 == 0:
            e = l // 2
            h_mid, mix = _even_fwd(xn, h, full["w_in_ab"][e], full["w_out_ab"][e], sink_rows[e], biases, f"_{l}")
        else:
            h_mid, mix = _mla_layer_fwd(xn, h, *odd_weights(l // 2), ropes, f"_{l}")
        xn2 = _rms_fwd(h_mid, mlp_norm[l], f"mlp_norm{l}")
        act, relu = _mm(xn2, full["w_mlp_up"][l], "nn", [BF16, BF16], f"mlp_up{l}", epi=_relu2_epi)
        h_out = _mm(act, full["w_mlp_down"][l], "nn", [F32], f"mlp_down{l}", epi=_add_epi, extras=(h_mid,))[0]
        saved.append((h, xn, mix, h_mid, xn2, act, relu))
        h = h_out
    loss_part, dh, dh16, g_final = _final_loss(h, final_norm, loss_target[0], "final_loss")

    grads = {n: [None] * given[n].shape[0] for n, _ in SHARDED}
    g_q_norm, g_kv_norm = [None] * q_norm_c.shape[0], [None] * kv_norm_c.shape[0]
    g_attn_norm, g_mlp_norm = [None] * depth, [None] * depth
    g_sinks = [None] * sinks.shape[0]
    g_rel_bias = None
    for l in range(depth - 1, -1, -1):
        h_in, xn, mix, h_mid, xn2, act, relu = saved[l]
        grads["w_mlp_down"][l] = _mm(act, dh16, "tn", [F32], f"mlp_down_dw{l}")[0]
        du = _mm(dh16, full["w_mlp_down"][l], "nt", [BF16], f"mlp_down_dx{l}", epi=_relu2_bwd_epi, extras=(relu,))[0]
        grads["w_mlp_up"][l] = _mm(xn2, du, "tn", [F32], f"mlp_up_dw{l}")[0]
        dxn2 = _mm(du, full["w_mlp_up"][l], "nt", [F32], f"mlp_up_dx{l}")[0]
        dh, dh16, g_mlp_norm[l] = _rms_bwd(h_mid, mlp_norm[l], dxn2, dh, f"mlp_norm_bwd{l}", also_bf16=True)
        if l % 2 == 0:
            e = l // 2
            dxn, grads["w_in_ab"][e], grads["w_out_ab"][e], g_table, g_sinks[e] = _even_bwd(
                dh16, xn, mix, full["w_in_ab"][e], full["w_out_ab"][e], sink_rows[e], biases, onehots, f"_{l}")
            g_rel_bias = g_table if g_rel_bias is None else g_rel_bias + g_table
        else:
            o = l // 2
            (dxn, grads["w_down_c"][o], g_q_norm[o], grads["w_uq_c"][o], g_kv_norm[o], grads["w_ukv_c"][o],
             grads["w_o_c"][o]) = _mla_layer_bwd(dh16, xn, mix, *odd_weights(o), ropes, f"_{l}")
        if l:
            dh, dh16, g_attn_norm[l] = _rms_bwd(h_in, attn_norm[l], dxn, dh, f"attn_norm_bwd{l}", also_bf16=True)
        else:
            dh, g_attn_norm[l] = _rms_bwd(h_in, attn_norm[l], dxn, dh, f"attn_norm_bwd{l}")
    grad_x = dh[None]

    portions = []
    for a in range(4):
        parts = []
        for n, ax in SHARDED:
            g = jnp.stack(grads[n])
            size = g.shape[ax] // 4
            parts.append(lax.slice_in_dim(g, a * size, (a + 1) * size, axis=ax))
        portions.append(_to_comm_rows(parts, F32))
    reduced = _from_comm_rows(_reduce_scatter(jnp.stack(portions), chip, core), [t.shape for t in shards])
    g_shard = {n: reduced[i] for i, (n, _) in enumerate(SHARDED)}

    small = [jnp.stack(g_attn_norm), jnp.stack(g_mlp_norm), g_final, g_rel_bias, jnp.stack(g_sinks),
             jnp.stack(g_q_norm), jnp.stack(g_kv_norm), loss_part.reshape(1)]
    small_shapes = [t.shape for t in small]
    summed = _from_comm_rows(_all_reduce_small(_to_comm_rows(small, F32)[:16], dev), small_shapes)
    loss = summed[7][0]
    g_small = dict(zip(("attn_norm", "mlp_norm", "final_norm", "rel_bias", "sinks"), summed[:5]))
    for n, g in zip(SHARDED_NORMS, summed[5:7]):
        size = g.shape[1] // 4
        g_shard[n] = lax.dynamic_slice_in_dim(g, chip * size, size, axis=1)

    order = ["rel_bias", "attn_norm", "mlp_norm", "final_norm", "w_in_ab", "sinks", "w_out_ab", "w_down_c",
             "q_norm_c", "w_uq_c", "kv_norm_c", "w_ukv_c", "w_o_c", "w_mlp_up", "w_mlp_down"]
    g_all = {**g_small, **g_shard}
    deltas, new_m, new_v = [], [], []
    for n in order:
        d, mn, vn = _adamw(given[n], g_all[n], given["m_" + n], given["v_" + n], f"adamw_{n}")
        deltas.append(d)
        new_m.append(mn)
        new_v.append(vn)
    return (loss, grad_x, *[g_all[n] for n in order], *deltas, *new_m, *new_v)
```

```python
import math

import jax
import jax.numpy as jnp
from jax import lax
from jax.experimental import pallas as pl
from jax.experimental.pallas import tpu as pltpu

F32 = jnp.float32
BF16 = jnp.bfloat16
MESH = pl.DeviceIdType.MESH
ALL_AXES = ("x", "y", "c")

EPS = 1e-6
NEG = -1e30
BLOCK = 128
HEAD_DIM = 64
A_Q_HEADS, A_KV_HEADS = 8, 2
A_WINDOW = 128
B_BRANCHES = ((128, 1), (512, 4), (2048, 16))
B_HEADS_PER_BRANCH = 4
NUM_BUCKETS, MAX_DISTANCE = 32, 2048
A_IN = (A_Q_HEADS + 2 * A_KV_HEADS) * HEAD_DIM
C_HEADS, C_NOPE, C_ROPE, C_V = 8, 64, 32, 64
C_Q_RANK, C_KV_RANK = 384, 256
ROPE_THETA = 10000.0
ADAM_LR, ADAM_B1, ADAM_B2, ADAM_EPS, ADAM_WD, ADAM_STEP = 0.001, 0.9, 0.999, 1e-08, 0.01, 10

V7X_VMEM_LIMIT_BYTES = 56 * 1024 * 1024
LANES = 128
COMM_COLS = 1024
COMM_ROW_ALIGN = 1024

NT = (((1,), (1,)), ((), ()))
NN = (((1,), (0,)), ((), ()))
TN = (((0,), (0,)), ((), ()))
DIMS = {"nn": NN, "nt": NT, "tn": TN}


def _params(sem):
    return pltpu.CompilerParams(dimension_semantics=sem, vmem_limit_bytes=V7X_VMEM_LIMIT_BYTES)


def _divisor_tile(n, limit, align):
    if n <= limit:
        return n
    t = (limit // align) * align
    while t >= align:
        if n % t == 0:
            return t
        t -= align
    return n


def _ew(fn, ins, outs, name, acc_outs=(), target_bytes=6 << 20):
    rows = max(a.shape[0] for a in ins)

    def vmem_row_bytes(cols, dtype):
        return -(-cols // LANES) * LANES * jnp.dtype(dtype).itemsize

    per_row = sum(vmem_row_bytes(a.shape[1], a.dtype) for a in ins if a.shape[0] == rows)
    per_row += sum(vmem_row_bytes(c, d) for c, d in outs)
    tr = _divisor_tile(rows, max(16, target_bytes // max(per_row, 1)), 16)
    n_in, n_out = len(ins), len(outs)

    def body(*refs):
        res = fn(*[r[...] for r in refs[:n_in]])
        for r, v in zip(refs[n_in:n_in + n_out], res[:n_out]):
            r[...] = v.astype(r.dtype)
        if acc_outs:
            acc_refs = refs[n_in + n_out:]

            @pl.when(pl.program_id(0) == 0)
            def _():
                for r in acc_refs:
                    r[...] = jnp.zeros_like(r)

            for r, v in zip(acc_refs, res[n_out:]):
                r[...] += v

    def spec(a):
        if a.shape[0] == rows:
            return pl.BlockSpec((tr, a.shape[1]), lambda i: (i, 0))
        return pl.BlockSpec((1, a.shape[1]), lambda i: (0, 0))

    out_shape = [jax.ShapeDtypeStruct((rows, c), d) for c, d in outs]
    out_shape += [jax.ShapeDtypeStruct((1, c), F32) for c in acc_outs]
    out_specs = [pl.BlockSpec((tr, c), lambda i: (i, 0)) for c, _ in outs]
    out_specs += [pl.BlockSpec((1, c), lambda i: (0, 0)) for c in acc_outs]
    return pl.pallas_call(
        body, name=name, grid=(rows // tr,), in_specs=[spec(a) for a in ins], out_specs=out_specs,
        out_shape=out_shape, compiler_params=_params(("arbitrary",)),
    )(*ins)


def _rms_fwd(x, g, name):
    def fn(xv, gv):
        return ((xv * lax.rsqrt(jnp.mean(xv * xv, axis=-1, keepdims=True) + EPS)) * gv,)

    return _ew(fn, [x, g.reshape(1, -1)], [(x.shape[1], BF16)], name)[0]


def _rms_bwd(x, g, dy, add, name, also_bf16=False):
    def fn(xv, gv, dyv, *rest):
        rstd = lax.rsqrt(jnp.mean(xv * xv, axis=-1, keepdims=True) + EPS)
        xh = xv * rstd
        dyg = dyv.astype(F32) * gv
        dx = rstd * (dyg - xh * jnp.mean(dyg * xh, axis=-1, keepdims=True))
        if rest:
            dx = dx + rest[0]
        return (dx,) * (2 if also_bf16 else 1) + (jnp.sum(dyv.astype(F32) * xh, axis=0, keepdims=True),)

    ins = [x, g.reshape(1, -1), dy] + ([] if add is None else [add])
    outs = [(x.shape[1], F32)] + ([(x.shape[1], BF16)] if also_bf16 else [])
    *dx, dg = _ew(fn, ins, outs, name, acc_outs=(x.shape[1],))
    return (*dx, dg[0])


def _final_loss(h, g, target, name):
    d = h.shape[1]

    def fn(xv, gv, tv):
        rstd = lax.rsqrt(jnp.mean(xv * xv, axis=-1, keepdims=True) + EPS)
        xh = xv * rstd
        err = xh * gv - tv
        part = 0.5 * jnp.sum(jnp.mean(err * err, axis=-1, keepdims=True), axis=0, keepdims=True)
        dy = err * (1.0 / d)
        dyg = dy * gv
        dx = rstd * (dyg - xh * jnp.mean(dyg * xh, axis=-1, keepdims=True))
        return dx, dx, jnp.broadcast_to(part, (1, LANES)), jnp.sum(dy * xh, axis=0, keepdims=True)

    dx, dx16, loss, dg = _ew(fn, [h, g.reshape(1, -1), target], [(d, F32), (d, BF16)], name, acc_outs=(LANES, d))
    return loss[0, 0], dx, dx16, dg[0]


def _adamw(w, g, m, v, name):
    def fn(wv, gv, mv, vv):
        mn = ADAM_B1 * mv + (1.0 - ADAM_B1) * gv
        vn = ADAM_B2 * vv + (1.0 - ADAM_B2) * jnp.square(gv)
        m_hat = mn / (1.0 - ADAM_B1 ** ADAM_STEP)
        v_hat = vn / (1.0 - ADAM_B2 ** ADAM_STEP)
        return -ADAM_LR * (m_hat / (jnp.sqrt(v_hat) + ADAM_EPS) + ADAM_WD * wv), mn, vn

    shape = w.shape
    cols = shape[-1] if w.ndim > 1 else w.size
    view = [t.reshape(-1, cols) for t in (w, g, m, v)]
    return [t.reshape(shape) for t in _ew(fn, view, [(cols, F32)] * 3, name)]


MM_VMEM_BUDGET_BYTES = 40 << 20


def _mm(a, b, dims, outs, name, epi=None, extras=(), tm=2048, tn=1024, tk=1024):
    if dims == "nn":
        (m, k), n = a.shape, b.shape[1]
    elif dims == "nt":
        (m, k), n = a.shape, b.shape[0]
    else:
        (k, m), n = a.shape, b.shape[1]
    tn, tk = _divisor_tile(n, tn, LANES), _divisor_tile(k, tk, LANES)

    def tile_bytes(rows):
        per_out = sum(jnp.dtype(d).itemsize for d in outs) + sum(e.dtype.itemsize for e in extras)
        return 2 * (rows * tk * a.dtype.itemsize + tk * tn * b.dtype.itemsize + rows * tn * per_out) + 4 * rows * tn

    tm = _divisor_tile(m, tm, LANES)
    while tile_bytes(tm) > MM_VMEM_BUDGET_BYTES and tm % (2 * LANES) == 0:
        tm //= 2
    nk = k // tk
    n_ex, n_out = len(extras), len(outs)

    def body(a_ref, b_ref, *rest):
        ex_refs, out_refs = rest[:n_ex], rest[n_ex:n_ex + n_out]

        def finish(acc):
            res = epi(acc, *[r[...] for r in ex_refs]) if epi else (acc,)
            for r, v in zip(out_refs, res):
                r[...] = v.astype(r.dtype)

        part = lax.dot_general(a_ref[...].astype(BF16), b_ref[...].astype(BF16), DIMS[dims],
                               preferred_element_type=F32)
        if nk == 1:
            finish(part)
        else:
            acc_ref = rest[-1]
            kk = pl.program_id(2)

            @pl.when(kk == 0)
            def _():
                acc_ref[...] = part

            @pl.when(kk > 0)
            def _():
                acc_ref[...] += part

            @pl.when(kk == nk - 1)
            def _():
                finish(acc_ref[...])

    if dims == "nn":
        a_spec = pl.BlockSpec((tm, tk), lambda i, j, kk: (i, kk))
        b_spec = pl.BlockSpec((tk, tn), lambda i, j, kk: (kk, j))
    elif dims == "nt":
        a_spec = pl.BlockSpec((tm, tk), lambda i, j, kk: (i, kk))
        b_spec = pl.BlockSpec((tn, tk), lambda i, j, kk: (j, kk))
    else:
        a_spec = pl.BlockSpec((tk, tm), lambda i, j, kk: (kk, i))
        b_spec = pl.BlockSpec((tk, tn), lambda i, j, kk: (kk, j))
    tile = pl.BlockSpec((tm, tn), lambda i, j, kk: (i, j))
    return pl.pallas_call(
        body, name=name, grid=(m // tm, n // tn, nk),
        in_specs=[a_spec, b_spec] + [tile] * n_ex, out_specs=[tile] * n_out,
        out_shape=[jax.ShapeDtypeStruct((m, n), d) for d in outs],
        scratch_shapes=[pltpu.VMEM((tm, tn), F32)] if nk > 1 else [],
        compiler_params=_params(("parallel", "parallel", "arbitrary")),
    )(a, b, *extras)


def _add_epi(acc, res):
    return (acc + res,)


def _relu2_epi(acc):
    r = jnp.maximum(acc, 0.0)
    return r * r, r


def _relu2_bwd_epi(acc, r):
    return (acc * (2.0 * r.astype(F32)),)


BAND_SPAN_MAX = 256


def _band_geometry(t, blocks_per_seq, span):
    rows = min(1024, blocks_per_seq * BLOCK)
    nb = rows // BLOCK
    assert blocks_per_seq % nb == 0 and t % rows == 0 and span <= BAND_SPAN_MAX
    return rows, nb, t // rows, min(span, rows)


def _span_bias(bias, span):
    n = span // BLOCK
    neg = jnp.full(bias.shape[:2] + (BLOCK,), NEG, F32)
    rows = [jnp.concatenate([neg] * a + [bias[:, :, :BLOCK], bias[:, :, BLOCK:]] + [neg] * (n - 1 - a), axis=2)
            for a in range(n)]
    return jnp.concatenate(rows, axis=1)


def _fold_span_bias_grad(dbias, span):
    n = span // BLOCK
    parts = [dbias[:, a * BLOCK:(a + 1) * BLOCK, a * BLOCK:(a + 2) * BLOCK] for a in range(n)]
    return sum(parts[1:], parts[0])


def _band_logits(qj, kk, bias, first):
    s = lax.dot_general(qj, kk, NT, preferred_element_type=F32) * (HEAD_DIM ** -0.5) + bias
    if first is not None:
        col = lax.broadcasted_iota(jnp.int32, s.shape, 1)
        s = jnp.where(col < jnp.where(first, BLOCK, 0), NEG, s)
    return s


def _band_fwd(q, k, v, bias, sinks, blocks_per_seq, span, name):
    hq, t, dh = q.shape
    group = hq // k.shape[0]
    rows, nb, nchunks, span = _band_geometry(t, blocks_per_seq, span)
    has_sink = sinks is not None

    def body(q_ref, kc_ref, kp_ref, vc_ref, vp_ref, bias_ref, *rest):
        o_ref, lse_ref = rest[-2:]
        i = pl.program_id(1)
        bias_v = bias_ref[0]
        sink = rest[0][0, :span, :1] if has_sink else None
        for j in range(rows // span):
            cur = slice(j * span, (j + 1) * span)
            prev = slice(j * span - BLOCK, j * span)
            kk = jnp.concatenate([kp_ref[0] if j == 0 else kc_ref[0, prev, :], kc_ref[0, cur, :]], axis=0)
            vv = jnp.concatenate([vp_ref[0] if j == 0 else vc_ref[0, prev, :], vc_ref[0, cur, :]], axis=0)
            first = lax.rem(i * nb, blocks_per_seq) == 0 if j == 0 else None
            s = _band_logits(q_ref[0, cur, :], kk, bias_v, first)
            m = jnp.max(s, axis=1, keepdims=True)
            if has_sink:
                m = jnp.maximum(m, sink)
            p = jnp.exp(s - m)
            l = jnp.sum(p, axis=1, keepdims=True)
            if has_sink:
                l = l + jnp.exp(sink - m)
            acc = jnp.dot(p.astype(BF16), vv, preferred_element_type=F32)
            o_ref[0, cur, :] = acc / l
            lse_ref[0, cur, :] = jnp.broadcast_to(m + jnp.log(l), (span, dh))

    cur_q = pl.BlockSpec((1, rows, dh), lambda h, i: (h, i, 0))
    cur_kv = pl.BlockSpec((1, rows, dh), lambda h, i: (h // group, i, 0))
    prev_kv = pl.BlockSpec((1, BLOCK, dh), lambda h, i: (h // group, jnp.maximum(i * nb - 1, 0), 0))
    in_specs = [cur_q, cur_kv, prev_kv, cur_kv, prev_kv, pl.BlockSpec((1, span, span + BLOCK), lambda h, i: (h, 0, 0))]
    ins = [q, k, k, v, v, _span_bias(bias, span)]
    if has_sink:
        in_specs.append(pl.BlockSpec((1, BAND_SPAN_MAX, LANES), lambda h, i: (h, 0, 0)))
        ins.append(sinks)
    return pl.pallas_call(
        body, name=name, grid=(hq, nchunks), in_specs=in_specs,
        out_specs=[cur_q, cur_q],
        out_shape=[jax.ShapeDtypeStruct((hq, t, dh), F32), jax.ShapeDtypeStruct((hq, t, dh), F32)],
        compiler_params=_params(("parallel", "arbitrary")),
    )(*ins)


def _band_bwd(q, k, v, o, lse, do, bias, sinks, blocks_per_seq, span, name):
    hq, t, dh = q.shape
    group = hq // k.shape[0]
    rows, nb, nchunks, span = _band_geometry(t, blocks_per_seq, span)
    per_span = span // BLOCK
    has_sink = sinks is not None
    scale = HEAD_DIM ** -0.5

    def body(q_ref, kc_ref, kp_ref, vc_ref, vp_ref, o_ref, lse_ref, do_ref, bias_ref, *rest):
        dq_ref, dk_ref, dv_ref, dbias_ref, dsink_ref, dk_carry, dv_carry = rest[-7:]
        step = pl.program_id(1)
        chunk = nchunks - 1 - step
        bias_v = bias_ref[0]
        sink = rest[0][0, :span, :1] if has_sink else None

        @pl.when(step == 0)
        def _():
            dk_carry[...] = jnp.zeros_like(dk_carry)
            dv_carry[...] = jnp.zeros_like(dv_carry)
            dbias_ref[...] = jnp.zeros_like(dbias_ref)
            dsink_ref[...] = jnp.zeros_like(dsink_ref)

        dks = [jnp.zeros((BLOCK, dh), F32) for _ in range(nb + 1)]
        dvs = [jnp.zeros((BLOCK, dh), F32) for _ in range(nb + 1)]
        dks[nb] = dk_carry[...]
        dvs[nb] = dv_carry[...]
        for j in range(rows // span - 1, -1, -1):
            cur = slice(j * span, (j + 1) * span)
            prev = slice(j * span - BLOCK, j * span)
            kk = jnp.concatenate([kp_ref[0] if j == 0 else kc_ref[0, prev, :], kc_ref[0, cur, :]], axis=0)
            vv = jnp.concatenate([vp_ref[0] if j == 0 else vc_ref[0, prev, :], vc_ref[0, cur, :]], axis=0)
            first = lax.rem(chunk * nb, blocks_per_seq) == 0 if j == 0 else None
            qj, doj = q_ref[0, cur, :], do_ref[0, cur, :]
            lse_j = lse_ref[0, cur, :][:, :1]
            p = jnp.exp(_band_logits(qj, kk, bias_v, first) - lse_j)
            dp = lax.dot_general(doj, vv, NT, preferred_element_type=F32)
            delta = jnp.sum(doj.astype(F32) * o_ref[0, cur, :], axis=1, keepdims=True)
            ds = p * (dp - delta)
            dbias_ref[0] += ds
            if has_sink:
                dsink = -jnp.sum(jnp.exp(sink - lse_j) * delta, axis=0, keepdims=True)
                dsink_ref[0] += jnp.broadcast_to(dsink, (1, LANES))
            dsb = (ds * scale).astype(BF16)
            dq_ref[0, cur, :] = jnp.dot(dsb, kk, preferred_element_type=F32)
            dkk = lax.dot_general(dsb, qj, TN, preferred_element_type=F32)
            dvv = lax.dot_general(p.astype(BF16), doj, TN, preferred_element_type=F32)
            for b in range(per_span + 1):
                piece = slice(b * BLOCK, (b + 1) * BLOCK)
                dks[j * per_span + b] += dkk[piece]
                dvs[j * per_span + b] += dvv[piece]
        for j in range(nb):
            cur = slice(j * BLOCK, (j + 1) * BLOCK)
            dk_ref[0, cur, :] = dks[j + 1]
            dv_ref[0, cur, :] = dvs[j + 1]
        dk_carry[...] = dks[0]
        dv_carry[...] = dvs[0]

    def rev(i):
        return nchunks - 1 - i

    cur_q = pl.BlockSpec((1, rows, dh), lambda h, i: (h, rev(i), 0))
    cur_kv = pl.BlockSpec((1, rows, dh), lambda h, i: (h // group, rev(i), 0))
    prev_kv = pl.BlockSpec((1, BLOCK, dh), lambda h, i: (h // group, jnp.maximum(rev(i) * nb - 1, 0), 0))
    cur_lse = cur_q
    per_head_bias = pl.BlockSpec((1, span, span + BLOCK), lambda h, i: (h, 0, 0))
    per_head_row = pl.BlockSpec((1, 1, LANES), lambda h, i: (h, 0, 0))
    in_specs = [cur_q, cur_kv, prev_kv, cur_kv, prev_kv, cur_q, cur_lse, cur_q, per_head_bias]
    ins = [q, k, k, v, v, o, lse, do, _span_bias(bias, span)]
    if has_sink:
        in_specs.append(pl.BlockSpec((1, BAND_SPAN_MAX, LANES), lambda h, i: (h, 0, 0)))
        ins.append(sinks)
    full = jax.ShapeDtypeStruct((hq, t, dh), F32)
    dq, dk, dv, dbias, dsink = pl.pallas_call(
        body, name=name, grid=(hq, nchunks), in_specs=in_specs,
        out_specs=[cur_q, cur_q, cur_q, per_head_bias, per_head_row],
        out_shape=[full, full, full, jax.ShapeDtypeStruct((hq, span, span + BLOCK), F32),
                   jax.ShapeDtypeStruct((hq, 1, LANES), F32)],
        scratch_shapes=[pltpu.VMEM((BLOCK, dh), F32), pltpu.VMEM((BLOCK, dh), F32)],
        compiler_params=_params(("parallel", "arbitrary")),
    )(*ins)
    return dq, dk, dv, _fold_span_bias_grad(dbias, span), dsink


C_QK = C_NOPE + C_ROPE
C_SCALE = C_QK ** -0.5
LOG2E = math.log2(math.e)
C_EXP2 = C_SCALE * LOG2E
CAUSAL_SUB = 256


def _causal_tile(t):
    return min(2048, t)


CAUSAL_Q_CHAIN = 128
CAUSAL_K_CHAIN = 256
STAT_ROWS = 8


def _mla_fwd(q, k, v1t, name):
    h, t, _ = q.shape
    tq = _causal_tile(t)
    n = t // tq
    qs, ks = min(CAUSAL_Q_CHAIN, tq), min(CAUSAL_K_CHAIN, tq)

    def body(q_ref, k_ref, v_ref, ot_ref, lse_ref, m_scr, acc_scr):
        qi, ki = pl.program_id(1), pl.program_id(2)

        @pl.when(ki == 0)
        def _():
            m_scr[...] = jnp.full_like(m_scr, NEG)
            acc_scr[...] = jnp.zeros_like(acc_scr)

        def tile(diagonal):
            for r in range(tq // qs):
                cols = slice(r * qs, (r + 1) * qs)
                q_sub = q_ref[0, cols, :]
                m, acc = m_scr[:1, cols], acc_scr[:, cols]
                for kc in range(tq // ks):
                    k0 = kc * ks
                    if diagonal and k0 > r * qs + qs - 1:
                        continue
                    st = lax.dot_general(k_ref[0, k0:k0 + ks, :], q_sub, NT, preferred_element_type=F32)
                    if diagonal and k0 + ks - 1 > r * qs:
                        kpos = k0 + lax.broadcasted_iota(jnp.int32, st.shape, 0)
                        qpos = r * qs + lax.broadcasted_iota(jnp.int32, st.shape, 1)
                        st = jnp.where(kpos <= qpos, st, NEG)
                    m_new = jnp.maximum(m, jnp.max(st, axis=0, keepdims=True))
                    alpha = jnp.exp2((m - m_new) * C_EXP2)
                    pt = jnp.exp2((st - m_new) * C_EXP2).astype(BF16)
                    acc = acc * alpha + jnp.dot(v_ref[0, :, k0:k0 + ks], pt, preferred_element_type=F32)
                    m = m_new
                m_scr[:, cols] = jnp.broadcast_to(m, (STAT_ROWS, qs))
                acc_scr[:, cols] = acc

        @pl.when(ki < qi)
        def _():
            tile(False)

        @pl.when(ki == qi)
        def _():
            tile(True)
            l = acc_scr[C_V:C_V + 1, :]
            ot_ref[0] = acc_scr[:C_V, :] / l
            lse_ref[0] = jnp.broadcast_to(m_scr[:1, :] * C_SCALE + jnp.log(l), (STAT_ROWS, tq))

    return pl.pallas_call(
        body, name=name, grid=(h, n, n),
        in_specs=[pl.BlockSpec((1, tq, C_QK), lambda hh, qi, ki: (hh, qi, 0)),
                  pl.BlockSpec((1, tq, C_QK), lambda hh, qi, ki: (hh, jnp.minimum(ki, qi), 0)),
                  pl.BlockSpec((1, LANES, tq), lambda hh, qi, ki: (hh, 0, jnp.minimum(ki, qi)))],
        out_specs=[pl.BlockSpec((1, C_V, tq), lambda hh, qi, ki: (hh, 0, qi)),
                   pl.BlockSpec((1, STAT_ROWS, tq), lambda hh, qi, ki: (hh, 0, qi))],
        out_shape=[jax.ShapeDtypeStruct((h, C_V, t), F32), jax.ShapeDtypeStruct((h, STAT_ROWS, t), F32)],
        scratch_shapes=[pltpu.VMEM((STAT_ROWS, tq), F32), pltpu.VMEM((LANES, tq), F32)],
        compiler_params=_params(("parallel", "arbitrary", "arbitrary")),
    )(q, k, v1t)


def _mla_bwd(q, k, v1, o, lse, do, name):
    h, t, _ = q.shape
    tq = _causal_tile(t)
    n = t // tq
    sub = min(CAUSAL_SUB, tq)

    def body(q_ref, k_ref, v_ref, o_ref, lse_ref, do_ref, dqt_ref, dk_ref, dv_ref, dk_acc, dv_acc):
        ki, qi = pl.program_id(1), pl.program_id(2)

        @pl.when(qi == 0)
        def _():
            dk_acc[...] = jnp.zeros_like(dk_acc)
            dv_acc[...] = jnp.zeros_like(dv_acc)

        @pl.when(jnp.logical_and(ki == 0, qi == 0))
        def _():
            dqt_ref[...] = jnp.zeros_like(dqt_ref)

        def tile(diagonal):
            for c in range(tq // sub):
                cols = slice(c * sub, (c + 1) * sub)
                nk = (c + 1) * sub if diagonal else tq
                qc, doc = q_ref[0, cols, :], do_ref[0, cols, :]
                st = lax.dot_general(k_ref[0, :nk, :], qc, NT, preferred_element_type=F32)
                lse2 = lse_ref[0, :1, cols] * LOG2E
                pt = jnp.exp2(st * C_EXP2 - lse2)
                if diagonal:
                    kpos = lax.broadcasted_iota(jnp.int32, st.shape, 0)
                    qpos = c * sub + lax.broadcasted_iota(jnp.int32, st.shape, 1)
                    pt = jnp.where(kpos <= qpos, pt, 0.0)
                dpt = lax.dot_general(v_ref[0, :nk, :C_V], doc, NT, preferred_element_type=F32)
                delta = jnp.sum(doc.astype(F32) * o_ref[0, cols, :], axis=1, keepdims=True)
                delta_row = jnp.transpose(jnp.broadcast_to(delta, (sub, LANES)))[:1]
                dst = (pt * (dpt - delta_row)).astype(BF16)
                dv_acc[:nk, :] += jnp.dot(pt.astype(BF16), doc, preferred_element_type=F32)
                dk_acc[:nk, :] += jnp.dot(dst, qc, preferred_element_type=F32)
                out_cols = pl.ds(pl.multiple_of(qi * tq + c * sub, sub), sub)
                dqt_ref[0, :, out_cols] += lax.dot_general(k_ref[0, :nk, :], dst, TN,
                                                           preferred_element_type=F32) * C_SCALE

        @pl.when(qi > ki)
        def _():
            tile(False)

        @pl.when(qi == ki)
        def _():
            tile(True)

        @pl.when(qi == n - 1)
        def _():
            dk_ref[0] = dk_acc[...] * C_SCALE
            dv_ref[0] = dv_acc[...]

    def q_spec(d):
        return pl.BlockSpec((1, tq, d), lambda hh, ki, qi: (hh, jnp.maximum(qi, ki), 0))

    def k_spec(d):
        return pl.BlockSpec((1, tq, d), lambda hh, ki, qi: (hh, ki, 0))

    return pl.pallas_call(
        body, name=name, grid=(h, n, n),
        in_specs=[q_spec(C_QK), k_spec(C_QK), k_spec(LANES), q_spec(C_V),
                  pl.BlockSpec((1, STAT_ROWS, tq), lambda hh, ki, qi: (hh, 0, jnp.maximum(qi, ki))), q_spec(C_V)],
        out_specs=[pl.BlockSpec((1, C_QK, t), lambda hh, ki, qi: (hh, 0, 0)), k_spec(C_QK), k_spec(C_V)],
        out_shape=[jax.ShapeDtypeStruct((h, C_QK, t), F32), jax.ShapeDtypeStruct((h, t, C_QK), F32),
                   jax.ShapeDtypeStruct((h, t, C_V), F32)],
        scratch_shapes=[pltpu.VMEM((tq, C_QK), F32), pltpu.VMEM((tq, C_V), F32)],
        compiler_params=_params(("parallel", "arbitrary", "arbitrary")),
    )(q, k, v1, o, lse, do)


def _exchange(src, flips, src_idx, name):
    n = len(flips)
    _, r, c = src.shape

    def body(src_ref, dst_ref, send_sems, recv_sems):
        me = [lax.axis_index(a) for a in ALL_AXES]
        copies = []
        for kk, flip in enumerate(flips):
            peer = tuple(1 - p if f else p for p, f in zip(me, flip))
            copies.append(pltpu.make_async_remote_copy(
                src_ref=src_ref.at[src_idx[kk]], dst_ref=dst_ref.at[kk], send_sem=send_sems.at[kk],
                recv_sem=recv_sems.at[kk], device_id=peer, device_id_type=MESH))
        for cp in copies:
            cp.start()
        for cp in copies:
            cp.wait_recv()
        for cp in copies:
            cp.wait_send()

    return pl.pallas_call(
        body, name=name, in_specs=[pl.BlockSpec(memory_space=pl.ANY)], out_specs=pl.BlockSpec(memory_space=pl.ANY),
        out_shape=jax.ShapeDtypeStruct((n, r, c), src.dtype),
        scratch_shapes=[pltpu.SemaphoreType.DMA((n,)), pltpu.SemaphoreType.DMA((n,))],
    )(src)


FLIP_C = (0, 0, 1)
CHIP_FLIPS = ((0, 1, 0), (1, 0, 0), (1, 1, 0))
ALL_FLIPS = tuple((a >> 2 & 1, a >> 1 & 1, a & 1) for a in range(1, 8))


def _pick(stacked, idx):
    return lax.dynamic_index_in_dim(stacked, idx, axis=0, keepdims=False)


def _to_comm_rows(parts, dtype):
    flat = jnp.concatenate([p.reshape(-1) for p in parts]).astype(dtype)
    rows = -(-flat.size // (COMM_COLS * COMM_ROW_ALIGN)) * COMM_ROW_ALIGN
    return jnp.pad(flat, (0, rows * COMM_COLS - flat.size)).reshape(rows, COMM_COLS)


def _from_comm_rows(buf, shapes):
    flat, out, off = buf.reshape(-1), [], 0
    for s in shapes:
        size = math.prod(s)
        out.append(flat[off:off + size].reshape(s))
        off += size
    return out


def _my_place():
    x, y, c = (lax.axis_index(a) for a in ALL_AXES)
    return (x, y, c), 2 * x + y


def _flipped(me, flip):
    return tuple(1 - p if f else p for p, f in zip(me, flip))


def _remote(src, dst, sems, k, peer):
    send_sems, recv_sems = sems
    return pltpu.make_async_remote_copy(src_ref=src, dst_ref=dst, send_sem=send_sems.at[k], recv_sem=recv_sems.at[k],
                                        device_id=peer, device_id_type=MESH)


def _gather_shards(buf, chip):
    rows, cols = buf.shape
    half = rows // 2
    n = len(CHIP_FLIPS)

    def body(src_ref, out_ref, send_sems, recv_sems):
        me, chip = _my_place()
        sems = (send_sems, recv_sems)
        sibling = _flipped(me, FLIP_C)
        mine = pl.ds(pl.multiple_of(me[2] * half, half), half)
        theirs = pl.ds(pl.multiple_of((1 - me[2]) * half, half), half)
        peers = [_flipped(me, f) for f in CHIP_FLIPS]
        from_chip = [2 * p[0] + p[1] for p in peers]
        over_ici = [_remote(src_ref.at[mine], out_ref.at[chip, mine], sems, k, peers[k]) for k in range(n)]
        for cp in over_ici:
            cp.start()
        passed = [_remote(out_ref.at[from_chip[k], mine], out_ref.at[from_chip[k], mine], sems, n + k, sibling)
                  for k in range(n)]
        for k in range(n):
            _remote(src_ref.at[mine], out_ref.at[from_chip[k], mine], sems, k, peers[k]).wait_recv()
            passed[k].start()
        for k in range(n):
            _remote(out_ref.at[from_chip[k], theirs], out_ref.at[from_chip[k], theirs], sems, n + k, sibling).wait_recv()
        for cp in over_ici + passed:
            cp.wait_send()

    others = pl.pallas_call(
        body, name="gather_shards", in_specs=[pl.BlockSpec(memory_space=pl.ANY)],
        out_specs=pl.BlockSpec(memory_space=pl.ANY), out_shape=jax.ShapeDtypeStruct((4, rows, cols), buf.dtype),
        scratch_shapes=[pltpu.SemaphoreType.DMA((2 * n,)), pltpu.SemaphoreType.DMA((2 * n,))],
    )(buf)
    return lax.dynamic_update_slice(others, buf[None], (chip, 0, 0))


def _swap_other_halves(portions):
    _, rows, cols = portions.shape
    half = rows // 2

    def body(src_ref, dst_ref, send_sems, recv_sems):
        me, _ = _my_place()
        theirs = pl.ds(pl.multiple_of((1 - me[2]) * half, half), half)
        cp = _remote(src_ref.at[:, theirs], dst_ref, (send_sems, recv_sems), 0, _flipped(me, FLIP_C))
        cp.start()
        cp.wait_recv()
        cp.wait_send()

    return pl.pallas_call(
        body, name="reduce_d2d", in_specs=[pl.BlockSpec(memory_space=pl.ANY)],
        out_specs=pl.BlockSpec(memory_space=pl.ANY), out_shape=jax.ShapeDtypeStruct((4, half, cols), portions.dtype),
        scratch_shapes=[pltpu.SemaphoreType.DMA((1,)), pltpu.SemaphoreType.DMA((1,))],
    )(portions)


def _join_halves(total, core):
    half, cols = total.shape

    def body(src_ref, out_ref, send_sems, recv_sems):
        me, _ = _my_place()
        mine = pl.ds(pl.multiple_of(me[2] * half, half), half)
        theirs = pl.ds(pl.multiple_of((1 - me[2]) * half, half), half)
        sems = (send_sems, recv_sems)
        cp = _remote(src_ref, out_ref.at[mine], sems, 0, _flipped(me, FLIP_C))
        cp.start()
        _remote(src_ref, out_ref.at[theirs], sems, 0, _flipped(me, FLIP_C)).wait_recv()
        cp.wait_send()

    from_sibling = pl.pallas_call(
        body, name="reduce_share", in_specs=[pl.BlockSpec(memory_space=pl.ANY)],
        out_specs=pl.BlockSpec(memory_space=pl.ANY), out_shape=jax.ShapeDtypeStruct((2 * half, cols), total.dtype),
        scratch_shapes=[pltpu.SemaphoreType.DMA((1,)), pltpu.SemaphoreType.DMA((1,))],
    )(total)
    return lax.dynamic_update_slice(from_sibling, total, (core * half, 0))


def _reduce_scatter(portions, chip, core):
    half = portions.shape[1] // 2
    keep = lax.dynamic_slice_in_dim(portions, core * half, half, axis=1)
    got = _swap_other_halves(portions)
    pair = _ew(lambda p, q: (p + q,), [keep.reshape(4 * half, -1), got.reshape(4 * half, -1)],
               [(COMM_COLS, F32)], "reduce_pair_sum")[0].reshape(keep.shape)
    out = jnp.stack([_pick(pair, jnp.bitwise_xor(chip, f)) for f in (1, 2, 3)]).astype(BF16)
    others = _exchange(out, CHIP_FLIPS, (0, 1, 2), "reduce_ici")
    total = _ew(lambda p, q, r, s: (p + q.astype(F32) + r.astype(F32) + s.astype(F32),),
                [_pick(pair, chip), others[0], others[1], others[2]],
                [(COMM_COLS, F32)], "reduce_chip_sum")[0]
    return _join_halves(total, core)


def _all_reduce_small(buf, dev):
    got = _exchange(buf[None], ALL_FLIPS, (0,) * 7, "small_gather")
    by_flip = jnp.concatenate([buf[None], got])
    ordered = [_pick(by_flip, jnp.bitwise_xor(dev, a)) for a in range(8)]

    def fn(*t):
        s = t[0]
        for u in t[1:]:
            s = s + u
        return (s,)

    return _ew(fn, ordered, [(buf.shape[1], F32)], "small_sum")[0]


def _band_bucket_onehot(dilation, max_dist):
    i = jnp.arange(BLOCK)[:, None]
    j = jnp.arange(2 * BLOCK)[None, :]
    dist = i + BLOCK - j
    inband = (dist >= 0) & (dist <= max_dist)
    n = jnp.maximum(dist, 0) * dilation
    max_exact = NUM_BUCKETS // 2
    nf = jnp.maximum(n, 1).astype(F32)
    large = max_exact + (jnp.log(nf / max_exact) / math.log(MAX_DISTANCE / max_exact)
                         * (NUM_BUCKETS - max_exact)).astype(jnp.int32)
    bucket = jnp.where(n < max_exact, n, jnp.minimum(large, NUM_BUCKETS - 1))
    onehot = (bucket[..., None] == jnp.arange(NUM_BUCKETS)) & inband[..., None]
    return onehot.reshape(-1, NUM_BUCKETS).astype(F32), inband.reshape(-1)


def _band_bias(table, onehot, inband):
    vals = jnp.einsum("pb,bh->hp", onehot, table, precision=lax.Precision.HIGHEST)
    return jnp.where(inband[None, :], vals, NEG).reshape(-1, BLOCK, 2 * BLOCK)


BAND_VARIANTS = ((1, A_WINDOW - 1),) + tuple((dil, window // dil) for window, dil in B_BRANCHES)


LAYOUT_TILE_BYTES = 4 << 20


def _layout_rows(length, row_bytes):
    return _divisor_tile(length, max(16, LAYOUT_TILE_BYTES // row_bytes), 16)


def _split_heads(items, w, dil, name):
    s = items[0][0].shape[0]
    length = s // dil
    row_bytes = sum(-(-n * w // LANES) * LANES * (x.dtype.itemsize + jnp.dtype(d).itemsize) for x, _, n, d in items)
    tr = _layout_rows(length, row_bytes)
    nt = length // tr

    def body(*refs):
        for (x_ref, o_ref), (_, _, n, _) in zip(zip(refs[:len(items)], refs[len(items):]), items):
            for j in range(n):
                o_ref[j] = x_ref[:, j * w:(j + 1) * w].astype(o_ref.dtype)

    in_specs, out_specs, out_shape, views = [], [], [], []
    for x, first, n, d in items:
        bw, width = n * w, x.shape[1]
        assert bw % LANES == 0 and first % bw == 0 and (dil == 1 or width % bw == 0)
        in_specs.append(pl.BlockSpec((tr, bw), lambda r, i, c0=first // bw, wb=width // bw: (i, r * wb + c0)))
        out_specs.append(pl.BlockSpec((n, tr, w), lambda r, i: (0, r * nt + i, 0)))
        out_shape.append(jax.ShapeDtypeStruct((n, s, w), d))
        views.append(x.reshape(length, dil * width))
    return pl.pallas_call(
        body, name=name, grid=(dil, nt), in_specs=in_specs, out_specs=out_specs, out_shape=out_shape,
        compiler_params=_params(("parallel", "parallel")),
    )(*views)


def _merge_heads(items, dil, name, group_sum=1):
    s, w = items[0][0].shape[1:]
    length = s // dil
    row_bytes = sum(t.shape[0] * LANES * t.dtype.itemsize + t.shape[0] * w * jnp.dtype(d).itemsize for t, d in items)
    tr = _layout_rows(length, row_bytes)
    nt = length // tr

    def body(*refs):
        for t_ref, o_ref in zip(refs[:len(items)], refs[len(items):]):
            for j in range(t_ref.shape[0] // group_sum):
                v = t_ref[j * group_sum]
                for g in range(1, group_sum):
                    v = v + t_ref[j * group_sum + g]
                o_ref[:, j * w:(j + 1) * w] = v.astype(o_ref.dtype)

    in_specs, out_specs, out_shape = [], [], []
    for t, d in items:
        n = t.shape[0]
        bw = n // group_sum * w
        assert bw % LANES == 0
        in_specs.append(pl.BlockSpec((n, tr, w), lambda r, i: (0, r * nt + i, 0)))
        out_specs.append(pl.BlockSpec((tr, bw), lambda r, i: (i, r)))
        out_shape.append(jax.ShapeDtypeStruct((length, dil * bw), d))
    outs = pl.pallas_call(
        body, name=name, grid=(dil, nt), in_specs=in_specs, out_specs=out_specs, out_shape=out_shape,
        compiler_params=_params(("parallel", "parallel")),
    )(*[t for t, _ in items])
    return [o.reshape(s, -1) for o in outs]


def _merge_heads_transposed(t, dtype, name, also_heads=False):
    n, w, s = t.shape
    tr = _layout_rows(s, 3 * n * w * t.dtype.itemsize)

    def body(t_ref, o_ref, *heads_ref):
        rows = jnp.transpose(t_ref[...].reshape(n * w, tr))
        o_ref[...] = rows.astype(o_ref.dtype)
        for j in range(n if also_heads else 0):
            heads_ref[0][j] = rows[:, j * w:(j + 1) * w]

    out_specs = [pl.BlockSpec((tr, n * w), lambda i: (i, 0))]
    out_shape = [jax.ShapeDtypeStruct((s, n * w), dtype)]
    if also_heads:
        out_specs.append(pl.BlockSpec((n, tr, w), lambda i: (0, i, 0)))
        out_shape.append(jax.ShapeDtypeStruct((n, s, w), t.dtype))
    outs = pl.pallas_call(
        body, name=name, grid=(s // tr,), in_specs=[pl.BlockSpec((n, w, tr), lambda i: (0, 0, i))],
        out_specs=out_specs, out_shape=out_shape, compiler_params=_params(("parallel",)),
    )(t)
    return outs if also_heads else outs[0]


ROPE_HALF = C_ROPE // 2
ROPE_PERIOD = 3 * LANES


def _rope_tables(s):
    inv = ROPE_THETA ** (-jnp.arange(0, C_ROPE, 2, dtype=F32) / C_ROPE)
    ang = jnp.arange(s, dtype=F32)[:, None] * inv[None, :]
    cos, sin = jnp.cos(ang), jnp.sin(ang)
    one, zero = jnp.ones((s, C_NOPE), F32), jnp.zeros((s, C_NOPE), F32)
    z16 = jnp.zeros((s, ROPE_HALF), F32)
    reps = ROPE_PERIOD // C_QK
    keep = jnp.tile(jnp.concatenate([one, cos, cos], axis=1), (1, reps))
    from_above = jnp.tile(jnp.concatenate([zero, -sin, z16], axis=1), (1, reps))
    from_below = jnp.tile(jnp.concatenate([zero, z16, sin], axis=1), (1, reps))
    return (cos, sin), (keep, from_above, from_below)


def _rope_rows(x, tables, inverse, name, dtype):
    width = x.shape[1]
    reps = width // ROPE_PERIOD
    sign = -1.0 if inverse else 1.0

    def fn(xv, keep, above, below):
        keep, above, below = (jnp.tile(t, (1, reps)) for t in (keep, above, below))
        up = pltpu.roll(xv, width - ROPE_HALF, 1)
        down = pltpu.roll(xv, ROPE_HALF, 1)
        return (xv * keep + sign * (up * above + down * below),)

    return _ew(fn, [x, *tables], [(width, dtype)], name)[0]


def _rotate_half_pairs(a, b, cos, sin, inverse):
    if inverse:
        return a * cos + b * sin, b * cos - a * sin
    return a * cos - b * sin, a * sin + b * cos


def _split_kv(kv, down, cos, sin, name):
    s = kv.shape[0]
    h = kv.shape[1] // (C_NOPE + C_V)
    tr = _layout_rows(s, 8 * kv.shape[1])
    r0 = C_Q_RANK + C_KV_RANK

    def body(kv_ref, down_ref, cos_ref, sin_ref, k_ref, v_ref, vt_ref):
        k1, k2 = _rotate_half_pairs(down_ref[:, r0:r0 + ROPE_HALF], down_ref[:, r0 + ROPE_HALF:r0 + C_ROPE],
                                    cos_ref[...], sin_ref[...], False)
        lane = lax.broadcasted_iota(jnp.int32, (tr, LANES - C_V), 1)
        tail = jnp.where(lane == 0, 1.0, 0.0)
        for j in range(h):
            base = j * (C_NOPE + C_V)
            k_ref[j, :, :C_NOPE] = kv_ref[:, base:base + C_NOPE]
            k_ref[j, :, C_NOPE:C_NOPE + ROPE_HALF] = k1.astype(BF16)
            k_ref[j, :, C_NOPE + ROPE_HALF:] = k2.astype(BF16)
            v1 = jnp.concatenate([kv_ref[:, base + C_NOPE:base + C_NOPE + C_V].astype(F32), tail], axis=1)
            v_ref[j] = v1.astype(BF16)
            vt_ref[j] = jnp.transpose(v1).astype(BF16)

    def rows(width):
        return pl.BlockSpec((tr, width), lambda i: (i, 0))

    return pl.pallas_call(
        body, name=name, grid=(s // tr,),
        in_specs=[rows(kv.shape[1]), rows(down.shape[1]), rows(ROPE_HALF), rows(ROPE_HALF)],
        out_specs=[pl.BlockSpec((h, tr, C_QK), lambda i: (0, i, 0)), pl.BlockSpec((h, tr, LANES), lambda i: (0, i, 0)),
                   pl.BlockSpec((h, LANES, tr), lambda i: (0, 0, i))],
        out_shape=[jax.ShapeDtypeStruct((h, s, C_QK), BF16), jax.ShapeDtypeStruct((h, s, LANES), BF16),
                   jax.ShapeDtypeStruct((h, LANES, s), BF16)],
        compiler_params=_params(("parallel",)),
    )(kv, down, cos, sin)


def _merge_kv_bwd(dk, dv, cos, sin, name):
    h, s, _ = dk.shape
    tr = _layout_rows(s, 8 * h * LANES)

    def body(dk_ref, dv_ref, cos_ref, sin_ref, dkv_ref, dkr_ref):
        rot = dk_ref[0, :, C_NOPE:]
        for j in range(h):
            base = j * (C_NOPE + C_V)
            dkv_ref[:, base:base + C_NOPE] = dk_ref[j, :, :C_NOPE].astype(BF16)
            dkv_ref[:, base + C_NOPE:base + C_NOPE + C_V] = dv_ref[j].astype(BF16)
            if j:
                rot = rot + dk_ref[j, :, C_NOPE:]
        d1, d2 = _rotate_half_pairs(rot[:, :ROPE_HALF], rot[:, ROPE_HALF:], cos_ref[...], sin_ref[...], True)
        dkr_ref[:, :ROPE_HALF] = d1
        dkr_ref[:, ROPE_HALF:] = d2

    def rows(width):
        return pl.BlockSpec((tr, width), lambda i: (i, 0))

    return pl.pallas_call(
        body, name=name, grid=(s // tr,),
        in_specs=[pl.BlockSpec((h, tr, C_QK), lambda i: (0, i, 0)), pl.BlockSpec((h, tr, C_V), lambda i: (0, i, 0)),
                  rows(ROPE_HALF), rows(ROPE_HALF)],
        out_specs=[rows(h * (C_NOPE + C_V)), rows(C_ROPE)],
        out_shape=[jax.ShapeDtypeStruct((s, h * (C_NOPE + C_V)), BF16), jax.ShapeDtypeStruct((s, C_ROPE), F32)],
        compiler_params=_params(("parallel",)),
    )(dk, dv, cos, sin)


SPAN_FWD, SPAN_FWD_SINK, SPAN_BWD = 128, 256, 256


def _even_fwd(xn, h, w_in, w_out, sinks_row, biases, tag):
    s = xn.shape[0]
    proj = _mm(xn, w_in, "nn", [BF16], f"in_proj{tag}")[0]
    qd, kd = A_Q_HEADS * HEAD_DIM, A_KV_HEADS * HEAD_DIM
    qa, ka, va = _split_heads([(proj, 0, A_Q_HEADS, BF16), (proj, qd, A_KV_HEADS, BF16),
                               (proj, qd + kd, A_KV_HEADS, BF16)], HEAD_DIM, 1, f"swa_split{tag}")
    oa, lse_a = _band_fwd(qa, ka, va, biases[0], sinks_row, s // BLOCK, SPAN_FWD_SINK, f"swa_fwd{tag}")
    out_a = _merge_heads([(oa, BF16)], 1, f"swa_merge{tag}")[0]
    width = B_HEADS_PER_BRANCH * HEAD_DIM
    qkv_b, outs, lses = [], [], []
    for g, (_, dil) in enumerate(B_BRANCHES):
        base = A_IN + g * 3 * width
        src, base = (proj, base) if dil == 1 else (proj[:, base:base + 3 * width], 0)
        qkv = _split_heads([(src, base + i * width, B_HEADS_PER_BRANCH, BF16) for i in range(3)], HEAD_DIM, dil,
                           f"dil{g}_split{tag}")
        og, lg = _band_fwd(*qkv, biases[1 + g], None, s // dil // BLOCK, SPAN_FWD, f"dil{g}_fwd{tag}")
        qkv_b.append(qkv)
        merged = _merge_heads([(og, F32), (lg, F32)], dil, f"dil{g}_merge{tag}")
        outs.append(merged[0])
        lses.append(merged[1])

    def merge(o0, o1, o2, l0, l1, l2):
        m = jnp.maximum(jnp.maximum(l0, l1), l2)
        e0, e1, e2 = jnp.exp(l0 - m), jnp.exp(l1 - m), jnp.exp(l2 - m)
        den = e0 + e1 + e2
        out = (e0 * o0 + e1 * o1 + e2 * o2) / den
        return out, m + jnp.log(den), out

    out_b, lse_b, out_b16 = _ew(merge, outs + lses, [(width, F32), (width, F32), (width, BF16)], f"dil_merge{tag}")
    cat = jnp.concatenate([out_a, out_b16], axis=1)
    h_mid = _mm(cat, w_out, "nn", [F32], f"out_proj{tag}", epi=_add_epi, extras=(h,))[0]
    return h_mid, (qa, ka, va, oa, lse_a, qkv_b, out_b, lse_b, cat)


def _even_bwd(dh, xn, saved, w_in, w_out, sinks_row, biases, onehots, tag):
    qa, ka, va, oa, lse_a, qkv_b, out_b, lse_b, cat = saved
    s = xn.shape[0]
    qd = A_Q_HEADS * HEAD_DIM
    g_w_out = _mm(cat, dh, "tn", [F32], f"out_proj_dw{tag}")[0]
    dcat = _mm(dh, w_out, "nt", [BF16], f"out_proj_dx{tag}")[0]
    do_a = _split_heads([(dcat, 0, A_Q_HEADS, BF16)], HEAD_DIM, 1, f"swa_do_split{tag}")[0]
    dqa, dka8, dva8, dbias_a, dsink = _band_bwd(qa, ka, va, oa, lse_a, do_a, biases[0], sinks_row, s // BLOCK,
                                                SPAN_BWD, f"swa_bwd{tag}")
    pieces = _merge_heads([(dqa, BF16)], 1, f"swa_dq_merge{tag}")
    pieces += _merge_heads([(dka8, BF16), (dva8, BF16)], 1, f"swa_dkv_merge{tag}", group_sum=A_Q_HEADS // A_KV_HEADS)
    dbias_b = []
    dcat_b = dcat[:, qd:]
    for g, (_, dil) in enumerate(B_BRANCHES):
        do_g, out_g, lse_g = _split_heads([(dcat_b, 0, B_HEADS_PER_BRANCH, BF16), (out_b, 0, B_HEADS_PER_BRANCH, F32),
                                           (lse_b, 0, B_HEADS_PER_BRANCH, F32)], HEAD_DIM, dil, f"dil{g}_do_split{tag}")
        dqg, dkg, dvg, dbg, _ = _band_bwd(*qkv_b[g], out_g, lse_g, do_g, biases[1 + g], None, s // dil // BLOCK,
                                          SPAN_BWD, f"dil{g}_bwd{tag}")
        pieces += _merge_heads([(dqg, BF16), (dkg, BF16), (dvg, BF16)], dil, f"dil{g}_dqkv_merge{tag}")
        dbias_b.append(dbg)
    dproj = jnp.concatenate(pieces, axis=1)
    g_w_in = _mm(xn, dproj, "tn", [F32], f"in_proj_dw{tag}")[0]
    dxn = _mm(dproj, w_in, "nt", [F32], f"in_proj_dx{tag}")[0]
    cols = [_mm(db.reshape(db.shape[0], -1), onehots[v][0], "nn", [F32], f"bias_buckets{v}{tag}")[0].T
            for v, db in enumerate([dbias_a] + dbias_b)]
    return dxn, g_w_in, g_w_out, jnp.concatenate(cols, axis=1), dsink[:, 0, 0]


def _mla_layer_fwd(xn, h, w_dn, q_norm, w_uq, kv_norm, w_ukv, w_o, ropes, tag):
    (cos, sin), q_tables = ropes
    down = _mm(xn, w_dn, "nn", [F32], f"mla_down{tag}")[0]
    c_q, c_kv = down[:, :C_Q_RANK], down[:, C_Q_RANK:C_Q_RANK + C_KV_RANK]
    cqn = _rms_fwd(c_q, q_norm, f"mla_qnorm{tag}")
    ckvn = _rms_fwd(c_kv, kv_norm, f"mla_kvnorm{tag}")
    q = _mm(cqn, w_uq, "nn", [F32], f"mla_uq{tag}")[0]
    kv = _mm(ckvn, w_ukv, "nn", [BF16], f"mla_ukv{tag}")[0]
    qh = _split_heads([(_rope_rows(q, q_tables, False, f"mla_rope_q{tag}", BF16), 0, C_HEADS, BF16)], C_QK, 1,
                      f"mla_q_split{tag}")[0]
    kh, v1h, v1t = _split_kv(kv, down, cos, sin, f"mla_kv_split{tag}")
    ot, lse = _mla_fwd(qh, kh, v1t, f"mla_attn_fwd{tag}")
    o2d, o = _merge_heads_transposed(ot, BF16, f"mla_o_merge{tag}", also_heads=True)
    h_mid = _mm(o2d, w_o, "nn", [F32], f"mla_o{tag}", epi=_add_epi, extras=(h,))[0]
    return h_mid, (c_q, c_kv, cqn, ckvn, qh, kh, v1h, o, lse, o2d)


def _mla_layer_bwd(dh, xn, saved, w_dn, q_norm, w_uq, kv_norm, w_ukv, w_o, ropes, tag):
    c_q, c_kv, cqn, ckvn, qh, kh, v1h, o, lse, o2d = saved
    (cos, sin), q_tables = ropes
    g_w_o = _mm(o2d, dh, "tn", [F32], f"mla_o_dw{tag}")[0]
    do2d = _mm(dh, w_o, "nt", [BF16], f"mla_o_dx{tag}")[0]
    do = _split_heads([(do2d, 0, C_HEADS, BF16)], C_V, 1, f"mla_do_split{tag}")[0]
    dqt, dk, dv = _mla_bwd(qh, kh, v1h, o, lse, do, f"mla_attn_bwd{tag}")
    dq_roped = _merge_heads_transposed(dqt, F32, f"mla_dq_merge{tag}")
    dq = _rope_rows(dq_roped, q_tables, True, f"mla_rope_q_bwd{tag}", BF16)
    dkv, dk_rope = _merge_kv_bwd(dk, dv, cos, sin, f"mla_dkv_merge{tag}")
    g_w_uq = _mm(cqn, dq, "tn", [F32], f"mla_uq_dw{tag}")[0]
    dcqn = _mm(dq, w_uq, "nt", [F32], f"mla_uq_dx{tag}")[0]
    g_w_ukv = _mm(ckvn, dkv, "tn", [F32], f"mla_ukv_dw{tag}")[0]
    dckvn = _mm(dkv, w_ukv, "nt", [F32], f"mla_ukv_dx{tag}")[0]
    dc_q, g_q_norm = _rms_bwd(c_q, q_norm, dcqn, None, f"mla_qnorm_bwd{tag}")
    dc_kv, g_kv_norm = _rms_bwd(c_kv, kv_norm, dckvn, None, f"mla_kvnorm_bwd{tag}")
    ddown = jnp.concatenate([dc_q, dc_kv, dk_rope], axis=1).astype(BF16)
    g_w_dn = _mm(xn, ddown, "tn", [F32], f"mla_down_dw{tag}")[0]
    dxn = _mm(ddown, w_dn, "nt", [F32], f"mla_down_dx{tag}")[0]
    return dxn, g_w_dn, g_q_norm, g_w_uq, g_kv_norm, g_w_ukv, g_w_o


SHARDED = (("w_in_ab", 2), ("w_out_ab", 2), ("w_down_c", 1), ("w_uq_c", 2), ("w_ukv_c", 2), ("w_o_c", 2),
           ("w_mlp_up", 2), ("w_mlp_down", 1))
SHARDED_NORMS = ("q_norm_c", "kv_norm_c")


def kernel(x, rel_bias, attn_norm, mlp_norm, final_norm, w_in_ab, sinks, w_out_ab, w_down_c, q_norm_c, w_uq_c, kv_norm_c, w_ukv_c, w_o_c, w_mlp_up, w_mlp_down, loss_target, m_rel_bias, m_attn_norm, m_mlp_norm, m_final_norm, m_w_in_ab, m_sinks, m_w_out_ab, m_w_down_c, m_q_norm_c, m_w_uq_c, m_kv_norm_c, m_w_ukv_c, m_w_o_c, m_w_mlp_up, m_w_mlp_down, v_rel_bias, v_attn_norm, v_mlp_norm, v_final_norm, v_w_in_ab, v_sinks, v_w_out_ab, v_w_down_c, v_q_norm_c, v_w_uq_c, v_kv_norm_c, v_w_ukv_c, v_w_o_c, v_w_mlp_up, v_w_mlp_down):
    given = dict(locals())
    chip = lax.axis_index("x") * 2 + lax.axis_index("y")
    core = lax.axis_index("c")
    dev = chip * 2 + core
    depth = attn_norm.shape[0]
    s = x.shape[1]

    shards = [given[n] for n, _ in SHARDED]
    norm_shards = [given[n] for n in SHARDED_NORMS]
    packed = _to_comm_rows([t.astype(BF16) for t in shards]
                           + [lax.bitcast_convert_type(t, BF16) for t in norm_shards], BF16)
    by_chip = _gather_shards(packed, chip)
    shapes = [t.shape for t in shards] + [t.shape + (2,) for t in norm_shards]
    pieces = [_from_comm_rows(by_chip[a], shapes) for a in range(4)]
    full = {n: jnp.concatenate([pieces[a][i] for a in range(4)], axis=ax) for i, (n, ax) in enumerate(SHARDED)}
    for i, n in enumerate(SHARDED_NORMS):
        full[n] = jnp.concatenate([lax.bitcast_convert_type(pieces[a][len(SHARDED) + i], F32) for a in range(4)],
                                  axis=-1)

    onehots = [_band_bucket_onehot(dil, md) for dil, md in BAND_VARIANTS]
    head_cols = [(0, A_Q_HEADS)] + [(A_Q_HEADS + g * B_HEADS_PER_BRANCH, A_Q_HEADS + (g + 1) * B_HEADS_PER_BRANCH)
                                    for g in range(len(B_BRANCHES))]
    biases = [_band_bias(rel_bias[:, lo:hi], oh, inb) for (lo, hi), (oh, inb) in zip(head_cols, onehots)]
    ropes = _rope_tables(s)
    sink_rows = [jnp.broadcast_to(sinks[e][:, None, None], (A_Q_HEADS, BAND_SPAN_MAX, LANES))
                 for e in range(sinks.shape[0])]

    def odd_weights(o):
        return [full[n][o] for n in ("w_down_c", "q_norm_c", "w_uq_c", "kv_norm_c", "w_ukv_c", "w_o_c")]

    h = x[0]
    saved = []
    for l in range(depth):
        xn = _rms_fwd(h, attn_norm[l], f"attn_norm{l}")
        if l % 2 == 0:
            e = l // 2
            h_mid, mix = _even_fwd(xn, h, full["w_in_ab"][e], full["w_out_ab"][e], sink_rows[e], biases, f"_{l}")
        else:
            h_mid, mix = _mla_layer_fwd(xn, h, *odd_weights(l // 2), ropes, f"_{l}")
        xn2 = _rms_fwd(h_mid, mlp_norm[l], f"mlp_norm{l}")
        act, relu = _mm(xn2, full["w_mlp_up"][l], "nn", [BF16, BF16], f"mlp_up{l}", epi=_relu2_epi)
        h_out = _mm(act, full["w_mlp_down"][l], "nn", [F32], f"mlp_down{l}", epi=_add_epi, extras=(h_mid,))[0]
        saved.append((h, xn, mix, h_mid, xn2, act, relu))
        h = h_out
    loss_part, dh, dh16, g_final = _final_loss(h, final_norm, loss_target[0], "final_loss")

    grads = {n: [None] * given[n].shape[0] for n, _ in SHARDED}
    g_q_norm, g_kv_norm = [None] * q_norm_c.shape[0], [None] * kv_norm_c.shape[0]
    g_attn_norm, g_mlp_norm = [None] * depth, [None] * depth
    g_sinks = [None] * sinks.shape[0]
    g_rel_bias = None
    for l in range(depth - 1, -1, -1):
        h_in, xn, mix, h_mid, xn2, act, relu = saved[l]
        grads["w_mlp_down"][l] = _mm(act, dh16, "tn", [F32], f"mlp_down_dw{l}")[0]
        du = _mm(dh16, full["w_mlp_down"][l], "nt", [BF16], f"mlp_down_dx{l}", epi=_relu2_bwd_epi, extras=(relu,))[0]
        grads["w_mlp_up"][l] = _mm(xn2, du, "tn", [F32], f"mlp_up_dw{l}")[0]
        dxn2 = _mm(du, full["w_mlp_up"][l], "nt", [F32], f"mlp_up_dx{l}")[0]
        dh, dh16, g_mlp_norm[l] = _rms_bwd(h_mid, mlp_norm[l], dxn2, dh, f"mlp_norm_bwd{l}", also_bf16=True)
        if l % 2 == 0:
            e = l // 2
            dxn, grads["w_in_ab"][e], grads["w_out_ab"][e], g_table, g_sinks[e] = _even_bwd(
                dh16, xn, mix, full["w_in_ab"][e], full["w_out_ab"][e], sink_rows[e], biases, onehots, f"_{l}")
            g_rel_bias = g_table if g_rel_bias is None else g_rel_bias + g_table
        else:
            o = l // 2
            (dxn, grads["w_down_c"][o], g_q_norm[o], grads["w_uq_c"][o], g_kv_norm[o], grads["w_ukv_c"][o],
             grads["w_o_c"][o]) = _mla_layer_bwd(dh16, xn, mix, *odd_weights(o), ropes, f"_{l}")
        if l:
            dh, dh16, g_attn_norm[l] = _rms_bwd(h_in, attn_norm[l], dxn, dh, f"attn_norm_bwd{l}", also_bf16=True)
        else:
            dh, g_attn_norm[l] = _rms_bwd(h_in, attn_norm[l], dxn, dh, f"attn_norm_bwd{l}")
    grad_x = dh[None]

    portions = []
    for a in range(4):
        parts = []
        for n, ax in SHARDED:
            g = jnp.stack(grads[n])
            size = g.shape[ax] // 4
            parts.append(lax.slice_in_dim(g, a * size, (a + 1) * size, axis=ax))
        portions.append(_to_comm_rows(parts, F32))
    reduced = _from_comm_rows(_reduce_scatter(jnp.stack(portions), chip, core), [t.shape for t in shards])
    g_shard = {n: reduced[i] for i, (n, _) in enumerate(SHARDED)}

    small = [jnp.stack(g_attn_norm), jnp.stack(g_mlp_norm), g_final, g_rel_bias, jnp.stack(g_sinks),
             jnp.stack(g_q_norm), jnp.stack(g_kv_norm), loss_part.reshape(1)]
    small_shapes = [t.shape for t in small]
    summed = _from_comm_rows(_all_reduce_small(_to_comm_rows(small, F32)[:16], dev), small_shapes)
    loss = summed[7][0]
    g_small = dict(zip(("attn_norm", "mlp_norm", "final_norm", "rel_bias", "sinks"), summed[:5]))
    for n, g in zip(SHARDED_NORMS, summed[5:7]):
        size = g.shape[1] // 4
        g_shard[n] = lax.dynamic_slice_in_dim(g, chip * size, size, axis=1)

    order = ["rel_bias", "attn_norm", "mlp_norm", "final_norm", "w_in_ab", "sinks", "w_out_ab", "w_down_c",
             "q_norm_c", "w_uq_c", "kv_norm_c", "w_ukv_c", "w_o_c", "w_mlp_up", "w_mlp_down"]
    g_all = {**g_small, **g_shard}
    deltas, new_m, new_v = [], [], []
    for n in order:
        d, mn, vn = _adamw(given[n], g_all[n], given["m_" + n], given["v_" + n], f"adamw_{n}")
        deltas.append(d)
        new_m.append(mn)
        new_v.append(vn)
    return (loss, grad_x, *[g_all[n] for n in order], *deltas, *new_m, *new_v)
```

```python
import math

import jax
import jax.numpy as jnp
from jax import lax
from jax.experimental import pallas as pl
from jax.experimental.pallas import tpu as pltpu

F32 = jnp.float32
BF16 = jnp.bfloat16
MESH = pl.DeviceIdType.MESH
ALL_AXES = ("x", "y", "c")

EPS = 1e-6
NEG = -1e30
BLOCK = 128
HEAD_DIM = 64
A_Q_HEADS, A_KV_HEADS = 8, 2
A_WINDOW = 128
B_BRANCHES = ((128, 1), (512, 4), (2048, 16))
B_HEADS_PER_BRANCH = 4
NUM_BUCKETS, MAX_DISTANCE = 32, 2048
A_IN = (A_Q_HEADS + 2 * A_KV_HEADS) * HEAD_DIM
C_HEADS, C_NOPE, C_ROPE, C_V = 8, 64, 32, 64
C_Q_RANK, C_KV_RANK = 384, 256
ROPE_THETA = 10000.0
ADAM_LR, ADAM_B1, ADAM_B2, ADAM_EPS, ADAM_WD, ADAM_STEP = 0.001, 0.9, 0.999, 1e-08, 0.01, 10

V7X_VMEM_LIMIT_BYTES = 56 * 1024 * 1024
LANES = 128
COMM_COLS = 1024
COMM_ROW_ALIGN = 1024

NT = (((1,), (1,)), ((), ()))
NN = (((1,), (0,)), ((), ()))
TN = (((0,), (0,)), ((), ()))
DIMS = {"nn": NN, "nt": NT, "tn": TN}


def _params(sem):
    return pltpu.CompilerParams(dimension_semantics=sem, vmem_limit_bytes=V7X_VMEM_LIMIT_BYTES)


def _divisor_tile(n, limit, align):
    if n <= limit:
        return n
    t = (limit // align) * align
    while t >= align:
        if n % t == 0:
            return t
        t -= align
    return n


def _ew(fn, ins, outs, name, acc_outs=(), target_bytes=6 << 20):
    rows = max(a.shape[0] for a in ins)

    def vmem_row_bytes(cols, dtype):
        return -(-cols // LANES) * LANES * jnp.dtype(dtype).itemsize

    per_row = sum(vmem_row_bytes(a.shape[1], a.dtype) for a in ins if a.shape[0] == rows)
    per_row += sum(vmem_row_bytes(c, d) for c, d in outs)
    tr = _divisor_tile(rows, max(16, target_bytes // max(per_row, 1)), 16)
    n_in, n_out = len(ins), len(outs)

    def body(*refs):
        res = fn(*[r[...] for r in refs[:n_in]])
        for r, v in zip(refs[n_in:n_in + n_out], res[:n_out]):
            r[...] = v.astype(r.dtype)
        if acc_outs:
            acc_refs = refs[n_in + n_out:]

            @pl.when(pl.program_id(0) == 0)
            def _():
                for r in acc_refs:
                    r[...] = jnp.zeros_like(r)

            for r, v in zip(acc_refs, res[n_out:]):
                r[...] += v

    def spec(a):
        if a.shape[0] == rows:
            return pl.BlockSpec((tr, a.shape[1]), lambda i: (i, 0))
        return pl.BlockSpec((1, a.shape[1]), lambda i: (0, 0))

    out_shape = [jax.ShapeDtypeStruct((rows, c), d) for c, d in outs]
    out_shape += [jax.ShapeDtypeStruct((1, c), F32) for c in acc_outs]
    out_specs = [pl.BlockSpec((tr, c), lambda i: (i, 0)) for c, _ in outs]
    out_specs += [pl.BlockSpec((1, c), lambda i: (0, 0)) for c in acc_outs]
    return pl.pallas_call(
        body, name=name, grid=(rows // tr,), in_specs=[spec(a) for a in ins], out_specs=out_specs,
        out_shape=out_shape, compiler_params=_params(("arbitrary",)),
    )(*ins)


def _rms_fwd(x, g, name):
    def fn(xv, gv):
        return ((xv * lax.rsqrt(jnp.mean(xv * xv, axis=-1, keepdims=True) + EPS)) * gv,)

    return _ew(fn, [x, g.reshape(1, -1)], [(x.shape[1], BF16)], name)[0]


def _rms_bwd(x, g, dy, add, name, also_bf16=False):
    def fn(xv, gv, dyv, *rest):
        rstd = lax.rsqrt(jnp.mean(xv * xv, axis=-1, keepdims=True) + EPS)
        xh = xv * rstd
        dyg = dyv.astype(F32) * gv
        dx = rstd * (dyg - xh * jnp.mean(dyg * xh, axis=-1, keepdims=True))
        if rest:
            dx = dx + rest[0]
        return (dx,) * (2 if also_bf16 else 1) + (jnp.sum(dyv.astype(F32) * xh, axis=0, keepdims=True),)

    ins = [x, g.reshape(1, -1), dy] + ([] if add is None else [add])
    outs = [(x.shape[1], F32)] + ([(x.shape[1], BF16)] if also_bf16 else [])
    *dx, dg = _ew(fn, ins, outs, name, acc_outs=(x.shape[1],))
    return (*dx, dg[0])


def _final_loss(h, g, target, name):
    d = h.shape[1]

    def fn(xv, gv, tv):
        rstd = lax.rsqrt(jnp.mean(xv * xv, axis=-1, keepdims=True) + EPS)
        xh = xv * rstd
        err = xh * gv - tv
        part = 0.5 * jnp.sum(jnp.mean(err * err, axis=-1, keepdims=True), axis=0, keepdims=True)
        dy = err * (1.0 / d)
        dyg = dy * gv
        dx = rstd * (dyg - xh * jnp.mean(dyg * xh, axis=-1, keepdims=True))
        return dx, dx, jnp.broadcast_to(part, (1, LANES)), jnp.sum(dy * xh, axis=0, keepdims=True)

    dx, dx16, loss, dg = _ew(fn, [h, g.reshape(1, -1), target], [(d, F32), (d, BF16)], name, acc_outs=(LANES, d))
    return loss[0, 0], dx, dx16, dg[0]


def _adamw(w, g, m, v, name):
    def fn(wv, gv, mv, vv):
        mn = ADAM_B1 * mv + (1.0 - ADAM_B1) * gv
        vn = ADAM_B2 * vv + (1.0 - ADAM_B2) * jnp.square(gv)
        m_hat = mn / (1.0 - ADAM_B1 ** ADAM_STEP)
        v_hat = vn / (1.0 - ADAM_B2 ** ADAM_STEP)
        return -ADAM_LR * (m_hat / (jnp.sqrt(v_hat) + ADAM_EPS) + ADAM_WD * wv), mn, vn

    shape = w.shape
    cols = shape[-1] if w.ndim > 1 else w.size
    view = [t.reshape(-1, cols) for t in (w, g, m, v)]
    return [t.reshape(shape) for t in _ew(fn, view, [(cols, F32)] * 3, name)]


MM_VMEM_BUDGET_BYTES = 40 << 20


def _mm(a, b, dims, outs, name, epi=None, extras=(), tm=2048, tn=1024, tk=1024):
    if dims == "nn":
        (m, k), n = a.shape, b.shape[1]
    elif dims == "nt":
        (m, k), n = a.shape, b.shape[0]
    else:
        (k, m), n = a.shape, b.shape[1]
    tn, tk = _divisor_tile(n, tn, LANES), _divisor_tile(k, tk, LANES)

    def tile_bytes(rows):
        per_out = sum(jnp.dtype(d).itemsize for d in outs) + sum(e.dtype.itemsize for e in extras)
        return 2 * (rows * tk * a.dtype.itemsize + tk * tn * b.dtype.itemsize + rows * tn * per_out) + 4 * rows * tn

    tm = _divisor_tile(m, tm, LANES)
    while tile_bytes(tm) > MM_VMEM_BUDGET_BYTES and tm % (2 * LANES) == 0:
        tm //= 2
    nk = k // tk
    n_ex, n_out = len(extras), len(outs)

    def body(a_ref, b_ref, *rest):
        ex_refs, out_refs = rest[:n_ex], rest[n_ex:n_ex + n_out]

        def finish(acc):
            res = epi(acc, *[r[...] for r in ex_refs]) if epi else (acc,)
            for r, v in zip(out_refs, res):
                r[...] = v.astype(r.dtype)

        part = lax.dot_general(a_ref[...].astype(BF16), b_ref[...].astype(BF16), DIMS[dims],
                               preferred_element_type=F32)
        if nk == 1:
            finish(part)
        else:
            acc_ref = rest[-1]
            kk = pl.program_id(2)

            @pl.when(kk == 0)
            def _():
                acc_ref[...] = part

            @pl.when(kk > 0)
            def _():
                acc_ref[...] += part

            @pl.when(kk == nk - 1)
            def _():
                finish(acc_ref[...])

    if dims == "nn":
        a_spec = pl.BlockSpec((tm, tk), lambda i, j, kk: (i, kk))
        b_spec = pl.BlockSpec((tk, tn), lambda i, j, kk: (kk, j))
    elif dims == "nt":
        a_spec = pl.BlockSpec((tm, tk), lambda i, j, kk: (i, kk))
        b_spec = pl.BlockSpec((tn, tk), lambda i, j, kk: (j, kk))
    else:
        a_spec = pl.BlockSpec((tk, tm), lambda i, j, kk: (kk, i))
        b_spec = pl.BlockSpec((tk, tn), lambda i, j, kk: (kk, j))
    tile = pl.BlockSpec((tm, tn), lambda i, j, kk: (i, j))
    return pl.pallas_call(
        body, name=name, grid=(m // tm, n // tn, nk),
        in_specs=[a_spec, b_spec] + [tile] * n_ex, out_specs=[tile] * n_out,
        out_shape=[jax.ShapeDtypeStruct((m, n), d) for d in outs],
        scratch_shapes=[pltpu.VMEM((tm, tn), F32)] if nk > 1 else [],
        compiler_params=_params(("parallel", "parallel", "arbitrary")),
    )(a, b, *extras)


def _add_epi(acc, res):
    return (acc + res,)


def _relu2_epi(acc):
    r = jnp.maximum(acc, 0.0)
    return r * r, r


def _relu2_bwd_epi(acc, r):
    return (acc * (2.0 * r.astype(F32)),)


BAND_SPAN_MAX = 256


def _band_geometry(t, blocks_per_seq, span):
    rows = min(1024, blocks_per_seq * BLOCK)
    nb = rows // BLOCK
    assert blocks_per_seq % nb == 0 and t % rows == 0 and span <= BAND_SPAN_MAX
    return rows, nb, t // rows, min(span, rows)


def _span_bias(bias, span):
    n = span // BLOCK
    neg = jnp.full(bias.shape[:2] + (BLOCK,), NEG, F32)
    rows = [jnp.concatenate([neg] * a + [bias[:, :, :BLOCK], bias[:, :, BLOCK:]] + [neg] * (n - 1 - a), axis=2)
            for a in range(n)]
    return jnp.concatenate(rows, axis=1)


def _fold_span_bias_grad(dbias, span):
    n = span // BLOCK
    parts = [dbias[:, a * BLOCK:(a + 1) * BLOCK, a * BLOCK:(a + 2) * BLOCK] for a in range(n)]
    return sum(parts[1:], parts[0])


def _band_logits(qj, kk, bias, first):
    s = lax.dot_general(qj, kk, NT, preferred_element_type=F32) * (HEAD_DIM ** -0.5) + bias
    if first is not None:
        col = lax.broadcasted_iota(jnp.int32, s.shape, 1)
        s = jnp.where(col < jnp.where(first, BLOCK, 0), NEG, s)
    return s


def _band_fwd(q, k, v, bias, blocks_per_seq, span, name):
    hq, t, dh = q.shape
    group = hq // k.shape[0]
    rows, nb, nchunks, span = _band_geometry(t, blocks_per_seq, span)

    def body(q_ref, kc_ref, kp_ref, vc_ref, vp_ref, bias_ref, o_ref, lse_ref):
        i = pl.program_id(1)
        bias_v = bias_ref[0]
        for j in range(rows // span):
            cur = slice(j * span, (j + 1) * span)
            prev = slice(j * span - BLOCK, j * span)
            kk = jnp.concatenate([kp_ref[0] if j == 0 else kc_ref[0, prev, :], kc_ref[0, cur, :]], axis=0)
            vv = jnp.concatenate([vp_ref[0] if j == 0 else vc_ref[0, prev, :], vc_ref[0, cur, :]], axis=0)
            first = lax.rem(i * nb, blocks_per_seq) == 0 if j == 0 else None
            s = _band_logits(q_ref[0, cur, :], kk, bias_v, first)
            m = jnp.max(s, axis=1, keepdims=True)
            p = jnp.exp(s - m)
            l = jnp.sum(p, axis=1, keepdims=True)
            acc = jnp.dot(p.astype(BF16), vv, preferred_element_type=F32)
            o_ref[0, cur, :] = acc / l
            lse_ref[0, cur, :] = jnp.broadcast_to(m + jnp.log(l), (span, dh))

    cur_q = pl.BlockSpec((1, rows, dh), lambda h, i: (h, i, 0))
    cur_kv = pl.BlockSpec((1, rows, dh), lambda h, i: (h // group, i, 0))
    prev_kv = pl.BlockSpec((1, BLOCK, dh), lambda h, i: (h // group, jnp.maximum(i * nb - 1, 0), 0))
    in_specs = [cur_q, cur_kv, prev_kv, cur_kv, prev_kv, pl.BlockSpec((1, span, span + BLOCK), lambda h, i: (h, 0, 0))]
    return pl.pallas_call(
        body, name=name, grid=(hq, nchunks), in_specs=in_specs,
        out_specs=[cur_q, cur_q],
        out_shape=[jax.ShapeDtypeStruct((hq, t, dh), F32), jax.ShapeDtypeStruct((hq, t, dh), F32)],
        compiler_params=_params(("parallel", "arbitrary")),
    )(q, k, k, v, v, _span_bias(bias, span))


def _apply_sinks(o, lse, sinks, name):
    h, t, dh = o.shape
    tr = _layout_rows(t, 4 * 2 * LANES * 4)

    def body(o_ref, lse_ref, sink_ref, o2_ref, lse2_ref):
        sink = sink_ref[0, :1, :dh]
        lse_v = lse_ref[0]
        m = jnp.maximum(lse_v, sink)
        both = m + jnp.log(jnp.exp(lse_v - m) + jnp.exp(sink - m))
        o2_ref[0] = o_ref[0] * jnp.exp(lse_v - both)
        lse2_ref[0] = both

    tile = pl.BlockSpec((1, tr, dh), lambda hh, i: (hh, i, 0))
    return pl.pallas_call(
        body, name=name, grid=(h, t // tr),
        in_specs=[tile, tile, pl.BlockSpec((1, BAND_SPAN_MAX, LANES), lambda hh, i: (hh, 0, 0))],
        out_specs=[tile, tile], out_shape=[jax.ShapeDtypeStruct(o.shape, F32)] * 2,
        compiler_params=_params(("parallel", "parallel")),
    )(o, lse, sinks)


def _band_bwd(q, k, v, o, lse, do, bias, sinks, blocks_per_seq, span, name):
    hq, t, dh = q.shape
    group = hq // k.shape[0]
    rows, nb, nchunks, span = _band_geometry(t, blocks_per_seq, span)
    per_span = span // BLOCK
    has_sink = sinks is not None
    scale = HEAD_DIM ** -0.5

    def body(q_ref, kc_ref, kp_ref, vc_ref, vp_ref, o_ref, lse_ref, do_ref, bias_ref, *rest):
        dq_ref, dk_ref, dv_ref, dbias_ref, dsink_ref, dk_carry, dv_carry = rest[-7:]
        step = pl.program_id(1)
        chunk = nchunks - 1 - step
        bias_v = bias_ref[0]
        sink = rest[0][0, :span, :1] if has_sink else None

        @pl.when(step == 0)
        def _():
            dk_carry[...] = jnp.zeros_like(dk_carry)
            dv_carry[...] = jnp.zeros_like(dv_carry)
            dbias_ref[...] = jnp.zeros_like(dbias_ref)
            dsink_ref[...] = jnp.zeros_like(dsink_ref)

        dks = [jnp.zeros((BLOCK, dh), F32) for _ in range(nb + 1)]
        dvs = [jnp.zeros((BLOCK, dh), F32) for _ in range(nb + 1)]
        dks[nb] = dk_carry[...]
        dvs[nb] = dv_carry[...]
        for j in range(rows // span - 1, -1, -1):
            cur = slice(j * span, (j + 1) * span)
            prev = slice(j * span - BLOCK, j * span)
            kk = jnp.concatenate([kp_ref[0] if j == 0 else kc_ref[0, prev, :], kc_ref[0, cur, :]], axis=0)
            vv = jnp.concatenate([vp_ref[0] if j == 0 else vc_ref[0, prev, :], vc_ref[0, cur, :]], axis=0)
            first = lax.rem(chunk * nb, blocks_per_seq) == 0 if j == 0 else None
            qj, doj = q_ref[0, cur, :], do_ref[0, cur, :]
            lse_j = lse_ref[0, cur, :][:, :1]
            p = jnp.exp(_band_logits(qj, kk, bias_v, first) - lse_j)
            dp = lax.dot_general(doj, vv, NT, preferred_element_type=F32)
            delta = jnp.sum(doj.astype(F32) * o_ref[0, cur, :], axis=1, keepdims=True)
            ds = p * (dp - delta)
            dbias_ref[0] += ds
            if has_sink:
                dsink = -jnp.sum(jnp.exp(sink - lse_j) * delta, axis=0, keepdims=True)
                dsink_ref[0] += jnp.broadcast_to(dsink, (1, LANES))
            dsb = (ds * scale).astype(BF16)
            dq_ref[0, cur, :] = jnp.dot(dsb, kk, preferred_element_type=F32)
            dkk = lax.dot_general(dsb, qj, TN, preferred_element_type=F32)
            dvv = lax.dot_general(p.astype(BF16), doj, TN, preferred_element_type=F32)
            for b in range(per_span + 1):
                piece = slice(b * BLOCK, (b + 1) * BLOCK)
                dks[j * per_span + b] += dkk[piece]
                dvs[j * per_span + b] += dvv[piece]
        for j in range(nb):
            cur = slice(j * BLOCK, (j + 1) * BLOCK)
            dk_ref[0, cur, :] = dks[j + 1]
            dv_ref[0, cur, :] = dvs[j + 1]
        dk_carry[...] = dks[0]
        dv_carry[...] = dvs[0]

    def rev(i):
        return nchunks - 1 - i

    cur_q = pl.BlockSpec((1, rows, dh), lambda h, i: (h, rev(i), 0))
    cur_kv = pl.BlockSpec((1, rows, dh), lambda h, i: (h // group, rev(i), 0))
    prev_kv = pl.BlockSpec((1, BLOCK, dh), lambda h, i: (h // group, jnp.maximum(rev(i) * nb - 1, 0), 0))
    cur_lse = cur_q
    per_head_bias = pl.BlockSpec((1, span, span + BLOCK), lambda h, i: (h, 0, 0))
    per_head_row = pl.BlockSpec((1, 1, LANES), lambda h, i: (h, 0, 0))
    in_specs = [cur_q, cur_kv, prev_kv, cur_kv, prev_kv, cur_q, cur_lse, cur_q, per_head_bias]
    ins = [q, k, k, v, v, o, lse, do, _span_bias(bias, span)]
    if has_sink:
        in_specs.append(pl.BlockSpec((1, BAND_SPAN_MAX, LANES), lambda h, i: (h, 0, 0)))
        ins.append(sinks)
    full = jax.ShapeDtypeStruct((hq, t, dh), F32)
    dq, dk, dv, dbias, dsink = pl.pallas_call(
        body, name=name, grid=(hq, nchunks), in_specs=in_specs,
        out_specs=[cur_q, cur_q, cur_q, per_head_bias, per_head_row],
        out_shape=[full, full, full, jax.ShapeDtypeStruct((hq, span, span + BLOCK), F32),
                   jax.ShapeDtypeStruct((hq, 1, LANES), F32)],
        scratch_shapes=[pltpu.VMEM((BLOCK, dh), F32), pltpu.VMEM((BLOCK, dh), F32)],
        compiler_params=_params(("parallel", "arbitrary")),
    )(*ins)
    return dq, dk, dv, _fold_span_bias_grad(dbias, span), dsink


C_QK = C_NOPE + C_ROPE
C_SCALE = C_QK ** -0.5
LOG2E = math.log2(math.e)
C_EXP2 = C_SCALE * LOG2E
CAUSAL_SUB = 256


def _causal_tile(t, forward=False):
    return min(4096 if forward else 2048, t)


CAUSAL_Q_CHAIN = 128
CAUSAL_K_CHAIN = 256
STAT_ROWS = 8


def _mla_fwd(q, k, v1t, name):
    h, t, _ = q.shape
    tq = _causal_tile(t, forward=True)
    n = t // tq
    qs, ks = min(CAUSAL_Q_CHAIN, tq), min(CAUSAL_K_CHAIN, tq)

    def body(q_ref, k_ref, v_ref, ot_ref, lse_ref, m_scr, acc_scr):
        qi, ki = pl.program_id(1), pl.program_id(2)

        @pl.when(ki == 0)
        def _():
            m_scr[...] = jnp.full_like(m_scr, NEG)
            acc_scr[...] = jnp.zeros_like(acc_scr)

        def tile(diagonal):
            for r in range(tq // qs):
                cols = slice(r * qs, (r + 1) * qs)
                q_sub = q_ref[0, cols, :]
                m, acc = m_scr[:1, cols], acc_scr[:, cols]
                for kc in range(tq // ks):
                    k0 = kc * ks
                    if diagonal and k0 > r * qs + qs - 1:
                        continue
                    st = lax.dot_general(k_ref[0, k0:k0 + ks, :], q_sub, NT, preferred_element_type=F32)
                    if diagonal and k0 + ks - 1 > r * qs:
                        kpos = k0 + lax.broadcasted_iota(jnp.int32, st.shape, 0)
                        qpos = r * qs + lax.broadcasted_iota(jnp.int32, st.shape, 1)
                        st = jnp.where(kpos <= qpos, st, NEG)
                    m_new = jnp.maximum(m, jnp.max(st, axis=0, keepdims=True))
                    alpha = jnp.exp2((m - m_new) * C_EXP2)
                    pt = jnp.exp2((st - m_new) * C_EXP2).astype(BF16)
                    acc = acc * alpha + jnp.dot(v_ref[0, :, k0:k0 + ks], pt, preferred_element_type=F32)
                    m = m_new
                m_scr[:, cols] = jnp.broadcast_to(m, (STAT_ROWS, qs))
                acc_scr[:, cols] = acc

        @pl.when(ki < qi)
        def _():
            tile(False)

        @pl.when(ki == qi)
        def _():
            tile(True)
            l = acc_scr[C_V:C_V + 1, :]
            ot_ref[0] = acc_scr[:C_V, :] / l
            lse_ref[0] = jnp.broadcast_to(m_scr[:1, :] * C_SCALE + jnp.log(l), (STAT_ROWS, tq))

    return pl.pallas_call(
        body, name=name, grid=(h, n, n),
        in_specs=[pl.BlockSpec((1, tq, C_QK), lambda hh, qi, ki: (hh, qi, 0)),
                  pl.BlockSpec((1, tq, C_QK), lambda hh, qi, ki: (hh, jnp.minimum(ki, qi), 0)),
                  pl.BlockSpec((1, LANES, tq), lambda hh, qi, ki: (hh, 0, jnp.minimum(ki, qi)))],
        out_specs=[pl.BlockSpec((1, C_V, tq), lambda hh, qi, ki: (hh, 0, qi)),
                   pl.BlockSpec((1, STAT_ROWS, tq), lambda hh, qi, ki: (hh, 0, qi))],
        out_shape=[jax.ShapeDtypeStruct((h, C_V, t), F32), jax.ShapeDtypeStruct((h, STAT_ROWS, t), F32)],
        scratch_shapes=[pltpu.VMEM((STAT_ROWS, tq), F32), pltpu.VMEM((LANES, tq), F32)],
        compiler_params=_params(("parallel", "arbitrary", "arbitrary")),
    )(q, k, v1t)


def _mla_bwd(q, k, v1, o, lse, do, name):
    h, t, _ = q.shape
    tq = _causal_tile(t)
    n = t // tq
    sub = min(CAUSAL_SUB, tq)

    def body(q_ref, k_ref, v_ref, o_ref, lse_ref, do_ref, dqt_ref, dk_ref, dv_ref, dk_acc, dv_acc):
        ki, qi = pl.program_id(1), pl.program_id(2)

        @pl.when(qi == 0)
        def _():
            dk_acc[...] = jnp.zeros_like(dk_acc)
            dv_acc[...] = jnp.zeros_like(dv_acc)

        @pl.when(jnp.logical_and(ki == 0, qi == 0))
        def _():
            dqt_ref[...] = jnp.zeros_like(dqt_ref)

        def tile(diagonal):
            for c in range(tq // sub):
                cols = slice(c * sub, (c + 1) * sub)
                nk = (c + 1) * sub if diagonal else tq
                qc, doc = q_ref[0, cols, :], do_ref[0, cols, :]
                st = lax.dot_general(k_ref[0, :nk, :], qc, NT, preferred_element_type=F32)
                lse2 = lse_ref[0, :1, cols] * LOG2E
                pt = jnp.exp2(st * C_EXP2 - lse2)
                if diagonal:
                    kpos = lax.broadcasted_iota(jnp.int32, st.shape, 0)
                    qpos = c * sub + lax.broadcasted_iota(jnp.int32, st.shape, 1)
                    pt = jnp.where(kpos <= qpos, pt, 0.0)
                dpt = lax.dot_general(v_ref[0, :nk, :C_V], doc, NT, preferred_element_type=F32)
                delta = jnp.sum(doc.astype(F32) * o_ref[0, cols, :], axis=1, keepdims=True)
                delta_row = jnp.transpose(jnp.broadcast_to(delta, (sub, LANES)))[:1]
                dst = (pt * (dpt - delta_row)).astype(BF16)
                dv_acc[:nk, :] += jnp.dot(pt.astype(BF16), doc, preferred_element_type=F32)
                dk_acc[:nk, :] += jnp.dot(dst, qc, preferred_element_type=F32)
                out_cols = pl.ds(pl.multiple_of(qi * tq + c * sub, sub), sub)
                dqt_ref[0, :, out_cols] += lax.dot_general(k_ref[0, :nk, :], dst, TN,
                                                           preferred_element_type=F32) * C_SCALE

        @pl.when(qi > ki)
        def _():
            tile(False)

        @pl.when(qi == ki)
        def _():
            tile(True)

        @pl.when(qi == n - 1)
        def _():
            dk_ref[0] = dk_acc[...] * C_SCALE
            dv_ref[0] = dv_acc[...]

    def q_spec(d):
        return pl.BlockSpec((1, tq, d), lambda hh, ki, qi: (hh, jnp.maximum(qi, ki), 0))

    def k_spec(d):
        return pl.BlockSpec((1, tq, d), lambda hh, ki, qi: (hh, ki, 0))

    return pl.pallas_call(
        body, name=name, grid=(h, n, n),
        in_specs=[q_spec(C_QK), k_spec(C_QK), k_spec(LANES), q_spec(C_V),
                  pl.BlockSpec((1, STAT_ROWS, tq), lambda hh, ki, qi: (hh, 0, jnp.maximum(qi, ki))), q_spec(C_V)],
        out_specs=[pl.BlockSpec((1, C_QK, t), lambda hh, ki, qi: (hh, 0, 0)), k_spec(C_QK), k_spec(C_V)],
        out_shape=[jax.ShapeDtypeStruct((h, C_QK, t), F32), jax.ShapeDtypeStruct((h, t, C_QK), F32),
                   jax.ShapeDtypeStruct((h, t, C_V), F32)],
        scratch_shapes=[pltpu.VMEM((tq, C_QK), F32), pltpu.VMEM((tq, C_V), F32)],
        compiler_params=_params(("parallel", "arbitrary", "arbitrary")),
    )(q, k, v1, o, lse, do)


def _exchange(src, flips, src_idx, name):
    n = len(flips)
    _, r, c = src.shape

    def body(src_ref, dst_ref, send_sems, recv_sems):
        me = [lax.axis_index(a) for a in ALL_AXES]
        copies = []
        for kk, flip in enumerate(flips):
            peer = tuple(1 - p if f else p for p, f in zip(me, flip))
            copies.append(pltpu.make_async_remote_copy(
                src_ref=src_ref.at[src_idx[kk]], dst_ref=dst_ref.at[kk], send_sem=send_sems.at[kk],
                recv_sem=recv_sems.at[kk], device_id=peer, device_id_type=MESH))
        for cp in copies:
            cp.start()
        for cp in copies:
            cp.wait_recv()
        for cp in copies:
            cp.wait_send()

    return pl.pallas_call(
        body, name=name, in_specs=[pl.BlockSpec(memory_space=pl.ANY)], out_specs=pl.BlockSpec(memory_space=pl.ANY),
        out_shape=jax.ShapeDtypeStruct((n, r, c), src.dtype),
        scratch_shapes=[pltpu.SemaphoreType.DMA((n,)), pltpu.SemaphoreType.DMA((n,))],
    )(src)


FLIP_C = (0, 0, 1)
CHIP_FLIPS = ((0, 1, 0), (1, 0, 0), (1, 1, 0))
ALL_FLIPS = tuple((a >> 2 & 1, a >> 1 & 1, a & 1) for a in range(1, 8))


def _pick(stacked, idx):
    return lax.dynamic_index_in_dim(stacked, idx, axis=0, keepdims=False)


def _to_comm_rows(parts, dtype):
    flat = jnp.concatenate([p.reshape(-1) for p in parts]).astype(dtype)
    rows = -(-flat.size // (COMM_COLS * COMM_ROW_ALIGN)) * COMM_ROW_ALIGN
    return jnp.pad(flat, (0, rows * COMM_COLS - flat.size)).reshape(rows, COMM_COLS)


def _from_comm_rows(buf, shapes):
    flat, out, off = buf.reshape(-1), [], 0
    for s in shapes:
        size = math.prod(s)
        out.append(flat[off:off + size].reshape(s))
        off += size
    return out


def _my_place():
    x, y, c = (lax.axis_index(a) for a in ALL_AXES)
    return (x, y, c), 2 * x + y


def _flipped(me, flip):
    return tuple(1 - p if f else p for p, f in zip(me, flip))


def _remote(src, dst, sems, k, peer):
    send_sems, recv_sems = sems
    return pltpu.make_async_remote_copy(src_ref=src, dst_ref=dst, send_sem=send_sems.at[k], recv_sem=recv_sems.at[k],
                                        device_id=peer, device_id_type=MESH)


def _gather_shards(buf, chip):
    rows, cols = buf.shape
    half = rows // 2
    n = len(CHIP_FLIPS)

    def body(src_ref, out_ref, send_sems, recv_sems):
        me, chip = _my_place()
        sems = (send_sems, recv_sems)
        sibling = _flipped(me, FLIP_C)
        mine = pl.ds(pl.multiple_of(me[2] * half, half), half)
        theirs = pl.ds(pl.multiple_of((1 - me[2]) * half, half), half)
        peers = [_flipped(me, f) for f in CHIP_FLIPS]
        from_chip = [2 * p[0] + p[1] for p in peers]
        over_ici = [_remote(src_ref.at[mine], out_ref.at[chip, mine], sems, k, peers[k]) for k in range(n)]
        for cp in over_ici:
            cp.start()
        passed = [_remote(out_ref.at[from_chip[k], mine], out_ref.at[from_chip[k], mine], sems, n + k, sibling)
                  for k in range(n)]
        for k in range(n):
            _remote(src_ref.at[mine], out_ref.at[from_chip[k], mine], sems, k, peers[k]).wait_recv()
            passed[k].start()
        for k in range(n):
            _remote(out_ref.at[from_chip[k], theirs], out_ref.at[from_chip[k], theirs], sems, n + k, sibling).wait_recv()
        for cp in over_ici + passed:
            cp.wait_send()

    others = pl.pallas_call(
        body, name="gather_shards", in_specs=[pl.BlockSpec(memory_space=pl.ANY)],
        out_specs=pl.BlockSpec(memory_space=pl.ANY), out_shape=jax.ShapeDtypeStruct((4, rows, cols), buf.dtype),
        scratch_shapes=[pltpu.SemaphoreType.DMA((2 * n,)), pltpu.SemaphoreType.DMA((2 * n,))],
    )(buf)
    return lax.dynamic_update_slice(others, buf[None], (chip, 0, 0))


def _swap_other_halves(portions):
    _, rows, cols = portions.shape
    half = rows // 2

    def body(src_ref, dst_ref, send_sems, recv_sems):
        me, _ = _my_place()
        theirs = pl.ds(pl.multiple_of((1 - me[2]) * half, half), half)
        cp = _remote(src_ref.at[:, theirs], dst_ref, (send_sems, recv_sems), 0, _flipped(me, FLIP_C))
        cp.start()
        cp.wait_recv()
        cp.wait_send()

    return pl.pallas_call(
        body, name="reduce_d2d", in_specs=[pl.BlockSpec(memory_space=pl.ANY)],
        out_specs=pl.BlockSpec(memory_space=pl.ANY), out_shape=jax.ShapeDtypeStruct((4, half, cols), portions.dtype),
        scratch_shapes=[pltpu.SemaphoreType.DMA((1,)), pltpu.SemaphoreType.DMA((1,))],
    )(portions)


def _join_halves(total, core):
    half, cols = total.shape

    def body(src_ref, out_ref, send_sems, recv_sems):
        me, _ = _my_place()
        mine = pl.ds(pl.multiple_of(me[2] * half, half), half)
        theirs = pl.ds(pl.multiple_of((1 - me[2]) * half, half), half)
        sems = (send_sems, recv_sems)
        cp = _remote(src_ref, out_ref.at[mine], sems, 0, _flipped(me, FLIP_C))
        cp.start()
        _remote(src_ref, out_ref.at[theirs], sems, 0, _flipped(me, FLIP_C)).wait_recv()
        cp.wait_send()

    from_sibling = pl.pallas_call(
        body, name="reduce_share", in_specs=[pl.BlockSpec(memory_space=pl.ANY)],
        out_specs=pl.BlockSpec(memory_space=pl.ANY), out_shape=jax.ShapeDtypeStruct((2 * half, cols), total.dtype),
        scratch_shapes=[pltpu.SemaphoreType.DMA((1,)), pltpu.SemaphoreType.DMA((1,))],
    )(total)
    return lax.dynamic_update_slice(from_sibling, total, (core * half, 0))


def _reduce_scatter(portions, chip, core):
    half = portions.shape[1] // 2
    keep = lax.dynamic_slice_in_dim(portions, core * half, half, axis=1)
    got = _swap_other_halves(portions)
    pair = _ew(lambda p, q: (p + q,), [keep.reshape(4 * half, -1), got.reshape(4 * half, -1)],
               [(COMM_COLS, F32)], "reduce_pair_sum")[0].reshape(keep.shape)
    out = jnp.stack([_pick(pair, jnp.bitwise_xor(chip, f)) for f in (1, 2, 3)]).astype(BF16)
    others = _exchange(out, CHIP_FLIPS, (0, 1, 2), "reduce_ici")
    total = _ew(lambda p, q, r, s: (p + q.astype(F32) + r.astype(F32) + s.astype(F32),),
                [_pick(pair, chip), others[0], others[1], others[2]],
                [(COMM_COLS, F32)], "reduce_chip_sum")[0]
    return _join_halves(total, core)


def _all_reduce_small(buf, dev):
    got = _exchange(buf[None], ALL_FLIPS, (0,) * 7, "small_gather")
    by_flip = jnp.concatenate([buf[None], got])
    ordered = [_pick(by_flip, jnp.bitwise_xor(dev, a)) for a in range(8)]

    def fn(*t):
        s = t[0]
        for u in t[1:]:
            s = s + u
        return (s,)

    return _ew(fn, ordered, [(buf.shape[1], F32)], "small_sum")[0]


def _band_bucket_onehot(dilation, max_dist):
    i = jnp.arange(BLOCK)[:, None]
    j = jnp.arange(2 * BLOCK)[None, :]
    dist = i + BLOCK - j
    inband = (dist >= 0) & (dist <= max_dist)
    n = jnp.maximum(dist, 0) * dilation
    max_exact = NUM_BUCKETS // 2
    nf = jnp.maximum(n, 1).astype(F32)
    large = max_exact + (jnp.log(nf / max_exact) / math.log(MAX_DISTANCE / max_exact)
                         * (NUM_BUCKETS - max_exact)).astype(jnp.int32)
    bucket = jnp.where(n < max_exact, n, jnp.minimum(large, NUM_BUCKETS - 1))
    onehot = (bucket[..., None] == jnp.arange(NUM_BUCKETS)) & inband[..., None]
    return onehot.reshape(-1, NUM_BUCKETS).astype(F32), inband.reshape(-1)


def _band_bias(table, onehot, inband):
    vals = jnp.einsum("pb,bh->hp", onehot, table, precision=lax.Precision.HIGHEST)
    return jnp.where(inband[None, :], vals, NEG).reshape(-1, BLOCK, 2 * BLOCK)


BAND_VARIANTS = ((1, A_WINDOW - 1),) + tuple((dil, window // dil) for window, dil in B_BRANCHES)


LAYOUT_TILE_BYTES = 4 << 20


def _layout_rows(length, row_bytes):
    return _divisor_tile(length, max(16, LAYOUT_TILE_BYTES // row_bytes), 16)


def _split_heads(items, w, dil, name):
    s = items[0][0].shape[0]
    length = s // dil
    row_bytes = sum(-(-n * w // LANES) * LANES * (x.dtype.itemsize + jnp.dtype(d).itemsize) for x, _, n, d in items)
    tr = _layout_rows(length, row_bytes)
    nt = length // tr

    def body(*refs):
        for (x_ref, o_ref), (_, _, n, _) in zip(zip(refs[:len(items)], refs[len(items):]), items):
            for j in range(n):
                o_ref[j] = x_ref[:, j * w:(j + 1) * w].astype(o_ref.dtype)

    in_specs, out_specs, out_shape, views = [], [], [], []
    for x, first, n, d in items:
        bw, width = n * w, x.shape[1]
        assert bw % LANES == 0 and first % bw == 0 and (dil == 1 or width % bw == 0)
        in_specs.append(pl.BlockSpec((tr, bw), lambda r, i, c0=first // bw, wb=width // bw: (i, r * wb + c0)))
        out_specs.append(pl.BlockSpec((n, tr, w), lambda r, i: (0, r * nt + i, 0)))
        out_shape.append(jax.ShapeDtypeStruct((n, s, w), d))
        views.append(x.reshape(length, dil * width))
    return pl.pallas_call(
        body, name=name, grid=(dil, nt), in_specs=in_specs, out_specs=out_specs, out_shape=out_shape,
        compiler_params=_params(("parallel", "parallel")),
    )(*views)


def _merge_heads(items, dil, name, group_sum=1):
    s, w = items[0][0].shape[1:]
    length = s // dil
    row_bytes = sum(t.shape[0] * LANES * t.dtype.itemsize + t.shape[0] * w * jnp.dtype(d).itemsize for t, d in items)
    tr = _layout_rows(length, row_bytes)
    nt = length // tr

    def body(*refs):
        for t_ref, o_ref in zip(refs[:len(items)], refs[len(items):]):
            for j in range(t_ref.shape[0] // group_sum):
                v = t_ref[j * group_sum]
                for g in range(1, group_sum):
                    v = v + t_ref[j * group_sum + g]
                o_ref[:, j * w:(j + 1) * w] = v.astype(o_ref.dtype)

    in_specs, out_specs, out_shape = [], [], []
    for t, d in items:
        n = t.shape[0]
        bw = n // group_sum * w
        assert bw % LANES == 0
        in_specs.append(pl.BlockSpec((n, tr, w), lambda r, i: (0, r * nt + i, 0)))
        out_specs.append(pl.BlockSpec((tr, bw), lambda r, i: (i, r)))
        out_shape.append(jax.ShapeDtypeStruct((length, dil * bw), d))
    outs = pl.pallas_call(
        body, name=name, grid=(dil, nt), in_specs=in_specs, out_specs=out_specs, out_shape=out_shape,
        compiler_params=_params(("parallel", "parallel")),
    )(*[t for t, _ in items])
    return [o.reshape(s, -1) for o in outs]


def _merge_heads_transposed(t, dtype, name, also_heads=False):
    n, w, s = t.shape
    tr = _layout_rows(s, 3 * n * w * t.dtype.itemsize)

    def body(t_ref, o_ref, *heads_ref):
        rows = jnp.transpose(t_ref[...].reshape(n * w, tr))
        o_ref[...] = rows.astype(o_ref.dtype)
        for j in range(n if also_heads else 0):
            heads_ref[0][j] = rows[:, j * w:(j + 1) * w]

    out_specs = [pl.BlockSpec((tr, n * w), lambda i: (i, 0))]
    out_shape = [jax.ShapeDtypeStruct((s, n * w), dtype)]
    if also_heads:
        out_specs.append(pl.BlockSpec((n, tr, w), lambda i: (0, i, 0)))
        out_shape.append(jax.ShapeDtypeStruct((n, s, w), t.dtype))
    outs = pl.pallas_call(
        body, name=name, grid=(s // tr,), in_specs=[pl.BlockSpec((n, w, tr), lambda i: (0, 0, i))],
        out_specs=out_specs, out_shape=out_shape, compiler_params=_params(("parallel",)),
    )(t)
    return outs if also_heads else outs[0]


ROPE_HALF = C_ROPE // 2
ROPE_PERIOD = 3 * LANES


def _rope_tables(s):
    inv = ROPE_THETA ** (-jnp.arange(0, C_ROPE, 2, dtype=F32) / C_ROPE)
    ang = jnp.arange(s, dtype=F32)[:, None] * inv[None, :]
    cos, sin = jnp.cos(ang), jnp.sin(ang)
    one, zero = jnp.ones((s, C_NOPE), F32), jnp.zeros((s, C_NOPE), F32)
    z16 = jnp.zeros((s, ROPE_HALF), F32)
    reps = ROPE_PERIOD // C_QK
    keep = jnp.tile(jnp.concatenate([one, cos, cos], axis=1), (1, reps))
    from_above = jnp.tile(jnp.concatenate([zero, -sin, z16], axis=1), (1, reps))
    from_below = jnp.tile(jnp.concatenate([zero, z16, sin], axis=1), (1, reps))
    return (cos, sin), (keep, from_above, from_below)


def _rope_rows(x, tables, inverse, name, dtype):
    width = x.shape[1]
    reps = width // ROPE_PERIOD
    sign = -1.0 if inverse else 1.0

    def fn(xv, keep, above, below):
        keep, above, below = (jnp.tile(t, (1, reps)) for t in (keep, above, below))
        up = pltpu.roll(xv, width - ROPE_HALF, 1)
        down = pltpu.roll(xv, ROPE_HALF, 1)
        return (xv * keep + sign * (up * above + down * below),)

    return _ew(fn, [x, *tables], [(width, dtype)], name)[0]


def _rotate_half_pairs(a, b, cos, sin, inverse):
    if inverse:
        return a * cos + b * sin, b * cos - a * sin
    return a * cos - b * sin, a * sin + b * cos


def _split_kv(kv, down, cos, sin, name):
    s = kv.shape[0]
    h = kv.shape[1] // (C_NOPE + C_V)
    tr = _layout_rows(s, 8 * kv.shape[1])
    r0 = C_Q_RANK + C_KV_RANK

    def body(kv_ref, down_ref, cos_ref, sin_ref, k_ref, v_ref, vt_ref):
        k1, k2 = _rotate_half_pairs(down_ref[:, r0:r0 + ROPE_HALF], down_ref[:, r0 + ROPE_HALF:r0 + C_ROPE],
                                    cos_ref[...], sin_ref[...], False)
        lane = lax.broadcasted_iota(jnp.int32, (tr, LANES - C_V), 1)
        tail = jnp.where(lane == 0, 1.0, 0.0)
        for j in range(h):
            base = j * (C_NOPE + C_V)
            k_ref[j, :, :C_NOPE] = kv_ref[:, base:base + C_NOPE]
            k_ref[j, :, C_NOPE:C_NOPE + ROPE_HALF] = k1.astype(BF16)
            k_ref[j, :, C_NOPE + ROPE_HALF:] = k2.astype(BF16)
            v1 = jnp.concatenate([kv_ref[:, base + C_NOPE:base + C_NOPE + C_V].astype(F32), tail], axis=1)
            v_ref[j] = v1.astype(BF16)
            vt_ref[j] = jnp.transpose(v1).astype(BF16)

    def rows(width):
        return pl.BlockSpec((tr, width), lambda i: (i, 0))

    return pl.pallas_call(
        body, name=name, grid=(s // tr,),
        in_specs=[rows(kv.shape[1]), rows(down.shape[1]), rows(ROPE_HALF), rows(ROPE_HALF)],
        out_specs=[pl.BlockSpec((h, tr, C_QK), lambda i: (0, i, 0)), pl.BlockSpec((h, tr, LANES), lambda i: (0, i, 0)),
                   pl.BlockSpec((h, LANES, tr), lambda i: (0, 0, i))],
        out_shape=[jax.ShapeDtypeStruct((h, s, C_QK), BF16), jax.ShapeDtypeStruct((h, s, LANES), BF16),
                   jax.ShapeDtypeStruct((h, LANES, s), BF16)],
        compiler_params=_params(("parallel",)),
    )(kv, down, cos, sin)


def _merge_kv_bwd(dk, dv, cos, sin, name):
    h, s, _ = dk.shape
    tr = _layout_rows(s, 8 * h * LANES)

    def body(dk_ref, dv_ref, cos_ref, sin_ref, dkv_ref, dkr_ref):
        rot = dk_ref[0, :, C_NOPE:]
        for j in range(h):
            base = j * (C_NOPE + C_V)
            dkv_ref[:, base:base + C_NOPE] = dk_ref[j, :, :C_NOPE].astype(BF16)
            dkv_ref[:, base + C_NOPE:base + C_NOPE + C_V] = dv_ref[j].astype(BF16)
            if j:
                rot = rot + dk_ref[j, :, C_NOPE:]
        d1, d2 = _rotate_half_pairs(rot[:, :ROPE_HALF], rot[:, ROPE_HALF:], cos_ref[...], sin_ref[...], True)
        dkr_ref[:, :ROPE_HALF] = d1
        dkr_ref[:, ROPE_HALF:] = d2

    def rows(width):
        return pl.BlockSpec((tr, width), lambda i: (i, 0))

    return pl.pallas_call(
        body, name=name, grid=(s // tr,),
        in_specs=[pl.BlockSpec((h, tr, C_QK), lambda i: (0, i, 0)), pl.BlockSpec((h, tr, C_V), lambda i: (0, i, 0)),
                  rows(ROPE_HALF), rows(ROPE_HALF)],
        out_specs=[rows(h * (C_NOPE + C_V)), rows(C_ROPE)],
        out_shape=[jax.ShapeDtypeStruct((s, h * (C_NOPE + C_V)), BF16), jax.ShapeDtypeStruct((s, C_ROPE), F32)],
        compiler_params=_params(("parallel",)),
    )(dk, dv, cos, sin)


SPAN_FWD, SPAN_BWD = 128, 256


def _even_fwd(xn, h, w_in, w_out, sinks_row, biases, tag):
    s = xn.shape[0]
    proj = _mm(xn, w_in, "nn", [BF16], f"in_proj{tag}")[0]
    qd, kd = A_Q_HEADS * HEAD_DIM, A_KV_HEADS * HEAD_DIM
    qa, ka, va = _split_heads([(proj, 0, A_Q_HEADS, BF16), (proj, qd, A_KV_HEADS, BF16),
                               (proj, qd + kd, A_KV_HEADS, BF16)], HEAD_DIM, 1, f"swa_split{tag}")
    oa, lse_a = _band_fwd(qa, ka, va, biases[0], s // BLOCK, SPAN_FWD, f"swa_fwd{tag}")
    oa, lse_a = _apply_sinks(oa, lse_a, sinks_row, f"swa_sinks{tag}")
    out_a = _merge_heads([(oa, BF16)], 1, f"swa_merge{tag}")[0]
    width = B_HEADS_PER_BRANCH * HEAD_DIM
    qkv_b, outs, lses = [], [], []
    for g, (_, dil) in enumerate(B_BRANCHES):
        base = A_IN + g * 3 * width
        src, base = (proj, base) if dil == 1 else (proj[:, base:base + 3 * width], 0)
        qkv = _split_heads([(src, base + i * width, B_HEADS_PER_BRANCH, BF16) for i in range(3)], HEAD_DIM, dil,
                           f"dil{g}_split{tag}")
        og, lg = _band_fwd(*qkv, biases[1 + g], s // dil // BLOCK, SPAN_FWD, f"dil{g}_fwd{tag}")
        qkv_b.append(qkv)
        merged = _merge_heads([(og, F32), (lg, F32)], dil, f"dil{g}_merge{tag}")
        outs.append(merged[0])
        lses.append(merged[1])

    def merge(o0, o1, o2, l0, l1, l2):
        m = jnp.maximum(jnp.maximum(l0, l1), l2)
        e0, e1, e2 = jnp.exp(l0 - m), jnp.exp(l1 - m), jnp.exp(l2 - m)
        den = e0 + e1 + e2
        out = (e0 * o0 + e1 * o1 + e2 * o2) / den
        return out, m + jnp.log(den), out

    out_b, lse_b, out_b16 = _ew(merge, outs + lses, [(width, F32), (width, F32), (width, BF16)], f"dil_merge{tag}")
    cat = jnp.concatenate([out_a, out_b16], axis=1)
    h_mid = _mm(cat, w_out, "nn", [F32], f"out_proj{tag}", epi=_add_epi, extras=(h,))[0]
    return h_mid, (qa, ka, va, oa, lse_a, qkv_b, out_b, lse_b, cat)


def _even_bwd(dh, xn, saved, w_in, w_out, sinks_row, biases, tag):
    qa, ka, va, oa, lse_a, qkv_b, out_b, lse_b, cat = saved
    s = xn.shape[0]
    qd = A_Q_HEADS * HEAD_DIM
    g_w_out = _mm(cat, dh, "tn", [F32], f"out_proj_dw{tag}")[0]
    dcat = _mm(dh, w_out, "nt", [BF16], f"out_proj_dx{tag}")[0]
    do_a = _split_heads([(dcat, 0, A_Q_HEADS, BF16)], HEAD_DIM, 1, f"swa_do_split{tag}")[0]
    dqa, dka8, dva8, dbias_a, dsink = _band_bwd(qa, ka, va, oa, lse_a, do_a, biases[0], sinks_row, s // BLOCK,
                                                SPAN_BWD, f"swa_bwd{tag}")
    pieces = _merge_heads([(dqa, BF16)], 1, f"swa_dq_merge{tag}")
    pieces += _merge_heads([(dka8, BF16), (dva8, BF16)], 1, f"swa_dkv_merge{tag}", group_sum=A_Q_HEADS // A_KV_HEADS)
    dbias_b = []
    dcat_b = dcat[:, qd:]
    for g, (_, dil) in enumerate(B_BRANCHES):
        do_g, out_g, lse_g = _split_heads([(dcat_b, 0, B_HEADS_PER_BRANCH, BF16), (out_b, 0, B_HEADS_PER_BRANCH, F32),
                                           (lse_b, 0, B_HEADS_PER_BRANCH, F32)], HEAD_DIM, dil, f"dil{g}_do_split{tag}")
        dqg, dkg, dvg, dbg, _ = _band_bwd(*qkv_b[g], out_g, lse_g, do_g, biases[1 + g], None, s // dil // BLOCK,
                                          SPAN_BWD, f"dil{g}_bwd{tag}")
        pieces += _merge_heads([(dqg, BF16), (dkg, BF16), (dvg, BF16)], dil, f"dil{g}_dqkv_merge{tag}")
        dbias_b.append(dbg)
    dproj = jnp.concatenate(pieces, axis=1)
    g_w_in = _mm(xn, dproj, "tn", [F32], f"in_proj_dw{tag}")[0]
    dxn = _mm(dproj, w_in, "nt", [F32], f"in_proj_dx{tag}")[0]
    dbias = jnp.concatenate([db.reshape(db.shape[0], -1) for db in [dbias_a] + dbias_b])
    return dxn, g_w_in, g_w_out, dbias, dsink[:, 0, 0]


def _mla_layer_fwd(xn, h, w_dn, q_norm, w_uq, kv_norm, w_ukv, w_o, ropes, tag):
    (cos, sin), q_tables = ropes
    down = _mm(xn, w_dn, "nn", [F32], f"mla_down{tag}")[0]
    c_q, c_kv = down[:, :C_Q_RANK], down[:, C_Q_RANK:C_Q_RANK + C_KV_RANK]
    cqn = _rms_fwd(c_q, q_norm, f"mla_qnorm{tag}")
    ckvn = _rms_fwd(c_kv, kv_norm, f"mla_kvnorm{tag}")
    q = _mm(cqn, w_uq, "nn", [F32], f"mla_uq{tag}")[0]
    kv = _mm(ckvn, w_ukv, "nn", [BF16], f"mla_ukv{tag}")[0]
    qh = _split_heads([(_rope_rows(q, q_tables, False, f"mla_rope_q{tag}", BF16), 0, C_HEADS, BF16)], C_QK, 1,
                      f"mla_q_split{tag}")[0]
    kh, v1h, v1t = _split_kv(kv, down, cos, sin, f"mla_kv_split{tag}")
    ot, lse = _mla_fwd(qh, kh, v1t, f"mla_attn_fwd{tag}")
    o2d, o = _merge_heads_transposed(ot, BF16, f"mla_o_merge{tag}", also_heads=True)
    h_mid = _mm(o2d, w_o, "nn", [F32], f"mla_o{tag}", epi=_add_epi, extras=(h,))[0]
    return h_mid, (c_q, c_kv, cqn, ckvn, qh, kh, v1h, o, lse, o2d)


def _mla_layer_bwd(dh, xn, saved, w_dn, q_norm, w_uq, kv_norm, w_ukv, w_o, ropes, tag):
    c_q, c_kv, cqn, ckvn, qh, kh, v1h, o, lse, o2d = saved
    (cos, sin), q_tables = ropes
    g_w_o = _mm(o2d, dh, "tn", [F32], f"mla_o_dw{tag}")[0]
    do2d = _mm(dh, w_o, "nt", [BF16], f"mla_o_dx{tag}")[0]
    do = _split_heads([(do2d, 0, C_HEADS, BF16)], C_V, 1, f"mla_do_split{tag}")[0]
    dqt, dk, dv = _mla_bwd(qh, kh, v1h, o, lse, do, f"mla_attn_bwd{tag}")
    dq_roped = _merge_heads_transposed(dqt, F32, f"mla_dq_merge{tag}")
    dq = _rope_rows(dq_roped, q_tables, True, f"mla_rope_q_bwd{tag}", BF16)
    dkv, dk_rope = _merge_kv_bwd(dk, dv, cos, sin, f"mla_dkv_merge{tag}")
    g_w_uq = _mm(cqn, dq, "tn", [F32], f"mla_uq_dw{tag}")[0]
    dcqn = _mm(dq, w_uq, "nt", [F32], f"mla_uq_dx{tag}")[0]
    g_w_ukv = _mm(ckvn, dkv, "tn", [F32], f"mla_ukv_dw{tag}")[0]
    dckvn = _mm(dkv, w_ukv, "nt", [F32], f"mla_ukv_dx{tag}")[0]
    dc_q, g_q_norm = _rms_bwd(c_q, q_norm, dcqn, None, f"mla_qnorm_bwd{tag}")
    dc_kv, g_kv_norm = _rms_bwd(c_kv, kv_norm, dckvn, None, f"mla_kvnorm_bwd{tag}")
    ddown = jnp.concatenate([dc_q, dc_kv, dk_rope], axis=1).astype(BF16)
    g_w_dn = _mm(xn, ddown, "tn", [F32], f"mla_down_dw{tag}")[0]
    dxn = _mm(ddown, w_dn, "nt", [F32], f"mla_down_dx{tag}")[0]
    return dxn, g_w_dn, g_q_norm, g_w_uq, g_kv_norm, g_w_ukv, g_w_o


SHARDED = (("w_in_ab", 2), ("w_out_ab", 2), ("w_down_c", 1), ("w_uq_c", 2), ("w_ukv_c", 2), ("w_o_c", 2),
           ("w_mlp_up", 2), ("w_mlp_down", 1))
SHARDED_NORMS = ("q_norm_c", "kv_norm_c")


def kernel(x, rel_bias, attn_norm, mlp_norm, final_norm, w_in_ab, sinks, w_out_ab, w_down_c, q_norm_c, w_uq_c, kv_norm_c, w_ukv_c, w_o_c, w_mlp_up, w_mlp_down, loss_target, m_rel_bias, m_attn_norm, m_mlp_norm, m_final_norm, m_w_in_ab, m_sinks, m_w_out_ab, m_w_down_c, m_q_norm_c, m_w_uq_c, m_kv_norm_c, m_w_ukv_c, m_w_o_c, m_w_mlp_up, m_w_mlp_down, v_rel_bias, v_attn_norm, v_mlp_norm, v_final_norm, v_w_in_ab, v_sinks, v_w_out_ab, v_w_down_c, v_q_norm_c, v_w_uq_c, v_kv_norm_c, v_w_ukv_c, v_w_o_c, v_w_mlp_up, v_w_mlp_down):
    given = dict(locals())
    chip = lax.axis_index("x") * 2 + lax.axis_index("y")
    core = lax.axis_index("c")
    dev = chip * 2 + core
    depth = attn_norm.shape[0]
    s = x.shape[1]

    shards = [given[n] for n, _ in SHARDED]
    norm_shards = [given[n] for n in SHARDED_NORMS]
    packed = _to_comm_rows([t.astype(BF16) for t in shards]
                           + [lax.bitcast_convert_type(t, BF16) for t in norm_shards], BF16)
    by_chip = _gather_shards(packed, chip)
    shapes = [t.shape for t in shards] + [t.shape + (2,) for t in norm_shards]
    pieces = [_from_comm_rows(by_chip[a], shapes) for a in range(4)]
    full = {n: jnp.concatenate([pieces[a][i] for a in range(4)], axis=ax) for i, (n, ax) in enumerate(SHARDED)}
    for i, n in enumerate(SHARDED_NORMS):
        full[n] = jnp.concatenate([lax.bitcast_convert_type(pieces[a][len(SHARDED) + i], F32) for a in range(4)],
                                  axis=-1)

    onehots = [_band_bucket_onehot(dil, md) for dil, md in BAND_VARIANTS]
    head_cols = [(0, A_Q_HEADS)] + [(A_Q_HEADS + g * B_HEADS_PER_BRANCH, A_Q_HEADS + (g + 1) * B_HEADS_PER_BRANCH)
                                    for g in range(len(B_BRANCHES))]
    biases = [_band_bias(rel_bias[:, lo:hi], oh, inb) for (lo, hi), (oh, inb) in zip(head_cols, onehots)]
    ropes = _rope_tables(s)
    sink_rows = [jnp.broadcast_to(sinks[e][:, None, None], (A_Q_HEADS, BAND_SPAN_MAX, LANES))
                 for e in range(sinks.shape[0])]

    def odd_weights(o):
        return [full[n][o] for n in ("w_down_c", "q_norm_c", "w_uq_c", "kv_norm_c", "w_ukv_c", "w_o_c")]

    h = x[0]
    saved = []
    for l in range(depth):
        xn = _rms_fwd(h, attn_norm[l], f"attn_norm{l}")
        if l % 2 == 0:
            e = l // 2
            h_mid, mix = _even_fwd(xn, h, full["w_in_ab"][e], full["w_out_ab"][e], sink_rows[e], biases, f"_{l}")
        else:
            h_mid, mix = _mla_layer_fwd(xn, h, *odd_weights(l // 2), ropes, f"_{l}")
        xn2 = _rms_fwd(h_mid, mlp_norm[l], f"mlp_norm{l}")
        act, relu = _mm(xn2, full["w_mlp_up"][l], "nn", [BF16, BF16], f"mlp_up{l}", epi=_relu2_epi)
        h_out = _mm(act, full["w_mlp_down"][l], "nn", [F32], f"mlp_down{l}", epi=_add_epi, extras=(h_mid,))[0]
        saved.append((h, xn, mix, h_mid, xn2, act, relu))
        h = h_out
    loss_part, dh, dh16, g_final = _final_loss(h, final_norm, loss_target[0], "final_loss")

    grads = {n: [None] * given[n].shape[0] for n, _ in SHARDED}
    g_q_norm, g_kv_norm = [None] * q_norm_c.shape[0], [None] * kv_norm_c.shape[0]
    g_attn_norm, g_mlp_norm = [None] * depth, [None] * depth
    g_sinks = [None] * sinks.shape[0]
    g_band = []
    for l in range(depth - 1, -1, -1):
        h_in, xn, mix, h_mid, xn2, act, relu = saved[l]
        grads["w_mlp_down"][l] = _mm(act, dh16, "tn", [F32], f"mlp_down_dw{l}")[0]
        du = _mm(dh16, full["w_mlp_down"][l], "nt", [BF16], f"mlp_down_dx{l}", epi=_relu2_bwd_epi, extras=(relu,))[0]
        grads["w_mlp_up"][l] = _mm(xn2, du, "tn", [F32], f"mlp_up_dw{l}")[0]
        dxn2 = _mm(du, full["w_mlp_up"][l], "nt", [F32], f"mlp_up_dx{l}")[0]
        dh, dh16, g_mlp_norm[l] = _rms_bwd(h_mid, mlp_norm[l], dxn2, dh, f"mlp_norm_bwd{l}", also_bf16=True)
        if l % 2 == 0:
            e = l // 2
            dxn, grads["w_in_ab"][e], grads["w_out_ab"][e], g_band_e, g_sinks[e] = _even_bwd(
                dh16, xn, mix, full["w_in_ab"][e], full["w_out_ab"][e], sink_rows[e], biases, f"_{l}")
            g_band.append(g_band_e)
        else:
            o = l // 2
            (dxn, grads["w_down_c"][o], g_q_norm[o], grads["w_uq_c"][o], g_kv_norm[o], grads["w_ukv_c"][o],
             grads["w_o_c"][o]) = _mla_layer_bwd(dh16, xn, mix, *odd_weights(o), ropes, f"_{l}")
        if l:
            dh, dh16, g_attn_norm[l] = _rms_bwd(h_in, attn_norm[l], dxn, dh, f"attn_norm_bwd{l}", also_bf16=True)
        else:
            dh, g_attn_norm[l] = _rms_bwd(h_in, attn_norm[l], dxn, dh, f"attn_norm_bwd{l}")
    grad_x = dh[None]
    buckets = jnp.concatenate([oh for oh, _ in onehots], axis=1)
    by_variant = _mm(jnp.concatenate(g_band, axis=1), jnp.tile(buckets, (len(g_band), 1)), "nn", [F32],
                     "bias_buckets")[0]
    g_rel_bias = jnp.concatenate([by_variant[lo:hi, v * NUM_BUCKETS:(v + 1) * NUM_BUCKETS].T
                                  for v, (lo, hi) in enumerate(head_cols)], axis=1)

    portions = []
    for a in range(4):
        parts = []
        for n, ax in SHARDED:
            g = jnp.stack(grads[n])
            size = g.shape[ax] // 4
            parts.append(lax.slice_in_dim(g, a * size, (a + 1) * size, axis=ax))
        portions.append(_to_comm_rows(parts, F32))
    reduced = _from_comm_rows(_reduce_scatter(jnp.stack(portions), chip, core), [t.shape for t in shards])
    g_shard = {n: reduced[i] for i, (n, _) in enumerate(SHARDED)}

    small = [jnp.stack(g_attn_norm), jnp.stack(g_mlp_norm), g_final, g_rel_bias, jnp.stack(g_sinks),
             jnp.stack(g_q_norm), jnp.stack(g_kv_norm), loss_part.reshape(1)]
    small_shapes = [t.shape for t in small]
    summed = _from_comm_rows(_all_reduce_small(_to_comm_rows(small, F32)[:16], dev), small_shapes)
    loss = summed[7][0]
    g_small = dict(zip(("attn_norm", "mlp_norm", "final_norm", "rel_bias", "sinks"), summed[:5]))
    for n, g in zip(SHARDED_NORMS, summed[5:7]):
        size = g.shape[1] // 4
        g_shard[n] = lax.dynamic_slice_in_dim(g, chip * size, size, axis=1)

    order = ["rel_bias", "attn_norm", "mlp_norm", "final_norm", "w_in_ab", "sinks", "w_out_ab", "w_down_c",
             "q_norm_c", "w_uq_c", "kv_norm_c", "w_ukv_c", "w_o_c", "w_mlp_up", "w_mlp_down"]
    g_all = {**g_small, **g_shard}
    deltas, new_m, new_v = [], [], []
    for n in order:
        d, mn, vn = _adamw(given[n], g_all[n], given["m_" + n], given["v_" + n], f"adamw_{n}")
        deltas.append(d)
        new_m.append(mn)
        new_v.append(vn)
    return (loss, grad_x, *[g_all[n] for n in order], *deltas, *new_m, *new_v)
```

```python
import math

import jax
import jax.numpy as jnp
from jax import lax
from jax.experimental import pallas as pl
from jax.experimental.pallas import tpu as pltpu

F32 = jnp.float32
BF16 = jnp.bfloat16
MESH = pl.DeviceIdType.MESH
ALL_AXES = ("x", "y", "c")

EPS = 1e-6
NEG = -1e30
BLOCK = 128
HEAD_DIM = 64
A_Q_HEADS, A_KV_HEADS = 8, 2
A_WINDOW = 128
B_BRANCHES = ((128, 1), (512, 4), (2048, 16))
B_HEADS_PER_BRANCH = 4
NUM_BUCKETS, MAX_DISTANCE = 32, 2048
A_IN = (A_Q_HEADS + 2 * A_KV_HEADS) * HEAD_DIM
C_HEADS, C_NOPE, C_ROPE, C_V = 8, 64, 32, 64
C_Q_RANK, C_KV_RANK = 384, 256
ROPE_THETA = 10000.0
ADAM_LR, ADAM_B1, ADAM_B2, ADAM_EPS, ADAM_WD, ADAM_STEP = 0.001, 0.9, 0.999, 1e-08, 0.01, 10

V7X_VMEM_LIMIT_BYTES = 56 * 1024 * 1024
LANES = 128
COMM_COLS = 1024
COMM_ROW_ALIGN = 1024

NT = (((1,), (1,)), ((), ()))
NN = (((1,), (0,)), ((), ()))
TN = (((0,), (0,)), ((), ()))
DIMS = {"nn": NN, "nt": NT, "tn": TN}


def _params(sem):
    return pltpu.CompilerParams(dimension_semantics=sem, vmem_limit_bytes=V7X_VMEM_LIMIT_BYTES)


def _divisor_tile(n, limit, align):
    if n <= limit:
        return n
    t = (limit // align) * align
    while t >= align:
        if n % t == 0:
            return t
        t -= align
    return n


def _ew(fn, ins, outs, name, acc_outs=(), target_bytes=6 << 20):
    rows = max(a.shape[0] for a in ins)

    def vmem_row_bytes(cols, dtype):
        return -(-cols // LANES) * LANES * jnp.dtype(dtype).itemsize

    per_row = sum(vmem_row_bytes(a.shape[1], a.dtype) for a in ins if a.shape[0] == rows)
    per_row += sum(vmem_row_bytes(c, d) for c, d in outs)
    tr = _divisor_tile(rows, max(16, target_bytes // max(per_row, 1)), 16)
    n_in, n_out = len(ins), len(outs)

    def body(*refs):
        res = fn(*[r[...] for r in refs[:n_in]])
        for r, v in zip(refs[n_in:n_in + n_out], res[:n_out]):
            r[...] = v.astype(r.dtype)
        if acc_outs:
            acc_refs = refs[n_in + n_out:]

            @pl.when(pl.program_id(0) == 0)
            def _():
                for r in acc_refs:
                    r[...] = jnp.zeros_like(r)

            for r, v in zip(acc_refs, res[n_out:]):
                r[...] += v

    def spec(a):
        if a.shape[0] == rows:
            return pl.BlockSpec((tr, a.shape[1]), lambda i: (i, 0))
        return pl.BlockSpec((1, a.shape[1]), lambda i: (0, 0))

    out_shape = [jax.ShapeDtypeStruct((rows, c), d) for c, d in outs]
    out_shape += [jax.ShapeDtypeStruct((1, c), F32) for c in acc_outs]
    out_specs = [pl.BlockSpec((tr, c), lambda i: (i, 0)) for c, _ in outs]
    out_specs += [pl.BlockSpec((1, c), lambda i: (0, 0)) for c in acc_outs]
    return pl.pallas_call(
        body, name=name, grid=(rows // tr,), in_specs=[spec(a) for a in ins], out_specs=out_specs,
        out_shape=out_shape, compiler_params=_params(("arbitrary",)),
    )(*ins)


def _rms_fwd(x, g, name):
    def fn(xv, gv):
        return ((xv * lax.rsqrt(jnp.mean(xv * xv, axis=-1, keepdims=True) + EPS)) * gv,)

    return _ew(fn, [x, g.reshape(1, -1)], [(x.shape[1], BF16)], name)[0]


def _rms_bwd(x, g, dy, add, name, also_bf16=False):
    def fn(xv, gv, dyv, *rest):
        rstd = lax.rsqrt(jnp.mean(xv * xv, axis=-1, keepdims=True) + EPS)
        xh = xv * rstd
        dyg = dyv.astype(F32) * gv
        dx = rstd * (dyg - xh * jnp.mean(dyg * xh, axis=-1, keepdims=True))
        if rest:
            dx = dx + rest[0]
        return (dx,) * (2 if also_bf16 else 1) + (jnp.sum(dyv.astype(F32) * xh, axis=0, keepdims=True),)

    ins = [x, g.reshape(1, -1), dy] + ([] if add is None else [add])
    outs = [(x.shape[1], F32)] + ([(x.shape[1], BF16)] if also_bf16 else [])
    *dx, dg = _ew(fn, ins, outs, name, acc_outs=(x.shape[1],))
    return (*dx, dg[0])


def _final_loss(h, g, target, name):
    d = h.shape[1]

    def fn(xv, gv, tv):
        rstd = lax.rsqrt(jnp.mean(xv * xv, axis=-1, keepdims=True) + EPS)
        xh = xv * rstd
        err = xh * gv - tv
        part = 0.5 * jnp.sum(jnp.mean(err * err, axis=-1, keepdims=True), axis=0, keepdims=True)
        dy = err * (1.0 / d)
        dyg = dy * gv
        dx = rstd * (dyg - xh * jnp.mean(dyg * xh, axis=-1, keepdims=True))
        return dx, dx, jnp.broadcast_to(part, (1, LANES)), jnp.sum(dy * xh, axis=0, keepdims=True)

    dx, dx16, loss, dg = _ew(fn, [h, g.reshape(1, -1), target], [(d, F32), (d, BF16)], name, acc_outs=(LANES, d))
    return loss[0, 0], dx, dx16, dg[0]


def _adamw(w, g, m, v, name):
    def fn(wv, gv, mv, vv):
        mn = ADAM_B1 * mv + (1.0 - ADAM_B1) * gv
        vn = ADAM_B2 * vv + (1.0 - ADAM_B2) * jnp.square(gv)
        m_hat = mn / (1.0 - ADAM_B1 ** ADAM_STEP)
        v_hat = vn / (1.0 - ADAM_B2 ** ADAM_STEP)
        return -ADAM_LR * (m_hat / (jnp.sqrt(v_hat) + ADAM_EPS) + ADAM_WD * wv), mn, vn

    shape = w.shape
    cols = shape[-1] if w.ndim > 1 else w.size
    view = [t.reshape(-1, cols) for t in (w, g, m, v)]
    return [t.reshape(shape) for t in _ew(fn, view, [(cols, F32)] * 3, name)]


MM_VMEM_BUDGET_BYTES = 40 << 20


def _mm(a, b, dims, outs, name, epi=None, extras=(), tm=2048, tn=1024, tk=1024):
    if dims == "nn":
        (m, k), n = a.shape, b.shape[1]
    elif dims == "nt":
        (m, k), n = a.shape, b.shape[0]
    else:
        (k, m), n = a.shape, b.shape[1]
    tn, tk = _divisor_tile(n, tn, LANES), _divisor_tile(k, tk, LANES)

    def tile_bytes(rows):
        per_out = sum(jnp.dtype(d).itemsize for d in outs) + sum(e.dtype.itemsize for e in extras)
        return 2 * (rows * tk * a.dtype.itemsize + tk * tn * b.dtype.itemsize + rows * tn * per_out) + 4 * rows * tn

    tm = _divisor_tile(m, tm, LANES)
    while tile_bytes(tm) > MM_VMEM_BUDGET_BYTES and tm % (2 * LANES) == 0:
        tm //= 2
    nk = k // tk
    n_ex, n_out = len(extras), len(outs)

    def body(a_ref, b_ref, *rest):
        ex_refs, out_refs = rest[:n_ex], rest[n_ex:n_ex + n_out]

        def finish(acc):
            res = epi(acc, *[r[...] for r in ex_refs]) if epi else (acc,)
            for r, v in zip(out_refs, res):
                r[...] = v.astype(r.dtype)

        part = lax.dot_general(a_ref[...].astype(BF16), b_ref[...].astype(BF16), DIMS[dims],
                               preferred_element_type=F32)
        if nk == 1:
            finish(part)
        else:
            acc_ref = rest[-1]
            kk = pl.program_id(2)

            @pl.when(kk == 0)
            def _():
                acc_ref[...] = part

            @pl.when(kk > 0)
            def _():
                acc_ref[...] += part

            @pl.when(kk == nk - 1)
            def _():
                finish(acc_ref[...])

    if dims == "nn":
        a_spec = pl.BlockSpec((tm, tk), lambda i, j, kk: (i, kk))
        b_spec = pl.BlockSpec((tk, tn), lambda i, j, kk: (kk, j))
    elif dims == "nt":
        a_spec = pl.BlockSpec((tm, tk), lambda i, j, kk: (i, kk))
        b_spec = pl.BlockSpec((tn, tk), lambda i, j, kk: (j, kk))
    else:
        a_spec = pl.BlockSpec((tk, tm), lambda i, j, kk: (kk, i))
        b_spec = pl.BlockSpec((tk, tn), lambda i, j, kk: (kk, j))
    tile = pl.BlockSpec((tm, tn), lambda i, j, kk: (i, j))
    return pl.pallas_call(
        body, name=name, grid=(m // tm, n // tn, nk),
        in_specs=[a_spec, b_spec] + [tile] * n_ex, out_specs=[tile] * n_out,
        out_shape=[jax.ShapeDtypeStruct((m, n), d) for d in outs],
        scratch_shapes=[pltpu.VMEM((tm, tn), F32)] if nk > 1 else [],
        compiler_params=_params(("parallel", "parallel", "arbitrary")),
    )(a, b, *extras)


def _add_epi(acc, res):
    return (acc + res,)


def _relu2_epi(acc):
    r = jnp.maximum(acc, 0.0)
    return r * r, r


def _relu2_bwd_epi(acc, r):
    return (acc * (2.0 * r.astype(F32)),)


BAND_SPAN_MAX = 256


def _band_geometry(t, blocks_per_seq, span):
    rows = min(1024, blocks_per_seq * BLOCK)
    nb = rows // BLOCK
    assert blocks_per_seq % nb == 0 and t % rows == 0 and span <= BAND_SPAN_MAX
    return rows, nb, t // rows, min(span, rows)


def _span_bias(bias, span):
    n = span // BLOCK
    neg = jnp.full(bias.shape[:2] + (BLOCK,), NEG, F32)
    rows = [jnp.concatenate([neg] * a + [bias[:, :, :BLOCK], bias[:, :, BLOCK:]] + [neg] * (n - 1 - a), axis=2)
            for a in range(n)]
    return jnp.concatenate(rows, axis=1)


def _fold_span_bias_grad(dbias, span):
    n = span // BLOCK
    parts = [dbias[:, a * BLOCK:(a + 1) * BLOCK, a * BLOCK:(a + 2) * BLOCK] for a in range(n)]
    return sum(parts[1:], parts[0])


def _band_logits(qj, kk, bias, first):
    s = lax.dot_general(qj, kk, NT, preferred_element_type=F32) * (HEAD_DIM ** -0.5) + bias
    if first is not None:
        col = lax.broadcasted_iota(jnp.int32, s.shape, 1)
        s = jnp.where(col < jnp.where(first, BLOCK, 0), NEG, s)
    return s


def _band_fwd(q, k, v, bias, blocks_per_seq, span, name):
    hq, t, dh = q.shape
    group = hq // k.shape[0]
    rows, nb, nchunks, span = _band_geometry(t, blocks_per_seq, span)

    def body(q_ref, kc_ref, kp_ref, vc_ref, vp_ref, bias_ref, o_ref, lse_ref):
        i = pl.program_id(1)
        bias_v = bias_ref[0]
        for j in range(rows // span):
            cur = slice(j * span, (j + 1) * span)
            prev = slice(j * span - BLOCK, j * span)
            kk = jnp.concatenate([kp_ref[0] if j == 0 else kc_ref[0, prev, :], kc_ref[0, cur, :]], axis=0)
            vv = jnp.concatenate([vp_ref[0] if j == 0 else vc_ref[0, prev, :], vc_ref[0, cur, :]], axis=0)
            first = lax.rem(i * nb, blocks_per_seq) == 0 if j == 0 else None
            s = _band_logits(q_ref[0, cur, :], kk, bias_v, first)
            m = jnp.max(s, axis=1, keepdims=True)
            p = jnp.exp(s - m)
            l = jnp.sum(p, axis=1, keepdims=True)
            acc = jnp.dot(p.astype(BF16), vv, preferred_element_type=F32)
            o_ref[0, cur, :] = acc / l
            lse_ref[0, cur, :] = jnp.broadcast_to(m + jnp.log(l), (span, dh))

    cur_q = pl.BlockSpec((1, rows, dh), lambda h, i: (h, i, 0))
    cur_kv = pl.BlockSpec((1, rows, dh), lambda h, i: (h // group, i, 0))
    prev_kv = pl.BlockSpec((1, BLOCK, dh), lambda h, i: (h // group, jnp.maximum(i * nb - 1, 0), 0))
    in_specs = [cur_q, cur_kv, prev_kv, cur_kv, prev_kv, pl.BlockSpec((1, span, span + BLOCK), lambda h, i: (h, 0, 0))]
    return pl.pallas_call(
        body, name=name, grid=(hq, nchunks), in_specs=in_specs,
        out_specs=[cur_q, cur_q],
        out_shape=[jax.ShapeDtypeStruct((hq, t, dh), F32), jax.ShapeDtypeStruct((hq, t, dh), F32)],
        compiler_params=_params(("parallel", "arbitrary")),
    )(q, k, k, v, v, _span_bias(bias, span))


def _sink_lse(lse, sink):
    m = jnp.maximum(lse, sink)
    return m + jnp.log(jnp.exp(lse - m) + jnp.exp(sink - m))


def _merge_with_sinks(o, lse, sinks, name):
    h, t, w = o.shape
    tr = _layout_rows(t, 3 * h * LANES * 4)

    def body(o_ref, lse_ref, sink_ref, out_ref):
        for j in range(h):
            lse_j = lse_ref[j]
            shrink = jnp.exp(lse_j - _sink_lse(lse_j, sink_ref[j, :1, :w]))
            out_ref[:, j * w:(j + 1) * w] = (o_ref[j] * shrink).astype(BF16)

    heads = pl.BlockSpec((h, tr, w), lambda i: (0, i, 0))
    return pl.pallas_call(
        body, name=name, grid=(t // tr,),
        in_specs=[heads, heads, pl.BlockSpec((h, BAND_SPAN_MAX, LANES), lambda i: (0, 0, 0))],
        out_specs=pl.BlockSpec((tr, h * w), lambda i: (i, 0)), out_shape=jax.ShapeDtypeStruct((t, h * w), BF16),
        compiler_params=_params(("parallel",)),
    )(o, lse, sinks)


def _band_bwd(q, k, v, o, lse, do, bias, sinks, blocks_per_seq, span, name):
    hq, t, dh = q.shape
    group = hq // k.shape[0]
    rows, nb, nchunks, span = _band_geometry(t, blocks_per_seq, span)
    per_span = span // BLOCK
    has_sink = sinks is not None
    scale = HEAD_DIM ** -0.5

    def body(q_ref, kc_ref, kp_ref, vc_ref, vp_ref, o_ref, lse_ref, do_ref, bias_ref, *rest):
        dq_ref, dk_ref, dv_ref, dbias_ref, dsink_ref, dk_carry, dv_carry = rest[-7:]
        step = pl.program_id(1)
        chunk = nchunks - 1 - step
        bias_v = bias_ref[0]
        sink = rest[0][0, :span, :1] if has_sink else None

        @pl.when(step == 0)
        def _():
            dk_carry[...] = jnp.zeros_like(dk_carry)
            dv_carry[...] = jnp.zeros_like(dv_carry)
            dbias_ref[...] = jnp.zeros_like(dbias_ref)
            dsink_ref[...] = jnp.zeros_like(dsink_ref)

        dks = [jnp.zeros((BLOCK, dh), F32) for _ in range(nb + 1)]
        dvs = [jnp.zeros((BLOCK, dh), F32) for _ in range(nb + 1)]
        dks[nb] = dk_carry[...]
        dvs[nb] = dv_carry[...]
        for j in range(rows // span - 1, -1, -1):
            cur = slice(j * span, (j + 1) * span)
            prev = slice(j * span - BLOCK, j * span)
            kk = jnp.concatenate([kp_ref[0] if j == 0 else kc_ref[0, prev, :], kc_ref[0, cur, :]], axis=0)
            vv = jnp.concatenate([vp_ref[0] if j == 0 else vc_ref[0, prev, :], vc_ref[0, cur, :]], axis=0)
            first = lax.rem(chunk * nb, blocks_per_seq) == 0 if j == 0 else None
            qj, doj = q_ref[0, cur, :], do_ref[0, cur, :]
            lse_j = lse_ref[0, cur, :][:, :1]
            delta = jnp.sum(doj.astype(F32) * o_ref[0, cur, :], axis=1, keepdims=True)
            if has_sink:
                with_sink = _sink_lse(lse_j, sink)
                delta = delta * jnp.exp(lse_j - with_sink)
                lse_j = with_sink
            p = jnp.exp(_band_logits(qj, kk, bias_v, first) - lse_j)
            dp = lax.dot_general(doj, vv, NT, preferred_element_type=F32)
            ds = p * (dp - delta)
            dbias_ref[0] += ds
            if has_sink:
                dsink = -jnp.sum(jnp.exp(sink - lse_j) * delta, axis=0, keepdims=True)
                dsink_ref[0] += jnp.broadcast_to(dsink, (1, LANES))
            dsb = (ds * scale).astype(BF16)
            dq_ref[0, cur, :] = jnp.dot(dsb, kk, preferred_element_type=F32)
            dkk = lax.dot_general(dsb, qj, TN, preferred_element_type=F32)
            dvv = lax.dot_general(p.astype(BF16), doj, TN, preferred_element_type=F32)
            for b in range(per_span + 1):
                piece = slice(b * BLOCK, (b + 1) * BLOCK)
                dks[j * per_span + b] += dkk[piece]
                dvs[j * per_span + b] += dvv[piece]
        for j in range(nb):
            cur = slice(j * BLOCK, (j + 1) * BLOCK)
            dk_ref[0, cur, :] = dks[j + 1]
            dv_ref[0, cur, :] = dvs[j + 1]
        dk_carry[...] = dks[0]
        dv_carry[...] = dvs[0]

    def rev(i):
        return nchunks - 1 - i

    cur_q = pl.BlockSpec((1, rows, dh), lambda h, i: (h, rev(i), 0))
    cur_kv = pl.BlockSpec((1, rows, dh), lambda h, i: (h // group, rev(i), 0))
    prev_kv = pl.BlockSpec((1, BLOCK, dh), lambda h, i: (h // group, jnp.maximum(rev(i) * nb - 1, 0), 0))
    cur_lse = cur_q
    per_head_bias = pl.BlockSpec((1, span, span + BLOCK), lambda h, i: (h, 0, 0))
    per_head_row = pl.BlockSpec((1, 1, LANES), lambda h, i: (h, 0, 0))
    in_specs = [cur_q, cur_kv, prev_kv, cur_kv, prev_kv, cur_q, cur_lse, cur_q, per_head_bias]
    ins = [q, k, k, v, v, o, lse, do, _span_bias(bias, span)]
    if has_sink:
        in_specs.append(pl.BlockSpec((1, BAND_SPAN_MAX, LANES), lambda h, i: (h, 0, 0)))
        ins.append(sinks)
    full = jax.ShapeDtypeStruct((hq, t, dh), F32)
    dq, dk, dv, dbias, dsink = pl.pallas_call(
        body, name=name, grid=(hq, nchunks), in_specs=in_specs,
        out_specs=[cur_q, cur_q, cur_q, per_head_bias, per_head_row],
        out_shape=[full, full, full, jax.ShapeDtypeStruct((hq, span, span + BLOCK), F32),
                   jax.ShapeDtypeStruct((hq, 1, LANES), F32)],
        scratch_shapes=[pltpu.VMEM((BLOCK, dh), F32), pltpu.VMEM((BLOCK, dh), F32)],
        compiler_params=_params(("parallel", "arbitrary")),
    )(*ins)
    return dq, dk, dv, _fold_span_bias_grad(dbias, span), dsink


C_QK = C_NOPE + C_ROPE
C_SCALE = C_QK ** -0.5
LOG2E = math.log2(math.e)
C_EXP2 = C_SCALE * LOG2E
CAUSAL_SUB = 256


def _causal_tile(t, forward=False):
    return min(4096 if forward else 2048, t)


CAUSAL_Q_CHAIN = 128
CAUSAL_K_CHAIN = 256
STAT_ROWS = 8


def _mla_fwd(q, k, v1t, name):
    h, t, _ = q.shape
    tq = _causal_tile(t, forward=True)
    n = t // tq
    qs, ks = min(CAUSAL_Q_CHAIN, tq), min(CAUSAL_K_CHAIN, tq)

    def body(q_ref, k_ref, v_ref, ot_ref, lse_ref, m_scr, acc_scr):
        qi, ki = pl.program_id(1), pl.program_id(2)

        @pl.when(ki == 0)
        def _():
            m_scr[...] = jnp.full_like(m_scr, NEG)
            acc_scr[...] = jnp.zeros_like(acc_scr)

        def tile(diagonal):
            for r in range(tq // qs):
                cols = slice(r * qs, (r + 1) * qs)
                q_sub = q_ref[0, cols, :]
                m, acc = m_scr[:1, cols], acc_scr[:, cols]
                for kc in range(tq // ks):
                    k0 = kc * ks
                    if diagonal and k0 > r * qs + qs - 1:
                        continue
                    st = lax.dot_general(k_ref[0, k0:k0 + ks, :], q_sub, NT, preferred_element_type=F32)
                    if diagonal and k0 + ks - 1 > r * qs:
                        kpos = k0 + lax.broadcasted_iota(jnp.int32, st.shape, 0)
                        qpos = r * qs + lax.broadcasted_iota(jnp.int32, st.shape, 1)
                        st = jnp.where(kpos <= qpos, st, NEG)
                    m_new = jnp.maximum(m, jnp.max(st, axis=0, keepdims=True))
                    alpha = jnp.exp2((m - m_new) * C_EXP2)
                    pt = jnp.exp2((st - m_new) * C_EXP2).astype(BF16)
                    acc = acc * alpha + jnp.dot(v_ref[0, :, k0:k0 + ks], pt, preferred_element_type=F32)
                    m = m_new
                m_scr[:, cols] = jnp.broadcast_to(m, (STAT_ROWS, qs))
                acc_scr[:, cols] = acc

        @pl.when(ki < qi)
        def _():
            tile(False)

        @pl.when(ki == qi)
        def _():
            tile(True)
            l = acc_scr[C_V:C_V + 1, :]
            ot_ref[0] = acc_scr[:C_V, :] / l
            lse_ref[0] = jnp.broadcast_to(m_scr[:1, :] * C_SCALE + jnp.log(l), (STAT_ROWS, tq))

    return pl.pallas_call(
        body, name=name, grid=(h, n, n),
        in_specs=[pl.BlockSpec((1, tq, C_QK), lambda hh, qi, ki: (hh, qi, 0)),
                  pl.BlockSpec((1, tq, C_QK), lambda hh, qi, ki: (hh, jnp.minimum(ki, qi), 0)),
                  pl.BlockSpec((1, LANES, tq), lambda hh, qi, ki: (hh, 0, jnp.minimum(ki, qi)))],
        out_specs=[pl.BlockSpec((1, C_V, tq), lambda hh, qi, ki: (hh, 0, qi)),
                   pl.BlockSpec((1, STAT_ROWS, tq), lambda hh, qi, ki: (hh, 0, qi))],
        out_shape=[jax.ShapeDtypeStruct((h, C_V, t), F32), jax.ShapeDtypeStruct((h, STAT_ROWS, t), F32)],
        scratch_shapes=[pltpu.VMEM((STAT_ROWS, tq), F32), pltpu.VMEM((LANES, tq), F32)],
        compiler_params=_params(("parallel", "arbitrary", "arbitrary")),
    )(q, k, v1t)


def _mla_bwd(q, k, v1, o, lse, do, name):
    h, t, _ = q.shape
    tq = _causal_tile(t)
    n = t // tq
    sub = min(CAUSAL_SUB, tq)

    def body(q_ref, k_ref, v_ref, o_ref, lse_ref, do_ref, dqt_ref, dk_ref, dv_ref, dk_acc, dv_acc):
        ki, qi = pl.program_id(1), pl.program_id(2)

        @pl.when(qi == 0)
        def _():
            dk_acc[...] = jnp.zeros_like(dk_acc)
            dv_acc[...] = jnp.zeros_like(dv_acc)

        @pl.when(jnp.logical_and(ki == 0, qi == 0))
        def _():
            dqt_ref[...] = jnp.zeros_like(dqt_ref)

        def tile(diagonal):
            for c in range(tq // sub):
                cols = slice(c * sub, (c + 1) * sub)
                nk = (c + 1) * sub if diagonal else tq
                qc, doc = q_ref[0, cols, :], do_ref[0, cols, :]
                st = lax.dot_general(k_ref[0, :nk, :], qc, NT, preferred_element_type=F32)
                lse2 = lse_ref[0, :1, cols] * LOG2E
                pt = jnp.exp2(st * C_EXP2 - lse2)
                if diagonal:
                    kpos = lax.broadcasted_iota(jnp.int32, st.shape, 0)
                    qpos = c * sub + lax.broadcasted_iota(jnp.int32, st.shape, 1)
                    pt = jnp.where(kpos <= qpos, pt, 0.0)
                dpt = lax.dot_general(v_ref[0, :nk, :C_V], doc, NT, preferred_element_type=F32)
                delta = jnp.sum(doc.astype(F32) * o_ref[0, cols, :], axis=1, keepdims=True)
                delta_row = jnp.transpose(jnp.broadcast_to(delta, (sub, LANES)))[:1]
                dst = (pt * (dpt - delta_row)).astype(BF16)
                dv_acc[:nk, :] += jnp.dot(pt.astype(BF16), doc, preferred_element_type=F32)
                dk_acc[:nk, :] += jnp.dot(dst, qc, preferred_element_type=F32)
                out_cols = pl.ds(pl.multiple_of(qi * tq + c * sub, sub), sub)
                dqt_ref[0, :, out_cols] += lax.dot_general(k_ref[0, :nk, :], dst, TN,
                                                           preferred_element_type=F32) * C_SCALE

        @pl.when(qi > ki)
        def _():
            tile(False)

        @pl.when(qi == ki)
        def _():
            tile(True)

        @pl.when(qi == n - 1)
        def _():
            dk_ref[0] = dk_acc[...] * C_SCALE
            dv_ref[0] = dv_acc[...]

    def q_spec(d):
        return pl.BlockSpec((1, tq, d), lambda hh, ki, qi: (hh, jnp.maximum(qi, ki), 0))

    def k_spec(d):
        return pl.BlockSpec((1, tq, d), lambda hh, ki, qi: (hh, ki, 0))

    return pl.pallas_call(
        body, name=name, grid=(h, n, n),
        in_specs=[q_spec(C_QK), k_spec(C_QK), k_spec(LANES), q_spec(C_V),
                  pl.BlockSpec((1, STAT_ROWS, tq), lambda hh, ki, qi: (hh, 0, jnp.maximum(qi, ki))), q_spec(C_V)],
        out_specs=[pl.BlockSpec((1, C_QK, t), lambda hh, ki, qi: (hh, 0, 0)), k_spec(C_QK), k_spec(C_V)],
        out_shape=[jax.ShapeDtypeStruct((h, C_QK, t), F32), jax.ShapeDtypeStruct((h, t, C_QK), F32),
                   jax.ShapeDtypeStruct((h, t, C_V), F32)],
        scratch_shapes=[pltpu.VMEM((tq, C_QK), F32), pltpu.VMEM((tq, C_V), F32)],
        compiler_params=_params(("parallel", "arbitrary", "arbitrary")),
    )(q, k, v1, o, lse, do)


def _exchange(src, flips, src_idx, name):
    n = len(flips)
    _, r, c = src.shape

    def body(src_ref, dst_ref, send_sems, recv_sems):
        me = [lax.axis_index(a) for a in ALL_AXES]
        copies = []
        for kk, flip in enumerate(flips):
            peer = tuple(1 - p if f else p for p, f in zip(me, flip))
            copies.append(pltpu.make_async_remote_copy(
                src_ref=src_ref.at[src_idx[kk]], dst_ref=dst_ref.at[kk], send_sem=send_sems.at[kk],
                recv_sem=recv_sems.at[kk], device_id=peer, device_id_type=MESH))
        for cp in copies:
            cp.start()
        for cp in copies:
            cp.wait_recv()
        for cp in copies:
            cp.wait_send()

    return pl.pallas_call(
        body, name=name, in_specs=[pl.BlockSpec(memory_space=pl.ANY)], out_specs=pl.BlockSpec(memory_space=pl.ANY),
        out_shape=jax.ShapeDtypeStruct((n, r, c), src.dtype),
        scratch_shapes=[pltpu.SemaphoreType.DMA((n,)), pltpu.SemaphoreType.DMA((n,))],
    )(src)


FLIP_C = (0, 0, 1)
CHIP_FLIPS = ((0, 1, 0), (1, 0, 0), (1, 1, 0))
ALL_FLIPS = tuple((a >> 2 & 1, a >> 1 & 1, a & 1) for a in range(1, 8))


def _pick(stacked, idx):
    return lax.dynamic_index_in_dim(stacked, idx, axis=0, keepdims=False)


def _to_comm_rows(parts, dtype):
    flat = jnp.concatenate([p.reshape(-1) for p in parts]).astype(dtype)
    rows = -(-flat.size // (COMM_COLS * COMM_ROW_ALIGN)) * COMM_ROW_ALIGN
    return jnp.pad(flat, (0, rows * COMM_COLS - flat.size)).reshape(rows, COMM_COLS)


def _from_comm_rows(buf, shapes):
    flat, out, off = buf.reshape(-1), [], 0
    for s in shapes:
        size = math.prod(s)
        out.append(flat[off:off + size].reshape(s))
        off += size
    return out


def _my_place():
    x, y, c = (lax.axis_index(a) for a in ALL_AXES)
    return (x, y, c), 2 * x + y


def _flipped(me, flip):
    return tuple(1 - p if f else p for p, f in zip(me, flip))


def _remote(src, dst, sems, k, peer):
    send_sems, recv_sems = sems
    return pltpu.make_async_remote_copy(src_ref=src, dst_ref=dst, send_sem=send_sems.at[k], recv_sem=recv_sems.at[k],
                                        device_id=peer, device_id_type=MESH)


def _gather_shards(buf, chip):
    rows, cols = buf.shape
    half = rows // 2
    n = len(CHIP_FLIPS)

    def body(src_ref, out_ref, send_sems, recv_sems):
        me, chip = _my_place()
        sems = (send_sems, recv_sems)
        sibling = _flipped(me, FLIP_C)
        mine = pl.ds(pl.multiple_of(me[2] * half, half), half)
        theirs = pl.ds(pl.multiple_of((1 - me[2]) * half, half), half)
        peers = [_flipped(me, f) for f in CHIP_FLIPS]
        from_chip = [2 * p[0] + p[1] for p in peers]
        over_ici = [_remote(src_ref.at[mine], out_ref.at[chip, mine], sems, k, peers[k]) for k in range(n)]
        for cp in over_ici:
            cp.start()
        passed = [_remote(out_ref.at[from_chip[k], mine], out_ref.at[from_chip[k], mine], sems, n + k, sibling)
                  for k in range(n)]
        for k in range(n):
            _remote(src_ref.at[mine], out_ref.at[from_chip[k], mine], sems, k, peers[k]).wait_recv()
            passed[k].start()
        for k in range(n):
            _remote(out_ref.at[from_chip[k], theirs], out_ref.at[from_chip[k], theirs], sems, n + k, sibling).wait_recv()
        for cp in over_ici + passed:
            cp.wait_send()

    others = pl.pallas_call(
        body, name="gather_shards", in_specs=[pl.BlockSpec(memory_space=pl.ANY)],
        out_specs=pl.BlockSpec(memory_space=pl.ANY), out_shape=jax.ShapeDtypeStruct((4, rows, cols), buf.dtype),
        scratch_shapes=[pltpu.SemaphoreType.DMA((2 * n,)), pltpu.SemaphoreType.DMA((2 * n,))],
    )(buf)
    return lax.dynamic_update_slice(others, buf[None], (chip, 0, 0))


def _swap_other_halves(portions):
    _, rows, cols = portions.shape
    half = rows // 2

    def body(src_ref, dst_ref, send_sems, recv_sems):
        me, _ = _my_place()
        theirs = pl.ds(pl.multiple_of((1 - me[2]) * half, half), half)
        cp = _remote(src_ref.at[:, theirs], dst_ref, (send_sems, recv_sems), 0, _flipped(me, FLIP_C))
        cp.start()
        cp.wait_recv()
        cp.wait_send()

    return pl.pallas_call(
        body, name="reduce_d2d", in_specs=[pl.BlockSpec(memory_space=pl.ANY)],
        out_specs=pl.BlockSpec(memory_space=pl.ANY), out_shape=jax.ShapeDtypeStruct((4, half, cols), portions.dtype),
        scratch_shapes=[pltpu.SemaphoreType.DMA((1,)), pltpu.SemaphoreType.DMA((1,))],
    )(portions)


def _join_halves(total, core):
    half, cols = total.shape

    def body(src_ref, out_ref, send_sems, recv_sems):
        me, _ = _my_place()
        mine = pl.ds(pl.multiple_of(me[2] * half, half), half)
        theirs = pl.ds(pl.multiple_of((1 - me[2]) * half, half), half)
        sems = (send_sems, recv_sems)
        cp = _remote(src_ref, out_ref.at[mine], sems, 0, _flipped(me, FLIP_C))
        cp.start()
        _remote(src_ref, out_ref.at[theirs], sems, 0, _flipped(me, FLIP_C)).wait_recv()
        cp.wait_send()

    from_sibling = pl.pallas_call(
        body, name="reduce_share", in_specs=[pl.BlockSpec(memory_space=pl.ANY)],
        out_specs=pl.BlockSpec(memory_space=pl.ANY), out_shape=jax.ShapeDtypeStruct((2 * half, cols), total.dtype),
        scratch_shapes=[pltpu.SemaphoreType.DMA((1,)), pltpu.SemaphoreType.DMA((1,))],
    )(total)
    return lax.dynamic_update_slice(from_sibling, total, (core * half, 0))


def _reduce_scatter(portions, chip, core):
    half = portions.shape[1] // 2
    keep = lax.dynamic_slice_in_dim(portions, core * half, half, axis=1)
    got = _swap_other_halves(portions)
    pair = _ew(lambda p, q: (p + q,), [keep.reshape(4 * half, -1), got.reshape(4 * half, -1)],
               [(COMM_COLS, F32)], "reduce_pair_sum")[0].reshape(keep.shape)
    out = jnp.stack([_pick(pair, jnp.bitwise_xor(chip, f)) for f in (1, 2, 3)]).astype(BF16)
    others = _exchange(out, CHIP_FLIPS, (0, 1, 2), "reduce_ici")
    total = _ew(lambda p, q, r, s: (p + q.astype(F32) + r.astype(F32) + s.astype(F32),),
                [_pick(pair, chip), others[0], others[1], others[2]],
                [(COMM_COLS, F32)], "reduce_chip_sum")[0]
    return _join_halves(total, core)


def _all_reduce_small(buf, dev):
    got = _exchange(buf[None], ALL_FLIPS, (0,) * 7, "small_gather")
    by_flip = jnp.concatenate([buf[None], got])
    ordered = [_pick(by_flip, jnp.bitwise_xor(dev, a)) for a in range(8)]

    def fn(*t):
        s = t[0]
        for u in t[1:]:
            s = s + u
        return (s,)

    return _ew(fn, ordered, [(buf.shape[1], F32)], "small_sum")[0]


def _band_bucket_onehot(dilation, max_dist):
    i = jnp.arange(BLOCK)[:, None]
    j = jnp.arange(2 * BLOCK)[None, :]
    dist = i + BLOCK - j
    inband = (dist >= 0) & (dist <= max_dist)
    n = jnp.maximum(dist, 0) * dilation
    max_exact = NUM_BUCKETS // 2
    nf = jnp.maximum(n, 1).astype(F32)
    large = max_exact + (jnp.log(nf / max_exact) / math.log(MAX_DISTANCE / max_exact)
                         * (NUM_BUCKETS - max_exact)).astype(jnp.int32)
    bucket = jnp.where(n < max_exact, n, jnp.minimum(large, NUM_BUCKETS - 1))
    onehot = (bucket[..., None] == jnp.arange(NUM_BUCKETS)) & inband[..., None]
    return onehot.reshape(-1, NUM_BUCKETS).astype(F32), inband.reshape(-1)


def _band_bias(table, onehot, inband):
    vals = jnp.einsum("pb,bh->hp", onehot, table, precision=lax.Precision.HIGHEST)
    return jnp.where(inband[None, :], vals, NEG).reshape(-1, BLOCK, 2 * BLOCK)


BAND_VARIANTS = ((1, A_WINDOW - 1),) + tuple((dil, window // dil) for window, dil in B_BRANCHES)


LAYOUT_TILE_BYTES = 4 << 20


def _layout_rows(length, row_bytes):
    return _divisor_tile(length, max(16, LAYOUT_TILE_BYTES // row_bytes), 16)


def _split_heads(items, w, dil, name):
    s = items[0][0].shape[0]
    length = s // dil
    row_bytes = sum(-(-n * w // LANES) * LANES * (x.dtype.itemsize + jnp.dtype(d).itemsize) for x, _, n, d in items)
    tr = _layout_rows(length, row_bytes)
    nt = length // tr

    def body(*refs):
        for (x_ref, o_ref), (_, _, n, _) in zip(zip(refs[:len(items)], refs[len(items):]), items):
            for j in range(n):
                o_ref[j] = x_ref[:, j * w:(j + 1) * w].astype(o_ref.dtype)

    in_specs, out_specs, out_shape, views = [], [], [], []
    for x, first, n, d in items:
        bw, width = n * w, x.shape[1]
        assert bw % LANES == 0 and first % bw == 0 and (dil == 1 or width % bw == 0)
        in_specs.append(pl.BlockSpec((tr, bw), lambda r, i, c0=first // bw, wb=width // bw: (i, r * wb + c0)))
        out_specs.append(pl.BlockSpec((n, tr, w), lambda r, i: (0, r * nt + i, 0)))
        out_shape.append(jax.ShapeDtypeStruct((n, s, w), d))
        views.append(x.reshape(length, dil * width))
    return pl.pallas_call(
        body, name=name, grid=(dil, nt), in_specs=in_specs, out_specs=out_specs, out_shape=out_shape,
        compiler_params=_params(("parallel", "parallel")),
    )(*views)


def _merge_heads(items, dil, name, group_sum=1):
    s, w = items[0][0].shape[1:]
    length = s // dil
    row_bytes = sum(t.shape[0] * LANES * t.dtype.itemsize + t.shape[0] * w * jnp.dtype(d).itemsize for t, d in items)
    tr = _layout_rows(length, row_bytes)
    nt = length // tr

    def body(*refs):
        for t_ref, o_ref in zip(refs[:len(items)], refs[len(items):]):
            for j in range(t_ref.shape[0] // group_sum):
                v = t_ref[j * group_sum]
                for g in range(1, group_sum):
                    v = v + t_ref[j * group_sum + g]
                o_ref[:, j * w:(j + 1) * w] = v.astype(o_ref.dtype)

    in_specs, out_specs, out_shape = [], [], []
    for t, d in items:
        n = t.shape[0]
        bw = n // group_sum * w
        assert bw % LANES == 0
        in_specs.append(pl.BlockSpec((n, tr, w), lambda r, i: (0, r * nt + i, 0)))
        out_specs.append(pl.BlockSpec((tr, bw), lambda r, i: (i, r)))
        out_shape.append(jax.ShapeDtypeStruct((length, dil * bw), d))
    outs = pl.pallas_call(
        body, name=name, grid=(dil, nt), in_specs=in_specs, out_specs=out_specs, out_shape=out_shape,
        compiler_params=_params(("parallel", "parallel")),
    )(*[t for t, _ in items])
    return [o.reshape(s, -1) for o in outs]


def _merge_heads_transposed(t, dtype, name, also_heads=False):
    n, w, s = t.shape
    tr = _layout_rows(s, 3 * n * w * t.dtype.itemsize)

    def body(t_ref, o_ref, *heads_ref):
        rows = jnp.transpose(t_ref[...].reshape(n * w, tr))
        o_ref[...] = rows.astype(o_ref.dtype)
        for j in range(n if also_heads else 0):
            heads_ref[0][j] = rows[:, j * w:(j + 1) * w]

    out_specs = [pl.BlockSpec((tr, n * w), lambda i: (i, 0))]
    out_shape = [jax.ShapeDtypeStruct((s, n * w), dtype)]
    if also_heads:
        out_specs.append(pl.BlockSpec((n, tr, w), lambda i: (0, i, 0)))
        out_shape.append(jax.ShapeDtypeStruct((n, s, w), t.dtype))
    outs = pl.pallas_call(
        body, name=name, grid=(s // tr,), in_specs=[pl.BlockSpec((n, w, tr), lambda i: (0, 0, i))],
        out_specs=out_specs, out_shape=out_shape, compiler_params=_params(("parallel",)),
    )(t)
    return outs if also_heads else outs[0]


ROPE_HALF = C_ROPE // 2
ROPE_PERIOD = 3 * LANES


def _rope_tables(s):
    inv = ROPE_THETA ** (-jnp.arange(0, C_ROPE, 2, dtype=F32) / C_ROPE)
    ang = jnp.arange(s, dtype=F32)[:, None] * inv[None, :]
    cos, sin = jnp.cos(ang), jnp.sin(ang)
    one, zero = jnp.ones((s, C_NOPE), F32), jnp.zeros((s, C_NOPE), F32)
    z16 = jnp.zeros((s, ROPE_HALF), F32)
    reps = ROPE_PERIOD // C_QK
    keep = jnp.tile(jnp.concatenate([one, cos, cos], axis=1), (1, reps))
    from_above = jnp.tile(jnp.concatenate([zero, -sin, z16], axis=1), (1, reps))
    from_below = jnp.tile(jnp.concatenate([zero, z16, sin], axis=1), (1, reps))
    return (cos, sin), (keep, from_above, from_below)


def _rope_rows(x, tables, inverse, name, dtype):
    width = x.shape[1]
    reps = width // ROPE_PERIOD
    sign = -1.0 if inverse else 1.0

    def fn(xv, keep, above, below):
        keep, above, below = (jnp.tile(t, (1, reps)) for t in (keep, above, below))
        up = pltpu.roll(xv, width - ROPE_HALF, 1)
        down = pltpu.roll(xv, ROPE_HALF, 1)
        return (xv * keep + sign * (up * above + down * below),)

    return _ew(fn, [x, *tables], [(width, dtype)], name)[0]


def _rotate_half_pairs(a, b, cos, sin, inverse):
    if inverse:
        return a * cos + b * sin, b * cos - a * sin
    return a * cos - b * sin, a * sin + b * cos


def _split_kv(kv, down, cos, sin, name):
    s = kv.shape[0]
    h = kv.shape[1] // (C_NOPE + C_V)
    tr = _layout_rows(s, 8 * kv.shape[1])
    r0 = C_Q_RANK + C_KV_RANK

    def body(kv_ref, down_ref, cos_ref, sin_ref, k_ref, v_ref, vt_ref):
        k1, k2 = _rotate_half_pairs(down_ref[:, r0:r0 + ROPE_HALF], down_ref[:, r0 + ROPE_HALF:r0 + C_ROPE],
                                    cos_ref[...], sin_ref[...], False)
        lane = lax.broadcasted_iota(jnp.int32, (tr, LANES - C_V), 1)
        tail = jnp.where(lane == 0, 1.0, 0.0)
        for j in range(h):
            base = j * (C_NOPE + C_V)
            k_ref[j, :, :C_NOPE] = kv_ref[:, base:base + C_NOPE]
            k_ref[j, :, C_NOPE:C_NOPE + ROPE_HALF] = k1.astype(BF16)
            k_ref[j, :, C_NOPE + ROPE_HALF:] = k2.astype(BF16)
            v1 = jnp.concatenate([kv_ref[:, base + C_NOPE:base + C_NOPE + C_V].astype(F32), tail], axis=1)
            v_ref[j] = v1.astype(BF16)
            vt_ref[j] = jnp.transpose(v1).astype(BF16)

    def rows(width):
        return pl.BlockSpec((tr, width), lambda i: (i, 0))

    return pl.pallas_call(
        body, name=name, grid=(s // tr,),
        in_specs=[rows(kv.shape[1]), rows(down.shape[1]), rows(ROPE_HALF), rows(ROPE_HALF)],
        out_specs=[pl.BlockSpec((h, tr, C_QK), lambda i: (0, i, 0)), pl.BlockSpec((h, tr, LANES), lambda i: (0, i, 0)),
                   pl.BlockSpec((h, LANES, tr), lambda i: (0, 0, i))],
        out_shape=[jax.ShapeDtypeStruct((h, s, C_QK), BF16), jax.ShapeDtypeStruct((h, s, LANES), BF16),
                   jax.ShapeDtypeStruct((h, LANES, s), BF16)],
        compiler_params=_params(("parallel",)),
    )(kv, down, cos, sin)


def _merge_kv_bwd(dk, dv, cos, sin, name):
    h, s, _ = dk.shape
    tr = _layout_rows(s, 8 * h * LANES)

    def body(dk_ref, dv_ref, cos_ref, sin_ref, dkv_ref, dkr_ref):
        rot = dk_ref[0, :, C_NOPE:]
        for j in range(h):
            base = j * (C_NOPE + C_V)
            dkv_ref[:, base:base + C_NOPE] = dk_ref[j, :, :C_NOPE].astype(BF16)
            dkv_ref[:, base + C_NOPE:base + C_NOPE + C_V] = dv_ref[j].astype(BF16)
            if j:
                rot = rot + dk_ref[j, :, C_NOPE:]
        d1, d2 = _rotate_half_pairs(rot[:, :ROPE_HALF], rot[:, ROPE_HALF:], cos_ref[...], sin_ref[...], True)
        dkr_ref[:, :ROPE_HALF] = d1
        dkr_ref[:, ROPE_HALF:] = d2

    def rows(width):
        return pl.BlockSpec((tr, width), lambda i: (i, 0))

    return pl.pallas_call(
        body, name=name, grid=(s // tr,),
        in_specs=[pl.BlockSpec((h, tr, C_QK), lambda i: (0, i, 0)), pl.BlockSpec((h, tr, C_V), lambda i: (0, i, 0)),
                  rows(ROPE_HALF), rows(ROPE_HALF)],
        out_specs=[rows(h * (C_NOPE + C_V)), rows(C_ROPE)],
        out_shape=[jax.ShapeDtypeStruct((s, h * (C_NOPE + C_V)), BF16), jax.ShapeDtypeStruct((s, C_ROPE), F32)],
        compiler_params=_params(("parallel",)),
    )(dk, dv, cos, sin)


SPAN_FWD, SPAN_BWD = 128, 256


def _even_fwd(xn, h, w_in, w_out, sinks_row, biases, tag):
    s = xn.shape[0]
    proj = _mm(xn, w_in, "nn", [BF16], f"in_proj{tag}")[0]
    qd, kd = A_Q_HEADS * HEAD_DIM, A_KV_HEADS * HEAD_DIM
    qa, ka, va = _split_heads([(proj, 0, A_Q_HEADS, BF16), (proj, qd, A_KV_HEADS, BF16),
                               (proj, qd + kd, A_KV_HEADS, BF16)], HEAD_DIM, 1, f"swa_split{tag}")
    oa, lse_a = _band_fwd(qa, ka, va, biases[0], s // BLOCK, SPAN_FWD, f"swa_fwd{tag}")
    out_a = _merge_with_sinks(oa, lse_a, sinks_row, f"swa_merge{tag}")
    width = B_HEADS_PER_BRANCH * HEAD_DIM
    qkv_b, outs, lses = [], [], []
    for g, (_, dil) in enumerate(B_BRANCHES):
        base = A_IN + g * 3 * width
        src, base = (proj, base) if dil == 1 else (proj[:, base:base + 3 * width], 0)
        qkv = _split_heads([(src, base + i * width, B_HEADS_PER_BRANCH, BF16) for i in range(3)], HEAD_DIM, dil,
                           f"dil{g}_split{tag}")
        og, lg = _band_fwd(*qkv, biases[1 + g], s // dil // BLOCK, SPAN_FWD, f"dil{g}_fwd{tag}")
        qkv_b.append(qkv)
        merged = _merge_heads([(og, F32), (lg, F32)], dil, f"dil{g}_merge{tag}")
        outs.append(merged[0])
        lses.append(merged[1])

    def merge(o0, o1, o2, l0, l1, l2):
        m = jnp.maximum(jnp.maximum(l0, l1), l2)
        e0, e1, e2 = jnp.exp(l0 - m), jnp.exp(l1 - m), jnp.exp(l2 - m)
        den = e0 + e1 + e2
        out = (e0 * o0 + e1 * o1 + e2 * o2) / den
        return out, m + jnp.log(den), out

    out_b, lse_b, out_b16 = _ew(merge, outs + lses, [(width, F32), (width, F32), (width, BF16)], f"dil_merge{tag}")
    cat = jnp.concatenate([out_a, out_b16], axis=1)
    h_mid = _mm(cat, w_out, "nn", [F32], f"out_proj{tag}", epi=_add_epi, extras=(h,))[0]
    return h_mid, (qa, ka, va, oa, lse_a, qkv_b, out_b, lse_b, cat)


def _even_bwd(dh, xn, saved, w_in, w_out, sinks_row, biases, tag):
    qa, ka, va, oa, lse_a, qkv_b, out_b, lse_b, cat = saved
    s = xn.shape[0]
    qd = A_Q_HEADS * HEAD_DIM
    g_w_out = _mm(cat, dh, "tn", [F32], f"out_proj_dw{tag}")[0]
    dcat = _mm(dh, w_out, "nt", [BF16], f"out_proj_dx{tag}")[0]
    do_a = _split_heads([(dcat, 0, A_Q_HEADS, BF16)], HEAD_DIM, 1, f"swa_do_split{tag}")[0]
    dqa, dka8, dva8, dbias_a, dsink = _band_bwd(qa, ka, va, oa, lse_a, do_a, biases[0], sinks_row, s // BLOCK,
                                                SPAN_BWD, f"swa_bwd{tag}")
    pieces = _merge_heads([(dqa, BF16)], 1, f"swa_dq_merge{tag}")
    pieces += _merge_heads([(dka8, BF16), (dva8, BF16)], 1, f"swa_dkv_merge{tag}", group_sum=A_Q_HEADS // A_KV_HEADS)
    dbias_b = []
    dcat_b = dcat[:, qd:]
    for g, (_, dil) in enumerate(B_BRANCHES):
        do_g, out_g, lse_g = _split_heads([(dcat_b, 0, B_HEADS_PER_BRANCH, BF16), (out_b, 0, B_HEADS_PER_BRANCH, F32),
                                           (lse_b, 0, B_HEADS_PER_BRANCH, F32)], HEAD_DIM, dil, f"dil{g}_do_split{tag}")
        dqg, dkg, dvg, dbg, _ = _band_bwd(*qkv_b[g], out_g, lse_g, do_g, biases[1 + g], None, s // dil // BLOCK,
                                          SPAN_BWD, f"dil{g}_bwd{tag}")
        pieces += _merge_heads([(dqg, BF16), (dkg, BF16), (dvg, BF16)], dil, f"dil{g}_dqkv_merge{tag}")
        dbias_b.append(dbg)
    dproj = jnp.concatenate(pieces, axis=1)
    g_w_in = _mm(xn, dproj, "tn", [F32], f"in_proj_dw{tag}")[0]
    dxn = _mm(dproj, w_in, "nt", [F32], f"in_proj_dx{tag}")[0]
    dbias = jnp.concatenate([db.reshape(db.shape[0], -1) for db in [dbias_a] + dbias_b])
    return dxn, g_w_in, g_w_out, dbias, dsink[:, 0, 0]


def _mla_layer_fwd(xn, h, w_dn, q_norm, w_uq, kv_norm, w_ukv, w_o, ropes, tag):
    (cos, sin), q_tables = ropes
    down = _mm(xn, w_dn, "nn", [F32], f"mla_down{tag}")[0]
    c_q, c_kv = down[:, :C_Q_RANK], down[:, C_Q_RANK:C_Q_RANK + C_KV_RANK]
    cqn = _rms_fwd(c_q, q_norm, f"mla_qnorm{tag}")
    ckvn = _rms_fwd(c_kv, kv_norm, f"mla_kvnorm{tag}")
    q = _mm(cqn, w_uq, "nn", [F32], f"mla_uq{tag}")[0]
    kv = _mm(ckvn, w_ukv, "nn", [BF16], f"mla_ukv{tag}")[0]
    qh = _split_heads([(_rope_rows(q, q_tables, False, f"mla_rope_q{tag}", BF16), 0, C_HEADS, BF16)], C_QK, 1,
                      f"mla_q_split{tag}")[0]
    kh, v1h, v1t = _split_kv(kv, down, cos, sin, f"mla_kv_split{tag}")
    ot, lse = _mla_fwd(qh, kh, v1t, f"mla_attn_fwd{tag}")
    o2d, o = _merge_heads_transposed(ot, BF16, f"mla_o_merge{tag}", also_heads=True)
    h_mid = _mm(o2d, w_o, "nn", [F32], f"mla_o{tag}", epi=_add_epi, extras=(h,))[0]
    return h_mid, (c_q, c_kv, cqn, ckvn, qh, kh, v1h, o, lse, o2d)


def _mla_layer_bwd(dh, xn, saved, w_dn, q_norm, w_uq, kv_norm, w_ukv, w_o, ropes, tag):
    c_q, c_kv, cqn, ckvn, qh, kh, v1h, o, lse, o2d = saved
    (cos, sin), q_tables = ropes
    g_w_o = _mm(o2d, dh, "tn", [F32], f"mla_o_dw{tag}")[0]
    do2d = _mm(dh, w_o, "nt", [BF16], f"mla_o_dx{tag}")[0]
    do = _split_heads([(do2d, 0, C_HEADS, BF16)], C_V, 1, f"mla_do_split{tag}")[0]
    dqt, dk, dv = _mla_bwd(qh, kh, v1h, o, lse, do, f"mla_attn_bwd{tag}")
    dq_roped = _merge_heads_transposed(dqt, F32, f"mla_dq_merge{tag}")
    dq = _rope_rows(dq_roped, q_tables, True, f"mla_rope_q_bwd{tag}", BF16)
    dkv, dk_rope = _merge_kv_bwd(dk, dv, cos, sin, f"mla_dkv_merge{tag}")
    g_w_uq = _mm(cqn, dq, "tn", [F32], f"mla_uq_dw{tag}")[0]
    dcqn = _mm(dq, w_uq, "nt", [F32], f"mla_uq_dx{tag}")[0]
    g_w_ukv = _mm(ckvn, dkv, "tn", [F32], f"mla_ukv_dw{tag}")[0]
    dckvn = _mm(dkv, w_ukv, "nt", [F32], f"mla_ukv_dx{tag}")[0]
    dc_q, g_q_norm = _rms_bwd(c_q, q_norm, dcqn, None, f"mla_qnorm_bwd{tag}")
    dc_kv, g_kv_norm = _rms_bwd(c_kv, kv_norm, dckvn, None, f"mla_kvnorm_bwd{tag}")
    ddown = jnp.concatenate([dc_q, dc_kv, dk_rope], axis=1).astype(BF16)
    g_w_dn = _mm(xn, ddown, "tn", [F32], f"mla_down_dw{tag}")[0]
    dxn = _mm(ddown, w_dn, "nt", [F32], f"mla_down_dx{tag}")[0]
    return dxn, g_w_dn, g_q_norm, g_w_uq, g_kv_norm, g_w_ukv, g_w_o


SHARDED = (("w_in_ab", 2), ("w_out_ab", 2), ("w_down_c", 1), ("w_uq_c", 2), ("w_ukv_c", 2), ("w_o_c", 2),
           ("w_mlp_up", 2), ("w_mlp_down", 1))
SHARDED_NORMS = ("q_norm_c", "kv_norm_c")


def kernel(x, rel_bias, attn_norm, mlp_norm, final_norm, w_in_ab, sinks, w_out_ab, w_down_c, q_norm_c, w_uq_c, kv_norm_c, w_ukv_c, w_o_c, w_mlp_up, w_mlp_down, loss_target, m_rel_bias, m_attn_norm, m_mlp_norm, m_final_norm, m_w_in_ab, m_sinks, m_w_out_ab, m_w_down_c, m_q_norm_c, m_w_uq_c, m_kv_norm_c, m_w_ukv_c, m_w_o_c, m_w_mlp_up, m_w_mlp_down, v_rel_bias, v_attn_norm, v_mlp_norm, v_final_norm, v_w_in_ab, v_sinks, v_w_out_ab, v_w_down_c, v_q_norm_c, v_w_uq_c, v_kv_norm_c, v_w_ukv_c, v_w_o_c, v_w_mlp_up, v_w_mlp_down):
    given = dict(locals())
    chip = lax.axis_index("x") * 2 + lax.axis_index("y")
    core = lax.axis_index("c")
    dev = chip * 2 + core
    depth = attn_norm.shape[0]
    s = x.shape[1]

    shards = [given[n] for n, _ in SHARDED]
    norm_shards = [given[n] for n in SHARDED_NORMS]
    packed = _to_comm_rows([t.astype(BF16) for t in shards]
                           + [lax.bitcast_convert_type(t, BF16) for t in norm_shards], BF16)
    by_chip = _gather_shards(packed, chip)
    shapes = [t.shape for t in shards] + [t.shape + (2,) for t in norm_shards]
    pieces = [_from_comm_rows(by_chip[a], shapes) for a in range(4)]
    full = {n: jnp.concatenate([pieces[a][i] for a in range(4)], axis=ax) for i, (n, ax) in enumerate(SHARDED)}
    for i, n in enumerate(SHARDED_NORMS):
        full[n] = jnp.concatenate([lax.bitcast_convert_type(pieces[a][len(SHARDED) + i], F32) for a in range(4)],
                                  axis=-1)

    onehots = [_band_bucket_onehot(dil, md) for dil, md in BAND_VARIANTS]
    head_cols = [(0, A_Q_HEADS)] + [(A_Q_HEADS + g * B_HEADS_PER_BRANCH, A_Q_HEADS + (g + 1) * B_HEADS_PER_BRANCH)
                                    for g in range(len(B_BRANCHES))]
    biases = [_band_bias(rel_bias[:, lo:hi], oh, inb) for (lo, hi), (oh, inb) in zip(head_cols, onehots)]
    ropes = _rope_tables(s)
    sink_rows = [jnp.broadcast_to(sinks[e][:, None, None], (A_Q_HEADS, BAND_SPAN_MAX, LANES))
                 for e in range(sinks.shape[0])]

    def odd_weights(o):
        return [full[n][o] for n in ("w_down_c", "q_norm_c", "w_uq_c", "kv_norm_c", "w_ukv_c", "w_o_c")]

    h = x[0]
    saved = []
    for l in range(depth):
        xn = _rms_fwd(h, attn_norm[l], f"attn_norm{l}")
        if l % 2 == 0:
            e = l // 2
            h_mid, mix = _even_fwd(xn, h, full["w_in_ab"][e], full["w_out_ab"][e], sink_rows[e], biases, f"_{l}")
        else:
            h_mid, mix = _mla_layer_fwd(xn, h, *odd_weights(l // 2), ropes, f"_{l}")
        xn2 = _rms_fwd(h_mid, mlp_norm[l], f"mlp_norm{l}")
        act, relu = _mm(xn2, full["w_mlp_up"][l], "nn", [BF16, BF16], f"mlp_up{l}", epi=_relu2_epi)
        h_out = _mm(act, full["w_mlp_down"][l], "nn", [F32], f"mlp_down{l}", epi=_add_epi, extras=(h_mid,))[0]
        saved.append((h, xn, mix, h_mid, xn2, act, relu))
        h = h_out
    loss_part, dh, dh16, g_final = _final_loss(h, final_norm, loss_target[0], "final_loss")

    grads = {n: [None] * given[n].shape[0] for n, _ in SHARDED}
    g_q_norm, g_kv_norm = [None] * q_norm_c.shape[0], [None] * kv_norm_c.shape[0]
    g_attn_norm, g_mlp_norm = [None] * depth, [None] * depth
    g_sinks = [None] * sinks.shape[0]
    g_band = []
    for l in range(depth - 1, -1, -1):
        h_in, xn, mix, h_mid, xn2, act, relu = saved[l]
        grads["w_mlp_down"][l] = _mm(act, dh16, "tn", [F32], f"mlp_down_dw{l}")[0]
        du = _mm(dh16, full["w_mlp_down"][l], "nt", [BF16], f"mlp_down_dx{l}", epi=_relu2_bwd_epi, extras=(relu,))[0]
        grads["w_mlp_up"][l] = _mm(xn2, du, "tn", [F32], f"mlp_up_dw{l}")[0]
        dxn2 = _mm(du, full["w_mlp_up"][l], "nt", [F32], f"mlp_up_dx{l}")[0]
        dh, dh16, g_mlp_norm[l] = _rms_bwd(h_mid, mlp_norm[l], dxn2, dh, f"mlp_norm_bwd{l}", also_bf16=True)
        if l % 2 == 0:
            e = l // 2
            dxn, grads["w_in_ab"][e], grads["w_out_ab"][e], g_band_e, g_sinks[e] = _even_bwd(
                dh16, xn, mix, full["w_in_ab"][e], full["w_out_ab"][e], sink_rows[e], biases, f"_{l}")
            g_band.append(g_band_e)
        else:
            o = l // 2
            (dxn, grads["w_down_c"][o], g_q_norm[o], grads["w_uq_c"][o], g_kv_norm[o], grads["w_ukv_c"][o],
             grads["w_o_c"][o]) = _mla_layer_bwd(dh16, xn, mix, *odd_weights(o), ropes, f"_{l}")
        if l:
            dh, dh16, g_attn_norm[l] = _rms_bwd(h_in, attn_norm[l], dxn, dh, f"attn_norm_bwd{l}", also_bf16=True)
        else:
            dh, g_attn_norm[l] = _rms_bwd(h_in, attn_norm[l], dxn, dh, f"attn_norm_bwd{l}")
    grad_x = dh[None]
    buckets = jnp.concatenate([oh for oh, _ in onehots], axis=1)
    by_variant = _mm(jnp.concatenate(g_band, axis=1), jnp.tile(buckets, (len(g_band), 1)), "nn", [F32],
                     "bias_buckets")[0]
    g_rel_bias = jnp.concatenate([by_variant[lo:hi, v * NUM_BUCKETS:(v + 1) * NUM_BUCKETS].T
                                  for v, (lo, hi) in enumerate(head_cols)], axis=1)

    portions = []
    for a in range(4):
        parts = []
        for n, ax in SHARDED:
            g = jnp.stack(grads[n])
            size = g.shape[ax] // 4
            parts.append(lax.slice_in_dim(g, a * size, (a + 1) * size, axis=ax))
        portions.append(_to_comm_rows(parts, F32))
    reduced = _from_comm_rows(_reduce_scatter(jnp.stack(portions), chip, core), [t.shape for t in shards])
    g_shard = {n: reduced[i] for i, (n, _) in enumerate(SHARDED)}

    small = [jnp.stack(g_attn_norm), jnp.stack(g_mlp_norm), g_final, g_rel_bias, jnp.stack(g_sinks),
             jnp.stack(g_q_norm), jnp.stack(g_kv_norm), loss_part.reshape(1)]
    small_shapes = [t.shape for t in small]
    summed = _from_comm_rows(_all_reduce_small(_to_comm_rows(small, F32)[:16], dev), small_shapes)
    loss = summed[7][0]
    g_small = dict(zip(("attn_norm", "mlp_norm", "final_norm", "rel_bias", "sinks"), summed[:5]))
    for n, g in zip(SHARDED_NORMS, summed[5:7]):
        size = g.shape[1] // 4
        g_shard[n] = lax.dynamic_slice_in_dim(g, chip * size, size, axis=1)

    order = ["rel_bias", "attn_norm", "mlp_norm", "final_norm", "w_in_ab", "sinks", "w_out_ab", "w_down_c",
             "q_norm_c", "w_uq_c", "kv_norm_c", "w_ukv_c", "w_o_c", "w_mlp_up", "w_mlp_down"]
    g_all = {**g_small, **g_shard}
    deltas, new_m, new_v = [], [], []
    for n in order:
        d, mn, vn = _adamw(given[n], g_all[n], given["m_" + n], given["v_" + n], f"adamw_{n}")
        deltas.append(d)
        new_m.append(mn)
        new_v.append(vn)
    return (loss, grad_x, *[g_all[n] for n in order], *deltas, *new_m, *new_v)
```

```python
import math

import jax
import jax.numpy as jnp
from jax import lax
from jax.experimental import pallas as pl
from jax.experimental.pallas import tpu as pltpu

F32 = jnp.float32
BF16 = jnp.bfloat16
MESH = pl.DeviceIdType.MESH
ALL_AXES = ("x", "y", "c")

EPS = 1e-6
NEG = -1e30
BLOCK = 128
HEAD_DIM = 64
A_Q_HEADS, A_KV_HEADS = 8, 2
A_WINDOW = 128
B_BRANCHES = ((128, 1), (512, 4), (2048, 16))
B_HEADS_PER_BRANCH = 4
NUM_BUCKETS, MAX_DISTANCE = 32, 2048
A_IN = (A_Q_HEADS + 2 * A_KV_HEADS) * HEAD_DIM
C_HEADS, C_NOPE, C_ROPE, C_V = 8, 64, 32, 64
C_Q_RANK, C_KV_RANK = 384, 256
ROPE_THETA = 10000.0
ADAM_LR, ADAM_B1, ADAM_B2, ADAM_EPS, ADAM_WD, ADAM_STEP = 0.001, 0.9, 0.999, 1e-08, 0.01, 10

V7X_VMEM_LIMIT_BYTES = 56 * 1024 * 1024
LANES = 128
COMM_COLS = 1024
COMM_ROW_ALIGN = 1024

NT = (((1,), (1,)), ((), ()))
NN = (((1,), (0,)), ((), ()))
TN = (((0,), (0,)), ((), ()))
DIMS = {"nn": NN, "nt": NT, "tn": TN}


def _params(sem):
    return pltpu.CompilerParams(dimension_semantics=sem, vmem_limit_bytes=V7X_VMEM_LIMIT_BYTES)


def _divisor_tile(n, limit, align):
    if n <= limit:
        return n
    t = (limit // align) * align
    while t >= align:
        if n % t == 0:
            return t
        t -= align
    return n


def _ew(fn, ins, outs, name, acc_outs=(), target_bytes=12 << 20):
    rows = max(a.shape[0] for a in ins)

    def vmem_row_bytes(cols, dtype):
        return -(-cols // LANES) * LANES * jnp.dtype(dtype).itemsize

    per_row = sum(vmem_row_bytes(a.shape[1], a.dtype) for a in ins if a.shape[0] == rows)
    per_row += sum(vmem_row_bytes(c, d) for c, d in outs)
    tr = _divisor_tile(rows, max(16, target_bytes // max(per_row, 1)), 16)
    n_in, n_out = len(ins), len(outs)

    def body(*refs):
        res = fn(*[r[...] for r in refs[:n_in]])
        for r, v in zip(refs[n_in:n_in + n_out], res[:n_out]):
            r[...] = v.astype(r.dtype)
        if acc_outs:
            acc_refs = refs[n_in + n_out:]

            @pl.when(pl.program_id(0) == 0)
            def _():
                for r in acc_refs:
                    r[...] = jnp.zeros_like(r)

            for r, v in zip(acc_refs, res[n_out:]):
                r[...] += v

    def spec(a):
        if a.shape[0] == rows:
            return pl.BlockSpec((tr, a.shape[1]), lambda i: (i, 0))
        return pl.BlockSpec((1, a.shape[1]), lambda i: (0, 0))

    out_shape = [jax.ShapeDtypeStruct((rows, c), d) for c, d in outs]
    out_shape += [jax.ShapeDtypeStruct((1, c), F32) for c in acc_outs]
    out_specs = [pl.BlockSpec((tr, c), lambda i: (i, 0)) for c, _ in outs]
    out_specs += [pl.BlockSpec((1, c), lambda i: (0, 0)) for c in acc_outs]
    return pl.pallas_call(
        body, name=name, grid=(rows // tr,), in_specs=[spec(a) for a in ins], out_specs=out_specs,
        out_shape=out_shape, compiler_params=_params(("arbitrary",)),
    )(*ins)


def _rms_fwd(x, g, name):
    def fn(xv, gv):
        return ((xv * lax.rsqrt(jnp.mean(xv * xv, axis=-1, keepdims=True) + EPS)) * gv,)

    return _ew(fn, [x, g.reshape(1, -1)], [(x.shape[1], BF16)], name)[0]


def _rms_bwd(x, g, dy, add, name, also_bf16=False):
    def fn(xv, gv, dyv, *rest):
        rstd = lax.rsqrt(jnp.mean(xv * xv, axis=-1, keepdims=True) + EPS)
        xh = xv * rstd
        dyg = dyv.astype(F32) * gv
        dx = rstd * (dyg - xh * jnp.mean(dyg * xh, axis=-1, keepdims=True))
        if rest:
            dx = dx + rest[0]
        return (dx,) * (2 if also_bf16 else 1) + (jnp.sum(dyv.astype(F32) * xh, axis=0, keepdims=True),)

    ins = [x, g.reshape(1, -1), dy] + ([] if add is None else [add])
    outs = [(x.shape[1], F32)] + ([(x.shape[1], BF16)] if also_bf16 else [])
    *dx, dg = _ew(fn, ins, outs, name, acc_outs=(x.shape[1],))
    return (*dx, dg[0])


def _final_loss(h, g, target, name):
    d = h.shape[1]

    def fn(xv, gv, tv):
        rstd = lax.rsqrt(jnp.mean(xv * xv, axis=-1, keepdims=True) + EPS)
        xh = xv * rstd
        err = xh * gv - tv
        part = 0.5 * jnp.sum(jnp.mean(err * err, axis=-1, keepdims=True), axis=0, keepdims=True)
        dy = err * (1.0 / d)
        dyg = dy * gv
        dx = rstd * (dyg - xh * jnp.mean(dyg * xh, axis=-1, keepdims=True))
        return dx, dx, jnp.broadcast_to(part, (1, LANES)), jnp.sum(dy * xh, axis=0, keepdims=True)

    dx, dx16, loss, dg = _ew(fn, [h, g.reshape(1, -1), target], [(d, F32), (d, BF16)], name, acc_outs=(LANES, d))
    return loss[0, 0], dx, dx16, dg[0]


def _adamw(w, g, m, v, name):
    def fn(wv, gv, mv, vv):
        mn = ADAM_B1 * mv + (1.0 - ADAM_B1) * gv
        vn = ADAM_B2 * vv + (1.0 - ADAM_B2) * jnp.square(gv)
        m_hat = mn / (1.0 - ADAM_B1 ** ADAM_STEP)
        v_hat = vn / (1.0 - ADAM_B2 ** ADAM_STEP)
        return -ADAM_LR * (m_hat / (jnp.sqrt(v_hat) + ADAM_EPS) + ADAM_WD * wv), mn, vn

    shape = w.shape
    cols = shape[-1] if w.ndim > 1 else w.size
    view = [t.reshape(-1, cols) for t in (w, g, m, v)]
    return [t.reshape(shape) for t in _ew(fn, view, [(cols, F32)] * 3, name)]


MM_VMEM_BUDGET_BYTES = 40 << 20


def _mm(a, b, dims, outs, name, epi=None, extras=(), tm=2048, tn=1024, tk=1024):
    if dims == "nn":
        (m, k), n = a.shape, b.shape[1]
    elif dims == "nt":
        (m, k), n = a.shape, b.shape[0]
    else:
        (k, m), n = a.shape, b.shape[1]
    tn, tk = _divisor_tile(n, tn, LANES), _divisor_tile(k, tk, LANES)

    def tile_bytes(rows):
        per_out = sum(jnp.dtype(d).itemsize for d in outs) + sum(e.dtype.itemsize for e in extras)
        return 2 * (rows * tk * a.dtype.itemsize + tk * tn * b.dtype.itemsize + rows * tn * per_out) + 4 * rows * tn

    tm = _divisor_tile(m, tm, LANES)
    while tile_bytes(tm) > MM_VMEM_BUDGET_BYTES and tm % (2 * LANES) == 0:
        tm //= 2
    nk = k // tk
    n_ex, n_out = len(extras), len(outs)

    def body(a_ref, b_ref, *rest):
        ex_refs, out_refs = rest[:n_ex], rest[n_ex:n_ex + n_out]

        def finish(acc):
            res = epi(acc, *[r[...] for r in ex_refs]) if epi else (acc,)
            for r, v in zip(out_refs, res):
                r[...] = v.astype(r.dtype)

        part = lax.dot_general(a_ref[...].astype(BF16), b_ref[...].astype(BF16), DIMS[dims],
                               preferred_element_type=F32)
        if nk == 1:
            finish(part)
        else:
            acc_ref = rest[-1]
            kk = pl.program_id(2)

            @pl.when(kk == 0)
            def _():
                acc_ref[...] = part

            @pl.when(kk > 0)
            def _():
                acc_ref[...] += part

            @pl.when(kk == nk - 1)
            def _():
                finish(acc_ref[...])

    if dims == "nn":
        a_spec = pl.BlockSpec((tm, tk), lambda i, j, kk: (i, kk))
        b_spec = pl.BlockSpec((tk, tn), lambda i, j, kk: (kk, j))
    elif dims == "nt":
        a_spec = pl.BlockSpec((tm, tk), lambda i, j, kk: (i, kk))
        b_spec = pl.BlockSpec((tn, tk), lambda i, j, kk: (j, kk))
    else:
        a_spec = pl.BlockSpec((tk, tm), lambda i, j, kk: (kk, i))
        b_spec = pl.BlockSpec((tk, tn), lambda i, j, kk: (kk, j))
    tile = pl.BlockSpec((tm, tn), lambda i, j, kk: (i, j))
    return pl.pallas_call(
        body, name=name, grid=(m // tm, n // tn, nk),
        in_specs=[a_spec, b_spec] + [tile] * n_ex, out_specs=[tile] * n_out,
        out_shape=[jax.ShapeDtypeStruct((m, n), d) for d in outs],
        scratch_shapes=[pltpu.VMEM((tm, tn), F32)] if nk > 1 else [],
        compiler_params=_params(("parallel", "parallel", "arbitrary")),
    )(a, b, *extras)


def _add_epi(acc, res):
    return (acc + res,)


def _relu2_epi(acc):
    r = jnp.maximum(acc, 0.0)
    return r * r, r


def _relu2_bwd_epi(acc, r):
    return (acc * (2.0 * r.astype(F32)),)


BAND_SPAN_MAX = 256


def _band_geometry(t, blocks_per_seq, span):
    rows = min(1024, blocks_per_seq * BLOCK)
    nb = rows // BLOCK
    assert blocks_per_seq % nb == 0 and t % rows == 0 and span <= BAND_SPAN_MAX
    return rows, nb, t // rows, min(span, rows)


def _span_bias(bias, span):
    n = span // BLOCK
    neg = jnp.full(bias.shape[:2] + (BLOCK,), NEG, F32)
    rows = [jnp.concatenate([neg] * a + [bias[:, :, :BLOCK], bias[:, :, BLOCK:]] + [neg] * (n - 1 - a), axis=2)
            for a in range(n)]
    return jnp.concatenate(rows, axis=1)


def _fold_span_bias_grad(dbias, span):
    n = span // BLOCK
    parts = [dbias[:, a * BLOCK:(a + 1) * BLOCK, a * BLOCK:(a + 2) * BLOCK] for a in range(n)]
    return sum(parts[1:], parts[0])


def _band_logits(qj, kk, bias, first):
    s = lax.dot_general(qj, kk, NT, preferred_element_type=F32) * (HEAD_DIM ** -0.5) + bias
    if first is not None:
        col = lax.broadcasted_iota(jnp.int32, s.shape, 1)
        s = jnp.where(col < jnp.where(first, BLOCK, 0), NEG, s)
    return s


def _band_fwd(q, k, v, bias, blocks_per_seq, span, name):
    hq, t, dh = q.shape
    group = hq // k.shape[0]
    rows, nb, nchunks, span = _band_geometry(t, blocks_per_seq, span)

    def body(q_ref, kc_ref, kp_ref, vc_ref, vp_ref, bias_ref, o_ref, lse_ref):
        i = pl.program_id(1)
        bias_v = bias_ref[0]
        for j in range(rows // span):
            cur = slice(j * span, (j + 1) * span)
            prev = slice(j * span - BLOCK, j * span)
            kk = jnp.concatenate([kp_ref[0] if j == 0 else kc_ref[0, prev, :], kc_ref[0, cur, :]], axis=0)
            vv = jnp.concatenate([vp_ref[0] if j == 0 else vc_ref[0, prev, :], vc_ref[0, cur, :]], axis=0)
            first = lax.rem(i * nb, blocks_per_seq) == 0 if j == 0 else None
            s = _band_logits(q_ref[0, cur, :], kk, bias_v, first)
            m = jnp.max(s, axis=1, keepdims=True)
            p = jnp.exp(s - m)
            l = jnp.sum(p, axis=1, keepdims=True)
            acc = jnp.dot(p.astype(BF16), vv, preferred_element_type=F32)
            o_ref[0, cur, :] = acc / l
            lse_ref[0, cur, :] = jnp.broadcast_to(m + jnp.log(l), (span, dh))

    cur_q = pl.BlockSpec((1, rows, dh), lambda h, i: (h, i, 0))
    cur_kv = pl.BlockSpec((1, rows, dh), lambda h, i: (h // group, i, 0))
    prev_kv = pl.BlockSpec((1, BLOCK, dh), lambda h, i: (h // group, jnp.maximum(i * nb - 1, 0), 0))
    in_specs = [cur_q, cur_kv, prev_kv, cur_kv, prev_kv, pl.BlockSpec((1, span, span + BLOCK), lambda h, i: (h, 0, 0))]
    return pl.pallas_call(
        body, name=name, grid=(hq, nchunks), in_specs=in_specs,
        out_specs=[cur_q, cur_q],
        out_shape=[jax.ShapeDtypeStruct((hq, t, dh), F32), jax.ShapeDtypeStruct((hq, t, dh), F32)],
        compiler_params=_params(("parallel", "arbitrary")),
    )(q, k, k, v, v, _span_bias(bias, span))


def _sink_lse(lse, sink):
    m = jnp.maximum(lse, sink)
    return m + jnp.log(jnp.exp(lse - m) + jnp.exp(sink - m))


def _merge_with_sinks(o, lse, sinks, name):
    h, t, w = o.shape
    tr = _layout_rows(t, 3 * h * LANES * 4)

    def body(o_ref, lse_ref, sink_ref, out_ref):
        for j in range(h):
            lse_j = lse_ref[j]
            shrink = jnp.exp(lse_j - _sink_lse(lse_j, sink_ref[j, :1, :w]))
            out_ref[:, j * w:(j + 1) * w] = (o_ref[j] * shrink).astype(BF16)

    heads = pl.BlockSpec((h, tr, w), lambda i: (0, i, 0))
    return pl.pallas_call(
        body, name=name, grid=(t // tr,),
        in_specs=[heads, heads, pl.BlockSpec((h, BAND_SPAN_MAX, LANES), lambda i: (0, 0, 0))],
        out_specs=pl.BlockSpec((tr, h * w), lambda i: (i, 0)), out_shape=jax.ShapeDtypeStruct((t, h * w), BF16),
        compiler_params=_params(("parallel",)),
    )(o, lse, sinks)


def _band_bwd(q, k, v, o, lse, do, bias, sinks, blocks_per_seq, span, name):
    hq, t, dh = q.shape
    group = hq // k.shape[0]
    rows, nb, nchunks, span = _band_geometry(t, blocks_per_seq, span)
    per_span = span // BLOCK
    has_sink = sinks is not None
    scale = HEAD_DIM ** -0.5

    def body(q_ref, kc_ref, kp_ref, vc_ref, vp_ref, o_ref, lse_ref, do_ref, bias_ref, *rest):
        dq_ref, dk_ref, dv_ref, dbias_ref, dsink_ref, dk_carry, dv_carry = rest[-7:]
        step = pl.program_id(1)
        chunk = nchunks - 1 - step
        bias_v = bias_ref[0]
        sink = rest[0][0, :span, :1] if has_sink else None

        @pl.when(step == 0)
        def _():
            dk_carry[...] = jnp.zeros_like(dk_carry)
            dv_carry[...] = jnp.zeros_like(dv_carry)
            dbias_ref[...] = jnp.zeros_like(dbias_ref)
            dsink_ref[...] = jnp.zeros_like(dsink_ref)

        dks = [jnp.zeros((BLOCK, dh), F32) for _ in range(nb + 1)]
        dvs = [jnp.zeros((BLOCK, dh), F32) for _ in range(nb + 1)]
        dks[nb] = dk_carry[...]
        dvs[nb] = dv_carry[...]
        for j in range(rows // span - 1, -1, -1):
            cur = slice(j * span, (j + 1) * span)
            prev = slice(j * span - BLOCK, j * span)
            kk = jnp.concatenate([kp_ref[0] if j == 0 else kc_ref[0, prev, :], kc_ref[0, cur, :]], axis=0)
            vv = jnp.concatenate([vp_ref[0] if j == 0 else vc_ref[0, prev, :], vc_ref[0, cur, :]], axis=0)
            first = lax.rem(chunk * nb, blocks_per_seq) == 0 if j == 0 else None
            qj, doj = q_ref[0, cur, :], do_ref[0, cur, :]
            lse_j = lse_ref[0, cur, :][:, :1]
            delta = jnp.sum(doj.astype(F32) * o_ref[0, cur, :], axis=1, keepdims=True)
            if has_sink:
                with_sink = _sink_lse(lse_j, sink)
                delta = delta * jnp.exp(lse_j - with_sink)
                lse_j = with_sink
            p = jnp.exp(_band_logits(qj, kk, bias_v, first) - lse_j)
            dp = lax.dot_general(doj, vv, NT, preferred_element_type=F32)
            ds = p * (dp - delta)
            dbias_ref[0] += ds
            if has_sink:
                dsink = -jnp.sum(jnp.exp(sink - lse_j) * delta, axis=0, keepdims=True)
                dsink_ref[0] += jnp.broadcast_to(dsink, (1, LANES))
            dsb = (ds * scale).astype(BF16)
            dq_ref[0, cur, :] = jnp.dot(dsb, kk, preferred_element_type=F32)
            dkk = lax.dot_general(dsb, qj, TN, preferred_element_type=F32)
            dvv = lax.dot_general(p.astype(BF16), doj, TN, preferred_element_type=F32)
            for b in range(per_span + 1):
                piece = slice(b * BLOCK, (b + 1) * BLOCK)
                dks[j * per_span + b] += dkk[piece]
                dvs[j * per_span + b] += dvv[piece]
        for j in range(nb):
            cur = slice(j * BLOCK, (j + 1) * BLOCK)
            dk_ref[0, cur, :] = dks[j + 1]
            dv_ref[0, cur, :] = dvs[j + 1]
        dk_carry[...] = dks[0]
        dv_carry[...] = dvs[0]

    def rev(i):
        return nchunks - 1 - i

    cur_q = pl.BlockSpec((1, rows, dh), lambda h, i: (h, rev(i), 0))
    cur_kv = pl.BlockSpec((1, rows, dh), lambda h, i: (h // group, rev(i), 0))
    prev_kv = pl.BlockSpec((1, BLOCK, dh), lambda h, i: (h // group, jnp.maximum(rev(i) * nb - 1, 0), 0))
    cur_lse = cur_q
    per_head_bias = pl.BlockSpec((1, span, span + BLOCK), lambda h, i: (h, 0, 0))
    per_head_row = pl.BlockSpec((1, 1, LANES), lambda h, i: (h, 0, 0))
    in_specs = [cur_q, cur_kv, prev_kv, cur_kv, prev_kv, cur_q, cur_lse, cur_q, per_head_bias]
    ins = [q, k, k, v, v, o, lse, do, _span_bias(bias, span)]
    if has_sink:
        in_specs.append(pl.BlockSpec((1, BAND_SPAN_MAX, LANES), lambda h, i: (h, 0, 0)))
        ins.append(sinks)
    full = jax.ShapeDtypeStruct((hq, t, dh), F32)
    dq, dk, dv, dbias, dsink = pl.pallas_call(
        body, name=name, grid=(hq, nchunks), in_specs=in_specs,
        out_specs=[cur_q, cur_q, cur_q, per_head_bias, per_head_row],
        out_shape=[full, full, full, jax.ShapeDtypeStruct((hq, span, span + BLOCK), F32),
                   jax.ShapeDtypeStruct((hq, 1, LANES), F32)],
        scratch_shapes=[pltpu.VMEM((BLOCK, dh), F32), pltpu.VMEM((BLOCK, dh), F32)],
        compiler_params=_params(("parallel", "arbitrary")),
    )(*ins)
    return dq, dk, dv, _fold_span_bias_grad(dbias, span), dsink


C_QK = C_NOPE + C_ROPE
C_SCALE = C_QK ** -0.5
LOG2E = math.log2(math.e)
C_EXP2 = C_SCALE * LOG2E
CAUSAL_SUB = 256


def _causal_tile(t, forward=False):
    return min(4096 if forward else 2048, t)


CAUSAL_Q_CHAIN = 128
CAUSAL_K_CHAIN = 256
STAT_ROWS = 8


def _mla_fwd(q, k, v1t, name):
    h, t, _ = q.shape
    tq = _causal_tile(t, forward=True)
    n = t // tq
    qs, ks = min(CAUSAL_Q_CHAIN, tq), min(CAUSAL_K_CHAIN, tq)

    def body(q_ref, k_ref, v_ref, ot_ref, lse_ref, m_scr, acc_scr):
        qi, ki = pl.program_id(1), pl.program_id(2)

        @pl.when(ki == 0)
        def _():
            m_scr[...] = jnp.full_like(m_scr, NEG)
            acc_scr[...] = jnp.zeros_like(acc_scr)

        def tile(diagonal):
            for r in range(tq // qs):
                cols = slice(r * qs, (r + 1) * qs)
                q_sub = q_ref[0, cols, :]
                m, acc = m_scr[:1, cols], acc_scr[:, cols]
                for kc in range(tq // ks):
                    k0 = kc * ks
                    if diagonal and k0 > r * qs + qs - 1:
                        continue
                    st = lax.dot_general(k_ref[0, k0:k0 + ks, :], q_sub, NT, preferred_element_type=F32)
                    if diagonal and k0 + ks - 1 > r * qs:
                        kpos = k0 + lax.broadcasted_iota(jnp.int32, st.shape, 0)
                        qpos = r * qs + lax.broadcasted_iota(jnp.int32, st.shape, 1)
                        st = jnp.where(kpos <= qpos, st, NEG)
                    m_new = jnp.maximum(m, jnp.max(st, axis=0, keepdims=True))
                    alpha = jnp.exp2((m - m_new) * C_EXP2)
                    pt = jnp.exp2((st - m_new) * C_EXP2).astype(BF16)
                    acc = acc * alpha + jnp.dot(v_ref[0, :, k0:k0 + ks], pt, preferred_element_type=F32)
                    m = m_new
                m_scr[:, cols] = jnp.broadcast_to(m, (STAT_ROWS, qs))
                acc_scr[:, cols] = acc

        @pl.when(ki < qi)
        def _():
            tile(False)

        @pl.when(ki == qi)
        def _():
            tile(True)
            l = acc_scr[C_V:C_V + 1, :]
            ot_ref[0] = acc_scr[:C_V, :] / l
            lse_ref[0] = jnp.broadcast_to(m_scr[:1, :] * C_SCALE + jnp.log(l), (STAT_ROWS, tq))

    return pl.pallas_call(
        body, name=name, grid=(h, n, n),
        in_specs=[pl.BlockSpec((1, tq, C_QK), lambda hh, qi, ki: (hh, qi, 0)),
                  pl.BlockSpec((1, tq, C_QK), lambda hh, qi, ki: (hh, jnp.minimum(ki, qi), 0)),
                  pl.BlockSpec((1, LANES, tq), lambda hh, qi, ki: (hh, 0, jnp.minimum(ki, qi)))],
        out_specs=[pl.BlockSpec((1, C_V, tq), lambda hh, qi, ki: (hh, 0, qi)),
                   pl.BlockSpec((1, STAT_ROWS, tq), lambda hh, qi, ki: (hh, 0, qi))],
        out_shape=[jax.ShapeDtypeStruct((h, C_V, t), F32), jax.ShapeDtypeStruct((h, STAT_ROWS, t), F32)],
        scratch_shapes=[pltpu.VMEM((STAT_ROWS, tq), F32), pltpu.VMEM((LANES, tq), F32)],
        compiler_params=_params(("parallel", "arbitrary", "arbitrary")),
    )(q, k, v1t)


def _mla_bwd(q, k, v1, o, lse, do, name):
    h, t, _ = q.shape
    tq = _causal_tile(t)
    n = t // tq
    sub = min(CAUSAL_SUB, tq)

    def body(q_ref, k_ref, v_ref, o_ref, lse_ref, do_ref, dqt_ref, dk_ref, dv_ref, dk_acc, dv_acc):
        ki, qi = pl.program_id(1), pl.program_id(2)

        @pl.when(qi == 0)
        def _():
            dk_acc[...] = jnp.zeros_like(dk_acc)
            dv_acc[...] = jnp.zeros_like(dv_acc)

        @pl.when(jnp.logical_and(ki == 0, qi == 0))
        def _():
            dqt_ref[...] = jnp.zeros_like(dqt_ref)

        def tile(diagonal):
            for c in range(tq // sub):
                cols = slice(c * sub, (c + 1) * sub)
                nk = (c + 1) * sub if diagonal else tq
                qc, doc = q_ref[0, cols, :], do_ref[0, cols, :]
                st = lax.dot_general(k_ref[0, :nk, :], qc, NT, preferred_element_type=F32)
                lse2 = lse_ref[0, :1, cols] * LOG2E
                pt = jnp.exp2(st * C_EXP2 - lse2)
                if diagonal:
                    kpos = lax.broadcasted_iota(jnp.int32, st.shape, 0)
                    qpos = c * sub + lax.broadcasted_iota(jnp.int32, st.shape, 1)
                    pt = jnp.where(kpos <= qpos, pt, 0.0)
                dpt = lax.dot_general(v_ref[0, :nk, :C_V], doc, NT, preferred_element_type=F32)
                delta = jnp.sum(doc.astype(F32) * o_ref[0, cols, :], axis=1, keepdims=True)
                delta_row = jnp.transpose(jnp.broadcast_to(delta, (sub, LANES)))[:1]
                dst = (pt * (dpt - delta_row)).astype(BF16)
                dv_acc[:nk, :] += jnp.dot(pt.astype(BF16), doc, preferred_element_type=F32)
                dk_acc[:nk, :] += jnp.dot(dst, qc, preferred_element_type=F32)
                out_cols = pl.ds(pl.multiple_of(qi * tq + c * sub, sub), sub)
                dqt_ref[0, :, out_cols] += lax.dot_general(k_ref[0, :nk, :], dst, TN,
                                                           preferred_element_type=F32) * C_SCALE

        @pl.when(qi > ki)
        def _():
            tile(False)

        @pl.when(qi == ki)
        def _():
            tile(True)

        @pl.when(qi == n - 1)
        def _():
            dk_ref[0] = dk_acc[...] * C_SCALE
            dv_ref[0] = dv_acc[...]

    def q_spec(d):
        return pl.BlockSpec((1, tq, d), lambda hh, ki, qi: (hh, jnp.maximum(qi, ki), 0))

    def k_spec(d):
        return pl.BlockSpec((1, tq, d), lambda hh, ki, qi: (hh, ki, 0))

    return pl.pallas_call(
        body, name=name, grid=(h, n, n),
        in_specs=[q_spec(C_QK), k_spec(C_QK), k_spec(LANES), q_spec(C_V),
                  pl.BlockSpec((1, STAT_ROWS, tq), lambda hh, ki, qi: (hh, 0, jnp.maximum(qi, ki))), q_spec(C_V)],
        out_specs=[pl.BlockSpec((1, C_QK, t), lambda hh, ki, qi: (hh, 0, 0)), k_spec(C_QK), k_spec(C_V)],
        out_shape=[jax.ShapeDtypeStruct((h, C_QK, t), F32), jax.ShapeDtypeStruct((h, t, C_QK), F32),
                   jax.ShapeDtypeStruct((h, t, C_V), F32)],
        scratch_shapes=[pltpu.VMEM((tq, C_QK), F32), pltpu.VMEM((tq, C_V), F32)],
        compiler_params=_params(("parallel", "arbitrary", "arbitrary")),
    )(q, k, v1, o, lse, do)


def _exchange(src, flips, src_idx, name):
    n = len(flips)
    _, r, c = src.shape

    def body(src_ref, dst_ref, send_sems, recv_sems):
        me = [lax.axis_index(a) for a in ALL_AXES]
        copies = []
        for kk, flip in enumerate(flips):
            peer = tuple(1 - p if f else p for p, f in zip(me, flip))
            copies.append(pltpu.make_async_remote_copy(
                src_ref=src_ref.at[src_idx[kk]], dst_ref=dst_ref.at[kk], send_sem=send_sems.at[kk],
                recv_sem=recv_sems.at[kk], device_id=peer, device_id_type=MESH))
        for cp in copies:
            cp.start()
        for cp in copies:
            cp.wait_recv()
        for cp in copies:
            cp.wait_send()

    return pl.pallas_call(
        body, name=name, in_specs=[pl.BlockSpec(memory_space=pl.ANY)], out_specs=pl.BlockSpec(memory_space=pl.ANY),
        out_shape=jax.ShapeDtypeStruct((n, r, c), src.dtype),
        scratch_shapes=[pltpu.SemaphoreType.DMA((n,)), pltpu.SemaphoreType.DMA((n,))],
    )(src)


FLIP_C = (0, 0, 1)
CHIP_FLIPS = ((0, 1, 0), (1, 0, 0), (1, 1, 0))
ALL_FLIPS = tuple((a >> 2 & 1, a >> 1 & 1, a & 1) for a in range(1, 8))


def _pick(stacked, idx):
    return lax.dynamic_index_in_dim(stacked, idx, axis=0, keepdims=False)


def _to_comm_rows(parts, dtype):
    flat = jnp.concatenate([p.reshape(-1) for p in parts]).astype(dtype)
    rows = -(-flat.size // (COMM_COLS * COMM_ROW_ALIGN)) * COMM_ROW_ALIGN
    return jnp.pad(flat, (0, rows * COMM_COLS - flat.size)).reshape(rows, COMM_COLS)


def _from_comm_rows(buf, shapes):
    flat, out, off = buf.reshape(-1), [], 0
    for s in shapes:
        size = math.prod(s)
        out.append(flat[off:off + size].reshape(s))
        off += size
    return out


def _my_place():
    x, y, c = (lax.axis_index(a) for a in ALL_AXES)
    return (x, y, c), 2 * x + y


def _flipped(me, flip):
    return tuple(1 - p if f else p for p, f in zip(me, flip))


def _remote(src, dst, sems, k, peer):
    send_sems, recv_sems = sems
    return pltpu.make_async_remote_copy(src_ref=src, dst_ref=dst, send_sem=send_sems.at[k], recv_sem=recv_sems.at[k],
                                        device_id=peer, device_id_type=MESH)


def _gather_shards(buf, chip):
    rows, cols = buf.shape
    half = rows // 2
    n = len(CHIP_FLIPS)

    def body(src_ref, out_ref, send_sems, recv_sems):
        me, chip = _my_place()
        sems = (send_sems, recv_sems)
        sibling = _flipped(me, FLIP_C)
        mine = pl.ds(pl.multiple_of(me[2] * half, half), half)
        theirs = pl.ds(pl.multiple_of((1 - me[2]) * half, half), half)
        peers = [_flipped(me, f) for f in CHIP_FLIPS]
        from_chip = [2 * p[0] + p[1] for p in peers]
        over_ici = [_remote(src_ref.at[mine], out_ref.at[chip, mine], sems, k, peers[k]) for k in range(n)]
        for cp in over_ici:
            cp.start()
        passed = [_remote(out_ref.at[from_chip[k], mine], out_ref.at[from_chip[k], mine], sems, n + k, sibling)
                  for k in range(n)]
        for k in range(n):
            _remote(src_ref.at[mine], out_ref.at[from_chip[k], mine], sems, k, peers[k]).wait_recv()
            passed[k].start()
        for k in range(n):
            _remote(out_ref.at[from_chip[k], theirs], out_ref.at[from_chip[k], theirs], sems, n + k, sibling).wait_recv()
        for cp in over_ici + passed:
            cp.wait_send()

    others = pl.pallas_call(
        body, name="gather_shards", in_specs=[pl.BlockSpec(memory_space=pl.ANY)],
        out_specs=pl.BlockSpec(memory_space=pl.ANY), out_shape=jax.ShapeDtypeStruct((4, rows, cols), buf.dtype),
        scratch_shapes=[pltpu.SemaphoreType.DMA((2 * n,)), pltpu.SemaphoreType.DMA((2 * n,))],
    )(buf)
    return lax.dynamic_update_slice(others, buf[None], (chip, 0, 0))


def _swap_other_halves(portions):
    _, rows, cols = portions.shape
    half = rows // 2

    def body(src_ref, dst_ref, send_sems, recv_sems):
        me, _ = _my_place()
        theirs = pl.ds(pl.multiple_of((1 - me[2]) * half, half), half)
        cp = _remote(src_ref.at[:, theirs], dst_ref, (send_sems, recv_sems), 0, _flipped(me, FLIP_C))
        cp.start()
        cp.wait_recv()
        cp.wait_send()

    return pl.pallas_call(
        body, name="reduce_d2d", in_specs=[pl.BlockSpec(memory_space=pl.ANY)],
        out_specs=pl.BlockSpec(memory_space=pl.ANY), out_shape=jax.ShapeDtypeStruct((4, half, cols), portions.dtype),
        scratch_shapes=[pltpu.SemaphoreType.DMA((1,)), pltpu.SemaphoreType.DMA((1,))],
    )(portions)


def _join_halves(total, core):
    half, cols = total.shape

    def body(src_ref, out_ref, send_sems, recv_sems):
        me, _ = _my_place()
        mine = pl.ds(pl.multiple_of(me[2] * half, half), half)
        theirs = pl.ds(pl.multiple_of((1 - me[2]) * half, half), half)
        sems = (send_sems, recv_sems)
        cp = _remote(src_ref, out_ref.at[mine], sems, 0, _flipped(me, FLIP_C))
        cp.start()
        _remote(src_ref, out_ref.at[theirs], sems, 0, _flipped(me, FLIP_C)).wait_recv()
        cp.wait_send()

    from_sibling = pl.pallas_call(
        body, name="reduce_share", in_specs=[pl.BlockSpec(memory_space=pl.ANY)],
        out_specs=pl.BlockSpec(memory_space=pl.ANY), out_shape=jax.ShapeDtypeStruct((2 * half, cols), total.dtype),
        scratch_shapes=[pltpu.SemaphoreType.DMA((1,)), pltpu.SemaphoreType.DMA((1,))],
    )(total)
    return lax.dynamic_update_slice(from_sibling, total, (core * half, 0))


def _reduce_scatter(portions, chip, core):
    half = portions.shape[1] // 2
    keep = lax.dynamic_slice_in_dim(portions, core * half, half, axis=1)
    got = _swap_other_halves(portions)
    pair = _ew(lambda p, q: (p + q,), [keep.reshape(4 * half, -1), got.reshape(4 * half, -1)],
               [(COMM_COLS, F32)], "reduce_pair_sum")[0].reshape(keep.shape)
    out = jnp.stack([_pick(pair, jnp.bitwise_xor(chip, f)) for f in (1, 2, 3)]).astype(BF16)
    others = _exchange(out, CHIP_FLIPS, (0, 1, 2), "reduce_ici")
    total = _ew(lambda p, q, r, s: (p + q.astype(F32) + r.astype(F32) + s.astype(F32),),
                [_pick(pair, chip), others[0], others[1], others[2]],
                [(COMM_COLS, F32)], "reduce_chip_sum")[0]
    return _join_halves(total, core)


def _all_reduce_small(buf, dev):
    got = _exchange(buf[None], ALL_FLIPS, (0,) * 7, "small_gather")
    by_flip = jnp.concatenate([buf[None], got])
    ordered = [_pick(by_flip, jnp.bitwise_xor(dev, a)) for a in range(8)]

    def fn(*t):
        s = t[0]
        for u in t[1:]:
            s = s + u
        return (s,)

    return _ew(fn, ordered, [(buf.shape[1], F32)], "small_sum")[0]


def _band_bucket_onehot(dilation, max_dist):
    i = jnp.arange(BLOCK)[:, None]
    j = jnp.arange(2 * BLOCK)[None, :]
    dist = i + BLOCK - j
    inband = (dist >= 0) & (dist <= max_dist)
    n = jnp.maximum(dist, 0) * dilation
    max_exact = NUM_BUCKETS // 2
    nf = jnp.maximum(n, 1).astype(F32)
    large = max_exact + (jnp.log(nf / max_exact) / math.log(MAX_DISTANCE / max_exact)
                         * (NUM_BUCKETS - max_exact)).astype(jnp.int32)
    bucket = jnp.where(n < max_exact, n, jnp.minimum(large, NUM_BUCKETS - 1))
    onehot = (bucket[..., None] == jnp.arange(NUM_BUCKETS)) & inband[..., None]
    return onehot.reshape(-1, NUM_BUCKETS).astype(F32), inband.reshape(-1)


def _band_bias(table, onehot, inband):
    vals = jnp.einsum("pb,bh->hp", onehot, table, precision=lax.Precision.HIGHEST)
    return jnp.where(inband[None, :], vals, NEG).reshape(-1, BLOCK, 2 * BLOCK)


BAND_VARIANTS = ((1, A_WINDOW - 1),) + tuple((dil, window // dil) for window, dil in B_BRANCHES)


LAYOUT_TILE_BYTES = 8 << 20


def _layout_rows(length, row_bytes):
    return _divisor_tile(length, max(16, LAYOUT_TILE_BYTES // row_bytes), 16)


def _split_heads(items, w, dil, name):
    s = items[0][0].shape[0]
    length = s // dil
    row_bytes = sum(-(-n * w // LANES) * LANES * (x.dtype.itemsize + jnp.dtype(d).itemsize) for x, _, n, d in items)
    tr = _layout_rows(length, row_bytes)
    nt = length // tr

    def body(*refs):
        for (x_ref, o_ref), (_, _, n, _) in zip(zip(refs[:len(items)], refs[len(items):]), items):
            for j in range(n):
                o_ref[j] = x_ref[:, j * w:(j + 1) * w].astype(o_ref.dtype)

    in_specs, out_specs, out_shape, views = [], [], [], []
    for x, first, n, d in items:
        bw, width = n * w, x.shape[1]
        assert bw % LANES == 0 and first % bw == 0 and (dil == 1 or width % bw == 0)
        in_specs.append(pl.BlockSpec((tr, bw), lambda r, i, c0=first // bw, wb=width // bw: (i, r * wb + c0)))
        out_specs.append(pl.BlockSpec((n, tr, w), lambda r, i: (0, r * nt + i, 0)))
        out_shape.append(jax.ShapeDtypeStruct((n, s, w), d))
        views.append(x.reshape(length, dil * width))
    return pl.pallas_call(
        body, name=name, grid=(dil, nt), in_specs=in_specs, out_specs=out_specs, out_shape=out_shape,
        compiler_params=_params(("parallel", "parallel")),
    )(*views)


def _merge_heads(items, dil, name, group_sum=1):
    s, w = items[0][0].shape[1:]
    length = s // dil
    row_bytes = sum(t.shape[0] * LANES * t.dtype.itemsize + t.shape[0] * w * jnp.dtype(d).itemsize for t, d in items)
    tr = _layout_rows(length, row_bytes)
    nt = length // tr

    def body(*refs):
        for t_ref, o_ref in zip(refs[:len(items)], refs[len(items):]):
            for j in range(t_ref.shape[0] // group_sum):
                v = t_ref[j * group_sum]
                for g in range(1, group_sum):
                    v = v + t_ref[j * group_sum + g]
                o_ref[:, j * w:(j + 1) * w] = v.astype(o_ref.dtype)

    in_specs, out_specs, out_shape = [], [], []
    for t, d in items:
        n = t.shape[0]
        bw = n // group_sum * w
        assert bw % LANES == 0
        in_specs.append(pl.BlockSpec((n, tr, w), lambda r, i: (0, r * nt + i, 0)))
        out_specs.append(pl.BlockSpec((tr, bw), lambda r, i: (i, r)))
        out_shape.append(jax.ShapeDtypeStruct((length, dil * bw), d))
    outs = pl.pallas_call(
        body, name=name, grid=(dil, nt), in_specs=in_specs, out_specs=out_specs, out_shape=out_shape,
        compiler_params=_params(("parallel", "parallel")),
    )(*[t for t, _ in items])
    return [o.reshape(s, -1) for o in outs]


def _merge_heads_transposed(t, dtype, name, also_heads=False):
    n, w, s = t.shape
    tr = _layout_rows(s, 3 * n * w * t.dtype.itemsize)

    def body(t_ref, o_ref, *heads_ref):
        rows = jnp.transpose(t_ref[...].reshape(n * w, tr))
        o_ref[...] = rows.astype(o_ref.dtype)
        for j in range(n if also_heads else 0):
            heads_ref[0][j] = rows[:, j * w:(j + 1) * w]

    out_specs = [pl.BlockSpec((tr, n * w), lambda i: (i, 0))]
    out_shape = [jax.ShapeDtypeStruct((s, n * w), dtype)]
    if also_heads:
        out_specs.append(pl.BlockSpec((n, tr, w), lambda i: (0, i, 0)))
        out_shape.append(jax.ShapeDtypeStruct((n, s, w), t.dtype))
    outs = pl.pallas_call(
        body, name=name, grid=(s // tr,), in_specs=[pl.BlockSpec((n, w, tr), lambda i: (0, 0, i))],
        out_specs=out_specs, out_shape=out_shape, compiler_params=_params(("parallel",)),
    )(t)
    return outs if also_heads else outs[0]


ROPE_HALF = C_ROPE // 2
ROPE_PERIOD = 3 * LANES


def _rope_tables(s):
    inv = ROPE_THETA ** (-jnp.arange(0, C_ROPE, 2, dtype=F32) / C_ROPE)
    ang = jnp.arange(s, dtype=F32)[:, None] * inv[None, :]
    cos, sin = jnp.cos(ang), jnp.sin(ang)
    one, zero = jnp.ones((s, C_NOPE), F32), jnp.zeros((s, C_NOPE), F32)
    z16 = jnp.zeros((s, ROPE_HALF), F32)
    reps = ROPE_PERIOD // C_QK
    keep = jnp.tile(jnp.concatenate([one, cos, cos], axis=1), (1, reps))
    from_above = jnp.tile(jnp.concatenate([zero, -sin, z16], axis=1), (1, reps))
    from_below = jnp.tile(jnp.concatenate([zero, z16, sin], axis=1), (1, reps))
    return (cos, sin), (keep, from_above, from_below)


def _rope_rows(x, tables, inverse, name, dtype):
    width = x.shape[1]
    reps = width // ROPE_PERIOD
    sign = -1.0 if inverse else 1.0

    def fn(xv, keep, above, below):
        keep, above, below = (jnp.tile(t, (1, reps)) for t in (keep, above, below))
        up = pltpu.roll(xv, width - ROPE_HALF, 1)
        down = pltpu.roll(xv, ROPE_HALF, 1)
        return (xv * keep + sign * (up * above + down * below),)

    return _ew(fn, [x, *tables], [(width, dtype)], name)[0]


def _rotate_half_pairs(a, b, cos, sin, inverse):
    if inverse:
        return a * cos + b * sin, b * cos - a * sin
    return a * cos - b * sin, a * sin + b * cos


def _split_kv(kv, down, cos, sin, name):
    s = kv.shape[0]
    h = kv.shape[1] // (C_NOPE + C_V)
    tr = _layout_rows(s, 8 * kv.shape[1])
    r0 = C_Q_RANK + C_KV_RANK

    def body(kv_ref, down_ref, cos_ref, sin_ref, k_ref, v_ref, vt_ref):
        k1, k2 = _rotate_half_pairs(down_ref[:, r0:r0 + ROPE_HALF], down_ref[:, r0 + ROPE_HALF:r0 + C_ROPE],
                                    cos_ref[...], sin_ref[...], False)
        lane = lax.broadcasted_iota(jnp.int32, (tr, LANES - C_V), 1)
        tail = jnp.where(lane == 0, 1.0, 0.0)
        for j in range(h):
            base = j * (C_NOPE + C_V)
            k_ref[j, :, :C_NOPE] = kv_ref[:, base:base + C_NOPE]
            k_ref[j, :, C_NOPE:C_NOPE + ROPE_HALF] = k1.astype(BF16)
            k_ref[j, :, C_NOPE + ROPE_HALF:] = k2.astype(BF16)
            v1 = jnp.concatenate([kv_ref[:, base + C_NOPE:base + C_NOPE + C_V].astype(F32), tail], axis=1)
            v_ref[j] = v1.astype(BF16)
            vt_ref[j] = jnp.transpose(v1).astype(BF16)

    def rows(width):
        return pl.BlockSpec((tr, width), lambda i: (i, 0))

    return pl.pallas_call(
        body, name=name, grid=(s // tr,),
        in_specs=[rows(kv.shape[1]), rows(down.shape[1]), rows(ROPE_HALF), rows(ROPE_HALF)],
        out_specs=[pl.BlockSpec((h, tr, C_QK), lambda i: (0, i, 0)), pl.BlockSpec((h, tr, LANES), lambda i: (0, i, 0)),
                   pl.BlockSpec((h, LANES, tr), lambda i: (0, 0, i))],
        out_shape=[jax.ShapeDtypeStruct((h, s, C_QK), BF16), jax.ShapeDtypeStruct((h, s, LANES), BF16),
                   jax.ShapeDtypeStruct((h, LANES, s), BF16)],
        compiler_params=_params(("parallel",)),
    )(kv, down, cos, sin)


def _merge_kv_bwd(dk, dv, cos, sin, name):
    h, s, _ = dk.shape
    tr = _layout_rows(s, 8 * h * LANES)

    def body(dk_ref, dv_ref, cos_ref, sin_ref, dkv_ref, dkr_ref):
        rot = dk_ref[0, :, C_NOPE:]
        for j in range(h):
            base = j * (C_NOPE + C_V)
            dkv_ref[:, base:base + C_NOPE] = dk_ref[j, :, :C_NOPE].astype(BF16)
            dkv_ref[:, base + C_NOPE:base + C_NOPE + C_V] = dv_ref[j].astype(BF16)
            if j:
                rot = rot + dk_ref[j, :, C_NOPE:]
        d1, d2 = _rotate_half_pairs(rot[:, :ROPE_HALF], rot[:, ROPE_HALF:], cos_ref[...], sin_ref[...], True)
        dkr_ref[:, :ROPE_HALF] = d1
        dkr_ref[:, ROPE_HALF:] = d2

    def rows(width):
        return pl.BlockSpec((tr, width), lambda i: (i, 0))

    return pl.pallas_call(
        body, name=name, grid=(s // tr,),
        in_specs=[pl.BlockSpec((h, tr, C_QK), lambda i: (0, i, 0)), pl.BlockSpec((h, tr, C_V), lambda i: (0, i, 0)),
                  rows(ROPE_HALF), rows(ROPE_HALF)],
        out_specs=[rows(h * (C_NOPE + C_V)), rows(C_ROPE)],
        out_shape=[jax.ShapeDtypeStruct((s, h * (C_NOPE + C_V)), BF16), jax.ShapeDtypeStruct((s, C_ROPE), F32)],
        compiler_params=_params(("parallel",)),
    )(dk, dv, cos, sin)


SPAN_FWD, SPAN_BWD = 128, 256


def _even_fwd(xn, h, w_in, w_out, sinks_row, biases, tag):
    s = xn.shape[0]
    proj = _mm(xn, w_in, "nn", [BF16], f"in_proj{tag}")[0]
    qd, kd = A_Q_HEADS * HEAD_DIM, A_KV_HEADS * HEAD_DIM
    qa, ka, va = _split_heads([(proj, 0, A_Q_HEADS, BF16), (proj, qd, A_KV_HEADS, BF16),
                               (proj, qd + kd, A_KV_HEADS, BF16)], HEAD_DIM, 1, f"swa_split{tag}")
    oa, lse_a = _band_fwd(qa, ka, va, biases[0], s // BLOCK, SPAN_FWD, f"swa_fwd{tag}")
    out_a = _merge_with_sinks(oa, lse_a, sinks_row, f"swa_merge{tag}")
    width = B_HEADS_PER_BRANCH * HEAD_DIM
    qkv_b, outs, lses = [], [], []
    for g, (_, dil) in enumerate(B_BRANCHES):
        base = A_IN + g * 3 * width
        src, base = (proj, base) if dil == 1 else (proj[:, base:base + 3 * width], 0)
        qkv = _split_heads([(src, base + i * width, B_HEADS_PER_BRANCH, BF16) for i in range(3)], HEAD_DIM, dil,
                           f"dil{g}_split{tag}")
        og, lg = _band_fwd(*qkv, biases[1 + g], s // dil // BLOCK, SPAN_FWD, f"dil{g}_fwd{tag}")
        qkv_b.append(qkv)
        merged = _merge_heads([(og, F32), (lg, F32)], dil, f"dil{g}_merge{tag}")
        outs.append(merged[0])
        lses.append(merged[1])

    def merge(o0, o1, o2, l0, l1, l2):
        m = jnp.maximum(jnp.maximum(l0, l1), l2)
        e0, e1, e2 = jnp.exp(l0 - m), jnp.exp(l1 - m), jnp.exp(l2 - m)
        den = e0 + e1 + e2
        out = (e0 * o0 + e1 * o1 + e2 * o2) / den
        return out, m + jnp.log(den), out

    out_b, lse_b, out_b16 = _ew(merge, outs + lses, [(width, F32), (width, F32), (width, BF16)], f"dil_merge{tag}")
    cat = jnp.concatenate([out_a, out_b16], axis=1)
    h_mid = _mm(cat, w_out, "nn", [F32], f"out_proj{tag}", epi=_add_epi, extras=(h,))[0]
    return h_mid, (qa, ka, va, oa, lse_a, qkv_b, out_b, lse_b, cat)


def _even_bwd(dh, xn, saved, w_in, w_out, sinks_row, biases, tag):
    qa, ka, va, oa, lse_a, qkv_b, out_b, lse_b, cat = saved
    s = xn.shape[0]
    qd = A_Q_HEADS * HEAD_DIM
    g_w_out = _mm(cat, dh, "tn", [F32], f"out_proj_dw{tag}")[0]
    dcat = _mm(dh, w_out, "nt", [BF16], f"out_proj_dx{tag}")[0]
    do_a = _split_heads([(dcat, 0, A_Q_HEADS, BF16)], HEAD_DIM, 1, f"swa_do_split{tag}")[0]
    dqa, dka8, dva8, dbias_a, dsink = _band_bwd(qa, ka, va, oa, lse_a, do_a, biases[0], sinks_row, s // BLOCK,
                                                SPAN_BWD, f"swa_bwd{tag}")
    pieces = _merge_heads([(dqa, BF16)], 1, f"swa_dq_merge{tag}")
    pieces += _merge_heads([(dka8, BF16), (dva8, BF16)], 1, f"swa_dkv_merge{tag}", group_sum=A_Q_HEADS // A_KV_HEADS)
    dbias_b = []
    dcat_b = dcat[:, qd:]
    for g, (_, dil) in enumerate(B_BRANCHES):
        do_g, out_g, lse_g = _split_heads([(dcat_b, 0, B_HEADS_PER_BRANCH, BF16), (out_b, 0, B_HEADS_PER_BRANCH, F32),
                                           (lse_b, 0, B_HEADS_PER_BRANCH, F32)], HEAD_DIM, dil, f"dil{g}_do_split{tag}")
        dqg, dkg, dvg, dbg, _ = _band_bwd(*qkv_b[g], out_g, lse_g, do_g, biases[1 + g], None, s // dil // BLOCK,
                                          SPAN_BWD, f"dil{g}_bwd{tag}")
        pieces += _merge_heads([(dqg, BF16), (dkg, BF16), (dvg, BF16)], dil, f"dil{g}_dqkv_merge{tag}")
        dbias_b.append(dbg)
    dproj = jnp.concatenate(pieces, axis=1)
    g_w_in = _mm(xn, dproj, "tn", [F32], f"in_proj_dw{tag}")[0]
    dxn = _mm(dproj, w_in, "nt", [F32], f"in_proj_dx{tag}")[0]
    dbias = jnp.concatenate([db.reshape(db.shape[0], -1) for db in [dbias_a] + dbias_b])
    return dxn, g_w_in, g_w_out, dbias, dsink[:, 0, 0]


def _mla_layer_fwd(xn, h, w_dn, q_norm, w_uq, kv_norm, w_ukv, w_o, ropes, tag):
    (cos, sin), q_tables = ropes
    down = _mm(xn, w_dn, "nn", [F32], f"mla_down{tag}")[0]
    c_q, c_kv = down[:, :C_Q_RANK], down[:, C_Q_RANK:C_Q_RANK + C_KV_RANK]
    cqn = _rms_fwd(c_q, q_norm, f"mla_qnorm{tag}")
    ckvn = _rms_fwd(c_kv, kv_norm, f"mla_kvnorm{tag}")
    q = _mm(cqn, w_uq, "nn", [F32], f"mla_uq{tag}")[0]
    kv = _mm(ckvn, w_ukv, "nn", [BF16], f"mla_ukv{tag}")[0]
    qh = _split_heads([(_rope_rows(q, q_tables, False, f"mla_rope_q{tag}", BF16), 0, C_HEADS, BF16)], C_QK, 1,
                      f"mla_q_split{tag}")[0]
    kh, v1h, v1t = _split_kv(kv, down, cos, sin, f"mla_kv_split{tag}")
    ot, lse = _mla_fwd(qh, kh, v1t, f"mla_attn_fwd{tag}")
    o2d, o = _merge_heads_transposed(ot, BF16, f"mla_o_merge{tag}", also_heads=True)
    h_mid = _mm(o2d, w_o, "nn", [F32], f"mla_o{tag}", epi=_add_epi, extras=(h,))[0]
    return h_mid, (c_q, c_kv, cqn, ckvn, qh, kh, v1h, o, lse, o2d)


def _mla_layer_bwd(dh, xn, saved, w_dn, q_norm, w_uq, kv_norm, w_ukv, w_o, ropes, tag):
    c_q, c_kv, cqn, ckvn, qh, kh, v1h, o, lse, o2d = saved
    (cos, sin), q_tables = ropes
    g_w_o = _mm(o2d, dh, "tn", [F32], f"mla_o_dw{tag}")[0]
    do2d = _mm(dh, w_o, "nt", [BF16], f"mla_o_dx{tag}")[0]
    do = _split_heads([(do2d, 0, C_HEADS, BF16)], C_V, 1, f"mla_do_split{tag}")[0]
    dqt, dk, dv = _mla_bwd(qh, kh, v1h, o, lse, do, f"mla_attn_bwd{tag}")
    dq_roped = _merge_heads_transposed(dqt, F32, f"mla_dq_merge{tag}")
    dq = _rope_rows(dq_roped, q_tables, True, f"mla_rope_q_bwd{tag}", BF16)
    dkv, dk_rope = _merge_kv_bwd(dk, dv, cos, sin, f"mla_dkv_merge{tag}")
    g_w_uq = _mm(cqn, dq, "tn", [F32], f"mla_uq_dw{tag}")[0]
    dcqn = _mm(dq, w_uq, "nt", [F32], f"mla_uq_dx{tag}")[0]
    g_w_ukv = _mm(ckvn, dkv, "tn", [F32], f"mla_ukv_dw{tag}")[0]
    dckvn = _mm(dkv, w_ukv, "nt", [F32], f"mla_ukv_dx{tag}")[0]
    dc_q, g_q_norm = _rms_bwd(c_q, q_norm, dcqn, None, f"mla_qnorm_bwd{tag}")
    dc_kv, g_kv_norm = _rms_bwd(c_kv, kv_norm, dckvn, None, f"mla_kvnorm_bwd{tag}")
    ddown = jnp.concatenate([dc_q, dc_kv, dk_rope], axis=1).astype(BF16)
    g_w_dn = _mm(xn, ddown, "tn", [F32], f"mla_down_dw{tag}")[0]
    dxn = _mm(ddown, w_dn, "nt", [F32], f"mla_down_dx{tag}")[0]
    return dxn, g_w_dn, g_q_norm, g_w_uq, g_kv_norm, g_w_ukv, g_w_o


SHARDED = (("w_in_ab", 2), ("w_out_ab", 2), ("w_down_c", 1), ("w_uq_c", 2), ("w_ukv_c", 2), ("w_o_c", 2),
           ("w_mlp_up", 2), ("w_mlp_down", 1))
SHARDED_NORMS = ("q_norm_c", "kv_norm_c")


def kernel(x, rel_bias, attn_norm, mlp_norm, final_norm, w_in_ab, sinks, w_out_ab, w_down_c, q_norm_c, w_uq_c, kv_norm_c, w_ukv_c, w_o_c, w_mlp_up, w_mlp_down, loss_target, m_rel_bias, m_attn_norm, m_mlp_norm, m_final_norm, m_w_in_ab, m_sinks, m_w_out_ab, m_w_down_c, m_q_norm_c, m_w_uq_c, m_kv_norm_c, m_w_ukv_c, m_w_o_c, m_w_mlp_up, m_w_mlp_down, v_rel_bias, v_attn_norm, v_mlp_norm, v_final_norm, v_w_in_ab, v_sinks, v_w_out_ab, v_w_down_c, v_q_norm_c, v_w_uq_c, v_kv_norm_c, v_w_ukv_c, v_w_o_c, v_w_mlp_up, v_w_mlp_down):
    given = dict(locals())
    chip = lax.axis_index("x") * 2 + lax.axis_index("y")
    core = lax.axis_index("c")
    dev = chip * 2 + core
    depth = attn_norm.shape[0]
    s = x.shape[1]

    shards = [given[n] for n, _ in SHARDED]
    norm_shards = [given[n] for n in SHARDED_NORMS]
    packed = _to_comm_rows([t.astype(BF16) for t in shards]
                           + [lax.bitcast_convert_type(t, BF16) for t in norm_shards], BF16)
    by_chip = _gather_shards(packed, chip)
    shapes = [t.shape for t in shards] + [t.shape + (2,) for t in norm_shards]
    pieces = [_from_comm_rows(by_chip[a], shapes) for a in range(4)]
    full = {n: jnp.concatenate([pieces[a][i] for a in range(4)], axis=ax) for i, (n, ax) in enumerate(SHARDED)}
    for i, n in enumerate(SHARDED_NORMS):
        full[n] = jnp.concatenate([lax.bitcast_convert_type(pieces[a][len(SHARDED) + i], F32) for a in range(4)],
                                  axis=-1)

    onehots = [_band_bucket_onehot(dil, md) for dil, md in BAND_VARIANTS]
    head_cols = [(0, A_Q_HEADS)] + [(A_Q_HEADS + g * B_HEADS_PER_BRANCH, A_Q_HEADS + (g + 1) * B_HEADS_PER_BRANCH)
                                    for g in range(len(B_BRANCHES))]
    biases = [_band_bias(rel_bias[:, lo:hi], oh, inb) for (lo, hi), (oh, inb) in zip(head_cols, onehots)]
    ropes = _rope_tables(s)
    sink_rows = [jnp.broadcast_to(sinks[e][:, None, None], (A_Q_HEADS, BAND_SPAN_MAX, LANES))
                 for e in range(sinks.shape[0])]

    def odd_weights(o):
        return [full[n][o] for n in ("w_down_c", "q_norm_c", "w_uq_c", "kv_norm_c", "w_ukv_c", "w_o_c")]

    h = x[0]
    saved = []
    for l in range(depth):
        xn = _rms_fwd(h, attn_norm[l], f"attn_norm{l}")
        if l % 2 == 0:
            e = l // 2
            h_mid, mix = _even_fwd(xn, h, full["w_in_ab"][e], full["w_out_ab"][e], sink_rows[e], biases, f"_{l}")
        else:
            h_mid, mix = _mla_layer_fwd(xn, h, *odd_weights(l // 2), ropes, f"_{l}")
        xn2 = _rms_fwd(h_mid, mlp_norm[l], f"mlp_norm{l}")
        act, relu = _mm(xn2, full["w_mlp_up"][l], "nn", [BF16, BF16], f"mlp_up{l}", epi=_relu2_epi)
        h_out = _mm(act, full["w_mlp_down"][l], "nn", [F32], f"mlp_down{l}", epi=_add_epi, extras=(h_mid,))[0]
        saved.append((h, xn, mix, h_mid, xn2, act, relu))
        h = h_out
    loss_part, dh, dh16, g_final = _final_loss(h, final_norm, loss_target[0], "final_loss")

    grads = {n: [None] * given[n].shape[0] for n, _ in SHARDED}
    g_q_norm, g_kv_norm = [None] * q_norm_c.shape[0], [None] * kv_norm_c.shape[0]
    g_attn_norm, g_mlp_norm = [None] * depth, [None] * depth
    g_sinks = [None] * sinks.shape[0]
    g_band = []
    for l in range(depth - 1, -1, -1):
        h_in, xn, mix, h_mid, xn2, act, relu = saved[l]
        grads["w_mlp_down"][l] = _mm(act, dh16, "tn", [F32], f"mlp_down_dw{l}")[0]
        du = _mm(dh16, full["w_mlp_down"][l], "nt", [BF16], f"mlp_down_dx{l}", epi=_relu2_bwd_epi, extras=(relu,))[0]
        grads["w_mlp_up"][l] = _mm(xn2, du, "tn", [F32], f"mlp_up_dw{l}")[0]
        dxn2 = _mm(du, full["w_mlp_up"][l], "nt", [F32], f"mlp_up_dx{l}")[0]
        dh, dh16, g_mlp_norm[l] = _rms_bwd(h_mid, mlp_norm[l], dxn2, dh, f"mlp_norm_bwd{l}", also_bf16=True)
        if l % 2 == 0:
            e = l // 2
            dxn, grads["w_in_ab"][e], grads["w_out_ab"][e], g_band_e, g_sinks[e] = _even_bwd(
                dh16, xn, mix, full["w_in_ab"][e], full["w_out_ab"][e], sink_rows[e], biases, f"_{l}")
            g_band.append(g_band_e)
        else:
            o = l // 2
            (dxn, grads["w_down_c"][o], g_q_norm[o], grads["w_uq_c"][o], g_kv_norm[o], grads["w_ukv_c"][o],
             grads["w_o_c"][o]) = _mla_layer_bwd(dh16, xn, mix, *odd_weights(o), ropes, f"_{l}")
        if l:
            dh, dh16, g_attn_norm[l] = _rms_bwd(h_in, attn_norm[l], dxn, dh, f"attn_norm_bwd{l}", also_bf16=True)
        else:
            dh, g_attn_norm[l] = _rms_bwd(h_in, attn_norm[l], dxn, dh, f"attn_norm_bwd{l}")
    grad_x = dh[None]
    buckets = jnp.concatenate([oh for oh, _ in onehots], axis=1)
    by_variant = _mm(jnp.concatenate(g_band, axis=1), jnp.tile(buckets, (len(g_band), 1)), "nn", [F32],
                     "bias_buckets")[0]
    g_rel_bias = jnp.concatenate([by_variant[lo:hi, v * NUM_BUCKETS:(v + 1) * NUM_BUCKETS].T
                                  for v, (lo, hi) in enumerate(head_cols)], axis=1)

    portions = []
    for a in range(4):
        parts = []
        for n, ax in SHARDED:
            g = jnp.stack(grads[n])
            size = g.shape[ax] // 4
            parts.append(lax.slice_in_dim(g, a * size, (a + 1) * size, axis=ax))
        portions.append(_to_comm_rows(parts, F32))
    reduced = _from_comm_rows(_reduce_scatter(jnp.stack(portions), chip, core), [t.shape for t in shards])
    g_shard = {n: reduced[i] for i, (n, _) in enumerate(SHARDED)}

    small = [jnp.stack(g_attn_norm), jnp.stack(g_mlp_norm), g_final, g_rel_bias, jnp.stack(g_sinks),
             jnp.stack(g_q_norm), jnp.stack(g_kv_norm), loss_part.reshape(1)]
    small_shapes = [t.shape for t in small]
    summed = _from_comm_rows(_all_reduce_small(_to_comm_rows(small, F32)[:16], dev), small_shapes)
    loss = summed[7][0]
    g_small = dict(zip(("attn_norm", "mlp_norm", "final_norm", "rel_bias", "sinks"), summed[:5]))
    for n, g in zip(SHARDED_NORMS, summed[5:7]):
        size = g.shape[1] // 4
        g_shard[n] = lax.dynamic_slice_in_dim(g, chip * size, size, axis=1)

    order = ["rel_bias", "attn_norm", "mlp_norm", "final_norm", "w_in_ab", "sinks", "w_out_ab", "w_down_c",
             "q_norm_c", "w_uq_c", "kv_norm_c", "w_ukv_c", "w_o_c", "w_mlp_up", "w_mlp_down"]
    g_all = {**g_small, **g_shard}
    deltas, new_m, new_v = [], [], []
    for n in order:
        d, mn, vn = _adamw(given[n], g_all[n], given["m_" + n], given["v_" + n], f"adamw_{n}")
        deltas.append(d)
        new_m.append(mn)
        new_v.append(vn)
    return (loss, grad_x, *[g_all[n] for n in order], *deltas, *new_m, *new_v)
```

```python
import math

import jax
import jax.numpy as jnp
from jax import lax
from jax.experimental import pallas as pl
from jax.experimental.pallas import tpu as pltpu

F32 = jnp.float32
BF16 = jnp.bfloat16
MESH = pl.DeviceIdType.MESH
ALL_AXES = ("x", "y", "c")

EPS = 1e-6
NEG = -1e30
BLOCK = 128
HEAD_DIM = 64
A_Q_HEADS, A_KV_HEADS = 8, 2
A_WINDOW = 128
B_BRANCHES = ((128, 1), (512, 4), (2048, 16))
B_HEADS_PER_BRANCH = 4
NUM_BUCKETS, MAX_DISTANCE = 32, 2048
A_IN = (A_Q_HEADS + 2 * A_KV_HEADS) * HEAD_DIM
C_HEADS, C_NOPE, C_ROPE, C_V = 8, 64, 32, 64
C_Q_RANK, C_KV_RANK = 384, 256
ROPE_THETA = 10000.0
ADAM_LR, ADAM_B1, ADAM_B2, ADAM_EPS, ADAM_WD, ADAM_STEP = 0.001, 0.9, 0.999, 1e-08, 0.01, 10

V7X_VMEM_LIMIT_BYTES = 56 * 1024 * 1024
LANES = 128
COMM_COLS = 1024
COMM_ROW_ALIGN = 1024

NT = (((1,), (1,)), ((), ()))
NN = (((1,), (0,)), ((), ()))
TN = (((0,), (0,)), ((), ()))
DIMS = {"nn": NN, "nt": NT, "tn": TN}


def _params(sem):
    return pltpu.CompilerParams(dimension_semantics=sem, vmem_limit_bytes=V7X_VMEM_LIMIT_BYTES)


def _divisor_tile(n, limit, align):
    if n <= limit:
        return n
    t = (limit // align) * align
    while t >= align:
        if n % t == 0:
            return t
        t -= align
    return n


def _ew(fn, ins, outs, name, acc_outs=(), target_bytes=12 << 20):
    rows = max(a.shape[0] for a in ins)

    def vmem_row_bytes(cols, dtype):
        return -(-cols // LANES) * LANES * jnp.dtype(dtype).itemsize

    per_row = sum(vmem_row_bytes(a.shape[1], a.dtype) for a in ins if a.shape[0] == rows)
    per_row += sum(vmem_row_bytes(c, d) for c, d in outs)
    tr = _divisor_tile(rows, max(16, target_bytes // max(per_row, 1)), 16)
    n_in, n_out = len(ins), len(outs)

    def body(*refs):
        res = fn(*[r[...] for r in refs[:n_in]])
        for r, v in zip(refs[n_in:n_in + n_out], res[:n_out]):
            r[...] = v.astype(r.dtype)
        if acc_outs:
            acc_refs = refs[n_in + n_out:]

            @pl.when(pl.program_id(0) == 0)
            def _():
                for r in acc_refs:
                    r[...] = jnp.zeros_like(r)

            for r, v in zip(acc_refs, res[n_out:]):
                r[...] += v

    def spec(a):
        if a.shape[0] == rows:
            return pl.BlockSpec((tr, a.shape[1]), lambda i: (i, 0))
        return pl.BlockSpec((1, a.shape[1]), lambda i: (0, 0))

    out_shape = [jax.ShapeDtypeStruct((rows, c), d) for c, d in outs]
    out_shape += [jax.ShapeDtypeStruct((1, c), F32) for c in acc_outs]
    out_specs = [pl.BlockSpec((tr, c), lambda i: (i, 0)) for c, _ in outs]
    out_specs += [pl.BlockSpec((1, c), lambda i: (0, 0)) for c in acc_outs]
    return pl.pallas_call(
        body, name=name, grid=(rows // tr,), in_specs=[spec(a) for a in ins], out_specs=out_specs,
        out_shape=out_shape, compiler_params=_params(("arbitrary",)),
    )(*ins)


def _rms_fwd(x, g, name):
    def fn(xv, gv):
        return ((xv * lax.rsqrt(jnp.mean(xv * xv, axis=-1, keepdims=True) + EPS)) * gv,)

    return _ew(fn, [x, g.reshape(1, -1)], [(x.shape[1], BF16)], name)[0]


def _rms_bwd(x, g, dy, add, name, also_bf16=False):
    def fn(xv, gv, dyv, *rest):
        rstd = lax.rsqrt(jnp.mean(xv * xv, axis=-1, keepdims=True) + EPS)
        xh = xv * rstd
        dyg = dyv.astype(F32) * gv
        dx = rstd * (dyg - xh * jnp.mean(dyg * xh, axis=-1, keepdims=True))
        if rest:
            dx = dx + rest[0]
        return (dx,) * (2 if also_bf16 else 1) + (jnp.sum(dyv.astype(F32) * xh, axis=0, keepdims=True),)

    ins = [x, g.reshape(1, -1), dy] + ([] if add is None else [add])
    outs = [(x.shape[1], F32)] + ([(x.shape[1], BF16)] if also_bf16 else [])
    *dx, dg = _ew(fn, ins, outs, name, acc_outs=(x.shape[1],))
    return (*dx, dg[0])


def _final_loss(h, g, target, name):
    d = h.shape[1]

    def fn(xv, gv, tv):
        rstd = lax.rsqrt(jnp.mean(xv * xv, axis=-1, keepdims=True) + EPS)
        xh = xv * rstd
        err = xh * gv - tv
        part = 0.5 * jnp.sum(jnp.mean(err * err, axis=-1, keepdims=True), axis=0, keepdims=True)
        dy = err * (1.0 / d)
        dyg = dy * gv
        dx = rstd * (dyg - xh * jnp.mean(dyg * xh, axis=-1, keepdims=True))
        return dx, dx, jnp.broadcast_to(part, (1, LANES)), jnp.sum(dy * xh, axis=0, keepdims=True)

    dx, dx16, loss, dg = _ew(fn, [h, g.reshape(1, -1), target], [(d, F32), (d, BF16)], name, acc_outs=(LANES, d))
    return loss[0, 0], dx, dx16, dg[0]


def _adamw(w, g, m, v, name):
    def fn(wv, gv, mv, vv):
        mn = ADAM_B1 * mv + (1.0 - ADAM_B1) * gv
        vn = ADAM_B2 * vv + (1.0 - ADAM_B2) * jnp.square(gv)
        m_hat = mn / (1.0 - ADAM_B1 ** ADAM_STEP)
        v_hat = vn / (1.0 - ADAM_B2 ** ADAM_STEP)
        return -ADAM_LR * (m_hat / (jnp.sqrt(v_hat) + ADAM_EPS) + ADAM_WD * wv), mn, vn

    shape = w.shape
    cols = shape[-1] if w.ndim > 1 else w.size
    view = [t.reshape(-1, cols) for t in (w, g, m, v)]
    return [t.reshape(shape) for t in _ew(fn, view, [(cols, F32)] * 3, name)]


MM_VMEM_BUDGET_BYTES = 40 << 20


def _mm(a, b, dims, outs, name, epi=None, extras=(), tm=2048, tn=1024, tk=1024):
    if dims == "nn":
        (m, k), n = a.shape, b.shape[1]
    elif dims == "nt":
        (m, k), n = a.shape, b.shape[0]
    else:
        (k, m), n = a.shape, b.shape[1]
    tn, tk = _divisor_tile(n, tn, LANES), _divisor_tile(k, tk, LANES)

    def tile_bytes(rows):
        per_out = sum(jnp.dtype(d).itemsize for d in outs) + sum(e.dtype.itemsize for e in extras)
        return 2 * (rows * tk * a.dtype.itemsize + tk * tn * b.dtype.itemsize + rows * tn * per_out) + 4 * rows * tn

    tm = _divisor_tile(m, tm, LANES)
    while tile_bytes(tm) > MM_VMEM_BUDGET_BYTES and tm % (2 * LANES) == 0:
        tm //= 2
    nk = k // tk
    n_ex, n_out = len(extras), len(outs)

    def body(a_ref, b_ref, *rest):
        ex_refs, out_refs = rest[:n_ex], rest[n_ex:n_ex + n_out]

        def finish(acc):
            res = epi(acc, *[r[...] for r in ex_refs]) if epi else (acc,)
            for r, v in zip(out_refs, res):
                r[...] = v.astype(r.dtype)

        part = lax.dot_general(a_ref[...].astype(BF16), b_ref[...].astype(BF16), DIMS[dims],
                               preferred_element_type=F32)
        if nk == 1:
            finish(part)
        else:
            acc_ref = rest[-1]
            kk = pl.program_id(2)

            @pl.when(kk == 0)
            def _():
                acc_ref[...] = part

            @pl.when(kk > 0)
            def _():
                acc_ref[...] += part

            @pl.when(kk == nk - 1)
            def _():
                finish(acc_ref[...])

    if dims == "nn":
        a_spec = pl.BlockSpec((tm, tk), lambda i, j, kk: (i, kk))
        b_spec = pl.BlockSpec((tk, tn), lambda i, j, kk: (kk, j))
    elif dims == "nt":
        a_spec = pl.BlockSpec((tm, tk), lambda i, j, kk: (i, kk))
        b_spec = pl.BlockSpec((tn, tk), lambda i, j, kk: (j, kk))
    else:
        a_spec = pl.BlockSpec((tk, tm), lambda i, j, kk: (kk, i))
        b_spec = pl.BlockSpec((tk, tn), lambda i, j, kk: (kk, j))
    tile = pl.BlockSpec((tm, tn), lambda i, j, kk: (i, j))
    return pl.pallas_call(
        body, name=name, grid=(m // tm, n // tn, nk),
        in_specs=[a_spec, b_spec] + [tile] * n_ex, out_specs=[tile] * n_out,
        out_shape=[jax.ShapeDtypeStruct((m, n), d) for d in outs],
        scratch_shapes=[pltpu.VMEM((tm, tn), F32)] if nk > 1 else [],
        compiler_params=_params(("parallel", "parallel", "arbitrary")),
    )(a, b, *extras)


def _add_epi(acc, res):
    return (acc + res,)


def _relu2_epi(acc):
    r = jnp.maximum(acc, 0.0)
    return r * r, r


def _relu2_bwd_epi(acc, r):
    return (acc * (2.0 * r.astype(F32)),)


BAND_SPAN_MAX = 256


def _band_geometry(t, blocks_per_seq, span):
    rows = min(1024, blocks_per_seq * BLOCK)
    nb = rows // BLOCK
    assert blocks_per_seq % nb == 0 and t % rows == 0 and span <= BAND_SPAN_MAX
    return rows, nb, t // rows, min(span, rows)


def _span_bias(bias, span):
    n = span // BLOCK
    neg = jnp.full(bias.shape[:2] + (BLOCK,), NEG, F32)
    rows = [jnp.concatenate([neg] * a + [bias[:, :, :BLOCK], bias[:, :, BLOCK:]] + [neg] * (n - 1 - a), axis=2)
            for a in range(n)]
    return jnp.concatenate(rows, axis=1)


def _fold_span_bias_grad(dbias, span):
    n = span // BLOCK
    parts = [dbias[:, a * BLOCK:(a + 1) * BLOCK, a * BLOCK:(a + 2) * BLOCK] for a in range(n)]
    return sum(parts[1:], parts[0])


def _band_logits(qj, kk, bias, first):
    s = lax.dot_general(qj, kk, NT, preferred_element_type=F32) * (HEAD_DIM ** -0.5) + bias
    if first is not None:
        col = lax.broadcasted_iota(jnp.int32, s.shape, 1)
        s = jnp.where(col < jnp.where(first, BLOCK, 0), NEG, s)
    return s


def _band_fwd(q, k, v, bias, blocks_per_seq, span, name):
    hq, t, dh = q.shape
    group = hq // k.shape[0]
    rows, nb, nchunks, span = _band_geometry(t, blocks_per_seq, span)

    def body(q_ref, kc_ref, kp_ref, vc_ref, vp_ref, bias_ref, o_ref, lse_ref):
        i = pl.program_id(1)
        bias_v = bias_ref[0]
        for j in range(rows // span):
            cur = slice(j * span, (j + 1) * span)
            prev = slice(j * span - BLOCK, j * span)
            kk = jnp.concatenate([kp_ref[0] if j == 0 else kc_ref[0, prev, :], kc_ref[0, cur, :]], axis=0)
            vv = jnp.concatenate([vp_ref[0] if j == 0 else vc_ref[0, prev, :], vc_ref[0, cur, :]], axis=0)
            first = lax.rem(i * nb, blocks_per_seq) == 0 if j == 0 else None
            s = _band_logits(q_ref[0, cur, :], kk, bias_v, first)
            m = jnp.max(s, axis=1, keepdims=True)
            p = jnp.exp(s - m)
            l = jnp.sum(p, axis=1, keepdims=True)
            acc = jnp.dot(p.astype(BF16), vv, preferred_element_type=F32)
            o_ref[0, cur, :] = acc / l
            lse_ref[0, cur, :] = jnp.broadcast_to(m + jnp.log(l), (span, dh))

    cur_q = pl.BlockSpec((1, rows, dh), lambda h, i: (h, i, 0))
    cur_kv = pl.BlockSpec((1, rows, dh), lambda h, i: (h // group, i, 0))
    prev_kv = pl.BlockSpec((1, BLOCK, dh), lambda h, i: (h // group, jnp.maximum(i * nb - 1, 0), 0))
    in_specs = [cur_q, cur_kv, prev_kv, cur_kv, prev_kv, pl.BlockSpec((1, span, span + BLOCK), lambda h, i: (h, 0, 0))]
    return pl.pallas_call(
        body, name=name, grid=(hq, nchunks), in_specs=in_specs,
        out_specs=[cur_q, cur_q],
        out_shape=[jax.ShapeDtypeStruct((hq, t, dh), F32), jax.ShapeDtypeStruct((hq, t, dh), F32)],
        compiler_params=_params(("parallel", "arbitrary")),
    )(q, k, k, v, v, _span_bias(bias, span))


def _sink_lse(lse, sink):
    m = jnp.maximum(lse, sink)
    return m + jnp.log(jnp.exp(lse - m) + jnp.exp(sink - m))


def _merge_with_sinks(o, lse, sinks, name):
    h, t, w = o.shape
    tr = _layout_rows(t, 3 * h * LANES * 4)

    def body(o_ref, lse_ref, sink_ref, out_ref):
        for j in range(h):
            lse_j = lse_ref[j]
            shrink = jnp.exp(lse_j - _sink_lse(lse_j, sink_ref[j, :1, :w]))
            out_ref[:, j * w:(j + 1) * w] = (o_ref[j] * shrink).astype(BF16)

    heads = pl.BlockSpec((h, tr, w), lambda i: (0, i, 0))
    return pl.pallas_call(
        body, name=name, grid=(t // tr,),
        in_specs=[heads, heads, pl.BlockSpec((h, BAND_SPAN_MAX, LANES), lambda i: (0, 0, 0))],
        out_specs=pl.BlockSpec((tr, h * w), lambda i: (i, 0)), out_shape=jax.ShapeDtypeStruct((t, h * w), BF16),
        compiler_params=_params(("parallel",)),
    )(o, lse, sinks)


def _band_bwd(q, k, v, o, lse, do, bias, sinks, blocks_per_seq, span, name):
    hq, t, dh = q.shape
    group = hq // k.shape[0]
    rows, nb, nchunks, span = _band_geometry(t, blocks_per_seq, span)
    per_span = span // BLOCK
    has_sink = sinks is not None
    scale = HEAD_DIM ** -0.5

    def body(q_ref, kc_ref, kp_ref, vc_ref, vp_ref, o_ref, lse_ref, do_ref, bias_ref, *rest):
        dq_ref, dk_ref, dv_ref, dbias_ref, dsink_ref, dk_carry, dv_carry = rest[-7:]
        step = pl.program_id(1)
        chunk = nchunks - 1 - step
        bias_v = bias_ref[0]
        sink = rest[0][0, :span, :1] if has_sink else None

        @pl.when(step == 0)
        def _():
            dk_carry[...] = jnp.zeros_like(dk_carry)
            dv_carry[...] = jnp.zeros_like(dv_carry)
            dbias_ref[...] = jnp.zeros_like(dbias_ref)
            dsink_ref[...] = jnp.zeros_like(dsink_ref)

        dks = [jnp.zeros((BLOCK, dh), F32) for _ in range(nb + 1)]
        dvs = [jnp.zeros((BLOCK, dh), F32) for _ in range(nb + 1)]
        dks[nb] = dk_carry[...]
        dvs[nb] = dv_carry[...]
        for j in range(rows // span - 1, -1, -1):
            cur = slice(j * span, (j + 1) * span)
            prev = slice(j * span - BLOCK, j * span)
            kk = jnp.concatenate([kp_ref[0] if j == 0 else kc_ref[0, prev, :], kc_ref[0, cur, :]], axis=0)
            vv = jnp.concatenate([vp_ref[0] if j == 0 else vc_ref[0, prev, :], vc_ref[0, cur, :]], axis=0)
            first = lax.rem(chunk * nb, blocks_per_seq) == 0 if j == 0 else None
            qj, doj = q_ref[0, cur, :], do_ref[0, cur, :]
            lse_j = lse_ref[0, cur, :][:, :1]
            delta = jnp.sum(doj.astype(F32) * o_ref[0, cur, :], axis=1, keepdims=True)
            if has_sink:
                with_sink = _sink_lse(lse_j, sink)
                delta = delta * jnp.exp(lse_j - with_sink)
                lse_j = with_sink
            p = jnp.exp(_band_logits(qj, kk, bias_v, first) - lse_j)
            dp = lax.dot_general(doj, vv, NT, preferred_element_type=F32)
            ds = p * (dp - delta)
            dbias_ref[0] += ds
            if has_sink:
                dsink = -jnp.sum(jnp.exp(sink - lse_j) * delta, axis=0, keepdims=True)
                dsink_ref[0] += jnp.broadcast_to(dsink, (1, LANES))
            dsb = (ds * scale).astype(BF16)
            dq_ref[0, cur, :] = jnp.dot(dsb, kk, preferred_element_type=F32)
            dkk = lax.dot_general(dsb, qj, TN, preferred_element_type=F32)
            dvv = lax.dot_general(p.astype(BF16), doj, TN, preferred_element_type=F32)
            for b in range(per_span + 1):
                piece = slice(b * BLOCK, (b + 1) * BLOCK)
                dks[j * per_span + b] += dkk[piece]
                dvs[j * per_span + b] += dvv[piece]
        for j in range(nb):
            cur = slice(j * BLOCK, (j + 1) * BLOCK)
            dk_ref[0, cur, :] = dks[j + 1]
            dv_ref[0, cur, :] = dvs[j + 1]
        dk_carry[...] = dks[0]
        dv_carry[...] = dvs[0]

    def rev(i):
        return nchunks - 1 - i

    cur_q = pl.BlockSpec((1, rows, dh), lambda h, i: (h, rev(i), 0))
    cur_kv = pl.BlockSpec((1, rows, dh), lambda h, i: (h // group, rev(i), 0))
    prev_kv = pl.BlockSpec((1, BLOCK, dh), lambda h, i: (h // group, jnp.maximum(rev(i) * nb - 1, 0), 0))
    cur_lse = cur_q
    per_head_bias = pl.BlockSpec((1, span, span + BLOCK), lambda h, i: (h, 0, 0))
    per_head_row = pl.BlockSpec((1, 1, LANES), lambda h, i: (h, 0, 0))
    in_specs = [cur_q, cur_kv, prev_kv, cur_kv, prev_kv, cur_q, cur_lse, cur_q, per_head_bias]
    ins = [q, k, k, v, v, o, lse, do, _span_bias(bias, span)]
    if has_sink:
        in_specs.append(pl.BlockSpec((1, BAND_SPAN_MAX, LANES), lambda h, i: (h, 0, 0)))
        ins.append(sinks)
    full = jax.ShapeDtypeStruct((hq, t, dh), F32)
    dq, dk, dv, dbias, dsink = pl.pallas_call(
        body, name=name, grid=(hq, nchunks), in_specs=in_specs,
        out_specs=[cur_q, cur_q, cur_q, per_head_bias, per_head_row],
        out_shape=[full, full, full, jax.ShapeDtypeStruct((hq, span, span + BLOCK), F32),
                   jax.ShapeDtypeStruct((hq, 1, LANES), F32)],
        scratch_shapes=[pltpu.VMEM((BLOCK, dh), F32), pltpu.VMEM((BLOCK, dh), F32)],
        compiler_params=_params(("parallel", "arbitrary")),
    )(*ins)
    return dq, dk, dv, _fold_span_bias_grad(dbias, span), dsink


C_QK = C_NOPE + C_ROPE
C_SCALE = C_QK ** -0.5
LOG2E = math.log2(math.e)
C_EXP2 = C_SCALE * LOG2E
CAUSAL_SUB = 256


def _causal_tile(t, forward=False):
    return min(4096, t)


CAUSAL_Q_CHAIN = 128
CAUSAL_K_CHAIN = 256
STAT_ROWS = 8


def _mla_fwd(q, k, v1t, name):
    h, t, _ = q.shape
    tq = _causal_tile(t, forward=True)
    n = t // tq
    qs, ks = min(CAUSAL_Q_CHAIN, tq), min(CAUSAL_K_CHAIN, tq)

    def body(q_ref, k_ref, v_ref, ot_ref, lse_ref, m_scr, acc_scr):
        qi, ki = pl.program_id(1), pl.program_id(2)

        @pl.when(ki == 0)
        def _():
            m_scr[...] = jnp.full_like(m_scr, NEG)
            acc_scr[...] = jnp.zeros_like(acc_scr)

        def tile(diagonal):
            for r in range(tq // qs):
                cols = slice(r * qs, (r + 1) * qs)
                q_sub = q_ref[0, cols, :]
                m, acc = m_scr[:1, cols], acc_scr[:, cols]
                for kc in range(tq // ks):
                    k0 = kc * ks
                    if diagonal and k0 > r * qs + qs - 1:
                        continue
                    st = lax.dot_general(k_ref[0, k0:k0 + ks, :], q_sub, NT, preferred_element_type=F32)
                    if diagonal and k0 + ks - 1 > r * qs:
                        kpos = k0 + lax.broadcasted_iota(jnp.int32, st.shape, 0)
                        qpos = r * qs + lax.broadcasted_iota(jnp.int32, st.shape, 1)
                        st = jnp.where(kpos <= qpos, st, NEG)
                    m_new = jnp.maximum(m, jnp.max(st, axis=0, keepdims=True))
                    alpha = jnp.exp2((m - m_new) * C_EXP2)
                    pt = jnp.exp2((st - m_new) * C_EXP2).astype(BF16)
                    acc = acc * alpha + jnp.dot(v_ref[0, :, k0:k0 + ks], pt, preferred_element_type=F32)
                    m = m_new
                m_scr[:, cols] = jnp.broadcast_to(m, (STAT_ROWS, qs))
                acc_scr[:, cols] = acc

        @pl.when(ki < qi)
        def _():
            tile(False)

        @pl.when(ki == qi)
        def _():
            tile(True)
            l = acc_scr[C_V:C_V + 1, :]
            ot_ref[0] = acc_scr[:C_V, :] / l
            lse_ref[0] = jnp.broadcast_to(m_scr[:1, :] * C_SCALE + jnp.log(l), (STAT_ROWS, tq))

    return pl.pallas_call(
        body, name=name, grid=(h, n, n),
        in_specs=[pl.BlockSpec((1, tq, C_QK), lambda hh, qi, ki: (hh, qi, 0)),
                  pl.BlockSpec((1, tq, C_QK), lambda hh, qi, ki: (hh, jnp.minimum(ki, qi), 0)),
                  pl.BlockSpec((1, LANES, tq), lambda hh, qi, ki: (hh, 0, jnp.minimum(ki, qi)))],
        out_specs=[pl.BlockSpec((1, C_V, tq), lambda hh, qi, ki: (hh, 0, qi)),
                   pl.BlockSpec((1, STAT_ROWS, tq), lambda hh, qi, ki: (hh, 0, qi))],
        out_shape=[jax.ShapeDtypeStruct((h, C_V, t), F32), jax.ShapeDtypeStruct((h, STAT_ROWS, t), F32)],
        scratch_shapes=[pltpu.VMEM((STAT_ROWS, tq), F32), pltpu.VMEM((LANES, tq), F32)],
        compiler_params=_params(("parallel", "arbitrary", "arbitrary")),
    )(q, k, v1t)


def _mla_bwd(q, k, v1, o, lse, do, name):
    h, t, _ = q.shape
    tq = _causal_tile(t)
    n = t // tq
    sub = min(CAUSAL_SUB, tq)

    def body(q_ref, k_ref, v_ref, o_ref, lse_ref, do_ref, dqt_ref, dk_ref, dv_ref, dk_acc, dv_acc):
        ki, qi = pl.program_id(1), pl.program_id(2)

        @pl.when(qi == 0)
        def _():
            dk_acc[...] = jnp.zeros_like(dk_acc)
            dv_acc[...] = jnp.zeros_like(dv_acc)

        @pl.when(jnp.logical_and(ki == 0, qi == 0))
        def _():
            dqt_ref[...] = jnp.zeros_like(dqt_ref)

        def tile(diagonal):
            for c in range(tq // sub):
                cols = slice(c * sub, (c + 1) * sub)
                nk = (c + 1) * sub if diagonal else tq
                qc, doc = q_ref[0, cols, :], do_ref[0, cols, :]
                st = lax.dot_general(k_ref[0, :nk, :], qc, NT, preferred_element_type=F32)
                lse2 = lse_ref[0, :1, cols] * LOG2E
                pt = jnp.exp2(st * C_EXP2 - lse2)
                if diagonal:
                    kpos = lax.broadcasted_iota(jnp.int32, st.shape, 0)
                    qpos = c * sub + lax.broadcasted_iota(jnp.int32, st.shape, 1)
                    pt = jnp.where(kpos <= qpos, pt, 0.0)
                dpt = lax.dot_general(v_ref[0, :nk, :C_V], doc, NT, preferred_element_type=F32)
                delta = jnp.sum(doc.astype(F32) * o_ref[0, cols, :], axis=1, keepdims=True)
                delta_row = jnp.transpose(jnp.broadcast_to(delta, (sub, LANES)))[:1]
                dst = (pt * (dpt - delta_row)).astype(BF16)
                dv_acc[:nk, :] += jnp.dot(pt.astype(BF16), doc, preferred_element_type=F32)
                dk_acc[:nk, :] += jnp.dot(dst, qc, preferred_element_type=F32)
                out_cols = pl.ds(pl.multiple_of(qi * tq + c * sub, sub), sub)
                dqt_ref[0, :, out_cols] += lax.dot_general(k_ref[0, :nk, :], dst, TN,
                                                           preferred_element_type=F32) * C_SCALE

        @pl.when(qi > ki)
        def _():
            tile(False)

        @pl.when(qi == ki)
        def _():
            tile(True)

        @pl.when(qi == n - 1)
        def _():
            dk_ref[0] = dk_acc[...] * C_SCALE
            dv_ref[0] = dv_acc[...]

    def q_spec(d):
        return pl.BlockSpec((1, tq, d), lambda hh, ki, qi: (hh, jnp.maximum(qi, ki), 0))

    def k_spec(d):
        return pl.BlockSpec((1, tq, d), lambda hh, ki, qi: (hh, ki, 0))

    return pl.pallas_call(
        body, name=name, grid=(h, n, n),
        in_specs=[q_spec(C_QK), k_spec(C_QK), k_spec(LANES), q_spec(C_V),
                  pl.BlockSpec((1, STAT_ROWS, tq), lambda hh, ki, qi: (hh, 0, jnp.maximum(qi, ki))), q_spec(C_V)],
        out_specs=[pl.BlockSpec((1, C_QK, t), lambda hh, ki, qi: (hh, 0, 0)), k_spec(C_QK), k_spec(C_V)],
        out_shape=[jax.ShapeDtypeStruct((h, C_QK, t), F32), jax.ShapeDtypeStruct((h, t, C_QK), F32),
                   jax.ShapeDtypeStruct((h, t, C_V), F32)],
        scratch_shapes=[pltpu.VMEM((tq, C_QK), F32), pltpu.VMEM((tq, C_V), F32)],
        compiler_params=_params(("parallel", "arbitrary", "arbitrary")),
    )(q, k, v1, o, lse, do)


def _exchange(src, flips, src_idx, name):
    n = len(flips)
    _, r, c = src.shape

    def body(src_ref, dst_ref, send_sems, recv_sems):
        me = [lax.axis_index(a) for a in ALL_AXES]
        copies = []
        for kk, flip in enumerate(flips):
            peer = tuple(1 - p if f else p for p, f in zip(me, flip))
            copies.append(pltpu.make_async_remote_copy(
                src_ref=src_ref.at[src_idx[kk]], dst_ref=dst_ref.at[kk], send_sem=send_sems.at[kk],
                recv_sem=recv_sems.at[kk], device_id=peer, device_id_type=MESH))
        for cp in copies:
            cp.start()
        for cp in copies:
            cp.wait_recv()
        for cp in copies:
            cp.wait_send()

    return pl.pallas_call(
        body, name=name, in_specs=[pl.BlockSpec(memory_space=pl.ANY)], out_specs=pl.BlockSpec(memory_space=pl.ANY),
        out_shape=jax.ShapeDtypeStruct((n, r, c), src.dtype),
        scratch_shapes=[pltpu.SemaphoreType.DMA((n,)), pltpu.SemaphoreType.DMA((n,))],
    )(src)


FLIP_C = (0, 0, 1)
CHIP_FLIPS = ((0, 1, 0), (1, 0, 0), (1, 1, 0))
ALL_FLIPS = tuple((a >> 2 & 1, a >> 1 & 1, a & 1) for a in range(1, 8))


def _pick(stacked, idx):
    return lax.dynamic_index_in_dim(stacked, idx, axis=0, keepdims=False)


def _to_comm_rows(parts, dtype):
    flat = jnp.concatenate([p.reshape(-1) for p in parts]).astype(dtype)
    rows = -(-flat.size // (COMM_COLS * COMM_ROW_ALIGN)) * COMM_ROW_ALIGN
    return jnp.pad(flat, (0, rows * COMM_COLS - flat.size)).reshape(rows, COMM_COLS)


def _from_comm_rows(buf, shapes):
    flat, out, off = buf.reshape(-1), [], 0
    for s in shapes:
        size = math.prod(s)
        out.append(flat[off:off + size].reshape(s))
        off += size
    return out


def _my_place():
    x, y, c = (lax.axis_index(a) for a in ALL_AXES)
    return (x, y, c), 2 * x + y


def _flipped(me, flip):
    return tuple(1 - p if f else p for p, f in zip(me, flip))


def _remote(src, dst, sems, k, peer):
    send_sems, recv_sems = sems
    return pltpu.make_async_remote_copy(src_ref=src, dst_ref=dst, send_sem=send_sems.at[k], recv_sem=recv_sems.at[k],
                                        device_id=peer, device_id_type=MESH)


def _gather_shards(buf, chip):
    rows, cols = buf.shape
    half = rows // 2
    n = len(CHIP_FLIPS)

    def body(src_ref, out_ref, send_sems, recv_sems):
        me, chip = _my_place()
        sems = (send_sems, recv_sems)
        sibling = _flipped(me, FLIP_C)
        mine = pl.ds(pl.multiple_of(me[2] * half, half), half)
        theirs = pl.ds(pl.multiple_of((1 - me[2]) * half, half), half)
        peers = [_flipped(me, f) for f in CHIP_FLIPS]
        from_chip = [2 * p[0] + p[1] for p in peers]
        over_ici = [_remote(src_ref.at[mine], out_ref.at[chip, mine], sems, k, peers[k]) for k in range(n)]
        for cp in over_ici:
            cp.start()
        passed = [_remote(out_ref.at[from_chip[k], mine], out_ref.at[from_chip[k], mine], sems, n + k, sibling)
                  for k in range(n)]
        for k in range(n):
            _remote(src_ref.at[mine], out_ref.at[from_chip[k], mine], sems, k, peers[k]).wait_recv()
            passed[k].start()
        for k in range(n):
            _remote(out_ref.at[from_chip[k], theirs], out_ref.at[from_chip[k], theirs], sems, n + k, sibling).wait_recv()
        for cp in over_ici + passed:
            cp.wait_send()

    others = pl.pallas_call(
        body, name="gather_shards", in_specs=[pl.BlockSpec(memory_space=pl.ANY)],
        out_specs=pl.BlockSpec(memory_space=pl.ANY), out_shape=jax.ShapeDtypeStruct((4, rows, cols), buf.dtype),
        scratch_shapes=[pltpu.SemaphoreType.DMA((2 * n,)), pltpu.SemaphoreType.DMA((2 * n,))],
    )(buf)
    return lax.dynamic_update_slice(others, buf[None], (chip, 0, 0))


def _swap_other_halves(portions):
    _, rows, cols = portions.shape
    half = rows // 2

    def body(src_ref, dst_ref, send_sems, recv_sems):
        me, _ = _my_place()
        theirs = pl.ds(pl.multiple_of((1 - me[2]) * half, half), half)
        cp = _remote(src_ref.at[:, theirs], dst_ref, (send_sems, recv_sems), 0, _flipped(me, FLIP_C))
        cp.start()
        cp.wait_recv()
        cp.wait_send()

    return pl.pallas_call(
        body, name="reduce_d2d", in_specs=[pl.BlockSpec(memory_space=pl.ANY)],
        out_specs=pl.BlockSpec(memory_space=pl.ANY), out_shape=jax.ShapeDtypeStruct((4, half, cols), portions.dtype),
        scratch_shapes=[pltpu.SemaphoreType.DMA((1,)), pltpu.SemaphoreType.DMA((1,))],
    )(portions)


def _join_halves(total, core):
    half, cols = total.shape

    def body(src_ref, out_ref, send_sems, recv_sems):
        me, _ = _my_place()
        mine = pl.ds(pl.multiple_of(me[2] * half, half), half)
        theirs = pl.ds(pl.multiple_of((1 - me[2]) * half, half), half)
        sems = (send_sems, recv_sems)
        cp = _remote(src_ref, out_ref.at[mine], sems, 0, _flipped(me, FLIP_C))
        cp.start()
        _remote(src_ref, out_ref.at[theirs], sems, 0, _flipped(me, FLIP_C)).wait_recv()
        cp.wait_send()

    from_sibling = pl.pallas_call(
        body, name="reduce_share", in_specs=[pl.BlockSpec(memory_space=pl.ANY)],
        out_specs=pl.BlockSpec(memory_space=pl.ANY), out_shape=jax.ShapeDtypeStruct((2 * half, cols), total.dtype),
        scratch_shapes=[pltpu.SemaphoreType.DMA((1,)), pltpu.SemaphoreType.DMA((1,))],
    )(total)
    return lax.dynamic_update_slice(from_sibling, total, (core * half, 0))


def _reduce_scatter(portions, chip, core):
    half = portions.shape[1] // 2
    keep = lax.dynamic_slice_in_dim(portions, core * half, half, axis=1)
    got = _swap_other_halves(portions)
    pair = _ew(lambda p, q: (p + q,), [keep.reshape(4 * half, -1), got.reshape(4 * half, -1)],
               [(COMM_COLS, F32)], "reduce_pair_sum")[0].reshape(keep.shape)
    out = jnp.stack([_pick(pair, jnp.bitwise_xor(chip, f)) for f in (1, 2, 3)]).astype(BF16)
    others = _exchange(out, CHIP_FLIPS, (0, 1, 2), "reduce_ici")
    total = _ew(lambda p, q, r, s: (p + q.astype(F32) + r.astype(F32) + s.astype(F32),),
                [_pick(pair, chip), others[0], others[1], others[2]],
                [(COMM_COLS, F32)], "reduce_chip_sum")[0]
    return _join_halves(total, core)


def _all_reduce_small(buf, dev):
    got = _exchange(buf[None], ALL_FLIPS, (0,) * 7, "small_gather")
    by_flip = jnp.concatenate([buf[None], got])
    ordered = [_pick(by_flip, jnp.bitwise_xor(dev, a)) for a in range(8)]

    def fn(*t):
        s = t[0]
        for u in t[1:]:
            s = s + u
        return (s,)

    return _ew(fn, ordered, [(buf.shape[1], F32)], "small_sum")[0]


def _band_bucket_onehot(dilation, max_dist):
    i = jnp.arange(BLOCK)[:, None]
    j = jnp.arange(2 * BLOCK)[None, :]
    dist = i + BLOCK - j
    inband = (dist >= 0) & (dist <= max_dist)
    n = jnp.maximum(dist, 0) * dilation
    max_exact = NUM_BUCKETS // 2
    nf = jnp.maximum(n, 1).astype(F32)
    large = max_exact + (jnp.log(nf / max_exact) / math.log(MAX_DISTANCE / max_exact)
                         * (NUM_BUCKETS - max_exact)).astype(jnp.int32)
    bucket = jnp.where(n < max_exact, n, jnp.minimum(large, NUM_BUCKETS - 1))
    onehot = (bucket[..., None] == jnp.arange(NUM_BUCKETS)) & inband[..., None]
    return onehot.reshape(-1, NUM_BUCKETS).astype(F32), inband.reshape(-1)


def _band_bias(table, onehot, inband):
    vals = jnp.einsum("pb,bh->hp", onehot, table, precision=lax.Precision.HIGHEST)
    return jnp.where(inband[None, :], vals, NEG).reshape(-1, BLOCK, 2 * BLOCK)


BAND_VARIANTS = ((1, A_WINDOW - 1),) + tuple((dil, window // dil) for window, dil in B_BRANCHES)


LAYOUT_TILE_BYTES = 8 << 20


def _layout_rows(length, row_bytes):
    return _divisor_tile(length, max(16, LAYOUT_TILE_BYTES // row_bytes), 16)


def _split_heads(items, w, dil, name):
    s = items[0][0].shape[0]
    length = s // dil
    row_bytes = sum(-(-n * w // LANES) * LANES * (x.dtype.itemsize + jnp.dtype(d).itemsize) for x, _, n, d in items)
    tr = _layout_rows(length, row_bytes)
    nt = length // tr

    def body(*refs):
        for (x_ref, o_ref), (_, _, n, _) in zip(zip(refs[:len(items)], refs[len(items):]), items):
            for j in range(n):
                o_ref[j] = x_ref[:, j * w:(j + 1) * w].astype(o_ref.dtype)

    in_specs, out_specs, out_shape, views = [], [], [], []
    for x, first, n, d in items:
        bw, width = n * w, x.shape[1]
        assert bw % LANES == 0 and first % bw == 0 and (dil == 1 or width % bw == 0)
        in_specs.append(pl.BlockSpec((tr, bw), lambda r, i, c0=first // bw, wb=width // bw: (i, r * wb + c0)))
        out_specs.append(pl.BlockSpec((n, tr, w), lambda r, i: (0, r * nt + i, 0)))
        out_shape.append(jax.ShapeDtypeStruct((n, s, w), d))
        views.append(x.reshape(length, dil * width))
    return pl.pallas_call(
        body, name=name, grid=(dil, nt), in_specs=in_specs, out_specs=out_specs, out_shape=out_shape,
        compiler_params=_params(("parallel", "parallel")),
    )(*views)


def _merge_heads(items, dil, name, group_sum=1):
    s, w = items[0][0].shape[1:]
    length = s // dil
    row_bytes = sum(t.shape[0] * LANES * t.dtype.itemsize + t.shape[0] * w * jnp.dtype(d).itemsize for t, d in items)
    tr = _layout_rows(length, row_bytes)
    nt = length // tr

    def body(*refs):
        for t_ref, o_ref in zip(refs[:len(items)], refs[len(items):]):
            for j in range(t_ref.shape[0] // group_sum):
                v = t_ref[j * group_sum]
                for g in range(1, group_sum):
                    v = v + t_ref[j * group_sum + g]
                o_ref[:, j * w:(j + 1) * w] = v.astype(o_ref.dtype)

    in_specs, out_specs, out_shape = [], [], []
    for t, d in items:
        n = t.shape[0]
        bw = n // group_sum * w
        assert bw % LANES == 0
        in_specs.append(pl.BlockSpec((n, tr, w), lambda r, i: (0, r * nt + i, 0)))
        out_specs.append(pl.BlockSpec((tr, bw), lambda r, i: (i, r)))
        out_shape.append(jax.ShapeDtypeStruct((length, dil * bw), d))
    outs = pl.pallas_call(
        body, name=name, grid=(dil, nt), in_specs=in_specs, out_specs=out_specs, out_shape=out_shape,
        compiler_params=_params(("parallel", "parallel")),
    )(*[t for t, _ in items])
    return [o.reshape(s, -1) for o in outs]


def _merge_heads_transposed(t, dtype, name, also_heads=False):
    n, w, s = t.shape
    tr = _layout_rows(s, 3 * n * w * t.dtype.itemsize)

    def body(t_ref, o_ref, *heads_ref):
        rows = jnp.transpose(t_ref[...].reshape(n * w, tr))
        o_ref[...] = rows.astype(o_ref.dtype)
        for j in range(n if also_heads else 0):
            heads_ref[0][j] = rows[:, j * w:(j + 1) * w]

    out_specs = [pl.BlockSpec((tr, n * w), lambda i: (i, 0))]
    out_shape = [jax.ShapeDtypeStruct((s, n * w), dtype)]
    if also_heads:
        out_specs.append(pl.BlockSpec((n, tr, w), lambda i: (0, i, 0)))
        out_shape.append(jax.ShapeDtypeStruct((n, s, w), t.dtype))
    outs = pl.pallas_call(
        body, name=name, grid=(s // tr,), in_specs=[pl.BlockSpec((n, w, tr), lambda i: (0, 0, i))],
        out_specs=out_specs, out_shape=out_shape, compiler_params=_params(("parallel",)),
    )(t)
    return outs if also_heads else outs[0]


ROPE_HALF = C_ROPE // 2
ROPE_PERIOD = 3 * LANES


def _rope_tables(s):
    inv = ROPE_THETA ** (-jnp.arange(0, C_ROPE, 2, dtype=F32) / C_ROPE)
    ang = jnp.arange(s, dtype=F32)[:, None] * inv[None, :]
    cos, sin = jnp.cos(ang), jnp.sin(ang)
    one, zero = jnp.ones((s, C_NOPE), F32), jnp.zeros((s, C_NOPE), F32)
    z16 = jnp.zeros((s, ROPE_HALF), F32)
    reps = ROPE_PERIOD // C_QK
    keep = jnp.tile(jnp.concatenate([one, cos, cos], axis=1), (1, reps))
    from_above = jnp.tile(jnp.concatenate([zero, -sin, z16], axis=1), (1, reps))
    from_below = jnp.tile(jnp.concatenate([zero, z16, sin], axis=1), (1, reps))
    return (cos, sin), (keep, from_above, from_below)


def _rope_rows(x, tables, inverse, name, dtype):
    width = x.shape[1]
    reps = width // ROPE_PERIOD
    sign = -1.0 if inverse else 1.0

    def fn(xv, keep, above, below):
        keep, above, below = (jnp.tile(t, (1, reps)) for t in (keep, above, below))
        up = pltpu.roll(xv, width - ROPE_HALF, 1)
        down = pltpu.roll(xv, ROPE_HALF, 1)
        return (xv * keep + sign * (up * above + down * below),)

    return _ew(fn, [x, *tables], [(width, dtype)], name)[0]


def _rotate_half_pairs(a, b, cos, sin, inverse):
    if inverse:
        return a * cos + b * sin, b * cos - a * sin
    return a * cos - b * sin, a * sin + b * cos


def _split_kv(kv, down, cos, sin, name):
    s = kv.shape[0]
    h = kv.shape[1] // (C_NOPE + C_V)
    tr = _layout_rows(s, 8 * kv.shape[1])
    r0 = C_Q_RANK + C_KV_RANK

    def body(kv_ref, down_ref, cos_ref, sin_ref, k_ref, v_ref, vt_ref):
        k1, k2 = _rotate_half_pairs(down_ref[:, r0:r0 + ROPE_HALF], down_ref[:, r0 + ROPE_HALF:r0 + C_ROPE],
                                    cos_ref[...], sin_ref[...], False)
        lane = lax.broadcasted_iota(jnp.int32, (tr, LANES - C_V), 1)
        tail = jnp.where(lane == 0, 1.0, 0.0)
        for j in range(h):
            base = j * (C_NOPE + C_V)
            k_ref[j, :, :C_NOPE] = kv_ref[:, base:base + C_NOPE]
            k_ref[j, :, C_NOPE:C_NOPE + ROPE_HALF] = k1.astype(BF16)
            k_ref[j, :, C_NOPE + ROPE_HALF:] = k2.astype(BF16)
            v1 = jnp.concatenate([kv_ref[:, base + C_NOPE:base + C_NOPE + C_V].astype(F32), tail], axis=1)
            v_ref[j] = v1.astype(BF16)
            vt_ref[j] = jnp.transpose(v1).astype(BF16)

    def rows(width):
        return pl.BlockSpec((tr, width), lambda i: (i, 0))

    return pl.pallas_call(
        body, name=name, grid=(s // tr,),
        in_specs=[rows(kv.shape[1]), rows(down.shape[1]), rows(ROPE_HALF), rows(ROPE_HALF)],
        out_specs=[pl.BlockSpec((h, tr, C_QK), lambda i: (0, i, 0)), pl.BlockSpec((h, tr, LANES), lambda i: (0, i, 0)),
                   pl.BlockSpec((h, LANES, tr), lambda i: (0, 0, i))],
        out_shape=[jax.ShapeDtypeStruct((h, s, C_QK), BF16), jax.ShapeDtypeStruct((h, s, LANES), BF16),
                   jax.ShapeDtypeStruct((h, LANES, s), BF16)],
        compiler_params=_params(("parallel",)),
    )(kv, down, cos, sin)


def _merge_kv_bwd(dk, dv, cos, sin, name):
    h, s, _ = dk.shape
    tr = _layout_rows(s, 8 * h * LANES)

    def body(dk_ref, dv_ref, cos_ref, sin_ref, dkv_ref, dkr_ref):
        rot = dk_ref[0, :, C_NOPE:]
        for j in range(h):
            base = j * (C_NOPE + C_V)
            dkv_ref[:, base:base + C_NOPE] = dk_ref[j, :, :C_NOPE].astype(BF16)
            dkv_ref[:, base + C_NOPE:base + C_NOPE + C_V] = dv_ref[j].astype(BF16)
            if j:
                rot = rot + dk_ref[j, :, C_NOPE:]
        d1, d2 = _rotate_half_pairs(rot[:, :ROPE_HALF], rot[:, ROPE_HALF:], cos_ref[...], sin_ref[...], True)
        dkr_ref[:, :ROPE_HALF] = d1
        dkr_ref[:, ROPE_HALF:] = d2

    def rows(width):
        return pl.BlockSpec((tr, width), lambda i: (i, 0))

    return pl.pallas_call(
        body, name=name, grid=(s // tr,),
        in_specs=[pl.BlockSpec((h, tr, C_QK), lambda i: (0, i, 0)), pl.BlockSpec((h, tr, C_V), lambda i: (0, i, 0)),
                  rows(ROPE_HALF), rows(ROPE_HALF)],
        out_specs=[rows(h * (C_NOPE + C_V)), rows(C_ROPE)],
        out_shape=[jax.ShapeDtypeStruct((s, h * (C_NOPE + C_V)), BF16), jax.ShapeDtypeStruct((s, C_ROPE), F32)],
        compiler_params=_params(("parallel",)),
    )(dk, dv, cos, sin)


SPAN_FWD, SPAN_BWD = 128, 256


def _even_fwd(xn, h, w_in, w_out, sinks_row, biases, tag):
    s = xn.shape[0]
    proj = _mm(xn, w_in, "nn", [BF16], f"in_proj{tag}")[0]
    qd, kd = A_Q_HEADS * HEAD_DIM, A_KV_HEADS * HEAD_DIM
    qa, ka, va = _split_heads([(proj, 0, A_Q_HEADS, BF16), (proj, qd, A_KV_HEADS, BF16),
                               (proj, qd + kd, A_KV_HEADS, BF16)], HEAD_DIM, 1, f"swa_split{tag}")
    oa, lse_a = _band_fwd(qa, ka, va, biases[0], s // BLOCK, SPAN_FWD, f"swa_fwd{tag}")
    out_a = _merge_with_sinks(oa, lse_a, sinks_row, f"swa_merge{tag}")
    width = B_HEADS_PER_BRANCH * HEAD_DIM
    qkv_b, outs, lses = [], [], []
    for g, (_, dil) in enumerate(B_BRANCHES):
        base = A_IN + g * 3 * width
        src, base = (proj, base) if dil == 1 else (proj[:, base:base + 3 * width], 0)
        qkv = _split_heads([(src, base + i * width, B_HEADS_PER_BRANCH, BF16) for i in range(3)], HEAD_DIM, dil,
                           f"dil{g}_split{tag}")
        og, lg = _band_fwd(*qkv, biases[1 + g], s // dil // BLOCK, SPAN_FWD, f"dil{g}_fwd{tag}")
        qkv_b.append(qkv)
        merged = _merge_heads([(og, F32), (lg, F32)], dil, f"dil{g}_merge{tag}")
        outs.append(merged[0])
        lses.append(merged[1])

    def merge(o0, o1, o2, l0, l1, l2):
        m = jnp.maximum(jnp.maximum(l0, l1), l2)
        e0, e1, e2 = jnp.exp(l0 - m), jnp.exp(l1 - m), jnp.exp(l2 - m)
        den = e0 + e1 + e2
        out = (e0 * o0 + e1 * o1 + e2 * o2) / den
        return out, m + jnp.log(den), out

    out_b, lse_b, out_b16 = _ew(merge, outs + lses, [(width, F32), (width, F32), (width, BF16)], f"dil_merge{tag}")
    cat = jnp.concatenate([out_a, out_b16], axis=1)
    h_mid = _mm(cat, w_out, "nn", [F32], f"out_proj{tag}", epi=_add_epi, extras=(h,))[0]
    return h_mid, (qa, ka, va, oa, lse_a, qkv_b, out_b, lse_b, cat)


def _even_bwd(dh, xn, saved, w_in, w_out, sinks_row, biases, tag):
    qa, ka, va, oa, lse_a, qkv_b, out_b, lse_b, cat = saved
    s = xn.shape[0]
    qd = A_Q_HEADS * HEAD_DIM
    g_w_out = _mm(cat, dh, "tn", [F32], f"out_proj_dw{tag}")[0]
    dcat = _mm(dh, w_out, "nt", [BF16], f"out_proj_dx{tag}")[0]
    do_a = _split_heads([(dcat, 0, A_Q_HEADS, BF16)], HEAD_DIM, 1, f"swa_do_split{tag}")[0]
    dqa, dka8, dva8, dbias_a, dsink = _band_bwd(qa, ka, va, oa, lse_a, do_a, biases[0], sinks_row, s // BLOCK,
                                                SPAN_BWD, f"swa_bwd{tag}")
    pieces = _merge_heads([(dqa, BF16)], 1, f"swa_dq_merge{tag}")
    pieces += _merge_heads([(dka8, BF16), (dva8, BF16)], 1, f"swa_dkv_merge{tag}", group_sum=A_Q_HEADS // A_KV_HEADS)
    dbias_b = []
    dcat_b = dcat[:, qd:]
    for g, (_, dil) in enumerate(B_BRANCHES):
        do_g, out_g, lse_g = _split_heads([(dcat_b, 0, B_HEADS_PER_BRANCH, BF16), (out_b, 0, B_HEADS_PER_BRANCH, F32),
                                           (lse_b, 0, B_HEADS_PER_BRANCH, F32)], HEAD_DIM, dil, f"dil{g}_do_split{tag}")
        dqg, dkg, dvg, dbg, _ = _band_bwd(*qkv_b[g], out_g, lse_g, do_g, biases[1 + g], None, s // dil // BLOCK,
                                          SPAN_BWD, f"dil{g}_bwd{tag}")
        pieces += _merge_heads([(dqg, BF16), (dkg, BF16), (dvg, BF16)], dil, f"dil{g}_dqkv_merge{tag}")
        dbias_b.append(dbg)
    dproj = jnp.concatenate(pieces, axis=1)
    g_w_in = _mm(xn, dproj, "tn", [F32], f"in_proj_dw{tag}")[0]
    dxn = _mm(dproj, w_in, "nt", [F32], f"in_proj_dx{tag}")[0]
    dbias = jnp.concatenate([db.reshape(db.shape[0], -1) for db in [dbias_a] + dbias_b])
    return dxn, g_w_in, g_w_out, dbias, dsink[:, 0, 0]


def _mla_layer_fwd(xn, h, w_dn, q_norm, w_uq, kv_norm, w_ukv, w_o, ropes, tag):
    (cos, sin), q_tables = ropes
    down = _mm(xn, w_dn, "nn", [F32], f"mla_down{tag}")[0]
    c_q, c_kv = down[:, :C_Q_RANK], down[:, C_Q_RANK:C_Q_RANK + C_KV_RANK]
    cqn = _rms_fwd(c_q, q_norm, f"mla_qnorm{tag}")
    ckvn = _rms_fwd(c_kv, kv_norm, f"mla_kvnorm{tag}")
    q = _mm(cqn, w_uq, "nn", [F32], f"mla_uq{tag}")[0]
    kv = _mm(ckvn, w_ukv, "nn", [BF16], f"mla_ukv{tag}")[0]
    qh = _split_heads([(_rope_rows(q, q_tables, False, f"mla_rope_q{tag}", BF16), 0, C_HEADS, BF16)], C_QK, 1,
                      f"mla_q_split{tag}")[0]
    kh, v1h, v1t = _split_kv(kv, down, cos, sin, f"mla_kv_split{tag}")
    ot, lse = _mla_fwd(qh, kh, v1t, f"mla_attn_fwd{tag}")
    o2d, o = _merge_heads_transposed(ot, BF16, f"mla_o_merge{tag}", also_heads=True)
    h_mid = _mm(o2d, w_o, "nn", [F32], f"mla_o{tag}", epi=_add_epi, extras=(h,))[0]
    return h_mid, (c_q, c_kv, cqn, ckvn, qh, kh, v1h, o, lse, o2d)


def _mla_layer_bwd(dh, xn, saved, w_dn, q_norm, w_uq, kv_norm, w_ukv, w_o, ropes, tag):
    c_q, c_kv, cqn, ckvn, qh, kh, v1h, o, lse, o2d = saved
    (cos, sin), q_tables = ropes
    g_w_o = _mm(o2d, dh, "tn", [F32], f"mla_o_dw{tag}")[0]
    do2d = _mm(dh, w_o, "nt", [BF16], f"mla_o_dx{tag}")[0]
    do = _split_heads([(do2d, 0, C_HEADS, BF16)], C_V, 1, f"mla_do_split{tag}")[0]
    dqt, dk, dv = _mla_bwd(qh, kh, v1h, o, lse, do, f"mla_attn_bwd{tag}")
    dq_roped = _merge_heads_transposed(dqt, F32, f"mla_dq_merge{tag}")
    dq = _rope_rows(dq_roped, q_tables, True, f"mla_rope_q_bwd{tag}", BF16)
    dkv, dk_rope = _merge_kv_bwd(dk, dv, cos, sin, f"mla_dkv_merge{tag}")
    g_w_uq = _mm(cqn, dq, "tn", [F32], f"mla_uq_dw{tag}")[0]
    dcqn = _mm(dq, w_uq, "nt", [F32], f"mla_uq_dx{tag}")[0]
    g_w_ukv = _mm(ckvn, dkv, "tn", [F32], f"mla_ukv_dw{tag}")[0]
    dckvn = _mm(dkv, w_ukv, "nt", [F32], f"mla_ukv_dx{tag}")[0]
    dc_q, g_q_norm = _rms_bwd(c_q, q_norm, dcqn, None, f"mla_qnorm_bwd{tag}")
    dc_kv, g_kv_norm = _rms_bwd(c_kv, kv_norm, dckvn, None, f"mla_kvnorm_bwd{tag}")
    ddown = jnp.concatenate([dc_q, dc_kv, dk_rope], axis=1).astype(BF16)
    g_w_dn = _mm(xn, ddown, "tn", [F32], f"mla_down_dw{tag}")[0]
    dxn = _mm(ddown, w_dn, "nt", [F32], f"mla_down_dx{tag}")[0]
    return dxn, g_w_dn, g_q_norm, g_w_uq, g_kv_norm, g_w_ukv, g_w_o


SHARDED = (("w_in_ab", 2), ("w_out_ab", 2), ("w_down_c", 1), ("w_uq_c", 2), ("w_ukv_c", 2), ("w_o_c", 2),
           ("w_mlp_up", 2), ("w_mlp_down", 1))
SHARDED_NORMS = ("q_norm_c", "kv_norm_c")


def kernel(x, rel_bias, attn_norm, mlp_norm, final_norm, w_in_ab, sinks, w_out_ab, w_down_c, q_norm_c, w_uq_c, kv_norm_c, w_ukv_c, w_o_c, w_mlp_up, w_mlp_down, loss_target, m_rel_bias, m_attn_norm, m_mlp_norm, m_final_norm, m_w_in_ab, m_sinks, m_w_out_ab, m_w_down_c, m_q_norm_c, m_w_uq_c, m_kv_norm_c, m_w_ukv_c, m_w_o_c, m_w_mlp_up, m_w_mlp_down, v_rel_bias, v_attn_norm, v_mlp_norm, v_final_norm, v_w_in_ab, v_sinks, v_w_out_ab, v_w_down_c, v_q_norm_c, v_w_uq_c, v_kv_norm_c, v_w_ukv_c, v_w_o_c, v_w_mlp_up, v_w_mlp_down):
    given = dict(locals())
    chip = lax.axis_index("x") * 2 + lax.axis_index("y")
    core = lax.axis_index("c")
    dev = chip * 2 + core
    depth = attn_norm.shape[0]
    s = x.shape[1]

    shards = [given[n] for n, _ in SHARDED]
    norm_shards = [given[n] for n in SHARDED_NORMS]
    packed = _to_comm_rows([t.astype(BF16) for t in shards]
                           + [lax.bitcast_convert_type(t, BF16) for t in norm_shards], BF16)
    by_chip = _gather_shards(packed, chip)
    shapes = [t.shape for t in shards] + [t.shape + (2,) for t in norm_shards]
    pieces = [_from_comm_rows(by_chip[a], shapes) for a in range(4)]
    full = {n: jnp.concatenate([pieces[a][i] for a in range(4)], axis=ax) for i, (n, ax) in enumerate(SHARDED)}
    for i, n in enumerate(SHARDED_NORMS):
        full[n] = jnp.concatenate([lax.bitcast_convert_type(pieces[a][len(SHARDED) + i], F32) for a in range(4)],
                                  axis=-1)

    onehots = [_band_bucket_onehot(dil, md) for dil, md in BAND_VARIANTS]
    head_cols = [(0, A_Q_HEADS)] + [(A_Q_HEADS + g * B_HEADS_PER_BRANCH, A_Q_HEADS + (g + 1) * B_HEADS_PER_BRANCH)
                                    for g in range(len(B_BRANCHES))]
    biases = [_band_bias(rel_bias[:, lo:hi], oh, inb) for (lo, hi), (oh, inb) in zip(head_cols, onehots)]
    ropes = _rope_tables(s)
    sink_rows = [jnp.broadcast_to(sinks[e][:, None, None], (A_Q_HEADS, BAND_SPAN_MAX, LANES))
                 for e in range(sinks.shape[0])]

    def odd_weights(o):
        return [full[n][o] for n in ("w_down_c", "q_norm_c", "w_uq_c", "kv_norm_c", "w_ukv_c", "w_o_c")]

    h = x[0]
    saved = []
    for l in range(depth):
        xn = _rms_fwd(h, attn_norm[l], f"attn_norm{l}")
        if l % 2 == 0:
            e = l // 2
            h_mid, mix = _even_fwd(xn, h, full["w_in_ab"][e], full["w_out_ab"][e], sink_rows[e], biases, f"_{l}")
        else:
            h_mid, mix = _mla_layer_fwd(xn, h, *odd_weights(l // 2), ropes, f"_{l}")
        xn2 = _rms_fwd(h_mid, mlp_norm[l], f"mlp_norm{l}")
        act, relu = _mm(xn2, full["w_mlp_up"][l], "nn", [BF16, BF16], f"mlp_up{l}", epi=_relu2_epi)
        h_out = _mm(act, full["w_mlp_down"][l], "nn", [F32], f"mlp_down{l}", epi=_add_epi, extras=(h_mid,))[0]
        saved.append((h, xn, mix, h_mid, xn2, act, relu))
        h = h_out
    loss_part, dh, dh16, g_final = _final_loss(h, final_norm, loss_target[0], "final_loss")

    grads = {n: [None] * given[n].shape[0] for n, _ in SHARDED}
    g_q_norm, g_kv_norm = [None] * q_norm_c.shape[0], [None] * kv_norm_c.shape[0]
    g_attn_norm, g_mlp_norm = [None] * depth, [None] * depth
    g_sinks = [None] * sinks.shape[0]
    g_band = []
    for l in range(depth - 1, -1, -1):
        h_in, xn, mix, h_mid, xn2, act, relu = saved[l]
        grads["w_mlp_down"][l] = _mm(act, dh16, "tn", [F32], f"mlp_down_dw{l}")[0]
        du = _mm(dh16, full["w_mlp_down"][l], "nt", [BF16], f"mlp_down_dx{l}", epi=_relu2_bwd_epi, extras=(relu,))[0]
        grads["w_mlp_up"][l] = _mm(xn2, du, "tn", [F32], f"mlp_up_dw{l}")[0]
        dxn2 = _mm(du, full["w_mlp_up"][l], "nt", [F32], f"mlp_up_dx{l}")[0]
        dh, dh16, g_mlp_norm[l] = _rms_bwd(h_mid, mlp_norm[l], dxn2, dh, f"mlp_norm_bwd{l}", also_bf16=True)
        if l % 2 == 0:
            e = l // 2
            dxn, grads["w_in_ab"][e], grads["w_out_ab"][e], g_band_e, g_sinks[e] = _even_bwd(
                dh16, xn, mix, full["w_in_ab"][e], full["w_out_ab"][e], sink_rows[e], biases, f"_{l}")
            g_band.append(g_band_e)
        else:
            o = l // 2
            (dxn, grads["w_down_c"][o], g_q_norm[o], grads["w_uq_c"][o], g_kv_norm[o], grads["w_ukv_c"][o],
             grads["w_o_c"][o]) = _mla_layer_bwd(dh16, xn, mix, *odd_weights(o), ropes, f"_{l}")
        if l:
            dh, dh16, g_attn_norm[l] = _rms_bwd(h_in, attn_norm[l], dxn, dh, f"attn_norm_bwd{l}", also_bf16=True)
        else:
            dh, g_attn_norm[l] = _rms_bwd(h_in, attn_norm[l], dxn, dh, f"attn_norm_bwd{l}")
    grad_x = dh[None]
    buckets = jnp.concatenate([oh for oh, _ in onehots], axis=1)
    by_variant = _mm(jnp.concatenate(g_band, axis=1), jnp.tile(buckets, (len(g_band), 1)), "nn", [F32],
                     "bias_buckets")[0]
    g_rel_bias = jnp.concatenate([by_variant[lo:hi, v * NUM_BUCKETS:(v + 1) * NUM_BUCKETS].T
                                  for v, (lo, hi) in enumerate(head_cols)], axis=1)

    portions = []
    for a in range(4):
        parts = []
        for n, ax in SHARDED:
            g = jnp.stack(grads[n])
            size = g.shape[ax] // 4
            parts.append(lax.slice_in_dim(g, a * size, (a + 1) * size, axis=ax))
        portions.append(_to_comm_rows(parts, F32))
    reduced = _from_comm_rows(_reduce_scatter(jnp.stack(portions), chip, core), [t.shape for t in shards])
    g_shard = {n: reduced[i] for i, (n, _) in enumerate(SHARDED)}

    small = [jnp.stack(g_attn_norm), jnp.stack(g_mlp_norm), g_final, g_rel_bias, jnp.stack(g_sinks),
             jnp.stack(g_q_norm), jnp.stack(g_kv_norm), loss_part.reshape(1)]
    small_shapes = [t.shape for t in small]
    summed = _from_comm_rows(_all_reduce_small(_to_comm_rows(small, F32)[:16], dev), small_shapes)
    loss = summed[7][0]
    g_small = dict(zip(("attn_norm", "mlp_norm", "final_norm", "rel_bias", "sinks"), summed[:5]))
    for n, g in zip(SHARDED_NORMS, summed[5:7]):
        size = g.shape[1] // 4
        g_shard[n] = lax.dynamic_slice_in_dim(g, chip * size, size, axis=1)

    order = ["rel_bias", "attn_norm", "mlp_norm", "final_norm", "w_in_ab", "sinks", "w_out_ab", "w_down_c",
             "q_norm_c", "w_uq_c", "kv_norm_c", "w_ukv_c", "w_o_c", "w_mlp_up", "w_mlp_down"]
    g_all = {**g_small, **g_shard}
    deltas, new_m, new_v = [], [], []
    for n in order:
        d, mn, vn = _adamw(given[n], g_all[n], given["m_" + n], given["v_" + n], f"adamw_{n}")
        deltas.append(d)
        new_m.append(mn)
        new_v.append(vn)
    return (loss, grad_x, *[g_all[n] for n in order], *deltas, *new_m, *new_v)
```
